```python
import math
import jax
import jax.numpy as jnp
from jax import lax
import numpy as np

D_MODEL = 1024
BATCH = 2
SEQ = 8192
DEPTH = 4
DEC_BATCH = 128
DEC_SEQ = 1
PAST_LEN = 2048
PAGE_SIZE = 128

N_MIXERS = 3
N_LAYERS_A = (DEPTH + 2) // 3
N_LAYERS_B = (DEPTH + 1) // 3
N_LAYERS_C = DEPTH // 3
EPS = 1e-6
D_FF = 4 * D_MODEL
CHUNK_A = 128
D_GATE_A = D_MODEL
N_GROUPS_A = 8
GROUP_W_A = D_GATE_A // N_GROUPS_A
N_HEADS_B = 16
N_KV_B = 4
HD_B = D_MODEL // N_HEADS_B
ROT_DIM = HD_B // 4
ROPE_THETA = 500000.0
CMP_BLOCK = 32
SEL_BLOCK = 64
N_SEL = 16
WINDOW = 512
Q_BLOCK = 128
N_KV_SLOTS = 4
D_IN_B = N_HEADS_B * HD_B + 3 * N_HEADS_B + 6 * N_KV_B * HD_B
N_HEADS_C = 8
DK_C = 64
DV_C = 128
CHUNK_C = 128
D_IN_C = 2 * N_HEADS_C * DK_C + 2 * N_HEADS_C * DV_C + 2 * N_HEADS_C

kernel_name = "hybrid_gmlp_nsa_mlstm_step"


def rms_norm(x, g):
    xf = x.astype(jnp.float32)
    y = xf * lax.rsqrt(jnp.mean(xf * xf, axis=-1, keepdims=True) + EPS)
    return (y * g.astype(jnp.float32)).astype(x.dtype)


def layer_norm(x, g, b):
    xf = x.astype(jnp.float32)
    mu = jnp.mean(xf, axis=-1, keepdims=True)
    var = jnp.mean(jnp.square(xf - mu), axis=-1, keepdims=True)
    y = (xf - mu) * lax.rsqrt(var + EPS) * g.astype(jnp.float32) + b.astype(jnp.float32)
    return y.astype(x.dtype)


def squared_relu_mlp(h, w1, w2):
    return jnp.square(jax.nn.relu(h @ w1)) @ w2


def masked_softmax(s, mask):
    s = jnp.where(mask, s, -jnp.inf)
    m = jnp.max(s, axis=-1, keepdims=True)
    m = jnp.where(jnp.isfinite(m), m, 0.0)
    e = jnp.where(mask, jnp.exp(s - m), 0.0)
    return e / jnp.maximum(jnp.sum(e, axis=-1, keepdims=True), 1e-30)


def rope_partial(x, pos):
    half = ROT_DIM // 2
    freq = jnp.power(ROPE_THETA, -jnp.arange(half, dtype=jnp.float32) * 2.0 / ROT_DIM)
    ang = pos.astype(jnp.float32)[:, None] * freq[None, :]
    cos, sin = jnp.cos(ang)[:, None, :], jnp.sin(ang)[:, None, :]
    xf = x.astype(jnp.float32)
    x1, x2 = xf[..., :half], xf[..., half:ROT_DIM]
    out = jnp.concatenate([x1 * cos - x2 * sin, x2 * cos + x1 * sin, xf[..., ROT_DIM:]], axis=-1)
    return out.astype(x.dtype)


def chunk_spatial_gate(v, w_s, b_s):
    B, T, _ = v.shape
    n_chunks = -(-T // CHUNK_A)
    vp = jnp.pad(v, ((0, 0), (0, n_chunks * CHUNK_A - T), (0, 0)))
    vp = vp.reshape(B, n_chunks, CHUNK_A, N_GROUPS_A, GROUP_W_A)
    causal = jnp.tril(jnp.ones((CHUNK_A, CHUNK_A), dtype=bool))
    w = jnp.where(causal[None], w_s, jnp.zeros((), w_s.dtype))
    out = jnp.einsum('gts,bcsgd->bctgd', w, vp) + b_s.T[None, None, :, :, None]
    return out.reshape(B, n_chunks * CHUNK_A, D_GATE_A)[:, :T]


def gmlp_mixer(h, w_in, ln_g, ln_b, w_s, b_s, w_out):
    z = jax.nn.gelu(h @ w_in)
    u, v = jnp.split(z, 2, axis=-1)
    v = layer_norm(v, ln_g, ln_b)
    return (u * chunk_spatial_gate(v, w_s, b_s)) @ w_out, v


def nsa_project(h, pos, w_in, q_g, k_g):
    B, T, _ = h.shape
    q, g, kv = jnp.split(h @ w_in, [N_HEADS_B * HD_B, N_HEADS_B * HD_B + 3 * N_HEADS_B], axis=-1)
    q_nope = rms_norm(q.reshape(B, T, N_HEADS_B, HD_B), q_g)
    kv = kv.reshape(B, T, 6, N_KV_B, HD_B)
    ks = rope_partial(rms_norm(kv[:, :, 2], k_g[1]), pos)
    kw = rope_partial(rms_norm(kv[:, :, 4], k_g[2]), pos)
    gates = jax.nn.sigmoid(g.astype(jnp.float32)).astype(h.dtype).reshape(B, T, N_HEADS_B, 3)
    return (rope_partial(q_nope, pos), q_nope, gates,
            kv[:, :, 0], kv[:, :, 1], ks, kv[:, :, 3], kw, kv[:, :, 5])


def nsa_compress(kc, vc, pe, w_c1, w_c2, kc_g):
    B, L = kc.shape[:2]
    nb = L // CMP_BLOCK

    def phi(x, j):
        xb = x[:, :nb * CMP_BLOCK].reshape(B, nb, CMP_BLOCK, N_KV_B, HD_B) + pe[j][None, None, :, None, :]
        hid = jax.nn.gelu(jnp.einsum('bnlgd,lde->bnge', xb, w_c1[j]))
        return hid @ w_c2[j]

    return rms_norm(phi(kc, 0), kc_g), phi(vc, 1)


def to_sel_blocks(x):
    B, L = x.shape[:2]
    nsb = -(-L // SEL_BLOCK)
    xp = jnp.pad(x, ((0, 0), (0, nsb * SEL_BLOCK - L), (0, 0), (0, 0)))
    return xp.reshape(B, nsb, SEL_BLOCK, N_KV_B, HD_B).transpose(0, 3, 1, 2, 4)


def nsa_core(q_rope, q_nope, gates, pos_q, kc_blk, vc_blk, ks_blk, vs_blk, kw, vw, pos_w):
    B, Tq = q_rope.shape[:2]
    R = N_HEADS_B // N_KV_B
    scale = HD_B ** -0.5
    qn = q_nope.reshape(B, Tq, N_KV_B, R, HD_B)
    qr = q_rope.reshape(B, Tq, N_KV_B, R, HD_B)
    t5 = pos_q[None, :, None, None, None]
    nb = kc_blk.shape[1]
    blk_end = (jnp.arange(nb) + 1) * CMP_BLOCK - 1
    s = jnp.einsum('btgrd,bngd->btgrn', qn, kc_blk).astype(jnp.float32) * scale
    p_cmp = masked_softmax(s, blk_end <= t5)
    o_cmp = jnp.einsum('btgrn,bngd->btgrd', p_cmp.astype(vc_blk.dtype), vc_blk)
    nsb = ks_blk.shape[2]
    ratio = SEL_BLOCK // CMP_BLOCK
    imp = jnp.pad(jnp.sum(p_cmp, axis=3), ((0, 0), (0, 0), (0, 0), (0, nsb * ratio - nb)))
    imp = imp.reshape(B, Tq, N_KV_B, nsb, ratio).sum(-1)
    blk = jnp.arange(nsb)
    t4 = pos_q[None, :, None, None]
    cur = t4 // SEL_BLOCK
    forced = (blk == 0) | (blk == cur) | (blk == cur - 1)
    valid = blk * SEL_BLOCK <= t4
    score = jnp.where(forced, jnp.inf, jnp.where(valid, imp, -jnp.inf))
    _, idx = lax.top_k(score, min(N_SEL, nsb))
    n_sel = idx.shape[-1]
    idx = idx.transpose(0, 2, 1, 3)
    bi = jnp.arange(B)[:, None, None, None]
    gi = jnp.arange(N_KV_B)[None, :, None, None]
    k_sel = ks_blk[bi, gi, idx].reshape(B, N_KV_B, Tq, n_sel * SEL_BLOCK, HD_B)
    v_sel = vs_blk[bi, gi, idx].reshape(B, N_KV_B, Tq, n_sel * SEL_BLOCK, HD_B)
    pos_sel = (idx[..., None] * SEL_BLOCK + jnp.arange(SEL_BLOCK)).reshape(B, N_KV_B, Tq, n_sel * SEL_BLOCK)
    mask_sel = pos_sel.transpose(0, 2, 1, 3)[:, :, :, None, :] <= t5
    s = jnp.einsum('btgrd,bgtsd->btgrs', qr, k_sel).astype(jnp.float32) * scale
    p = masked_softmax(s, mask_sel)
    o_sel = jnp.einsum('btgrs,bgtsd->btgrd', p.astype(v_sel.dtype), v_sel)
    dist = t5 - pos_w
    mask_w = (dist >= 0) & (dist <= WINDOW) & (pos_w >= 0)
    s = jnp.einsum('btgrd,bsgd->btgrs', qr, kw).astype(jnp.float32) * scale
    p = masked_softmax(s, mask_w)
    o_win = jnp.einsum('btgrs,bsgd->btgrd', p.astype(vw.dtype), vw)
    g = gates.reshape(B, Tq, N_KV_B, R, 3)
    o = g[..., 0:1] * o_cmp + g[..., 1:2] * o_sel + g[..., 2:3] * o_win
    return o.reshape(B, Tq, N_HEADS_B * HD_B)


def nsa_prompt(h, w_in, q_g, k_g, pe, w_c1, w_c2, w_out):
    B, T, _ = h.shape
    pos = jnp.arange(T)
    q_r, q_n, gates, kc, vc, ks, vs, kw, vw = nsa_project(h, pos, w_in, q_g, k_g)
    kc_blk, vc_blk = nsa_compress(kc, vc, pe, w_c1, w_c2, k_g[0])
    ks_blk, vs_blk = to_sel_blocks(ks), to_sel_blocks(vs)
    pad = ((0, 0), (WINDOW, 0), (0, 0), (0, 0))
    kw_pad, vw_pad = jnp.pad(kw, pad), jnp.pad(vw, pad)

    def query_block(q0):
        sl = lambda a: lax.dynamic_slice_in_dim(a, q0, Q_BLOCK, axis=1)
        wl = lambda a: lax.dynamic_slice_in_dim(a, q0, WINDOW + Q_BLOCK, axis=1)
        return nsa_core(sl(q_r), sl(q_n), sl(gates), q0 + jnp.arange(Q_BLOCK),
                        kc_blk, vc_blk, ks_blk, vs_blk, wl(kw_pad), wl(vw_pad),
                        q0 - WINDOW + jnp.arange(WINDOW + Q_BLOCK))

    o = lax.map(query_block, jnp.arange(T // Q_BLOCK) * Q_BLOCK)
    o = jnp.moveaxis(o, 0, 1).reshape(B, T, N_HEADS_B * HD_B)
    wb = min(WINDOW, T)
    kv_rows = jnp.stack([kc, vc, ks, vs], axis=2)
    win_rows = jnp.stack([kw[:, T - wb:], vw[:, T - wb:]], axis=2)
    return o @ w_out, kv_rows, win_rows


def nsa_sample(h, kv_cache, win_cache, page_table, w_in, q_g, k_g, pe, w_c1, w_c2, w_out):
    B, T, _ = h.shape
    pos = PAST_LEN + jnp.arange(T)
    q_r, q_n, gates, kc, vc, ks, vs, kw, vw = nsa_project(h, pos, w_in, q_g, k_g)
    past = kv_cache[page_table]
    past = past.reshape(B, -1, N_KV_SLOTS, N_KV_B, HD_B)
    new_kv = jnp.stack([kc, vc, ks, vs], axis=2)
    full = jnp.concatenate([past.astype(new_kv.dtype), new_kv], axis=1)
    kc_blk, vc_blk = nsa_compress(full[:, :, 0], full[:, :, 1], pe, w_c1, w_c2, k_g[0])
    ks_blk, vs_blk = to_sel_blocks(full[:, :, 2]), to_sel_blocks(full[:, :, 3])
    new_win = jnp.stack([kw, vw], axis=2)
    win = jnp.concatenate([win_cache.astype(new_win.dtype), new_win], axis=1)
    wb = win_cache.shape[1]
    pos_w = PAST_LEN - wb + jnp.arange(wb + T)
    o = nsa_core(q_r, q_n, gates, pos, kc_blk, vc_blk, ks_blk, vs_blk, win[:, :, 0], win[:, :, 1], pos_w)
    return o @ w_out, new_kv, new_win


def mlstm_scan(q, k, v, i_pre, log_f, C0, n0, m0):
    B, T, H, _ = q.shape
    L = math.gcd(T, CHUNK_C)
    nc = T // L
    f32 = jnp.float32

    def chunks(arr):
        return jnp.swapaxes(arr.astype(f32).reshape(B, nc, L, *arr.shape[2:]), 0, 1)

    causal = jnp.tril(jnp.ones((L, L), dtype=bool))

    def step(carry, xs):
        C, n, m = carry
        qc, kc, vc, ic, fc = xs
        b = jnp.cumsum(fc, axis=1).transpose(0, 2, 1)
        it = ic.transpose(0, 2, 1)
        d_log = jnp.where(causal, b[..., :, None] - b[..., None, :] + it[..., None, :], -jnp.inf)
        inter = b + m[..., None]
        m_t = jnp.maximum(inter, jnp.max(d_log, axis=-1))
        w = jnp.exp(d_log - m_t[..., None])
        a = jnp.exp(inter - m_t)
        s = jnp.einsum('bthd,bshd->bhts', qc, kc) * w
        num = a[..., None] * jnp.einsum('bhvd,bthd->bhtv', C, qc) + jnp.einsum('bhts,bshv->bhtv', s, vc)
        den = a * jnp.einsum('bhd,bthd->bht', n, qc) + jnp.sum(s, axis=-1)
        h = num / jnp.maximum(jnp.abs(den), jnp.exp(-m_t))[..., None]
        b_end = b[..., -1]
        w_log = b_end[..., None] - b + it
        m_new = jnp.maximum(b_end + m, jnp.max(w_log, axis=-1))
        wk = jnp.exp(w_log - m_new[..., None])
        decay = jnp.exp(b_end + m - m_new)
        C_new = decay[..., None, None] * C + jnp.einsum('bhs,bshv,bshd->bhvd', wk, vc, kc)
        n_new = decay[..., None] * n + jnp.einsum('bhs,bshd->bhd', wk, kc)
        return (C_new, n_new, m_new), jnp.swapaxes(h, 1, 2)

    init = (C0.astype(f32), n0.astype(f32), m0.astype(f32))
    xs = (chunks(q), chunks(k), chunks(v), chunks(i_pre), chunks(log_f))
    (C, n, m), hs = lax.scan(step, init, xs)
    return jnp.swapaxes(hs, 0, 1).reshape(B, T, H, -1), C, n, m


def mlstm_mixer(h, C0, n0, m0, w_in, b_if, h_g, w_out):
    B, T, _ = h.shape
    HK, HV = N_HEADS_C * DK_C, N_HEADS_C * DV_C
    q, k, v, gi, o = jnp.split(h @ w_in, [HK, 2 * HK, 2 * HK + HV, 2 * HK + HV + 2 * N_HEADS_C], axis=-1)
    gi = gi.astype(jnp.float32) + b_if.astype(jnp.float32)
    i_pre, f_pre = gi[..., :N_HEADS_C], gi[..., N_HEADS_C:]
    hs, C, n, m = mlstm_scan(q.reshape(B, T, N_HEADS_C, DK_C),
                             k.reshape(B, T, N_HEADS_C, DK_C) * (DK_C ** -0.5),
                             v.reshape(B, T, N_HEADS_C, DV_C),
                             i_pre, jax.nn.log_sigmoid(f_pre), C0, n0, m0)
    hs = rms_norm(hs.astype(h.dtype), h_g.reshape(N_HEADS_C, DV_C)).reshape(B, T, HV)
    return (jax.nn.sigmoid(o) * hs) @ w_out, C, n, m


def setup_inputs(seed: int = 0) -> dict:
    key = jax.random.key(seed)
    keys = iter(jax.random.split(key, 64))

    def nrm(shape, scale=1.0):
        return jax.random.normal(next(keys), shape, jnp.float32) * scale

    def gain(shape):
        return 1.0 + nrm(shape, 0.02)

    n_pages = PAST_LEN // PAGE_SIZE
    pool = (DEC_BATCH * n_pages * 5) // 4
    wb = min(WINDOW, PAST_LEN)
    page_table = jax.random.permutation(next(keys), pool)[:DEC_BATCH * n_pages]
    page_table = page_table.reshape(DEC_BATCH, n_pages).astype(jnp.int32)
    f_bias = jnp.linspace(3.0, 6.0, N_HEADS_C, dtype=jnp.float32)
    c_b_if = jnp.concatenate([nrm((N_LAYERS_C, N_HEADS_C), 0.1),
                              f_bias[None, :] + nrm((N_LAYERS_C, N_HEADS_C), 0.1)], axis=-1)
    return {
        'x_prompt': nrm((BATCH, SEQ, D_MODEL)),
        'x_sample': nrm((DEC_BATCH, DEC_SEQ, D_MODEL)),
        'cache_nsa_kv': nrm((N_LAYERS_B, pool, PAGE_SIZE, N_KV_SLOTS, N_KV_B, HD_B)),
        'cache_nsa_win': nrm((N_LAYERS_B, DEC_BATCH, wb, 2, N_KV_B, HD_B)),
        'state_mlstm_C': nrm((N_LAYERS_C, DEC_BATCH, N_HEADS_C, DV_C, DK_C), 0.3),
        'state_mlstm_n': nrm((N_LAYERS_C, DEC_BATCH, N_HEADS_C, DK_C), 0.3),
        'state_mlstm_m': nrm((N_LAYERS_C, DEC_BATCH, N_HEADS_C)),
        'page_table': page_table,
        'norm_mix_g': gain((DEPTH, D_MODEL)),
        'norm_ffn_g': gain((DEPTH, D_MODEL)),
        'ffn_w1': nrm((DEPTH, D_MODEL, D_FF), D_MODEL ** -0.5),
        'ffn_w2': nrm((DEPTH, D_FF, D_MODEL), D_FF ** -0.5),
        'a_w_in': nrm((N_LAYERS_A, D_MODEL, 2 * D_GATE_A), D_MODEL ** -0.5),
        'a_ln_g': gain((N_LAYERS_A, D_GATE_A)),
        'a_ln_b': nrm((N_LAYERS_A, D_GATE_A), 0.02),
        'a_w_s': nrm((N_LAYERS_A, N_GROUPS_A, CHUNK_A, CHUNK_A), CHUNK_A ** -0.5),
        'a_b_s': gain((N_LAYERS_A, N_GROUPS_A, CHUNK_A)),
        'a_w_out': nrm((N_LAYERS_A, D_GATE_A, D_MODEL), D_GATE_A ** -0.5),
        'b_w_in': nrm((N_LAYERS_B, D_MODEL, D_IN_B), D_MODEL ** -0.5),
        'b_q_g': gain((N_LAYERS_B, HD_B)),
        'b_k_g': gain((N_LAYERS_B, 3, HD_B)),
        'b_pe': nrm((N_LAYERS_B, 2, CMP_BLOCK, HD_B), 0.1),
        'b_w_c1': nrm((N_LAYERS_B, 2, CMP_BLOCK, HD_B, HD_B), (CMP_BLOCK * HD_B) ** -0.5),
        'b_w_c2': nrm((N_LAYERS_B, 2, HD_B, HD_B), HD_B ** -0.5),
        'b_w_out': nrm((N_LAYERS_B, N_HEADS_B * HD_B, D_MODEL), (N_HEADS_B * HD_B) ** -0.5),
        'c_w_in': nrm((N_LAYERS_C, D_MODEL, D_IN_C), D_MODEL ** -0.5),
        'c_b_if': c_b_if,
        'c_h_g': gain((N_LAYERS_C, N_HEADS_C * DV_C)),
        'c_w_out': nrm((N_LAYERS_C, N_HEADS_C * DV_C, D_MODEL), (N_HEADS_C * DV_C) ** -0.5),
    }


def reference(x_prompt, x_sample, cache_nsa_kv, cache_nsa_win, state_mlstm_C, state_mlstm_n,
              state_mlstm_m, page_table, norm_mix_g, norm_ffn_g, ffn_w1, ffn_w2,
              a_w_in, a_ln_g, a_ln_b, a_w_s, a_b_s, a_w_out,
              b_w_in, b_q_g, b_k_g, b_pe, b_w_c1, b_w_c2, b_w_out,
              c_w_in, c_b_if, c_h_g, c_w_out):
    xp, xs = x_prompt, x_sample
    gmlp_v_s, kv_p, win_p, kv_s, win_s = [], [], [], [], []
    C_p, n_p, m_p, C_s, n_s, m_s = [], [], [], [], [], []
    for layer in range(DEPTH):
        kind = layer % N_MIXERS
        j = layer // N_MIXERS
        hp = rms_norm(xp, norm_mix_g[layer])
        hs = rms_norm(xs, norm_mix_g[layer])
        if kind == 0:
            yp, _ = gmlp_mixer(hp, a_w_in[j], a_ln_g[j], a_ln_b[j], a_w_s[j], a_b_s[j], a_w_out[j])
            ys, v_rows = gmlp_mixer(hs, a_w_in[j], a_ln_g[j], a_ln_b[j], a_w_s[j], a_b_s[j], a_w_out[j])
            gmlp_v_s.append(v_rows)
        elif kind == 1:
            yp, kvr, winr = nsa_prompt(hp, b_w_in[j], b_q_g[j], b_k_g[j], b_pe[j], b_w_c1[j], b_w_c2[j], b_w_out[j])
            kv_p.append(kvr)
            win_p.append(winr)
            ys, kvr, winr = nsa_sample(hs, cache_nsa_kv[j], cache_nsa_win[j], page_table, b_w_in[j], b_q_g[j],
                                       b_k_g[j], b_pe[j], b_w_c1[j], b_w_c2[j], b_w_out[j])
            kv_s.append(kvr)
            win_s.append(winr)
        else:
            Bp = xp.shape[0]
            C0 = jnp.zeros((Bp, N_HEADS_C, DV_C, DK_C), jnp.float32)
            n0 = jnp.zeros((Bp, N_HEADS_C, DK_C), jnp.float32)
            m0 = jnp.zeros((Bp, N_HEADS_C), jnp.float32)
            yp, C, n, m = mlstm_mixer(hp, C0, n0, m0, c_w_in[j], c_b_if[j], c_h_g[j], c_w_out[j])
            C_p.append(C)
            n_p.append(n)
            m_p.append(m)
            ys, C, n, m = mlstm_mixer(hs, state_mlstm_C[j], state_mlstm_n[j], state_mlstm_m[j],
                                      c_w_in[j], c_b_if[j], c_h_g[j], c_w_out[j])
            C_s.append(C)
            n_s.append(n)
            m_s.append(m)
        xp = xp + yp.astype(xp.dtype)
        xs = xs + ys.astype(xs.dtype)
        xp = xp + squared_relu_mlp(rms_norm(xp, norm_ffn_g[layer]), ffn_w1[layer], ffn_w2[layer])
        xs = xs + squared_relu_mlp(rms_norm(xs, norm_ffn_g[layer]), ffn_w1[layer], ffn_w2[layer])
    return (xp, xs, jnp.stack(gmlp_v_s), jnp.stack(kv_p), jnp.stack(win_p), jnp.stack(kv_s), jnp.stack(win_s),
            jnp.stack(C_p), jnp.stack(n_p), jnp.stack(m_p), jnp.stack(C_s), jnp.stack(n_s), jnp.stack(m_s))
```

```python
import functools
import math

import jax
import jax.numpy as jnp
from jax import lax
from jax.experimental import pallas as pl
from jax.experimental.pallas import tpu as pltpu

F32 = jnp.float32
BF16 = jnp.bfloat16

EPS = 1e-6
CHUNK_A = 128
N_GROUPS_A = 8
N_HEADS_B = 16
N_KV_B = 4
REP_B = N_HEADS_B // N_KV_B
HD_B = 64
ROT_DIM = 16
ROPE_THETA = 500000.0
CMP_BLOCK = 32
SEL_BLOCK = 64
N_SEL = 16
WINDOW = 512
N_KV_SLOTS = 4
N_HEADS_C = 8
DK_C = 64
DV_C = 128
CHUNK_C = 128
SCALE_B = HD_B ** -0.5

LANE = 128
VMEM_LIMIT_BYTES = 56 * 1024 * 1024
MASK_BIG = 1e30
M_INIT = -1e20
N_SEL_PAD = 128
N_CMP_PAD = 2 * N_SEL_PAD


def _params(*sem):
    return pltpu.CompilerParams(dimension_semantics=sem, vmem_limit_bytes=VMEM_LIMIT_BYTES)


def _dot(a, b):
    return jnp.dot(a, b, preferred_element_type=F32)


def _dot_nt(a, b):
    return lax.dot_general(a, b, (((1,), (1,)), ((), ())), preferred_element_type=F32)


def _dot_tn(a, b):
    return lax.dot_general(a, b, (((0,), (0,)), ((), ())), preferred_element_type=F32)


def _rms(x, g):
    return x * lax.rsqrt(jnp.mean(x * x, axis=-1, keepdims=True) + EPS) * g


def _split3(x):
    a = x.astype(BF16)
    r = x - a.astype(F32)
    b = r.astype(BF16)
    c = (r - b.astype(F32)).astype(BF16)
    return a, b, c


def _const_spec(shape):
    n = len(shape)
    return pl.BlockSpec(shape, lambda *_: (0,) * n)


def _row_tile(m, pref):
    t = min(pref, m)
    while m % t:
        t //= 2
    return t


def _ffn_body(x_ref, g_ref, w1_ref, w2_ref, o_ref, *, ck):
    x = x_ref[...]
    xb = _rms(x, g_ref[...]).astype(BF16)
    acc = x
    for j in range(w1_ref.shape[1] // ck):
        h = jnp.maximum(_dot(xb, w1_ref[:, j * ck:(j + 1) * ck]), 0.0)
        acc = acc + _dot((h * h).astype(BF16), w2_ref[j * ck:(j + 1) * ck, :])
    o_ref[...] = acc


def _ffn(x, g, w1, w2):
    m, d = x.shape
    tm = _row_tile(m, 512)
    return pl.pallas_call(
        functools.partial(_ffn_body, ck=1024),
        out_shape=jax.ShapeDtypeStruct((m, d), F32),
        grid=(m // tm,),
        in_specs=[pl.BlockSpec((tm, d), lambda i: (i, 0)), _const_spec(g.shape),
                  _const_spec(w1.shape), _const_spec(w2.shape)],
        out_specs=pl.BlockSpec((tm, d), lambda i: (i, 0)),
        compiler_params=_params("parallel"),
        name="ffn",
    )(x, g, w1, w2)


def _gmlp_body(x_ref, g_ref, win_ref, lng_ref, lnb_ref, ws_ref, bs_ref, wout_ref, o_ref, v_ref, *, single):
    x = x_ref[...]
    dg = lng_ref.shape[1]
    xb = _rms(x, g_ref[...]).astype(BF16)
    u = jax.nn.gelu(_dot(xb, win_ref[:, :dg]))
    v = jax.nn.gelu(_dot(xb, win_ref[:, dg:]))
    mu = jnp.mean(v, axis=-1, keepdims=True)
    vc = v - mu
    var = jnp.mean(vc * vc, axis=-1, keepdims=True)
    v = vc * lax.rsqrt(var + EPS) * lng_ref[...] + lnb_ref[...]
    v_ref[...] = v
    if single:
        gate = v * ws_ref[...] + bs_ref[...]
    else:
        gw = dg // N_GROUPS_A
        row = lax.broadcasted_iota(jnp.int32, (CHUNK_A, CHUNK_A), 0)
        col = lax.broadcasted_iota(jnp.int32, (CHUNK_A, CHUNK_A), 1)
        causal = col <= row
        vb = v.astype(BF16)
        chunks = []
        for c in range(x.shape[0] // CHUNK_A):
            parts = []
            for gi in range(N_GROUPS_A):
                w = jnp.where(causal, ws_ref[gi], 0.0).astype(BF16)
                parts.append(_dot(w, vb[c * CHUNK_A:(c + 1) * CHUNK_A, gi * gw:(gi + 1) * gw]))
            chunks.append(jnp.concatenate(parts, axis=1) + bs_ref[...])
        gate = jnp.concatenate(chunks, axis=0)
    o_ref[...] = x + _dot((u * gate).astype(BF16), wout_ref[...])


def _gmlp(x, g, w_in, ln_g, ln_b, ws, bs, w_out, *, single):
    m, d = x.shape
    dg = w_out.shape[0]
    tm = _row_tile(m, 256)
    out, v = pl.pallas_call(
        functools.partial(_gmlp_body, single=single),
        out_shape=(jax.ShapeDtypeStruct((m, d), F32), jax.ShapeDtypeStruct((m, dg), F32)),
        grid=(m // tm,),
        in_specs=[pl.BlockSpec((tm, d), lambda i: (i, 0)), _const_spec(g.shape), _const_spec(w_in.shape),
                  _const_spec(ln_g.shape), _const_spec(ln_b.shape), _const_spec(ws.shape), _const_spec(bs.shape),
                  _const_spec(w_out.shape)],
        out_specs=(pl.BlockSpec((tm, d), lambda i: (i, 0)), pl.BlockSpec((tm, dg), lambda i: (i, 0))),
        compiler_params=_params("parallel"),
        name="gmlp_single" if single else "gmlp",
    )(x, g, w_in, ln_g, ln_b, ws, bs, w_out)
    return out, v


def _rope_tables(pos, seg):
    half = ROT_DIM // 2
    freq = jnp.power(ROPE_THETA, -jnp.arange(half, dtype=F32) * 2.0 / ROT_DIM)
    ang = pos.astype(F32)[:, None] * freq[None, :]
    cos, sin = jnp.cos(ang), jnp.sin(ang)
    t = pos.shape[0]
    one = jnp.ones((t, seg - ROT_DIM), F32)
    zero = jnp.zeros((t, seg - ROT_DIM), F32)
    z8 = jnp.zeros((t, half), F32)
    tabs = [jnp.concatenate([cos, cos, one], 1), jnp.concatenate([-sin, z8, zero], 1),
            jnp.concatenate([z8, sin, zero], 1)]
    return jnp.stack([jnp.tile(a, (1, LANE // seg)) for a in tabs])


def _rope128(x, tab):
    return x * tab[0] + pltpu.roll(x, LANE - ROT_DIM // 2, 1) * tab[1] + pltpu.roll(x, ROT_DIM // 2, 1) * tab[2]


def _nsa_proj_body(x_ref, g_ref, wq_ref, wg_ref, wkv_ref, seg_ref, spread_ref, qg_ref, kg_ref, tq_ref, tk_ref,
                   qcat_ref, gates_ref, kv_ref, win_ref, ks_ref, vs_ref, kw_ref, vw_ref):
    x = x_ref[...]
    xb = _rms(x, g_ref[...]).astype(BF16)
    tq = tq_ref[...]
    tk = tk_ref[...]
    qg = qg_ref[...]
    for h in range(N_HEADS_B):
        q = _dot(xb, wq_ref[:, h * LANE:(h + 1) * LANE])
        ms = jnp.sum(q * q, axis=-1, keepdims=True) * (1.0 / LANE)
        qn = q * lax.rsqrt(ms + EPS) * qg
        qcat_ref[:, h * LANE:(h + 1) * LANE] = (_rope128(qn, tq) * SCALE_B).astype(BF16)
    gates_ref[...] = jax.nn.sigmoid(_dot(xb, wg_ref[...]))
    kv = _dot(xb, wkv_ref[...])
    w = N_KV_B * HD_B
    seg = seg_ref[...]
    spread = spread_ref[...]

    def head_norm(k, gain):
        k2 = k * k
        hi = k2.astype(BF16)
        lo = (k2 - hi.astype(F32)).astype(BF16)
        ss = _dot(hi, seg) + _dot(lo, seg)
        return k * lax.rsqrt(ss * (1.0 / HD_B) + EPS) * gain

    def rope(k):
        return jnp.concatenate([_rope128(k[:, j * LANE:(j + 1) * LANE], tk) for j in range(w // LANE)], axis=1)

    ks = rope(head_norm(kv[:, 2 * w:3 * w], kg_ref[0:1, :]))
    kw = rope(head_norm(kv[:, 4 * w:5 * w], kg_ref[1:2, :]))
    vs = kv[:, 3 * w:4 * w]
    vw = kv[:, 5 * w:6 * w]
    kv_ref[:, :2 * w] = kv[:, :2 * w]
    kv_ref[:, 2 * w:3 * w] = ks
    kv_ref[:, 3 * w:] = vs
    win_ref[:, :w] = kw
    win_ref[:, w:] = vw
    lane = lax.broadcasted_iota(jnp.int32, (1, N_KV_B * LANE), 1)
    ones_hi = ((lane & HD_B) != 0).astype(F32)
    ks_ref[...] = _dot(ks.astype(BF16), spread).astype(BF16)
    kw_ref[...] = _dot(kw.astype(BF16), spread).astype(BF16)
    vs_ref[...] = (_dot(vs.astype(BF16), spread) + ones_hi).astype(BF16)
    vw_ref[...] = (_dot(vw.astype(BF16), spread) + ones_hi).astype(BF16)


def _nsa_proj(x, g, wts, tab_q, tab_k, n_tab_tiles, tm):
    m, d = x.shape
    w = N_KV_B * HD_B
    ws = N_KV_B * LANE
    tile = lambda n: pl.BlockSpec((tm, n), lambda i: (i, 0))
    tab = pl.BlockSpec((3, tm, LANE), lambda i: (0, i % n_tab_tiles, 0))
    consts = [g, wts["wq"], wts["wg"], wts["wkv"], wts["seg"], wts["spread"], wts["qg"], wts["kg"]]
    return pl.pallas_call(
        _nsa_proj_body,
        out_shape=(jax.ShapeDtypeStruct((m, N_HEADS_B * LANE), BF16), jax.ShapeDtypeStruct((m, ws), F32),
                   jax.ShapeDtypeStruct((m, 4 * w), F32), jax.ShapeDtypeStruct((m, 2 * w), F32),
                   jax.ShapeDtypeStruct((m, ws), BF16), jax.ShapeDtypeStruct((m, ws), BF16),
                   jax.ShapeDtypeStruct((m, ws), BF16), jax.ShapeDtypeStruct((m, ws), BF16)),
        grid=(m // tm,),
        in_specs=[tile(d)] + [_const_spec(c.shape) for c in consts] + [tab, tab],
        out_specs=(tile(N_HEADS_B * LANE), tile(ws), tile(4 * w), tile(2 * w), tile(ws), tile(ws), tile(ws), tile(ws)),
        compiler_params=_params("parallel"),
        name="nsa_proj",
    )(x, *consts, tab_q, tab_k)


def _compress_body(x_ref, pe_ref, w1_ref, w2_ref, g_ref, o_ref, *, norm):
    xb = (x_ref[...] + pe_ref[...]).astype(BF16)
    hid = jax.nn.gelu(_dot(xb, w1_ref[...]))
    y = _dot(hid.astype(BF16), w2_ref[...])
    if norm:
        y = _rms(y, g_ref[...])
    o_ref[...] = y


def _compress(x, pe, w1, w2, g, *, norm):
    r, kdim = x.shape
    tr = _row_tile(r, 512)
    return pl.pallas_call(
        functools.partial(_compress_body, norm=norm),
        out_shape=jax.ShapeDtypeStruct((r, HD_B), F32),
        grid=(r // tr,),
        in_specs=[pl.BlockSpec((tr, kdim), lambda i: (i, 0)), _const_spec(pe.shape), _const_spec(w1.shape),
                  _const_spec(w2.shape), _const_spec(g.shape)],
        out_specs=pl.BlockSpec((tr, HD_B), lambda i: (i, 0)),
        compiler_params=_params("parallel"),
        name="nsa_compress_k" if norm else "nsa_compress_v",
    )(x, pe, w1, w2, g)


def _masked_softmax(s, mask, axis):
    sm = jnp.where(mask, s, -jnp.inf)
    mx = jnp.max(sm, axis=axis, keepdims=True)
    mx = jnp.where(mx > -jnp.inf, mx, 0.0)
    e = jnp.where(mask, jnp.exp(s - mx), 0.0)
    return e / jnp.maximum(jnp.sum(e, axis=axis, keepdims=True), 1e-30)


def _cmp_select_body(q_ref, kc_ref, kct_ref, vc_ref, gates_ref, o_ref, mnot_ref, *, tq):
    q0 = pl.program_id(2) * tq
    kc = kc_ref[...]
    kct = kct_ref[...]
    vc = vc_ref[...]
    gates = gates_ref[...]
    t_r = q0 + lax.broadcasted_iota(jnp.int32, (tq, N_CMP_PAD), 0)
    blk_r = lax.broadcasted_iota(jnp.int32, (tq, N_CMP_PAD), 1)
    mask_r = (blk_r + 1) * CMP_BLOCK - 1 <= t_r
    row_c = lax.broadcasted_iota(jnp.int32, (N_CMP_PAD, tq), 0)
    t_c = q0 + lax.broadcasted_iota(jnp.int32, (N_CMP_PAD, tq), 1)
    blk_c = jnp.where(row_c < N_SEL_PAD, 2 * row_c, 2 * (row_c - N_SEL_PAD) + 1)
    mask_c = (blk_c + 1) * CMP_BLOCK - 1 <= t_c
    imp = jnp.zeros((N_SEL_PAD, tq), F32)
    for r in range(REP_B):
        qh = q_ref[:, r * LANE:(r + 1) * LANE]
        p = _masked_softmax(_dot_nt(qh, kc), mask_r, 1)
        o = _dot(p.astype(BF16), vc)
        o_ref[:, r * LANE:(r + 1) * LANE] = o * gates[:, 3 * r:3 * r + 1]
        pt = _masked_softmax(_dot_nt(kct, qh), mask_c, 0)
        imp = imp + pt[:N_SEL_PAD] + pt[N_SEL_PAD:]
    blk = lax.broadcasted_iota(jnp.int32, (N_SEL_PAD, tq), 0)
    t_s = q0 + lax.broadcasted_iota(jnp.int32, (N_SEL_PAD, tq), 1)
    cur = t_s // SEL_BLOCK
    forced = (blk == 0) | (blk == cur) | (blk == cur - 1)
    valid = blk * SEL_BLOCK <= t_s
    score = jnp.where(forced, jnp.inf, jnp.where(valid, imp, -jnp.inf))
    blk_f = blk.astype(F32)
    sel = jnp.zeros((N_SEL_PAD, tq), F32)
    for _ in range(N_SEL):
        mx = jnp.max(score, axis=0, keepdims=True)
        first = jnp.min(jnp.where(score == mx, blk_f, float(N_SEL_PAD)), axis=0, keepdims=True)
        hit = blk_f == first
        sel = jnp.where(hit & (mx > -jnp.inf), 1.0, sel)
        score = jnp.where(hit, -jnp.inf, score)
    mnot_ref[...] = (1.0 - sel).T.astype(BF16)


def _cmp_select(qcat, kc, kct, vc, gates, b, t, tq):
    m = b * t
    nq = t // tq
    blk = pl.BlockSpec((None, None, N_CMP_PAD, LANE), lambda bi, g, i: (bi, g, 0, 0))
    return pl.pallas_call(
        functools.partial(_cmp_select_body, tq=tq),
        out_shape=(jax.ShapeDtypeStruct((m, N_HEADS_B * LANE), F32), jax.ShapeDtypeStruct((m, N_KV_B * LANE), BF16)),
        grid=(b, N_KV_B, nq),
        in_specs=[pl.BlockSpec((tq, REP_B * LANE), lambda bi, g, i: (bi * nq + i, g)), blk, blk, blk,
                  pl.BlockSpec((tq, LANE), lambda bi, g, i: (bi * nq + i, g))],
        out_specs=(pl.BlockSpec((tq, REP_B * LANE), lambda bi, g, i: (bi * nq + i, g)),
                   pl.BlockSpec((tq, LANE), lambda bi, g, i: (bi * nq + i, g))),
        compiler_params=_params("parallel", "parallel", "parallel"),
        name="nsa_cmp_select",
    )(qcat, kc, kct, vc, gates)


def _flash_body(tab_ref, q_ref, mnot_ref, k_ref, v_ref, gates_ref, o_ref, qs_ref, ks_ref, m_ref, acc_ref, *,
                tq, tk, sel, gate_col):
    step_id = pl.program_id(2)
    i = tab_ref[0, step_id]
    j = tab_ref[1, step_id]
    rows = REP_B * tq
    first = j == (0 if sel else jnp.maximum(i - WINDOW // tk, 0))

    @pl.when(first)
    def _():
        m_ref[...] = jnp.full(m_ref.shape, M_INIT, F32)
        acc_ref[...] = jnp.zeros(acc_ref.shape, F32)
        for r in range(REP_B):
            if sel:
                qs_ref[r * tq:(r + 1) * tq, :LANE] = mnot_ref[...]
                qs_ref[r * tq:(r + 1) * tq, LANE:] = q_ref[:, r * LANE:(r + 1) * LANE]
            else:
                qs_ref[r * tq:(r + 1) * tq, :] = q_ref[:, r * LANE:(r + 1) * LANE]

    def step(masked):
        if sel:
            kpos = j * tk + lax.broadcasted_iota(jnp.int32, (tk, LANE), 0)
            lane = lax.broadcasted_iota(jnp.int32, (tk, LANE), 1)
            ks_ref[:, :LANE] = jnp.where(kpos // SEL_BLOCK == lane, -MASK_BIG, 0.0).astype(BF16)
            ks_ref[:, LANE:] = k_ref[...]
            s = _dot_nt(qs_ref[...], ks_ref[...])
        else:
            s = _dot_nt(qs_ref[...], k_ref[...])
        if masked is not None:
            t = i * tq + lax.broadcasted_iota(jnp.int32, (rows, tk), 0) % tq
            key = j * tk + lax.broadcasted_iota(jnp.int32, (rows, tk), 1)
            ok = key <= t if masked == "causal" else t - key <= WINDOW
            s = jnp.where(ok, s, -MASK_BIG)
        m_old = m_ref[...]
        m_new = jnp.maximum(m_old, jnp.max(s, axis=-1, keepdims=True))
        p = jnp.exp(s - m_new[:, :1])
        acc_ref[...] = jnp.exp(m_old - m_new) * acc_ref[...] + _dot(p.astype(BF16), v_ref[...])
        m_ref[...] = m_new

    if sel:
        pl.when(j < i)(lambda: step(None))
    else:
        pl.when(j == i - WINDOW // tk)(lambda: step("band"))
        pl.when((j < i) & (j > i - WINDOW // tk))(lambda: step(None))

    @pl.when(j == i)
    def _():
        step("causal")
        gates = gates_ref[...]
        lane = lax.broadcasted_iota(jnp.int32, (tq, LANE), 1)
        for r in range(REP_B):
            a = acc_ref[r * tq:(r + 1) * tq, :]
            o = a / a[:, HD_B:HD_B + 1]
            g = gates[:, 3 * r + gate_col:3 * r + gate_col + 1]
            o_ref[:, r * LANE:(r + 1) * LANE] = jnp.where(lane < HD_B, o * g, 0.0)


def _flash(qcat, mnot, k, v, gates, b, t, tq, sel):
    m = b * t
    nq = t // tq
    tk = tq
    assert WINDOW % tk == 0
    lo = (lambda i: 0) if sel else (lambda i: max(i - WINDOW // tk, 0))
    pairs = [(i, j) for i in range(nq) for j in range(lo(i), i + 1)]
    tab = jnp.asarray(pairs, jnp.int32).T
    qidx = lambda bi, g, p, tab: (bi * nq + tab[0, p], g)
    kidx = lambda bi, g, p, tab: (bi * nq + tab[1, p], g)
    kdim = 2 * LANE if sel else LANE
    return pl.pallas_call(
        functools.partial(_flash_body, tq=tq, tk=tk, sel=sel, gate_col=1 if sel else 2),
        out_shape=jax.ShapeDtypeStruct((m, N_HEADS_B * LANE), F32),
        grid_spec=pltpu.PrefetchScalarGridSpec(
            num_scalar_prefetch=1,
            grid=(b, N_KV_B, len(pairs)),
            in_specs=[pl.BlockSpec((tq, REP_B * LANE), qidx), pl.BlockSpec((tq, LANE), qidx),
                      pl.BlockSpec((tk, LANE), kidx), pl.BlockSpec((tk, LANE), kidx),
                      pl.BlockSpec((tq, LANE), qidx)],
            out_specs=pl.BlockSpec((tq, REP_B * LANE), qidx),
            scratch_shapes=[pltpu.VMEM((REP_B * tq, kdim), BF16), pltpu.VMEM((tk, 2 * LANE), BF16),
                            pltpu.VMEM((REP_B * tq, LANE), F32), pltpu.VMEM((REP_B * tq, LANE), F32)]),
        compiler_params=_params("parallel", "parallel", "arbitrary"),
        name="nsa_flash_sel" if sel else "nsa_flash_win",
    )(tab, qcat, mnot, k, v, gates)


def _sum_proj_body(a_ref, b_ref, c_ref, x_ref, w_ref, o_ref):
    o = (a_ref[...] + b_ref[...] + c_ref[...]).astype(BF16)
    o_ref[...] = x_ref[...] + _dot(o, w_ref[...])


def _sum_proj(a, b, c, x, w):
    m, d = x.shape
    kdim = a.shape[1]
    tm = _row_tile(m, 512)
    big = pl.BlockSpec((tm, kdim), lambda i: (i, 0))
    row = pl.BlockSpec((tm, d), lambda i: (i, 0))
    return pl.pallas_call(
        _sum_proj_body,
        out_shape=jax.ShapeDtypeStruct((m, d), F32),
        grid=(m // tm,),
        in_specs=[big, big, big, row, _const_spec(w.shape)],
        out_specs=row,
        compiler_params=_params("parallel"),
        name="nsa_out_proj",
    )(a, b, c, x, w)


def _nsa_weights(w_in, q_g, k_g, w_out):
    d = w_in.shape[0]
    nq = N_HEADS_B * HD_B
    w = N_KV_B * HD_B
    wq = w_in[:, :nq].reshape(d, N_HEADS_B, 1, HD_B)
    wq = jnp.broadcast_to(wq, (d, N_HEADS_B, 2, HD_B)).reshape(d, N_HEADS_B * LANE)
    wg = w_in[:, nq:nq + 3 * N_HEADS_B].reshape(d, N_KV_B, REP_B * 3)
    wg = jnp.pad(wg, ((0, 0), (0, 0), (0, LANE - REP_B * 3))).reshape(d, N_KV_B * LANE)
    wkv = w_in[:, nq + 3 * N_HEADS_B:]
    lane = jnp.arange(w)
    seg = (lane[:, None] // HD_B == lane[None, :] // HD_B).astype(BF16)
    spread = (lane[:, None] // HD_B * LANE + lane[:, None] % HD_B == jnp.arange(N_KV_B * LANE)[None, :]).astype(BF16)
    wo = jnp.pad(w_out.reshape(N_HEADS_B, HD_B, -1), ((0, 0), (0, LANE - HD_B), (0, 0)))
    return {
        "wq": wq.astype(BF16), "wg": wg.astype(BF16), "wkv": wkv.astype(BF16), "seg": seg, "spread": spread,
        "qg": jnp.tile(q_g, 2)[None, :], "kg": jnp.stack([jnp.tile(k_g[1], N_KV_B), jnp.tile(k_g[2], N_KV_B)]),
        "wo": wo.reshape(N_HEADS_B * LANE, -1).astype(BF16),
    }


def _cmp_inputs(rows, b, length):
    nb = length // CMP_BLOCK
    x = rows.reshape(b, length, N_KV_B, HD_B)[:, :nb * CMP_BLOCK].reshape(b, nb, CMP_BLOCK, N_KV_B, HD_B)
    return x.transpose(0, 1, 3, 2, 4).reshape(b * nb * N_KV_B, CMP_BLOCK * HD_B)


def _cmp_blocks(kc_rows, vc_rows, b, length, pe, w_c1, w_c2, kc_g):
    nb = length // CMP_BLOCK
    pe_k, pe_v = pe[0].reshape(1, -1), pe[1].reshape(1, -1)
    w1k = w_c1[0].reshape(CMP_BLOCK * HD_B, HD_B).astype(BF16)
    w1v = w_c1[1].reshape(CMP_BLOCK * HD_B, HD_B).astype(BF16)
    g = kc_g[None, :]
    kc = _compress(_cmp_inputs(kc_rows, b, length), pe_k, w1k, w_c2[0].astype(BF16), g, norm=True)
    vc = _compress(_cmp_inputs(vc_rows, b, length), pe_v, w1v, w_c2[1].astype(BF16), g, norm=False)
    return kc.reshape(b, nb, N_KV_B, HD_B), vc.reshape(b, nb, N_KV_B, HD_B)


def _nsa_prompt(x, g_mix, b, t, w_in, q_g, k_g, pe, w_c1, w_c2, w_out):
    wts = _nsa_weights(w_in, q_g, k_g, w_out)
    w = N_KV_B * HD_B
    tm = 256
    pos = jnp.arange(t)
    qcat, gates, kv_rows, win_rows, ks_s, vs_s, kw_s, vw_s = _nsa_proj(
        x, g_mix, wts, _rope_tables(pos, LANE), _rope_tables(pos, HD_B), t // tm, tm)
    kc_blk, vc_blk = _cmp_blocks(kv_rows[:, :w], kv_rows[:, w:2 * w], b, t, pe, w_c1, w_c2, k_g[0])
    nb = t // CMP_BLOCK
    assert nb <= N_CMP_PAD and t % SEL_BLOCK == 0

    def blocks(a, lo):
        a = jnp.pad(a.transpose(0, 2, 1, 3), ((0, 0), (0, 0), (0, N_CMP_PAD - nb), (lo, LANE - HD_B - lo)))
        return a.astype(BF16)

    kc = blocks(kc_blk, HD_B)
    kct = jnp.concatenate([kc[:, :, 0::2], kc[:, :, 1::2]], axis=2)
    vc = blocks(vc_blk, 0)
    tq = 256
    o_cmp, mnot = _cmp_select(qcat, kc, kct, vc, gates, b, t, tq)
    o_sel = _flash(qcat, mnot, ks_s, vs_s, gates, b, t, tq, True)
    o_win = _flash(qcat, mnot, kw_s, vw_s, gates, b, t, tq, False)
    y = _sum_proj(o_cmp, o_sel, o_win, x, wts["wo"])
    wb = min(WINDOW, t)
    kv_out = kv_rows.reshape(b, t, N_KV_SLOTS, N_KV_B, HD_B)
    win_out = win_rows.reshape(b, t, 2, N_KV_B, HD_B)[:, t - wb:]
    return y, kv_out, win_out


def _gmlp_layer(x, g, w_in, ln_g, ln_b, w_s, b_s, w_out, *, single):
    gw = w_out.shape[0] // N_GROUPS_A
    if single:
        ws = jnp.repeat(w_s[:, 0, 0], gw)[None, :]
        bs = jnp.repeat(b_s[:, 0], gw)[None, :]
    else:
        ws = w_s
        bs = jnp.repeat(b_s.T, gw, axis=1)
    return _gmlp(x, g[None, :], w_in.astype(BF16), ln_g[None, :], ln_b[None, :], ws, bs, w_out.astype(BF16),
                 single=single)


def _mlstm_proj_body(x_ref, g_ref, wq_ref, wk_ref, wv_ref, wgi_ref, wo_ref, bif_ref,
                     q_ref, k_ref, v_ref, gi_ref, og_ref):
    xb = _rms(x_ref[...], g_ref[...]).astype(BF16)
    q_ref[...] = _dot(xb, wq_ref[...]).astype(BF16)
    k_ref[...] = _dot(xb, wk_ref[...]).astype(BF16)
    v_ref[...] = _dot(xb, wv_ref[...]).astype(BF16)
    gi_ref[...] = _dot(xb, wgi_ref[...]) + bif_ref[...]
    og_ref[...] = jax.nn.sigmoid(_dot(xb, wo_ref[...]))


def _mlstm_proj(x, g, wts):
    m, d = x.shape
    hv = N_HEADS_C * DV_C
    tm = _row_tile(m, 512)
    consts = [g, wts["wq"], wts["wk"], wts["wv"], wts["wgi"], wts["wo"], wts["bif"]]
    tile = lambda n: pl.BlockSpec((tm, n), lambda i: (i, 0))
    return pl.pallas_call(
        _mlstm_proj_body,
        out_shape=(jax.ShapeDtypeStruct((m, N_HEADS_C * LANE), BF16), jax.ShapeDtypeStruct((m, N_HEADS_C * LANE), BF16),
                   jax.ShapeDtypeStruct((m, hv), BF16), jax.ShapeDtypeStruct((m, LANE), F32),
                   jax.ShapeDtypeStruct((m, hv), F32)),
        grid=(m // tm,),
        in_specs=[tile(d)] + [_const_spec(c.shape) for c in consts],
        out_specs=(tile(N_HEADS_C * LANE), tile(N_HEADS_C * LANE), tile(hv), tile(LANE), tile(hv)),
        compiler_params=_params("parallel"),
        name="mlstm_proj",
    )(x, *consts)


def _mlstm_scan_body(q_ref, k_ref, v_ref, gi_ref, git_ref, hs_ref, c_out, n_out, m_out, c_s, n_s, m_s):
    c = pl.program_id(1)
    L = q_ref.shape[0]

    @pl.when(c == 0)
    def _():
        c_s[...] = jnp.zeros(c_s.shape, F32)
        n_s[...] = jnp.zeros(n_s.shape, F32)
        m_s[...] = jnp.zeros(m_s.shape, F32)

    row = lax.broadcasted_iota(jnp.int32, (L, L), 0)
    col = lax.broadcasted_iota(jnp.int32, (L, L), 1)
    causal = col <= row
    tril = causal.astype(BF16)
    gi = gi_ref[...]
    git = git_ref[...]
    fcol = jax.nn.log_sigmoid(gi)
    frow = jax.nn.log_sigmoid(git[N_HEADS_C:, :])
    bcol_all = sum(_dot(tril, part) for part in _split3(fcol))
    brow_all = sum(_dot_nt(part, tril) for part in _split3(frow))
    for h in range(N_HEADS_C):
        sl = slice(h * LANE, (h + 1) * LANE)
        q = q_ref[:, sl]
        k = k_ref[:, sl]
        v = v_ref[:, sl]
        bcol = bcol_all[:, N_HEADS_C + h:N_HEADS_C + h + 1]
        icol = gi[:, h:h + 1]
        brow = brow_all[h:h + 1, :]
        irow = git[h:h + 1, :]
        m_prev = m_s[h:h + 1, 0:1]
        dlog = jnp.where(causal, bcol - brow + irow, -jnp.inf)
        inter = bcol + m_prev
        m_t = jnp.maximum(inter, jnp.max(dlog, axis=1, keepdims=True))
        w = jnp.exp(dlog - m_t)
        a = jnp.exp(inter - m_t)
        s = _dot_nt(q, k) * w
        cq = _dot_nt(q, c_s[h].astype(BF16))
        num = a * cq + _dot(s.astype(BF16), v)
        nq = jnp.sum(q.astype(F32) * n_s[h:h + 1, :], axis=1, keepdims=True)
        den = a * nq + jnp.sum(s, axis=1, keepdims=True)
        hs_ref[:, sl] = num / jnp.maximum(jnp.abs(den), jnp.exp(-m_t))
        b_end = bcol[L - 1:L, :]
        wlog = b_end - bcol + icol
        m_new = jnp.maximum(b_end + m_prev, jnp.max(wlog, axis=0, keepdims=True))
        wk = jnp.exp(wlog - m_new)
        decay = jnp.exp(b_end + m_prev - m_new)
        vw = (v.astype(F32) * wk).astype(BF16)
        c_s[h] = decay * c_s[h] + _dot_tn(vw, k)
        n_s[h:h + 1, :] = decay * n_s[h:h + 1, :] + jnp.sum(k.astype(F32) * wk, axis=0, keepdims=True)
        m_s[h:h + 1, :] = jnp.broadcast_to(m_new, (1, LANE))

    @pl.when(c == pl.num_programs(1) - 1)
    def _():
        c_out[...] = c_s[...]
        n_out[...] = n_s[...]
        m_out[...] = m_s[...]


def _mlstm_scan(q, k, v, gi, git, b, t):
    L = math.gcd(t, CHUNK_C)
    nc = t // L
    hv = N_HEADS_C * DV_C
    tile = lambda n: pl.BlockSpec((L, n), lambda bi, c: (bi * nc + c, 0))
    return pl.pallas_call(
        _mlstm_scan_body,
        out_shape=(jax.ShapeDtypeStruct((b * t, hv), F32),
                   jax.ShapeDtypeStruct((b, N_HEADS_C, DV_C, LANE), F32),
                   jax.ShapeDtypeStruct((b, N_HEADS_C, LANE), F32), jax.ShapeDtypeStruct((b, N_HEADS_C, LANE), F32)),
        grid=(b, nc),
        in_specs=[tile(N_HEADS_C * LANE), tile(N_HEADS_C * LANE), tile(hv), tile(LANE),
                  pl.BlockSpec((None, 2 * N_HEADS_C, L), lambda bi, c: (bi, 0, c))],
        out_specs=(tile(hv), pl.BlockSpec((None, N_HEADS_C, DV_C, LANE), lambda bi, c: (bi, 0, 0, 0)),
                   pl.BlockSpec((None, N_HEADS_C, LANE), lambda bi, c: (bi, 0, 0)),
                   pl.BlockSpec((None, N_HEADS_C, LANE), lambda bi, c: (bi, 0, 0))),
        scratch_shapes=[pltpu.VMEM((N_HEADS_C, DV_C, LANE), F32), pltpu.VMEM((N_HEADS_C, LANE), F32),
                        pltpu.VMEM((N_HEADS_C, LANE), F32)],
        compiler_params=_params("parallel", "arbitrary"),
        name="mlstm_scan",
    )(q, k, v, gi, git)


def _mlstm_out_body(hs_ref, og_ref, hg_ref, x_ref, w_ref, o_ref):
    parts = []
    for h in range(N_HEADS_C):
        sl = slice(h * DV_C, (h + 1) * DV_C)
        parts.append((og_ref[:, sl] * _rms(hs_ref[:, sl], hg_ref[:, sl])).astype(BF16))
    o_ref[...] = x_ref[...] + _dot(jnp.concatenate(parts, axis=1), w_ref[...])


def _mlstm_out(hs, og, hg, x, w):
    m, d = x.shape
    hv = hs.shape[1]
    tm = _row_tile(m, 512)
    wide = pl.BlockSpec((tm, hv), lambda i: (i, 0))
    row = pl.BlockSpec((tm, d), lambda i: (i, 0))
    return pl.pallas_call(
        _mlstm_out_body,
        out_shape=jax.ShapeDtypeStruct((m, d), F32),
        grid=(m // tm,),
        in_specs=[wide, wide, _const_spec(hg.shape), row, _const_spec(w.shape)],
        out_specs=row,
        compiler_params=_params("parallel"),
        name="mlstm_out",
    )(hs, og, hg, x, w)


def _mlstm_weights(w_in, b_if):
    d = w_in.shape[0]
    hk, hv = N_HEADS_C * DK_C, N_HEADS_C * DV_C

    def spread(w):
        w = w.reshape(d, N_HEADS_C, DK_C)
        return jnp.pad(w, ((0, 0), (0, 0), (0, LANE - DK_C))).reshape(d, N_HEADS_C * LANE)

    wgi = jnp.pad(w_in[:, 2 * hk + hv:2 * hk + hv + 2 * N_HEADS_C], ((0, 0), (0, LANE - 2 * N_HEADS_C)))
    return {
        "wq": spread(w_in[:, :hk]).astype(BF16),
        "wk": (spread(w_in[:, hk:2 * hk]) * (DK_C ** -0.5)).astype(BF16),
        "wv": w_in[:, 2 * hk:2 * hk + hv].astype(BF16),
        "wgi": wgi.astype(BF16),
        "wo": w_in[:, 2 * hk + hv + 2 * N_HEADS_C:].astype(BF16),
        "bif": jnp.pad(b_if, (0, LANE - 2 * N_HEADS_C))[None, :],
    }


def _mlstm_prompt(x, g_mix, b, t, w_in, b_if, h_g, w_out):
    wts = _mlstm_weights(w_in, b_if)
    q, k, v, gi, og = _mlstm_proj(x, g_mix, wts)
    git = gi[:, :2 * N_HEADS_C].reshape(b, t, 2 * N_HEADS_C).transpose(0, 2, 1)
    hs, c, n, m = _mlstm_scan(q, k, v, gi, git, b, t)
    y = _mlstm_out(hs, og, h_g[None, :], x, w_out.astype(BF16))
    return y, c[..., :DK_C], n[..., :DK_C], m[..., 0]


def _mlstm_sample(x, g_mix, c0, n0, m0, w_in, b_if, h_g, w_out):
    bsz = x.shape[0]
    wts = _mlstm_weights(w_in, b_if)
    q, k, v, gi, og = _mlstm_proj(x, g_mix, wts)
    q = q.astype(F32).reshape(bsz, N_HEADS_C, LANE)[..., :DK_C]
    k = k.astype(F32).reshape(bsz, N_HEADS_C, LANE)[..., :DK_C]
    v = v.astype(F32).reshape(bsz, N_HEADS_C, DV_C)
    it = gi[:, :N_HEADS_C]
    b = jax.nn.log_sigmoid(gi[:, N_HEADS_C:2 * N_HEADS_C])
    inter = b + m0
    m_t = jnp.maximum(inter, it)
    w = jnp.exp(it - m_t)
    a = jnp.exp(inter - m_t)
    s = jnp.sum(q * k, axis=-1) * w
    num = a[..., None] * jnp.einsum('bhvd,bhd->bhv', c0, q) + s[..., None] * v
    den = a * jnp.sum(n0 * q, axis=-1) + s
    hs = num / jnp.maximum(jnp.abs(den), jnp.exp(-m_t))[..., None]
    c = a[..., None, None] * c0 + (w[..., None] * v)[..., :, None] * k[..., None, :]
    n = a[..., None] * n0 + w[..., None] * k
    y = _mlstm_out(hs.reshape(bsz, N_HEADS_C * DV_C), og, h_g[None, :], x, w_out.astype(BF16))
    return y, c, n, m_t


def _proj_add_body(o_ref, x_ref, w_ref, out_ref):
    out_ref[...] = x_ref[...] + _dot(o_ref[...].astype(BF16), w_ref[...])


def _proj_add(o, x, w):
    m, d = x.shape
    tm = _row_tile(m, 512)
    return pl.pallas_call(
        _proj_add_body,
        out_shape=jax.ShapeDtypeStruct((m, d), F32),
        grid=(m // tm,),
        in_specs=[pl.BlockSpec((tm, o.shape[1]), lambda i: (i, 0)), pl.BlockSpec((tm, d), lambda i: (i, 0)),
                  _const_spec(w.shape)],
        out_specs=pl.BlockSpec((tm, d), lambda i: (i, 0)),
        compiler_params=_params("parallel"),
        name="proj_add",
    )(o, x, w)


def _softmax_masked_jax(s, mask):
    s = jnp.where(mask, s, -jnp.inf)
    m = jnp.max(s, axis=-1, keepdims=True)
    m = jnp.where(jnp.isfinite(m), m, 0.0)
    e = jnp.where(mask, jnp.exp(s - m), 0.0)
    return e / jnp.maximum(jnp.sum(e, axis=-1, keepdims=True), 1e-30)


def _sel_blocks_jax(x):
    b, length = x.shape[:2]
    nsb = -(-length // SEL_BLOCK)
    xp = jnp.pad(x, ((0, 0), (0, nsb * SEL_BLOCK - length), (0, 0), (0, 0)))
    return xp.reshape(b, nsb, SEL_BLOCK, N_KV_B, HD_B).transpose(0, 3, 1, 2, 4)


def _nsa_core_jax(q_rope, q_nope, gates, pos_q, kc_blk, vc_blk, ks_blk, vs_blk, kw, vw, pos_w):
    b, tq = q_rope.shape[:2]
    qn = q_nope.reshape(b, tq, N_KV_B, REP_B, HD_B)
    qr = q_rope.reshape(b, tq, N_KV_B, REP_B, HD_B)
    t5 = pos_q[None, :, None, None, None]
    nb = kc_blk.shape[1]
    blk_end = (jnp.arange(nb) + 1) * CMP_BLOCK - 1
    s = jnp.einsum('btgrd,bngd->btgrn', qn, kc_blk).astype(F32) * SCALE_B
    p_cmp = _softmax_masked_jax(s, blk_end <= t5)
    o_cmp = jnp.einsum('btgrn,bngd->btgrd', p_cmp, vc_blk)
    nsb = ks_blk.shape[2]
    ratio = SEL_BLOCK // CMP_BLOCK
    imp = jnp.pad(jnp.sum(p_cmp, axis=3), ((0, 0), (0, 0), (0, 0), (0, nsb * ratio - nb)))
    imp = imp.reshape(b, tq, N_KV_B, nsb, ratio).sum(-1)
    blk = jnp.arange(nsb)
    t4 = pos_q[None, :, None, None]
    cur = t4 // SEL_BLOCK
    forced = (blk == 0) | (blk == cur) | (blk == cur - 1)
    valid = blk * SEL_BLOCK <= t4
    score = jnp.where(forced, jnp.inf, jnp.where(valid, imp, -jnp.inf))
    _, idx = lax.top_k(score, min(N_SEL, nsb))
    n_sel = idx.shape[-1]
    idx = idx.transpose(0, 2, 1, 3)
    bi = jnp.arange(b)[:, None, None, None]
    gi = jnp.arange(N_KV_B)[None, :, None, None]
    k_sel = ks_blk[bi, gi, idx].reshape(b, N_KV_B, tq, n_sel * SEL_BLOCK, HD_B)
    v_sel = vs_blk[bi, gi, idx].reshape(b, N_KV_B, tq, n_sel * SEL_BLOCK, HD_B)
    pos_sel = (idx[..., None] * SEL_BLOCK + jnp.arange(SEL_BLOCK)).reshape(b, N_KV_B, tq, n_sel * SEL_BLOCK)
    mask_sel = pos_sel.transpose(0, 2, 1, 3)[:, :, :, None, :] <= t5
    s = jnp.einsum('btgrd,bgtsd->btgrs', qr, k_sel).astype(F32) * SCALE_B
    p = _softmax_masked_jax(s, mask_sel)
    o_sel = jnp.einsum('btgrs,bgtsd->btgrd', p, v_sel)
    dist = t5 - pos_w
    mask_w = (dist >= 0) & (dist <= WINDOW) & (pos_w >= 0)
    s = jnp.einsum('btgrd,bsgd->btgrs', qr, kw).astype(F32) * SCALE_B
    p = _softmax_masked_jax(s, mask_w)
    o_win = jnp.einsum('btgrs,bsgd->btgrd', p, vw)
    g = gates.reshape(b, tq, N_KV_B, REP_B, 3)
    o = g[..., 0:1] * o_cmp + g[..., 1:2] * o_sel + g[..., 2:3] * o_win
    return o.reshape(b, tq, N_HEADS_B * HD_B)


def _nsa_sample(x, g_mix, past_len, kv_cache, win_cache, page_table, w_in, q_g, k_g, pe, w_c1, w_c2, w_out):
    bsz = x.shape[0]
    wts = _nsa_weights(w_in, q_g, k_g, w_out)
    pos = jnp.full((bsz,), past_len, jnp.int32)
    qcat, gates, kv_rows, win_rows, _, _, _, _ = _nsa_proj(
        x, g_mix, wts, _rope_tables(pos, LANE), _rope_tables(pos, HD_B), 1, bsz)
    q = qcat.astype(F32).reshape(bsz, 1, N_HEADS_B, 2, HD_B) * (1.0 / SCALE_B)
    gts = gates.reshape(bsz, N_KV_B, LANE)[:, :, :REP_B * 3].reshape(bsz, 1, N_HEADS_B, 3)
    new_kv = kv_rows.reshape(bsz, 1, N_KV_SLOTS, N_KV_B, HD_B)
    new_win = win_rows.reshape(bsz, 1, 2, N_KV_B, HD_B)
    past = kv_cache[page_table].reshape(bsz, -1, N_KV_SLOTS, N_KV_B, HD_B)
    full = jnp.concatenate([past, new_kv], axis=1)
    length = full.shape[1]
    kc_blk, vc_blk = _cmp_blocks(full[:, :, 0].reshape(bsz * length, -1), full[:, :, 1].reshape(bsz * length, -1),
                                 bsz, length, pe, w_c1, w_c2, k_g[0])
    win = jnp.concatenate([win_cache, new_win], axis=1)
    wb = win_cache.shape[1]
    pos_w = past_len - wb + jnp.arange(wb + 1)
    o = _nsa_core_jax(q[:, :, :, 0], q[:, :, :, 1], gts, past_len + jnp.arange(1), kc_blk, vc_blk,
                      _sel_blocks_jax(full[:, :, 2]), _sel_blocks_jax(full[:, :, 3]), win[:, :, 0], win[:, :, 1], pos_w)
    y = _proj_add(o.reshape(bsz, -1), x, w_out.astype(BF16))
    return y, new_kv, new_win


def kernel(x_prompt, x_sample, cache_nsa_kv, cache_nsa_win, state_mlstm_C, state_mlstm_n, state_mlstm_m, page_table,
           norm_mix_g, norm_ffn_g, ffn_w1, ffn_w2, a_w_in, a_ln_g, a_ln_b, a_w_s, a_b_s, a_w_out,
           b_w_in, b_q_g, b_k_g, b_pe, b_w_c1, b_w_c2, b_w_out, c_w_in, c_b_if, c_h_g, c_w_out):
    bp, t, d = x_prompt.shape
    bs, ts, _ = x_sample.shape
    assert ts == 1
    past_len = page_table.shape[1] * cache_nsa_kv.shape[2]
    xp = x_prompt.reshape(bp * t, d)
    xs = x_sample.reshape(bs, d)
    out = {k: [] for k in ("v_s", "kv_p", "win_p", "kv_s", "win_s", "C_p", "n_p", "m_p", "C_s", "n_s", "m_s")}
    for layer in range(norm_mix_g.shape[0]):
        kind, j = layer % 3, layer // 3
        gm = norm_mix_g[layer]
        if kind == 0:
            args = (a_w_in[j], a_ln_g[j], a_ln_b[j], a_w_s[j], a_b_s[j], a_w_out[j])
            xp = _gmlp_layer(xp, gm, *args, single=False)[0]
            xs, v = _gmlp_layer(xs, gm, *args, single=True)
            out["v_s"].append(v.reshape(bs, ts, -1))
        elif kind == 1:
            args = (b_w_in[j], b_q_g[j], b_k_g[j], b_pe[j], b_w_c1[j], b_w_c2[j], b_w_out[j])
            xp, kv, win = _nsa_prompt(xp, gm[None, :], bp, t, *args)
            out["kv_p"].append(kv)
            out["win_p"].append(win)
            xs, kv, win = _nsa_sample(xs, gm[None, :], past_len, cache_nsa_kv[j], cache_nsa_win[j], page_table, *args)
            out["kv_s"].append(kv)
            out["win_s"].append(win)
        else:
            args = (c_w_in[j], c_b_if[j], c_h_g[j], c_w_out[j])
            xp, c, n, m = _mlstm_prompt(xp, gm[None, :], bp, t, *args)
            out["C_p"].append(c)
            out["n_p"].append(n)
            out["m_p"].append(m)
            xs, c, n, m = _mlstm_sample(xs, gm[None, :], state_mlstm_C[j], state_mlstm_n[j], state_mlstm_m[j], *args)
            out["C_s"].append(c)
            out["n_s"].append(n)
            out["m_s"].append(m)
        gf = norm_ffn_g[layer][None, :]
        w1, w2 = ffn_w1[layer].astype(BF16), ffn_w2[layer].astype(BF16)
        xp = _ffn(xp, gf, w1, w2)
        xs = _ffn(xs, gf, w1, w2)
    st = {k: jnp.stack(v) for k, v in out.items()}
    return (xp.reshape(bp, t, d), xs.reshape(bs, ts, d), st["v_s"], st["kv_p"], st["win_p"], st["kv_s"], st["win_s"],
            st["C_p"], st["n_p"], st["m_p"], st["C_s"], st["n_s"], st["m_s"])
```

```python
import functools
import math

import jax
import jax.numpy as jnp
from jax import lax
from jax.experimental import pallas as pl
from jax.experimental.pallas import tpu as pltpu

F32 = jnp.float32
BF16 = jnp.bfloat16

EPS = 1e-6
CHUNK_A = 128
N_GROUPS_A = 8
N_HEADS_B = 16
N_KV_B = 4
REP_B = N_HEADS_B // N_KV_B
HD_B = 64
ROT_DIM = 16
ROPE_THETA = 500000.0
CMP_BLOCK = 32
SEL_BLOCK = 64
N_SEL = 16
WINDOW = 512
N_KV_SLOTS = 4
N_HEADS_C = 8
DK_C = 64
DV_C = 128
CHUNK_C = 128
SCALE_B = HD_B ** -0.5

LANE = 128
VMEM_LIMIT_BYTES = 56 * 1024 * 1024
MASK_BIG = 1e30
M_INIT = -1e20
N_SEL_PAD = 128
N_CMP_PAD = 2 * N_SEL_PAD
SHIFT_MAX = 40.0


def _params(*sem):
    return pltpu.CompilerParams(dimension_semantics=sem, vmem_limit_bytes=VMEM_LIMIT_BYTES)


def _dot(a, b):
    return jnp.dot(a, b, preferred_element_type=F32)


def _dot_nt(a, b):
    return lax.dot_general(a, b, (((1,), (1,)), ((), ())), preferred_element_type=F32)


def _dot_tn(a, b):
    return lax.dot_general(a, b, (((0,), (0,)), ((), ())), preferred_element_type=F32)


def _rms(x, g):
    return x * lax.rsqrt(jnp.mean(x * x, axis=-1, keepdims=True) + EPS) * g


def _split3(x):
    a = x.astype(BF16)
    r = x - a.astype(F32)
    b = r.astype(BF16)
    c = (r - b.astype(F32)).astype(BF16)
    return a, b, c


def _const_spec(shape):
    n = len(shape)
    return pl.BlockSpec(shape, lambda *_: (0,) * n)


def _row_tile(m, pref):
    t = min(pref, m)
    while m % t:
        t //= 2
    return t


def _ffn_body(x_ref, g_ref, w1_ref, w2_ref, o_ref, *, ck):
    x = x_ref[...]
    xb = _rms(x, g_ref[...]).astype(BF16)
    acc = x
    for j in range(w1_ref.shape[1] // ck):
        h = jnp.maximum(_dot(xb, w1_ref[:, j * ck:(j + 1) * ck]), 0.0)
        acc = acc + _dot((h * h).astype(BF16), w2_ref[j * ck:(j + 1) * ck, :])
    o_ref[...] = acc


def _ffn(x, g, w1, w2):
    m, d = x.shape
    tm = _row_tile(m, 512)
    return pl.pallas_call(
        functools.partial(_ffn_body, ck=1024),
        out_shape=jax.ShapeDtypeStruct((m, d), F32),
        grid=(m // tm,),
        in_specs=[pl.BlockSpec((tm, d), lambda i: (i, 0)), _const_spec(g.shape),
                  _const_spec(w1.shape), _const_spec(w2.shape)],
        out_specs=pl.BlockSpec((tm, d), lambda i: (i, 0)),
        compiler_params=_params("parallel"),
        name="ffn",
    )(x, g, w1, w2)


def _gmlp_body(x_ref, g_ref, win_ref, lng_ref, lnb_ref, ws_ref, bs_ref, wout_ref, o_ref, v_ref, *, single):
    x = x_ref[...]
    dg = lng_ref.shape[1]
    xb = _rms(x, g_ref[...]).astype(BF16)
    u = jax.nn.gelu(_dot(xb, win_ref[:, :dg]))
    v = jax.nn.gelu(_dot(xb, win_ref[:, dg:]))
    mu = jnp.mean(v, axis=-1, keepdims=True)
    vc = v - mu
    var = jnp.mean(vc * vc, axis=-1, keepdims=True)
    v = vc * lax.rsqrt(var + EPS) * lng_ref[...] + lnb_ref[...]
    v_ref[...] = v
    if single:
        gate = v * ws_ref[...] + bs_ref[...]
    else:
        gw = dg // N_GROUPS_A
        row = lax.broadcasted_iota(jnp.int32, (CHUNK_A, CHUNK_A), 0)
        col = lax.broadcasted_iota(jnp.int32, (CHUNK_A, CHUNK_A), 1)
        causal = col <= row
        vb = v.astype(BF16)
        chunks = []
        for c in range(x.shape[0] // CHUNK_A):
            parts = []
            for gi in range(N_GROUPS_A):
                w = jnp.where(causal, ws_ref[gi], 0.0).astype(BF16)
                parts.append(_dot(w, vb[c * CHUNK_A:(c + 1) * CHUNK_A, gi * gw:(gi + 1) * gw]))
            chunks.append(jnp.concatenate(parts, axis=1) + bs_ref[...])
        gate = jnp.concatenate(chunks, axis=0)
    o_ref[...] = x + _dot((u * gate).astype(BF16), wout_ref[...])


def _gmlp(x, g, w_in, ln_g, ln_b, ws, bs, w_out, *, single):
    m, d = x.shape
    dg = w_out.shape[0]
    tm = _row_tile(m, 256)
    out, v = pl.pallas_call(
        functools.partial(_gmlp_body, single=single),
        out_shape=(jax.ShapeDtypeStruct((m, d), F32), jax.ShapeDtypeStruct((m, dg), F32)),
        grid=(m // tm,),
        in_specs=[pl.BlockSpec((tm, d), lambda i: (i, 0)), _const_spec(g.shape), _const_spec(w_in.shape),
                  _const_spec(ln_g.shape), _const_spec(ln_b.shape), _const_spec(ws.shape), _const_spec(bs.shape),
                  _const_spec(w_out.shape)],
        out_specs=(pl.BlockSpec((tm, d), lambda i: (i, 0)), pl.BlockSpec((tm, dg), lambda i: (i, 0))),
        compiler_params=_params("parallel"),
        name="gmlp_single" if single else "gmlp",
    )(x, g, w_in, ln_g, ln_b, ws, bs, w_out)
    return out, v


def _rope_tables(pos, seg):
    half = ROT_DIM // 2
    freq = jnp.power(ROPE_THETA, -jnp.arange(half, dtype=F32) * 2.0 / ROT_DIM)
    ang = pos.astype(F32)[:, None] * freq[None, :]
    cos, sin = jnp.cos(ang), jnp.sin(ang)
    t = pos.shape[0]
    one = jnp.ones((t, seg - ROT_DIM), F32)
    zero = jnp.zeros((t, seg - ROT_DIM), F32)
    z8 = jnp.zeros((t, half), F32)
    tabs = [jnp.concatenate([cos, cos, one], 1), jnp.concatenate([-sin, z8, zero], 1),
            jnp.concatenate([z8, sin, zero], 1)]
    return jnp.stack([jnp.tile(a, (1, LANE // seg)) for a in tabs])


def _rope128(x, tab):
    return x * tab[0] + pltpu.roll(x, LANE - ROT_DIM // 2, 1) * tab[1] + pltpu.roll(x, ROT_DIM // 2, 1) * tab[2]


def _nsa_proj_body(x_ref, g_ref, wq_ref, wg_ref, wkv_ref, seg_ref, spread_ref, qg_ref, kg_ref, tq_ref, tk_ref,
                   qcat_ref, gates_ref, kv_ref, win_ref, ks_ref, vs_ref, kw_ref, vw_ref):
    x = x_ref[...]
    xb = _rms(x, g_ref[...]).astype(BF16)
    tq = tq_ref[...]
    tk = tk_ref[...]
    qg = qg_ref[...]
    for h in range(N_HEADS_B):
        q = _dot(xb, wq_ref[:, h * LANE:(h + 1) * LANE])
        ms = jnp.sum(q * q, axis=-1, keepdims=True) * (1.0 / LANE)
        qn = q * lax.rsqrt(ms + EPS) * qg
        qcat_ref[:, h * LANE:(h + 1) * LANE] = (_rope128(qn, tq) * SCALE_B).astype(BF16)
    gates_ref[...] = jax.nn.sigmoid(_dot(xb, wg_ref[...]))
    kv = _dot(xb, wkv_ref[...])
    w = N_KV_B * HD_B
    seg = seg_ref[...]
    spread = spread_ref[...]

    def head_norm(k, gain):
        k2 = k * k
        hi = k2.astype(BF16)
        lo = (k2 - hi.astype(F32)).astype(BF16)
        ss = _dot(hi, seg) + _dot(lo, seg)
        return k * lax.rsqrt(ss * (1.0 / HD_B) + EPS) * gain

    def rope(k):
        return jnp.concatenate([_rope128(k[:, j * LANE:(j + 1) * LANE], tk) for j in range(w // LANE)], axis=1)

    ks = rope(head_norm(kv[:, 2 * w:3 * w], kg_ref[0:1, :]))
    kw = rope(head_norm(kv[:, 4 * w:5 * w], kg_ref[1:2, :]))
    vs = kv[:, 3 * w:4 * w]
    vw = kv[:, 5 * w:6 * w]
    kv_ref[:, :2 * w] = kv[:, :2 * w]
    kv_ref[:, 2 * w:3 * w] = ks
    kv_ref[:, 3 * w:] = vs
    win_ref[:, :w] = kw
    win_ref[:, w:] = vw
    lane = lax.broadcasted_iota(jnp.int32, (1, N_KV_B * LANE), 1)
    ones_hi = ((lane & HD_B) != 0).astype(F32)
    ks_ref[...] = _dot(ks.astype(BF16), spread).astype(BF16)
    kw_ref[...] = _dot(kw.astype(BF16), spread).astype(BF16)
    vs_ref[...] = (_dot(vs.astype(BF16), spread) + ones_hi).astype(BF16)
    vw_ref[...] = (_dot(vw.astype(BF16), spread) + ones_hi).astype(BF16)


def _nsa_proj(x, g, wts, tab_q, tab_k, n_tab_tiles, tm):
    m, d = x.shape
    w = N_KV_B * HD_B
    ws = N_KV_B * LANE
    tile = lambda n: pl.BlockSpec((tm, n), lambda i: (i, 0))
    tab = pl.BlockSpec((3, tm, LANE), lambda i: (0, i % n_tab_tiles, 0))
    consts = [g, wts["wq"], wts["wg"], wts["wkv"], wts["seg"], wts["spread"], wts["qg"], wts["kg"]]
    return pl.pallas_call(
        _nsa_proj_body,
        out_shape=(jax.ShapeDtypeStruct((m, N_HEADS_B * LANE), BF16), jax.ShapeDtypeStruct((m, ws), F32),
                   jax.ShapeDtypeStruct((m, 4 * w), F32), jax.ShapeDtypeStruct((m, 2 * w), F32),
                   jax.ShapeDtypeStruct((m, ws), BF16), jax.ShapeDtypeStruct((m, ws), BF16),
                   jax.ShapeDtypeStruct((m, ws), BF16), jax.ShapeDtypeStruct((m, ws), BF16)),
        grid=(m // tm,),
        in_specs=[tile(d)] + [_const_spec(c.shape) for c in consts] + [tab, tab],
        out_specs=(tile(N_HEADS_B * LANE), tile(ws), tile(4 * w), tile(2 * w), tile(ws), tile(ws), tile(ws), tile(ws)),
        compiler_params=_params("parallel"),
        name="nsa_proj",
    )(x, *consts, tab_q, tab_k)


def _compress_body(x_ref, pe_ref, w1_ref, w2_ref, g_ref, o_ref, *, norm):
    xb = (x_ref[...] + pe_ref[...]).astype(BF16)
    hid = jax.nn.gelu(_dot(xb, w1_ref[...]))
    y = _dot(hid.astype(BF16), w2_ref[...])
    if norm:
        y = _rms(y, g_ref[...])
    o_ref[...] = y


def _compress(x, pe, w1, w2, g, *, norm):
    r, kdim = x.shape
    tr = _row_tile(r, 512)
    return pl.pallas_call(
        functools.partial(_compress_body, norm=norm),
        out_shape=jax.ShapeDtypeStruct((r, HD_B), F32),
        grid=(r // tr,),
        in_specs=[pl.BlockSpec((tr, kdim), lambda i: (i, 0)), _const_spec(pe.shape), _const_spec(w1.shape),
                  _const_spec(w2.shape), _const_spec(g.shape)],
        out_specs=pl.BlockSpec((tr, HD_B), lambda i: (i, 0)),
        compiler_params=_params("parallel"),
        name="nsa_compress_k" if norm else "nsa_compress_v",
    )(x, pe, w1, w2, g)


def _masked_softmax(s, mask, axis):
    sm = jnp.where(mask, s, -jnp.inf)
    mx = jnp.max(sm, axis=axis, keepdims=True)
    mx = jnp.where(mx > -jnp.inf, mx, 0.0)
    e = jnp.where(mask, jnp.exp(s - mx), 0.0)
    return e / jnp.maximum(jnp.sum(e, axis=axis, keepdims=True), 1e-30)


def _cmp_select_body(q_ref, kc_ref, kct_ref, vc_ref, gates_ref, o_ref, mnot_ref, *, tq):
    q0 = pl.program_id(2) * tq
    kc = kc_ref[...]
    kct = kct_ref[...]
    vc = vc_ref[...]
    gates = gates_ref[...]
    t_r = q0 + lax.broadcasted_iota(jnp.int32, (tq, N_CMP_PAD), 0)
    blk_r = lax.broadcasted_iota(jnp.int32, (tq, N_CMP_PAD), 1)
    mask_r = (blk_r + 1) * CMP_BLOCK - 1 <= t_r
    row_c = lax.broadcasted_iota(jnp.int32, (N_CMP_PAD, tq), 0)
    t_c = q0 + lax.broadcasted_iota(jnp.int32, (N_CMP_PAD, tq), 1)
    blk_c = jnp.where(row_c < N_SEL_PAD, 2 * row_c, 2 * (row_c - N_SEL_PAD) + 1)
    mask_c = (blk_c + 1) * CMP_BLOCK - 1 <= t_c
    imp = jnp.zeros((N_SEL_PAD, tq), F32)
    for r in range(REP_B):
        qh = q_ref[:, r * LANE:(r + 1) * LANE]
        p = _masked_softmax(_dot_nt(qh, kc), mask_r, 1)
        o = _dot(p.astype(BF16), vc)
        o_ref[:, r * LANE:(r + 1) * LANE] = o * gates[:, 3 * r:3 * r + 1]
        pt = _masked_softmax(_dot_nt(kct, qh), mask_c, 0)
        imp = imp + pt[:N_SEL_PAD] + pt[N_SEL_PAD:]
    blk = lax.broadcasted_iota(jnp.int32, (N_SEL_PAD, tq), 0)
    t_s = q0 + lax.broadcasted_iota(jnp.int32, (N_SEL_PAD, tq), 1)
    cur = t_s // SEL_BLOCK
    forced = (blk == 0) | (blk == cur) | (blk == cur - 1)
    valid = blk * SEL_BLOCK <= t_s
    score = jnp.where(forced, jnp.inf, jnp.where(valid, imp, -jnp.inf))
    blk_f = blk.astype(F32)
    sel = jnp.zeros((N_SEL_PAD, tq), F32)
    for _ in range(N_SEL):
        mx = jnp.max(score, axis=0, keepdims=True)
        first = jnp.min(jnp.where(score == mx, blk_f, float(N_SEL_PAD)), axis=0, keepdims=True)
        hit = blk_f == first
        sel = jnp.where(hit & (mx > -jnp.inf), 1.0, sel)
        score = jnp.where(hit, -jnp.inf, score)
    mnot_ref[...] = (1.0 - sel).T.astype(BF16)


def _cmp_select(qcat, kc, kct, vc, gates, b, t, tq):
    m = b * t
    nq = t // tq
    blk = pl.BlockSpec((None, None, N_CMP_PAD, LANE), lambda bi, g, i: (bi, g, 0, 0))
    return pl.pallas_call(
        functools.partial(_cmp_select_body, tq=tq),
        out_shape=(jax.ShapeDtypeStruct((m, N_HEADS_B * LANE), F32), jax.ShapeDtypeStruct((m, N_KV_B * LANE), BF16)),
        grid=(b, N_KV_B, nq),
        in_specs=[pl.BlockSpec((tq, REP_B * LANE), lambda bi, g, i: (bi * nq + i, g)), blk, blk, blk,
                  pl.BlockSpec((tq, LANE), lambda bi, g, i: (bi * nq + i, g))],
        out_specs=(pl.BlockSpec((tq, REP_B * LANE), lambda bi, g, i: (bi * nq + i, g)),
                   pl.BlockSpec((tq, LANE), lambda bi, g, i: (bi * nq + i, g))),
        compiler_params=_params("parallel", "parallel", "parallel"),
        name="nsa_cmp_select",
    )(qcat, kc, kct, vc, gates)


def _flash_body(tab_ref, bound_ref, q_ref, mnot_ref, k_ref, v_ref, gates_ref, o_ref, qs_ref, ks_ref, m_ref, acc_ref, *,
                tq, tk, sel, gate_col, fixed):
    step_id = pl.program_id(2)
    i = tab_ref[0, step_id]
    j = tab_ref[1, step_id]
    rows = REP_B * tq
    first = j == (0 if sel else jnp.maximum(i - WINDOW // tk, 0))

    @pl.when(first)
    def _():
        if not fixed:
            m_ref[...] = jnp.full(m_ref.shape, M_INIT, F32)
        acc_ref[...] = jnp.zeros(acc_ref.shape, F32)
        lane = lax.broadcasted_iota(jnp.int32, (tq, LANE), 1)
        for r in range(REP_B):
            qh = q_ref[:, r * LANE:(r + 1) * LANE]
            if fixed:
                qh = jnp.where(lane < HD_B, qh, jnp.ones_like(qh))
            if sel:
                qs_ref[r * tq:(r + 1) * tq, :LANE] = mnot_ref[...]
                qs_ref[r * tq:(r + 1) * tq, LANE:] = qh
            else:
                qs_ref[r * tq:(r + 1) * tq, :] = qh

    def step(masked):
        kt = k_ref[...]
        if fixed:
            lane = lax.broadcasted_iota(jnp.int32, (tk, LANE), 1)
            kt = jnp.where(lane == HD_B, -bound_ref[0], kt.astype(F32)).astype(BF16)
        if sel:
            kpos = j * tk + lax.broadcasted_iota(jnp.int32, (tk, LANE), 0)
            lane = lax.broadcasted_iota(jnp.int32, (tk, LANE), 1)
            ks_ref[:, :LANE] = jnp.where(kpos // SEL_BLOCK == lane, -MASK_BIG, 0.0).astype(BF16)
            ks_ref[:, LANE:] = kt
            kmat = ks_ref[...]
        else:
            kmat = kt
        s = _dot_nt(qs_ref[...], kmat)
        if masked is not None:
            t = i * tq + lax.broadcasted_iota(jnp.int32, (rows, tk), 0) % tq
            key = j * tk + lax.broadcasted_iota(jnp.int32, (rows, tk), 1)
            ok = key <= t if masked == "causal" else t - key <= WINDOW
            s = jnp.where(ok, s, -MASK_BIG)
        if fixed:
            acc_ref[...] += _dot(jnp.exp(s).astype(BF16), v_ref[...])
        else:
            m_old = m_ref[...]
            m_new = jnp.maximum(m_old, jnp.max(s, axis=-1, keepdims=True))
            p = jnp.exp(s - m_new[:, :1])
            acc_ref[...] = jnp.exp(m_old - m_new) * acc_ref[...] + _dot(p.astype(BF16), v_ref[...])
            m_ref[...] = m_new

    if sel:
        pl.when(j < i)(lambda: step(None))
    else:
        pl.when(j == i - WINDOW // tk)(lambda: step("band"))
        pl.when((j < i) & (j > i - WINDOW // tk))(lambda: step(None))

    @pl.when(j == i)
    def _():
        step("causal")
        gates = gates_ref[...]
        lane = lax.broadcasted_iota(jnp.int32, (tq, LANE), 1)
        for r in range(REP_B):
            a = acc_ref[r * tq:(r + 1) * tq, :]
            o = a / a[:, HD_B:HD_B + 1]
            g = gates[:, 3 * r + gate_col:3 * r + gate_col + 1]
            o_ref[:, r * LANE:(r + 1) * LANE] = jnp.where(lane < HD_B, o * g, 0.0)


def _flash(qcat, mnot, k, v, gates, bound, b, t, tq, sel):
    m = b * t
    nq = t // tq
    tk = tq
    assert WINDOW % tk == 0
    lo = (lambda i: 0) if sel else (lambda i: max(i - WINDOW // tk, 0))
    pairs = [(i, j) for i in range(nq) for j in range(lo(i), i + 1)]
    tab = jnp.asarray(pairs, jnp.int32).T
    kdim = 2 * LANE if sel else LANE
    qidx = lambda bi, g, p, *pf: (bi * nq + pf[0][0, p], g)
    kidx = lambda bi, g, p, *pf: (bi * nq + pf[0][1, p], g)

    def call(fixed):
        name = ("nsa_flash_sel" if sel else "nsa_flash_win") + ("" if fixed else "_online")
        return pl.pallas_call(
            functools.partial(_flash_body, tq=tq, tk=tk, sel=sel, gate_col=1 if sel else 2, fixed=fixed),
            out_shape=jax.ShapeDtypeStruct((m, N_HEADS_B * LANE), F32),
            grid_spec=pltpu.PrefetchScalarGridSpec(
                num_scalar_prefetch=2,
                grid=(b, N_KV_B, len(pairs)),
                in_specs=[pl.BlockSpec((tq, REP_B * LANE), qidx), pl.BlockSpec((tq, LANE), qidx),
                          pl.BlockSpec((tk, LANE), kidx), pl.BlockSpec((tk, LANE), kidx),
                          pl.BlockSpec((tq, LANE), qidx)],
                out_specs=pl.BlockSpec((tq, REP_B * LANE), qidx),
                scratch_shapes=[pltpu.VMEM((REP_B * tq, kdim), BF16), pltpu.VMEM((tk, 2 * LANE), BF16),
                                pltpu.VMEM((REP_B * tq, LANE), F32), pltpu.VMEM((REP_B * tq, LANE), F32)]),
            compiler_params=_params("parallel", "parallel", "arbitrary"),
            name=name,
        )(tab, bound.reshape(1), qcat, mnot, k, v, gates)

    return lax.cond(bound <= SHIFT_MAX, lambda: call(True), lambda: call(False))


def _sum_proj_body(a_ref, b_ref, c_ref, x_ref, w_ref, o_ref):
    o = (a_ref[...] + b_ref[...] + c_ref[...]).astype(BF16)
    o_ref[...] = x_ref[...] + _dot(o, w_ref[...])


def _sum_proj(a, b, c, x, w):
    m, d = x.shape
    kdim = a.shape[1]
    tm = _row_tile(m, 512)
    big = pl.BlockSpec((tm, kdim), lambda i: (i, 0))
    row = pl.BlockSpec((tm, d), lambda i: (i, 0))
    return pl.pallas_call(
        _sum_proj_body,
        out_shape=jax.ShapeDtypeStruct((m, d), F32),
        grid=(m // tm,),
        in_specs=[big, big, big, row, _const_spec(w.shape)],
        out_specs=row,
        compiler_params=_params("parallel"),
        name="nsa_out_proj",
    )(a, b, c, x, w)


def _nsa_weights(w_in, q_g, k_g, w_out):
    d = w_in.shape[0]
    nq = N_HEADS_B * HD_B
    w = N_KV_B * HD_B
    wq = w_in[:, :nq].reshape(d, N_HEADS_B, 1, HD_B)
    wq = jnp.broadcast_to(wq, (d, N_HEADS_B, 2, HD_B)).reshape(d, N_HEADS_B * LANE)
    wg = w_in[:, nq:nq + 3 * N_HEADS_B].reshape(d, N_KV_B, REP_B * 3)
    wg = jnp.pad(wg, ((0, 0), (0, 0), (0, LANE - REP_B * 3))).reshape(d, N_KV_B * LANE)
    wkv = w_in[:, nq + 3 * N_HEADS_B:]
    lane = jnp.arange(w)
    seg = (lane[:, None] // HD_B == lane[None, :] // HD_B).astype(BF16)
    spread = (lane[:, None] // HD_B * LANE + lane[:, None] % HD_B == jnp.arange(N_KV_B * LANE)[None, :]).astype(BF16)
    wo = jnp.pad(w_out.reshape(N_HEADS_B, HD_B, -1), ((0, 0), (0, LANE - HD_B), (0, 0)))
    return {
        "wq": wq.astype(BF16), "wg": wg.astype(BF16), "wkv": wkv.astype(BF16), "seg": seg, "spread": spread,
        "qg": jnp.tile(q_g, 2)[None, :], "kg": jnp.stack([jnp.tile(k_g[1], N_KV_B), jnp.tile(k_g[2], N_KV_B)]),
        "wo": wo.reshape(N_HEADS_B * LANE, -1).astype(BF16),
    }


def _cmp_inputs(rows, b, length):
    nb = length // CMP_BLOCK
    x = rows.reshape(b, length, N_KV_B, HD_B)[:, :nb * CMP_BLOCK].reshape(b, nb, CMP_BLOCK, N_KV_B, HD_B)
    return x.transpose(0, 1, 3, 2, 4).reshape(b * nb * N_KV_B, CMP_BLOCK * HD_B)


def _cmp_blocks(kc_rows, vc_rows, b, length, pe, w_c1, w_c2, kc_g):
    nb = length // CMP_BLOCK
    pe_k, pe_v = pe[0].reshape(1, -1), pe[1].reshape(1, -1)
    w1k = w_c1[0].reshape(CMP_BLOCK * HD_B, HD_B).astype(BF16)
    w1v = w_c1[1].reshape(CMP_BLOCK * HD_B, HD_B).astype(BF16)
    g = kc_g[None, :]
    kc = _compress(_cmp_inputs(kc_rows, b, length), pe_k, w1k, w_c2[0].astype(BF16), g, norm=True)
    vc = _compress(_cmp_inputs(vc_rows, b, length), pe_v, w1v, w_c2[1].astype(BF16), g, norm=False)
    return kc.reshape(b, nb, N_KV_B, HD_B), vc.reshape(b, nb, N_KV_B, HD_B)


def _nsa_prompt(x, g_mix, b, t, w_in, q_g, k_g, pe, w_c1, w_c2, w_out):
    wts = _nsa_weights(w_in, q_g, k_g, w_out)
    w = N_KV_B * HD_B
    tm = 256
    pos = jnp.arange(t)
    qcat, gates, kv_rows, win_rows, ks_s, vs_s, kw_s, vw_s = _nsa_proj(
        x, g_mix, wts, _rope_tables(pos, LANE), _rope_tables(pos, HD_B), t // tm, tm)
    kc_blk, vc_blk = _cmp_blocks(kv_rows[:, :w], kv_rows[:, w:2 * w], b, t, pe, w_c1, w_c2, k_g[0])
    nb = t // CMP_BLOCK
    assert nb <= N_CMP_PAD and t % SEL_BLOCK == 0

    def blocks(a, lo):
        a = jnp.pad(a.transpose(0, 2, 1, 3), ((0, 0), (0, 0), (0, N_CMP_PAD - nb), (lo, LANE - HD_B - lo)))
        return a.astype(BF16)

    kc = blocks(kc_blk, HD_B)
    kct = jnp.concatenate([kc[:, :, 0::2], kc[:, :, 1::2]], axis=2)
    vc = blocks(vc_blk, 0)
    tq = 256
    o_cmp, mnot = _cmp_select(qcat, kc, kct, vc, gates, b, t, tq)
    qmax = jnp.max(jnp.abs(q_g))
    o_sel = _flash(qcat, mnot, ks_s, vs_s, gates, qmax * jnp.max(jnp.abs(k_g[1])) * math.sqrt(HD_B), b, t, tq, True)
    o_win = _flash(qcat, mnot, kw_s, vw_s, gates, qmax * jnp.max(jnp.abs(k_g[2])) * math.sqrt(HD_B), b, t, tq, False)
    y = _sum_proj(o_cmp, o_sel, o_win, x, wts["wo"])
    wb = min(WINDOW, t)
    kv_out = kv_rows.reshape(b, t, N_KV_SLOTS, N_KV_B, HD_B)
    win_out = win_rows.reshape(b, t, 2, N_KV_B, HD_B)[:, t - wb:]
    return y, kv_out, win_out


def _gmlp_layer(x, g, w_in, ln_g, ln_b, w_s, b_s, w_out, *, single):
    gw = w_out.shape[0] // N_GROUPS_A
    if single:
        ws = jnp.repeat(w_s[:, 0, 0], gw)[None, :]
        bs = jnp.repeat(b_s[:, 0], gw)[None, :]
    else:
        ws = w_s
        bs = jnp.repeat(b_s.T, gw, axis=1)
    return _gmlp(x, g[None, :], w_in.astype(BF16), ln_g[None, :], ln_b[None, :], ws, bs, w_out.astype(BF16),
                 single=single)


def _mlstm_proj_body(x_ref, g_ref, wq_ref, wk_ref, wv_ref, wgi_ref, wo_ref, bif_ref,
                     q_ref, k_ref, v_ref, gi_ref, og_ref):
    xb = _rms(x_ref[...], g_ref[...]).astype(BF16)
    q_ref[...] = _dot(xb, wq_ref[...]).astype(BF16)
    k_ref[...] = _dot(xb, wk_ref[...]).astype(BF16)
    v_ref[...] = _dot(xb, wv_ref[...]).astype(BF16)
    gi_ref[...] = _dot(xb, wgi_ref[...]) + bif_ref[...]
    og_ref[...] = jax.nn.sigmoid(_dot(xb, wo_ref[...]))


def _mlstm_proj(x, g, wts):
    m, d = x.shape
    hv = N_HEADS_C * DV_C
    tm = _row_tile(m, 512)
    consts = [g, wts["wq"], wts["wk"], wts["wv"], wts["wgi"], wts["wo"], wts["bif"]]
    tile = lambda n: pl.BlockSpec((tm, n), lambda i: (i, 0))
    return pl.pallas_call(
        _mlstm_proj_body,
        out_shape=(jax.ShapeDtypeStruct((m, N_HEADS_C * LANE), BF16), jax.ShapeDtypeStruct((m, N_HEADS_C * LANE), BF16),
                   jax.ShapeDtypeStruct((m, hv), BF16), jax.ShapeDtypeStruct((m, LANE), F32),
                   jax.ShapeDtypeStruct((m, hv), F32)),
        grid=(m // tm,),
        in_specs=[tile(d)] + [_const_spec(c.shape) for c in consts],
        out_specs=(tile(N_HEADS_C * LANE), tile(N_HEADS_C * LANE), tile(hv), tile(LANE), tile(hv)),
        compiler_params=_params("parallel"),
        name="mlstm_proj",
    )(x, *consts)


def _mlstm_scan_body(q_ref, k_ref, v_ref, gi_ref, git_ref, hs_ref, c_out, n_out, m_out, c_s, n_s, m_s):
    c = pl.program_id(1)
    L = q_ref.shape[0]

    @pl.when(c == 0)
    def _():
        c_s[...] = jnp.zeros(c_s.shape, F32)
        n_s[...] = jnp.zeros(n_s.shape, F32)
        m_s[...] = jnp.zeros(m_s.shape, F32)

    row = lax.broadcasted_iota(jnp.int32, (L, L), 0)
    col = lax.broadcasted_iota(jnp.int32, (L, L), 1)
    causal = col <= row
    tril = causal.astype(BF16)
    gi = gi_ref[...]
    git = git_ref[...]
    fcol = jax.nn.log_sigmoid(gi)
    frow = jax.nn.log_sigmoid(git[N_HEADS_C:, :])
    bcol_all = sum(_dot(tril, part) for part in _split3(fcol))
    brow_all = sum(_dot_nt(part, tril) for part in _split3(frow))
    for h in range(N_HEADS_C):
        sl = slice(h * LANE, (h + 1) * LANE)
        q = q_ref[:, sl]
        k = k_ref[:, sl]
        v = v_ref[:, sl]
        bcol = bcol_all[:, N_HEADS_C + h:N_HEADS_C + h + 1]
        icol = gi[:, h:h + 1]
        brow = brow_all[h:h + 1, :]
        irow = git[h:h + 1, :]
        m_prev = m_s[h:h + 1, 0:1]
        dlog = jnp.where(causal, bcol - brow + irow, -jnp.inf)
        inter = bcol + m_prev
        m_t = jnp.maximum(inter, jnp.max(dlog, axis=1, keepdims=True))
        w = jnp.exp(dlog - m_t)
        a = jnp.exp(inter - m_t)
        s = _dot_nt(q, k) * w
        cq = _dot_nt(q, c_s[h].astype(BF16))
        num = a * cq + _dot(s.astype(BF16), v)
        nq = jnp.sum(q.astype(F32) * n_s[h:h + 1, :], axis=1, keepdims=True)
        den = a * nq + jnp.sum(s, axis=1, keepdims=True)
        hs_ref[:, sl] = num / jnp.maximum(jnp.abs(den), jnp.exp(-m_t))
        b_end = bcol[L - 1:L, :]
        wlog = b_end - bcol + icol
        m_new = jnp.maximum(b_end + m_prev, jnp.max(wlog, axis=0, keepdims=True))
        wk = jnp.exp(wlog - m_new)
        decay = jnp.exp(b_end + m_prev - m_new)
        vw = (v.astype(F32) * wk).astype(BF16)
        c_s[h] = decay * c_s[h] + _dot_tn(vw, k)
        n_s[h:h + 1, :] = decay * n_s[h:h + 1, :] + jnp.sum(k.astype(F32) * wk, axis=0, keepdims=True)
        m_s[h:h + 1, :] = jnp.broadcast_to(m_new, (1, LANE))

    @pl.when(c == pl.num_programs(1) - 1)
    def _():
        c_out[...] = c_s[...]
        n_out[...] = n_s[...]
        m_out[...] = m_s[...]


def _mlstm_scan(q, k, v, gi, git, b, t):
    L = math.gcd(t, CHUNK_C)
    nc = t // L
    hv = N_HEADS_C * DV_C
    tile = lambda n: pl.BlockSpec((L, n), lambda bi, c: (bi * nc + c, 0))
    return pl.pallas_call(
        _mlstm_scan_body,
        out_shape=(jax.ShapeDtypeStruct((b * t, hv), F32),
                   jax.ShapeDtypeStruct((b, N_HEADS_C, DV_C, LANE), F32),
                   jax.ShapeDtypeStruct((b, N_HEADS_C, LANE), F32), jax.ShapeDtypeStruct((b, N_HEADS_C, LANE), F32)),
        grid=(b, nc),
        in_specs=[tile(N_HEADS_C * LANE), tile(N_HEADS_C * LANE), tile(hv), tile(LANE),
                  pl.BlockSpec((None, 2 * N_HEADS_C, L), lambda bi, c: (bi, 0, c))],
        out_specs=(tile(hv), pl.BlockSpec((None, N_HEADS_C, DV_C, LANE), lambda bi, c: (bi, 0, 0, 0)),
                   pl.BlockSpec((None, N_HEADS_C, LANE), lambda bi, c: (bi, 0, 0)),
                   pl.BlockSpec((None, N_HEADS_C, LANE), lambda bi, c: (bi, 0, 0))),
        scratch_shapes=[pltpu.VMEM((N_HEADS_C, DV_C, LANE), F32), pltpu.VMEM((N_HEADS_C, LANE), F32),
                        pltpu.VMEM((N_HEADS_C, LANE), F32)],
        compiler_params=_params("parallel", "arbitrary"),
        name="mlstm_scan",
    )(q, k, v, gi, git)


def _mlstm_out_body(hs_ref, og_ref, hg_ref, x_ref, w_ref, o_ref):
    parts = []
    for h in range(N_HEADS_C):
        sl = slice(h * DV_C, (h + 1) * DV_C)
        parts.append((og_ref[:, sl] * _rms(hs_ref[:, sl], hg_ref[:, sl])).astype(BF16))
    o_ref[...] = x_ref[...] + _dot(jnp.concatenate(parts, axis=1), w_ref[...])


def _mlstm_out(hs, og, hg, x, w):
    m, d = x.shape
    hv = hs.shape[1]
    tm = _row_tile(m, 512)
    wide = pl.BlockSpec((tm, hv), lambda i: (i, 0))
    row = pl.BlockSpec((tm, d), lambda i: (i, 0))
    return pl.pallas_call(
        _mlstm_out_body,
        out_shape=jax.ShapeDtypeStruct((m, d), F32),
        grid=(m // tm,),
        in_specs=[wide, wide, _const_spec(hg.shape), row, _const_spec(w.shape)],
        out_specs=row,
        compiler_params=_params("parallel"),
        name="mlstm_out",
    )(hs, og, hg, x, w)


def _mlstm_weights(w_in, b_if):
    d = w_in.shape[0]
    hk, hv = N_HEADS_C * DK_C, N_HEADS_C * DV_C

    def spread(w):
        w = w.reshape(d, N_HEADS_C, DK_C)
        return jnp.pad(w, ((0, 0), (0, 0), (0, LANE - DK_C))).reshape(d, N_HEADS_C * LANE)

    wgi = jnp.pad(w_in[:, 2 * hk + hv:2 * hk + hv + 2 * N_HEADS_C], ((0, 0), (0, LANE - 2 * N_HEADS_C)))
    return {
        "wq": spread(w_in[:, :hk]).astype(BF16),
        "wk": (spread(w_in[:, hk:2 * hk]) * (DK_C ** -0.5)).astype(BF16),
        "wv": w_in[:, 2 * hk:2 * hk + hv].astype(BF16),
        "wgi": wgi.astype(BF16),
        "wo": w_in[:, 2 * hk + hv + 2 * N_HEADS_C:].astype(BF16),
        "bif": jnp.pad(b_if, (0, LANE - 2 * N_HEADS_C))[None, :],
    }


def _mlstm_prompt(x, g_mix, b, t, w_in, b_if, h_g, w_out):
    wts = _mlstm_weights(w_in, b_if)
    q, k, v, gi, og = _mlstm_proj(x, g_mix, wts)
    git = gi[:, :2 * N_HEADS_C].reshape(b, t, 2 * N_HEADS_C).transpose(0, 2, 1)
    hs, c, n, m = _mlstm_scan(q, k, v, gi, git, b, t)
    y = _mlstm_out(hs, og, h_g[None, :], x, w_out.astype(BF16))
    return y, c[..., :DK_C], n[..., :DK_C], m[..., 0]


def _proj_add_body(o_ref, x_ref, w_ref, out_ref):
    out_ref[...] = x_ref[...] + _dot(o_ref[...].astype(BF16), w_ref[...])


def _proj_add(o, x, w):
    m, d = x.shape
    tm = _row_tile(m, 512)
    return pl.pallas_call(
        _proj_add_body,
        out_shape=jax.ShapeDtypeStruct((m, d), F32),
        grid=(m // tm,),
        in_specs=[pl.BlockSpec((tm, o.shape[1]), lambda i: (i, 0)), pl.BlockSpec((tm, d), lambda i: (i, 0)),
                  _const_spec(w.shape)],
        out_specs=pl.BlockSpec((tm, d), lambda i: (i, 0)),
        compiler_params=_params("parallel"),
        name="proj_add",
    )(o, x, w)


def _nsa_step_body(pt_ref, *refs, n_pages, page, past_len):
    pages = refs[:n_pages]
    (win_ref, qr_ref, qn_ref, gates_ref, nkv_ref, nwin_ref, pe_ref, w1_ref, w2_ref, kcg_ref,
     o_ref, c_s, ks_s, vs_s, x_s) = refs[n_pages:]
    w = N_KV_B * HD_B
    length = n_pages * page
    nb = length // CMP_BLOCK
    t = past_len
    for p in range(n_pages):
        rows = slice(p * page, (p + 1) * page)
        for c in range(2 * w // LANE):
            c_s[c, rows, :] = pages[p][:, c * LANE:(c + 1) * LANE]
        ks_s[rows, :] = pages[p][:, 2 * w:3 * w].astype(BF16)
        vs_s[rows, :] = pages[p][:, 3 * w:].astype(BF16)
    for ng in range(nb // 8):
        for l in range(CMP_BLOCK):
            for c in range(2 * w // LANE):
                rows = c_s[c, pl.ds(ng * 8 * CMP_BLOCK + l, 8, stride=CMP_BLOCK), :]
                for half in range(LANE // HD_B):
                    x_s[c * (LANE // HD_B) + half, ng * 8:(ng + 1) * 8, l * HD_B:(l + 1) * HD_B] = (
                        rows[:, half * HD_B:(half + 1) * HD_B])
    lane_w = lax.broadcasted_iota(jnp.int32, (HD_B, w), 1)
    row_w = lax.broadcasted_iota(jnp.int32, (HD_B, w), 0)
    cmp_nat = []
    for slot in range(2):
        xs = x_s[slot * N_KV_B:(slot + 1) * N_KV_B].reshape(N_KV_B * nb, CMP_BLOCK * HD_B)
        hid = jax.nn.gelu(_dot((xs + pe_ref[slot]).astype(BF16), w1_ref[slot]))
        y = _dot(hid.astype(BF16), w2_ref[slot])
        if slot == 0:
            y = _rms(y, kcg_ref[...])
        nat = jnp.zeros((nb, w), F32)
        for g in range(N_KV_B):
            place = (lane_w == row_w + g * HD_B).astype(BF16)
            nat = nat + _dot(y[g * nb:(g + 1) * nb].astype(BF16), place)
        cmp_nat.append(nat.astype(BF16))
    kc, vc = cmp_nat
    qr = qr_ref[...]
    qn = qn_ref[...]
    gates = gates_ref[...]
    nh = N_HEADS_B
    blk = lax.broadcasted_iota(jnp.int32, (nh, nb), 1)
    p_c = _masked_softmax(_dot_nt(qn, kc), (blk + 1) * CMP_BLOCK - 1 <= t, 1)
    o_c = _dot(p_c.astype(BF16), vc)
    gsum = (lax.broadcasted_iota(jnp.int32, (8, nh), 1) // REP_B == lax.broadcasted_iota(jnp.int32, (8, nh), 0))
    pair = (lax.broadcasted_iota(jnp.int32, (nb, N_SEL_PAD), 0) // (SEL_BLOCK // CMP_BLOCK)
            == lax.broadcasted_iota(jnp.int32, (nb, N_SEL_PAD), 1))
    imp = sum(_dot(gsum.astype(BF16), part) for part in _split3(p_c))
    imp = sum(_dot(part, pair.astype(BF16)) for part in _split3(imp))
    sblk = lax.broadcasted_iota(jnp.int32, (8, N_SEL_PAD), 1)
    cur = t // SEL_BLOCK
    forced = (sblk == 0) | (sblk == cur) | (sblk == cur - 1)
    score = jnp.where(forced, jnp.inf, jnp.where(sblk * SEL_BLOCK <= t, imp, -jnp.inf))
    sblk_f = sblk.astype(F32)
    sel = jnp.zeros((8, N_SEL_PAD), F32)
    for _ in range(N_SEL):
        mx = jnp.max(score, axis=1, keepdims=True)
        first = jnp.min(jnp.where(score == mx, sblk_f, float(N_SEL_PAD)), axis=1, keepdims=True)
        hit = sblk_f == first
        sel = jnp.where(hit & (mx > -jnp.inf), 1.0, sel)
        score = jnp.where(hit, -jnp.inf, score)
    gexp = (lax.broadcasted_iota(jnp.int32, (nh, 8), 0) // REP_B == lax.broadcasted_iota(jnp.int32, (nh, 8), 1))
    bias = _dot(gexp.astype(BF16), ((1.0 - sel) * -MASK_BIG).astype(BF16)).astype(BF16)
    expand = (lax.broadcasted_iota(jnp.int32, (N_SEL_PAD, length), 1) // SEL_BLOCK
              == lax.broadcasted_iota(jnp.int32, (N_SEL_PAD, length), 0)).astype(BF16)
    nkv = nkv_ref[...]
    nwin = nwin_ref[...]

    def attend(s, k_new, v_cache, v_new):
        s_new = jnp.sum(qr.astype(F32) * k_new.astype(BF16).astype(F32), axis=1, keepdims=True)
        m = jnp.maximum(jnp.max(s, axis=1, keepdims=True), s_new)
        e = jnp.exp(s - m)
        e_new = jnp.exp(s_new - m)
        den = jnp.sum(e, axis=1, keepdims=True) + e_new
        num = _dot(e.astype(BF16), v_cache) + e_new.astype(BF16).astype(F32) * v_new.astype(BF16).astype(F32)
        return num / den

    o_s = attend(_dot_nt(qr, ks_s[...]) + _dot(bias, expand), nkv[:, 2 * w:3 * w], vs_s[...], nkv[:, 3 * w:])
    wb = win_ref.shape[0]
    pos_w = t - wb + lax.broadcasted_iota(jnp.int32, (nh, wb), 1)
    ok_w = (pos_w >= 0) & (t - pos_w <= WINDOW)
    s_w = jnp.where(ok_w, _dot_nt(qr, win_ref[:, :w].astype(BF16)), -MASK_BIG)
    o_w = attend(s_w, nwin[:, :w], win_ref[:, w:].astype(BF16), nwin[:, w:])
    o_ref[...] = gates[:, 0:1] * o_c + gates[:, 1:2] * o_s + gates[:, 2:3] * o_w


def _nsa_step(page_table, cache, win_cache, qr, qn, gates, new_kv, new_win, pe, w1, w2, kc_g, past_len):
    bsz, n_pages = page_table.shape
    page = cache.shape[1]
    w = N_KV_B * HD_B
    nb = n_pages * page // CMP_BLOCK
    wb = win_cache.shape[1]
    per = lambda shape: pl.BlockSpec((None,) + shape, lambda b, pt: (b,) + (0,) * len(shape))
    const = lambda a: pl.BlockSpec(a.shape, lambda b, pt: (0,) * a.ndim)
    page_specs = [pl.BlockSpec((None, page, 4 * w), lambda b, pt, p=p: (pt[b, p], 0, 0)) for p in range(n_pages)]
    return pl.pallas_call(
        functools.partial(_nsa_step_body, n_pages=n_pages, page=page, past_len=past_len),
        out_shape=jax.ShapeDtypeStruct((bsz, N_HEADS_B, w), F32),
        grid_spec=pltpu.PrefetchScalarGridSpec(
            num_scalar_prefetch=1,
            grid=(bsz,),
            in_specs=page_specs + [per((wb, 2 * w)), per((N_HEADS_B, w)), per((N_HEADS_B, w)), per((N_HEADS_B, LANE)),
                                   per((1, 4 * w)), per((1, 2 * w)), const(pe), const(w1), const(w2), const(kc_g)],
            out_specs=per((N_HEADS_B, w)),
            scratch_shapes=[pltpu.VMEM((2 * w // LANE, n_pages * page, LANE), F32),
                            pltpu.VMEM((n_pages * page, w), BF16),
                            pltpu.VMEM((n_pages * page, w), BF16),
                            pltpu.VMEM((2 * N_KV_B, nb, CMP_BLOCK * HD_B), F32)]),
        compiler_params=_params("parallel"),
        name="nsa_step",
    )(page_table, *([cache] * n_pages), win_cache, qr, qn, gates, new_kv, new_win, pe, w1, w2, kc_g)


def _nsa_sample_step(x, g_mix, past_len, kv_cache, win_cache, page_table, w_in, q_g, k_g, pe, w_c1, w_c2, w_out):
    bsz, d = x.shape
    w = N_KV_B * HD_B
    assert past_len % CMP_BLOCK == 0 and past_len // SEL_BLOCK + 1 <= N_SEL_PAD
    wts = _nsa_weights(w_in, q_g, k_g, w_out)
    pos = jnp.full((bsz,), past_len, jnp.int32)
    qcat, gates, kv_rows, win_rows, _, _, _, _ = _nsa_proj(
        x, g_mix, wts, _rope_tables(pos, LANE), _rope_tables(pos, HD_B), 1, bsz)
    q5 = qcat.reshape(bsz, N_KV_B, REP_B, 2, HD_B)
    eye = jnp.eye(N_KV_B, dtype=BF16)
    qrows = (q5[:, :, :, :, None, :] * eye[None, :, None, None, :, None])
    qr = qrows[:, :, :, 0].reshape(bsz, N_HEADS_B, w)
    qn = qrows[:, :, :, 1].reshape(bsz, N_HEADS_B, w)
    gts = gates.reshape(bsz, N_KV_B, LANE)[:, :, :REP_B * 3].reshape(bsz, N_HEADS_B, 3)
    gts = jnp.pad(gts, ((0, 0), (0, 0), (0, LANE - 3)))
    pool, page = kv_cache.shape[:2]
    o = _nsa_step(page_table, kv_cache.reshape(pool, page, 4 * w), win_cache.reshape(bsz, -1, 2 * w), qr, qn, gts,
                  kv_rows.reshape(bsz, 1, 4 * w), win_rows.reshape(bsz, 1, 2 * w),
                  pe.reshape(2, 1, CMP_BLOCK * HD_B), w_c1.reshape(2, CMP_BLOCK * HD_B, HD_B).astype(BF16),
                  w_c2.astype(BF16), k_g[0][None, :], past_len)
    own = (jnp.arange(N_HEADS_B)[:, None] // REP_B == jnp.arange(N_KV_B)[None, :]).astype(F32)
    w_exp = own[:, :, None, None] * w_out.reshape(N_HEADS_B, 1, HD_B, d)
    y = _proj_add(o.reshape(bsz, N_HEADS_B * w), x, w_exp.reshape(N_HEADS_B * w, d).astype(BF16))
    return y, kv_rows.reshape(bsz, 1, N_KV_SLOTS, N_KV_B, HD_B), win_rows.reshape(bsz, 1, 2, N_KV_B, HD_B)


def _mlstm_step_body(q_ref, k_ref, vt_ref, gi_ref, c_ref, n_ref, m_ref, ht_ref, co_ref, no_ref, mo_ref, *, sb):
    lane8 = lax.broadcasted_iota(jnp.int32, (DV_C, N_HEADS_C), 1)
    gi = gi_ref[...]
    logf = jax.nn.log_sigmoid(gi)
    m_all = m_ref[...]
    for s in range(sb):
        ht = jnp.zeros((DV_C, N_HEADS_C), F32)
        m_new_row = jnp.zeros((1, N_HEADS_C), F32)
        lane_m = lax.broadcasted_iota(jnp.int32, (1, N_HEADS_C), 1)
        for h in range(N_HEADS_C):
            q = q_ref[s:s + 1, h * LANE:h * LANE + DK_C].astype(F32)
            k = k_ref[s:s + 1, h * LANE:h * LANE + DK_C].astype(F32)
            v = vt_ref[s, :, h:h + 1]
            c = c_ref[s, h]
            n = n_ref[s, h:h + 1, :]
            it = gi[s:s + 1, h:h + 1]
            b = logf[s:s + 1, N_HEADS_C + h:N_HEADS_C + h + 1]
            m0 = m_all[s:s + 1, h:h + 1]
            inter = b + m0
            m_t = jnp.maximum(inter, it)
            wgt = jnp.exp(it - m_t)
            a = jnp.exp(inter - m_t)
            sc = jnp.sum(q * k, axis=1, keepdims=True) * wgt
            num = a * jnp.sum(c * q, axis=1, keepdims=True) + sc * v
            den = a * jnp.sum(n * q, axis=1, keepdims=True) + sc
            hcol = num / jnp.maximum(jnp.abs(den), jnp.exp(-m_t))
            ht = jnp.where(lane8 == h, hcol, ht)
            co_ref[s, h] = a * c + (wgt * v) * k
            no_ref[s, h:h + 1, :] = a * n + wgt * k
            m_new_row = jnp.where(lane_m == h, m_t, m_new_row)
        ht_ref[s] = ht
        mo_ref[s:s + 1, :] = m_new_row


def _mlstm_step(q, k, vt, gi, c0, n0, m0):
    bsz = q.shape[0]
    sb = 8
    row = lambda n: pl.BlockSpec((sb, n), lambda i: (i, 0))
    c_spec = pl.BlockSpec((sb, N_HEADS_C, DV_C, DK_C), lambda i: (i, 0, 0, 0))
    n_spec = pl.BlockSpec((sb, N_HEADS_C, DK_C), lambda i: (i, 0, 0))
    vt_spec = pl.BlockSpec((sb, DV_C, N_HEADS_C), lambda i: (i, 0, 0))
    return pl.pallas_call(
        functools.partial(_mlstm_step_body, sb=sb),
        out_shape=(jax.ShapeDtypeStruct((bsz, DV_C, N_HEADS_C), F32), jax.ShapeDtypeStruct(c0.shape, F32),
                   jax.ShapeDtypeStruct(n0.shape, F32), jax.ShapeDtypeStruct(m0.shape, F32)),
        grid=(bsz // sb,),
        in_specs=[row(N_HEADS_C * LANE), row(N_HEADS_C * LANE), vt_spec, row(LANE), c_spec, n_spec, row(N_HEADS_C)],
        out_specs=(vt_spec, c_spec, n_spec, row(N_HEADS_C)),
        compiler_params=_params("parallel"),
        name="mlstm_step",
    )(q, k, vt, gi, c0, n0, m0)


def _mlstm_sample_step(x, g_mix, c0, n0, m0, w_in, b_if, h_g, w_out):
    bsz = x.shape[0]
    wts = _mlstm_weights(w_in, b_if)
    q, k, v, gi, og = _mlstm_proj(x, g_mix, wts)
    vt = v.astype(F32).reshape(bsz, N_HEADS_C, DV_C).transpose(0, 2, 1)
    ht, c, n, m = _mlstm_step(q, k, vt, gi, c0, n0, m0)
    hs = ht.transpose(0, 2, 1).reshape(bsz, N_HEADS_C * DV_C)
    y = _mlstm_out(hs, og, h_g[None, :], x, w_out.astype(BF16))
    return y, c, n, m


def kernel(x_prompt, x_sample, cache_nsa_kv, cache_nsa_win, state_mlstm_C, state_mlstm_n, state_mlstm_m, page_table,
           norm_mix_g, norm_ffn_g, ffn_w1, ffn_w2, a_w_in, a_ln_g, a_ln_b, a_w_s, a_b_s, a_w_out,
           b_w_in, b_q_g, b_k_g, b_pe, b_w_c1, b_w_c2, b_w_out, c_w_in, c_b_if, c_h_g, c_w_out):
    bp, t, d = x_prompt.shape
    bs, ts, _ = x_sample.shape
    assert ts == 1
    past_len = page_table.shape[1] * cache_nsa_kv.shape[2]
    xp = x_prompt.reshape(bp * t, d)
    xs = x_sample.reshape(bs, d)
    out = {k: [] for k in ("v_s", "kv_p", "win_p", "kv_s", "win_s", "C_p", "n_p", "m_p", "C_s", "n_s", "m_s")}
    for layer in range(norm_mix_g.shape[0]):
        kind, j = layer % 3, layer // 3
        gm = norm_mix_g[layer]
        if kind == 0:
            args = (a_w_in[j], a_ln_g[j], a_ln_b[j], a_w_s[j], a_b_s[j], a_w_out[j])
            xp = _gmlp_layer(xp, gm, *args, single=False)[0]
            xs, v = _gmlp_layer(xs, gm, *args, single=True)
            out["v_s"].append(v.reshape(bs, ts, -1))
        elif kind == 1:
            args = (b_w_in[j], b_q_g[j], b_k_g[j], b_pe[j], b_w_c1[j], b_w_c2[j], b_w_out[j])
            xp, kv, win = _nsa_prompt(xp, gm[None, :], bp, t, *args)
            out["kv_p"].append(kv)
            out["win_p"].append(win)
            xs, kv, win = _nsa_sample_step(xs, gm[None, :], past_len, cache_nsa_kv[j], cache_nsa_win[j], page_table,
                                           *args)
            out["kv_s"].append(kv)
            out["win_s"].append(win)
        else:
            args = (c_w_in[j], c_b_if[j], c_h_g[j], c_w_out[j])
            xp, c, n, m = _mlstm_prompt(xp, gm[None, :], bp, t, *args)
            out["C_p"].append(c)
            out["n_p"].append(n)
            out["m_p"].append(m)
            xs, c, n, m = _mlstm_sample_step(xs, gm[None, :], state_mlstm_C[j], state_mlstm_n[j], state_mlstm_m[j],
                                             *args)
            out["C_s"].append(c)
            out["n_s"].append(n)
            out["m_s"].append(m)
        gf = norm_ffn_g[layer][None, :]
        w1, w2 = ffn_w1[layer].astype(BF16), ffn_w2[layer].astype(BF16)
        xp = _ffn(xp, gf, w1, w2)
        xs = _ffn(xs, gf, w1, w2)
    st = {k: jnp.stack(v) for k, v in out.items()}
    return (xp.reshape(bp, t, d), xs.reshape(bs, ts, d), st["v_s"], st["kv_p"], st["win_p"], st["kv_s"], st["win_s"],
            st["C_p"], st["n_p"], st["m_p"], st["C_s"], st["n_s"], st["m_s"])
```

```python
import functools
import math

import jax
import jax.numpy as jnp
from jax import lax
from jax.experimental import pallas as pl
from jax.experimental.pallas import tpu as pltpu

F32 = jnp.float32
BF16 = jnp.bfloat16

EPS = 1e-6
CHUNK_A = 128
N_GROUPS_A = 8
N_HEADS_B = 16
N_KV_B = 4
REP_B = N_HEADS_B // N_KV_B
HD_B = 64
ROT_DIM = 16
ROPE_THETA = 500000.0
CMP_BLOCK = 32
SEL_BLOCK = 64
N_SEL = 16
WINDOW = 512
N_KV_SLOTS = 4
N_HEADS_C = 8
DK_C = 64
DV_C = 128
CHUNK_C = 128
SCALE_B = HD_B ** -0.5

LANE = 128
VMEM_LIMIT_BYTES = 56 * 1024 * 1024
MASK_BIG = 1e30
M_INIT = -1e20
N_SEL_PAD = 128
N_CMP_PAD = 2 * N_SEL_PAD
SHIFT_MAX = 40.0
FLASH_SEL_TILE = 512
FLASH_WIN_TILE = 512


def _params(*sem):
    return pltpu.CompilerParams(dimension_semantics=sem, vmem_limit_bytes=VMEM_LIMIT_BYTES)


def _dot(a, b):
    return jnp.dot(a, b, preferred_element_type=F32)


def _dot_nt(a, b):
    return lax.dot_general(a, b, (((1,), (1,)), ((), ())), preferred_element_type=F32)


def _dot_tn(a, b):
    return lax.dot_general(a, b, (((0,), (0,)), ((), ())), preferred_element_type=F32)


def _rms(x, g):
    return x * lax.rsqrt(jnp.mean(x * x, axis=-1, keepdims=True) + EPS) * g


def _split3(x):
    a = x.astype(BF16)
    r = x - a.astype(F32)
    b = r.astype(BF16)
    c = (r - b.astype(F32)).astype(BF16)
    return a, b, c


def _const_spec(shape):
    n = len(shape)
    return pl.BlockSpec(shape, lambda *_: (0,) * n)


def _row_tile(m, pref):
    t = min(pref, m)
    while m % t:
        t //= 2
    return t


def _ffn_body(x_ref, g_ref, w1_ref, w2_ref, o_ref, *, ck):
    x = x_ref[...]
    xb = _rms(x, g_ref[...]).astype(BF16)
    acc = x
    for j in range(w1_ref.shape[1] // ck):
        h = jnp.maximum(_dot(xb, w1_ref[:, j * ck:(j + 1) * ck]), 0.0)
        acc = acc + _dot((h * h).astype(BF16), w2_ref[j * ck:(j + 1) * ck, :])
    o_ref[...] = acc


def _ffn(x, g, w1, w2):
    m, d = x.shape
    tm = _row_tile(m, 512)
    return pl.pallas_call(
        functools.partial(_ffn_body, ck=1024),
        out_shape=jax.ShapeDtypeStruct((m, d), F32),
        grid=(m // tm,),
        in_specs=[pl.BlockSpec((tm, d), lambda i: (i, 0)), _const_spec(g.shape),
                  _const_spec(w1.shape), _const_spec(w2.shape)],
        out_specs=pl.BlockSpec((tm, d), lambda i: (i, 0)),
        compiler_params=_params("parallel"),
        name="ffn",
    )(x, g, w1, w2)


def _gmlp_body(x_ref, g_ref, win_ref, lng_ref, lnb_ref, ws_ref, bs_ref, wout_ref, o_ref, *maybe_v_ref, single):
    x = x_ref[...]
    dg = lng_ref.shape[1]
    xb = _rms(x, g_ref[...]).astype(BF16)
    u = jax.nn.gelu(_dot(xb, win_ref[:, :dg]))
    v = jax.nn.gelu(_dot(xb, win_ref[:, dg:]))
    mu = jnp.mean(v, axis=-1, keepdims=True)
    vc = v - mu
    var = jnp.mean(vc * vc, axis=-1, keepdims=True)
    v = vc * lax.rsqrt(var + EPS) * lng_ref[...] + lnb_ref[...]
    if single:
        maybe_v_ref[0][...] = v
        gate = v * ws_ref[...] + bs_ref[...]
    else:
        gw = dg // N_GROUPS_A
        row = lax.broadcasted_iota(jnp.int32, (CHUNK_A, CHUNK_A), 0)
        col = lax.broadcasted_iota(jnp.int32, (CHUNK_A, CHUNK_A), 1)
        causal = col <= row
        vb = v.astype(BF16)
        chunks = []
        for c in range(x.shape[0] // CHUNK_A):
            parts = []
            for gi in range(N_GROUPS_A):
                w = jnp.where(causal, ws_ref[gi], 0.0).astype(BF16)
                parts.append(_dot(w, vb[c * CHUNK_A:(c + 1) * CHUNK_A, gi * gw:(gi + 1) * gw]))
            chunks.append(jnp.concatenate(parts, axis=1) + bs_ref[...])
        gate = jnp.concatenate(chunks, axis=0)
    o_ref[...] = x + _dot((u * gate).astype(BF16), wout_ref[...])


def _gmlp(x, g, w_in, ln_g, ln_b, ws, bs, w_out, *, single):
    m, d = x.shape
    dg = w_out.shape[0]
    tm = _row_tile(m, 256)
    n_out = 2 if single else 1
    outs = pl.pallas_call(
        functools.partial(_gmlp_body, single=single),
        out_shape=(jax.ShapeDtypeStruct((m, d), F32), jax.ShapeDtypeStruct((m, dg), F32))[:n_out],
        grid=(m // tm,),
        in_specs=[pl.BlockSpec((tm, d), lambda i: (i, 0)), _const_spec(g.shape), _const_spec(w_in.shape),
                  _const_spec(ln_g.shape), _const_spec(ln_b.shape), _const_spec(ws.shape), _const_spec(bs.shape),
                  _const_spec(w_out.shape)],
        out_specs=(pl.BlockSpec((tm, d), lambda i: (i, 0)), pl.BlockSpec((tm, dg), lambda i: (i, 0)))[:n_out],
        compiler_params=_params("parallel"),
        name="gmlp_single" if single else "gmlp",
    )(x, g, w_in, ln_g, ln_b, ws, bs, w_out)
    return outs if single else (outs[0], None)


def _rope_tables(pos, seg):
    half = ROT_DIM // 2
    freq = jnp.power(ROPE_THETA, -jnp.arange(half, dtype=F32) * 2.0 / ROT_DIM)
    ang = pos.astype(F32)[:, None] * freq[None, :]
    cos, sin = jnp.cos(ang), jnp.sin(ang)
    t = pos.shape[0]
    one = jnp.ones((t, seg - ROT_DIM), F32)
    zero = jnp.zeros((t, seg - ROT_DIM), F32)
    z8 = jnp.zeros((t, half), F32)
    tabs = [jnp.concatenate([cos, cos, one], 1), jnp.concatenate([-sin, z8, zero], 1),
            jnp.concatenate([z8, sin, zero], 1)]
    return jnp.stack([jnp.tile(a, (1, LANE // seg)) for a in tabs])


def _rope128(x, tab):
    return x * tab[0] + pltpu.roll(x, LANE - ROT_DIM // 2, 1) * tab[1] + pltpu.roll(x, ROT_DIM // 2, 1) * tab[2]


def _nsa_proj_body(x_ref, g_ref, wq_ref, wg_ref, wkv_ref, seg_ref, spread_ref, qg_ref, kg_ref, tq_ref, tk_ref,
                   qcat_ref, gates_ref, kv_ref, win_ref, ks_ref, vs_ref, kw_ref, vw_ref):
    x = x_ref[...]
    xb = _rms(x, g_ref[...]).astype(BF16)
    tq = tq_ref[...]
    tk = tk_ref[...]
    qg = qg_ref[...]
    for h in range(N_HEADS_B):
        q = _dot(xb, wq_ref[:, h * LANE:(h + 1) * LANE])
        ms = jnp.sum(q * q, axis=-1, keepdims=True) * (1.0 / LANE)
        qn = q * lax.rsqrt(ms + EPS) * qg
        qcat_ref[:, h * LANE:(h + 1) * LANE] = (_rope128(qn, tq) * SCALE_B).astype(BF16)
    gates_ref[...] = jax.nn.sigmoid(_dot(xb, wg_ref[...]))
    kv = _dot(xb, wkv_ref[...])
    w = N_KV_B * HD_B
    seg = seg_ref[...]
    spread = spread_ref[...]

    def head_norm(k, gain):
        k2 = k * k
        hi = k2.astype(BF16)
        lo = (k2 - hi.astype(F32)).astype(BF16)
        ss = _dot(hi, seg) + _dot(lo, seg)
        return k * lax.rsqrt(ss * (1.0 / HD_B) + EPS) * gain

    def rope(k):
        return jnp.concatenate([_rope128(k[:, j * LANE:(j + 1) * LANE], tk) for j in range(w // LANE)], axis=1)

    ks = rope(head_norm(kv[:, 2 * w:3 * w], kg_ref[0:1, :]))
    kw = rope(head_norm(kv[:, 4 * w:5 * w], kg_ref[1:2, :]))
    vs = kv[:, 3 * w:4 * w]
    vw = kv[:, 5 * w:6 * w]
    kv_ref[:, :2 * w] = kv[:, :2 * w]
    kv_ref[:, 2 * w:3 * w] = ks
    kv_ref[:, 3 * w:] = vs
    win_ref[:, :w] = kw
    win_ref[:, w:] = vw
    lane = lax.broadcasted_iota(jnp.int32, (1, N_KV_B * LANE), 1)
    ones_hi = ((lane & HD_B) != 0).astype(F32)
    ks_ref[...] = _dot(ks.astype(BF16), spread).astype(BF16)
    kw_ref[...] = _dot(kw.astype(BF16), spread).astype(BF16)
    vs_ref[...] = (_dot(vs.astype(BF16), spread) + ones_hi).astype(BF16)
    vw_ref[...] = (_dot(vw.astype(BF16), spread) + ones_hi).astype(BF16)


def _nsa_proj(x, g, wts, tab_q, tab_k, n_tab_tiles, tm):
    m, d = x.shape
    w = N_KV_B * HD_B
    ws = N_KV_B * LANE
    tile = lambda n: pl.BlockSpec((tm, n), lambda i: (i, 0))
    tab = pl.BlockSpec((3, tm, LANE), lambda i: (0, i % n_tab_tiles, 0))
    consts = [g, wts["wq"], wts["wg"], wts["wkv"], wts["seg"], wts["spread"], wts["qg"], wts["kg"]]
    return pl.pallas_call(
        _nsa_proj_body,
        out_shape=(jax.ShapeDtypeStruct((m, N_HEADS_B * LANE), BF16), jax.ShapeDtypeStruct((m, ws), F32),
                   jax.ShapeDtypeStruct((m, 4 * w), F32), jax.ShapeDtypeStruct((m, 2 * w), F32),
                   jax.ShapeDtypeStruct((m, ws), BF16), jax.ShapeDtypeStruct((m, ws), BF16),
                   jax.ShapeDtypeStruct((m, ws), BF16), jax.ShapeDtypeStruct((m, ws), BF16)),
        grid=(m // tm,),
        in_specs=[tile(d)] + [_const_spec(c.shape) for c in consts] + [tab, tab],
        out_specs=(tile(N_HEADS_B * LANE), tile(ws), tile(4 * w), tile(2 * w), tile(ws), tile(ws), tile(ws), tile(ws)),
        compiler_params=_params("parallel"),
        name="nsa_proj",
    )(x, *consts, tab_q, tab_k)


def _compress_body(x_ref, pe_ref, w1_ref, w2_ref, g_ref, o_ref, *, norm):
    xb = (x_ref[...] + pe_ref[...]).astype(BF16)
    hid = jax.nn.gelu(_dot(xb, w1_ref[...]))
    y = _dot(hid.astype(BF16), w2_ref[...])
    if norm:
        y = _rms(y, g_ref[...])
    o_ref[...] = y


def _compress(x, pe, w1, w2, g, *, norm):
    r, kdim = x.shape
    tr = _row_tile(r, 512)
    return pl.pallas_call(
        functools.partial(_compress_body, norm=norm),
        out_shape=jax.ShapeDtypeStruct((r, HD_B), F32),
        grid=(r // tr,),
        in_specs=[pl.BlockSpec((tr, kdim), lambda i: (i, 0)), _const_spec(pe.shape), _const_spec(w1.shape),
                  _const_spec(w2.shape), _const_spec(g.shape)],
        out_specs=pl.BlockSpec((tr, HD_B), lambda i: (i, 0)),
        compiler_params=_params("parallel"),
        name="nsa_compress_k" if norm else "nsa_compress_v",
    )(x, pe, w1, w2, g)


def _masked_softmax(s, mask, axis):
    sm = jnp.where(mask, s, -jnp.inf)
    mx = jnp.max(sm, axis=axis, keepdims=True)
    mx = jnp.where(mx > -jnp.inf, mx, 0.0)
    e = jnp.where(mask, jnp.exp(s - mx), 0.0)
    return e / jnp.maximum(jnp.sum(e, axis=axis, keepdims=True), 1e-30)


def _cmp_select_body(q_ref, kc_ref, kct_ref, vc_ref, gates_ref, o_ref, mnot_ref, *, tq):
    q0 = pl.program_id(2) * tq
    kc = kc_ref[...]
    kct = kct_ref[...]
    vc = vc_ref[...]
    gates = gates_ref[...]
    t_r = q0 + lax.broadcasted_iota(jnp.int32, (tq, N_CMP_PAD), 0)
    blk_r = lax.broadcasted_iota(jnp.int32, (tq, N_CMP_PAD), 1)
    mask_r = (blk_r + 1) * CMP_BLOCK - 1 <= t_r
    row_c = lax.broadcasted_iota(jnp.int32, (N_CMP_PAD, tq), 0)
    t_c = q0 + lax.broadcasted_iota(jnp.int32, (N_CMP_PAD, tq), 1)
    blk_c = jnp.where(row_c < N_SEL_PAD, 2 * row_c, 2 * (row_c - N_SEL_PAD) + 1)
    mask_c = (blk_c + 1) * CMP_BLOCK - 1 <= t_c
    imp = jnp.zeros((N_SEL_PAD, tq), F32)
    for r in range(REP_B):
        qh = q_ref[:, r * LANE:(r + 1) * LANE]
        p = _masked_softmax(_dot_nt(qh, kc), mask_r, 1)
        o = _dot(p.astype(BF16), vc)
        o_ref[:, r * LANE:(r + 1) * LANE] = o * gates[:, 3 * r:3 * r + 1]
        pt = _masked_softmax(_dot_nt(kct, qh), mask_c, 0)
        imp = imp + pt[:N_SEL_PAD] + pt[N_SEL_PAD:]
    blk = lax.broadcasted_iota(jnp.int32, (N_SEL_PAD, tq), 0)
    t_s = q0 + lax.broadcasted_iota(jnp.int32, (N_SEL_PAD, tq), 1)
    cur = t_s // SEL_BLOCK
    forced = (blk == 0) | (blk == cur) | (blk == cur - 1)
    valid = blk * SEL_BLOCK <= t_s
    score = jnp.where(forced, jnp.inf, jnp.where(valid, imp, -jnp.inf))
    blk_f = blk.astype(F32)
    pickable = score > -jnp.inf
    for _ in range(N_SEL):
        mx = jnp.max(score, axis=0, keepdims=True)
        first = jnp.min(jnp.where(score == mx, blk_f, float(N_SEL_PAD)), axis=0, keepdims=True)
        score = jnp.where(blk_f == first, -jnp.inf, score)
    mnot_ref[...] = jnp.where(pickable, jnp.where(score > -jnp.inf, 1.0, 0.0), 1.0).T.astype(BF16)


def _cmp_select(qcat, kc, kct, vc, gates, b, t, tq):
    m = b * t
    nq = t // tq
    blk = pl.BlockSpec((None, None, N_CMP_PAD, LANE), lambda bi, g, i: (bi, g, 0, 0))
    return pl.pallas_call(
        functools.partial(_cmp_select_body, tq=tq),
        out_shape=(jax.ShapeDtypeStruct((m, N_HEADS_B * LANE), F32), jax.ShapeDtypeStruct((m, N_KV_B * LANE), BF16)),
        grid=(b, N_KV_B, nq),
        in_specs=[pl.BlockSpec((tq, REP_B * LANE), lambda bi, g, i: (bi * nq + i, g)), blk, blk, blk,
                  pl.BlockSpec((tq, LANE), lambda bi, g, i: (bi * nq + i, g))],
        out_specs=(pl.BlockSpec((tq, REP_B * LANE), lambda bi, g, i: (bi * nq + i, g)),
                   pl.BlockSpec((tq, LANE), lambda bi, g, i: (bi * nq + i, g))),
        compiler_params=_params("parallel", "parallel", "parallel"),
        name="nsa_cmp_select",
    )(qcat, kc, kct, vc, gates)


def _flash_body(tab_ref, bound_ref, q_ref, mnot_ref, k_ref, v_ref, gates_ref, o_ref, qs_ref, ks_ref, m_ref, acc_ref, *,
                tq, tk, sel, gate_col, fixed):
    step_id = pl.program_id(2)
    i = tab_ref[0, step_id]
    j = tab_ref[1, step_id]
    rows = REP_B * tq
    first = j == (0 if sel else jnp.maximum(i - WINDOW // tk, 0))

    @pl.when(first)
    def _():
        if not fixed:
            m_ref[...] = jnp.full(m_ref.shape, M_INIT, F32)
        acc_ref[...] = jnp.zeros(acc_ref.shape, F32)
        lane = lax.broadcasted_iota(jnp.int32, (tq, LANE), 1)
        for r in range(REP_B):
            qh = q_ref[:, r * LANE:(r + 1) * LANE]
            if fixed:
                qh = jnp.where(lane < HD_B, qh, jnp.ones_like(qh))
            if sel:
                qs_ref[r * tq:(r + 1) * tq, :LANE] = mnot_ref[...]
                qs_ref[r * tq:(r + 1) * tq, LANE:] = qh
            else:
                qs_ref[r * tq:(r + 1) * tq, :] = qh

    def step(masked):
        kt = k_ref[...]
        if fixed:
            lane = lax.broadcasted_iota(jnp.int32, (tk, LANE), 1)
            kt = jnp.where(lane == HD_B, -bound_ref[0], kt.astype(F32)).astype(BF16)
        if sel:
            kpos = j * tk + lax.broadcasted_iota(jnp.int32, (tk, LANE), 0)
            lane = lax.broadcasted_iota(jnp.int32, (tk, LANE), 1)
            ks_ref[:, :LANE] = jnp.where(kpos // SEL_BLOCK == lane, -MASK_BIG, 0.0).astype(BF16)
            ks_ref[:, LANE:] = kt
            kmat = ks_ref[...]
        else:
            kmat = kt
        s = _dot_nt(qs_ref[...], kmat)
        if masked is not None:
            t = i * tq + lax.broadcasted_iota(jnp.int32, (rows, tk), 0) % tq
            key = j * tk + lax.broadcasted_iota(jnp.int32, (rows, tk), 1)
            ok = key <= t if masked == "causal" else t - key <= WINDOW
            s = jnp.where(ok, s, -MASK_BIG)
        if fixed:
            acc_ref[...] += _dot(jnp.exp(s).astype(BF16), v_ref[...])
        else:
            m_old = m_ref[...]
            m_new = jnp.maximum(m_old, jnp.max(s, axis=-1, keepdims=True))
            p = jnp.exp(s - m_new[:, :1])
            acc_ref[...] = jnp.exp(m_old - m_new) * acc_ref[...] + _dot(p.astype(BF16), v_ref[...])
            m_ref[...] = m_new

    if sel:
        pl.when(j < i)(lambda: step(None))
    else:
        pl.when(j == i - WINDOW // tk)(lambda: step("band"))
        pl.when((j < i) & (j > i - WINDOW // tk))(lambda: step(None))

    @pl.when(j == i)
    def _():
        step("causal")
        gates = gates_ref[...]
        lane = lax.broadcasted_iota(jnp.int32, (tq, LANE), 1)
        for r in range(REP_B):
            a = acc_ref[r * tq:(r + 1) * tq, :]
            o = a / a[:, HD_B:HD_B + 1]
            g = gates[:, 3 * r + gate_col:3 * r + gate_col + 1]
            o_ref[:, r * LANE:(r + 1) * LANE] = jnp.where(lane < HD_B, o * g, 0.0)


def _flash(qcat, mnot, k, v, gates, bound, b, t, tq, sel):
    m = b * t
    nq = t // tq
    tk = tq
    assert WINDOW % tk == 0
    lo = (lambda i: 0) if sel else (lambda i: max(i - WINDOW // tk, 0))
    pairs = [(i, j) for i in range(nq) for j in range(lo(i), i + 1)]
    tab = jnp.asarray(pairs, jnp.int32).T
    kdim = 2 * LANE if sel else LANE
    qidx = lambda bi, g, p, *pf: (bi * nq + pf[0][0, p], g)
    kidx = lambda bi, g, p, *pf: (bi * nq + pf[0][1, p], g)

    def call(fixed):
        name = ("nsa_flash_sel" if sel else "nsa_flash_win") + ("" if fixed else "_online")
        return pl.pallas_call(
            functools.partial(_flash_body, tq=tq, tk=tk, sel=sel, gate_col=1 if sel else 2, fixed=fixed),
            out_shape=jax.ShapeDtypeStruct((m, N_HEADS_B * LANE), F32),
            grid_spec=pltpu.PrefetchScalarGridSpec(
                num_scalar_prefetch=2,
                grid=(b, N_KV_B, len(pairs)),
                in_specs=[pl.BlockSpec((tq, REP_B * LANE), qidx), pl.BlockSpec((tq, LANE), qidx),
                          pl.BlockSpec((tk, LANE), kidx), pl.BlockSpec((tk, LANE), kidx),
                          pl.BlockSpec((tq, LANE), qidx)],
                out_specs=pl.BlockSpec((tq, REP_B * LANE), qidx),
                scratch_shapes=[pltpu.VMEM((REP_B * tq, kdim), BF16), pltpu.VMEM((tk, 2 * LANE), BF16),
                                pltpu.VMEM((REP_B * tq, LANE), F32), pltpu.VMEM((REP_B * tq, LANE), F32)]),
            compiler_params=_params("parallel", "parallel", "arbitrary"),
            name=name,
        )(tab, bound.reshape(1), qcat, mnot, k, v, gates)

    return lax.cond(bound <= SHIFT_MAX, lambda: call(True), lambda: call(False))


def _sum_proj_body(a_ref, b_ref, c_ref, x_ref, w_ref, o_ref):
    o = (a_ref[...] + b_ref[...] + c_ref[...]).astype(BF16)
    o_ref[...] = x_ref[...] + _dot(o, w_ref[...])


def _sum_proj(a, b, c, x, w):
    m, d = x.shape
    kdim = a.shape[1]
    tm = _row_tile(m, 512)
    big = pl.BlockSpec((tm, kdim), lambda i: (i, 0))
    row = pl.BlockSpec((tm, d), lambda i: (i, 0))
    return pl.pallas_call(
        _sum_proj_body,
        out_shape=jax.ShapeDtypeStruct((m, d), F32),
        grid=(m // tm,),
        in_specs=[big, big, big, row, _const_spec(w.shape)],
        out_specs=row,
        compiler_params=_params("parallel"),
        name="nsa_out_proj",
    )(a, b, c, x, w)


def _nsa_weights(w_in, q_g, k_g, w_out):
    d = w_in.shape[0]
    nq = N_HEADS_B * HD_B
    w = N_KV_B * HD_B
    wq = w_in[:, :nq].reshape(d, N_HEADS_B, 1, HD_B)
    wq = jnp.broadcast_to(wq, (d, N_HEADS_B, 2, HD_B)).reshape(d, N_HEADS_B * LANE)
    wg = w_in[:, nq:nq + 3 * N_HEADS_B].reshape(d, N_KV_B, REP_B * 3)
    wg = jnp.pad(wg, ((0, 0), (0, 0), (0, LANE - REP_B * 3))).reshape(d, N_KV_B * LANE)
    wkv = w_in[:, nq + 3 * N_HEADS_B:]
    lane = jnp.arange(w)
    seg = (lane[:, None] // HD_B == lane[None, :] // HD_B).astype(BF16)
    spread = (lane[:, None] // HD_B * LANE + lane[:, None] % HD_B == jnp.arange(N_KV_B * LANE)[None, :]).astype(BF16)
    wo = jnp.pad(w_out.reshape(N_HEADS_B, HD_B, -1), ((0, 0), (0, LANE - HD_B), (0, 0)))
    return {
        "wq": wq.astype(BF16), "wg": wg.astype(BF16), "wkv": wkv.astype(BF16), "seg": seg, "spread": spread,
        "qg": jnp.tile(q_g, 2)[None, :], "kg": jnp.stack([jnp.tile(k_g[1], N_KV_B), jnp.tile(k_g[2], N_KV_B)]),
        "wo": wo.reshape(N_HEADS_B * LANE, -1).astype(BF16),
    }


def _cmp_inputs(rows, b, length):
    nb = length // CMP_BLOCK
    x = rows.reshape(b, length, N_KV_B, HD_B)[:, :nb * CMP_BLOCK].reshape(b, nb, CMP_BLOCK, N_KV_B, HD_B)
    return x.transpose(0, 1, 3, 2, 4).reshape(b * nb * N_KV_B, CMP_BLOCK * HD_B)


def _cmp_blocks(kc_rows, vc_rows, b, length, pe, w_c1, w_c2, kc_g):
    nb = length // CMP_BLOCK
    pe_k, pe_v = pe[0].reshape(1, -1), pe[1].reshape(1, -1)
    w1k = w_c1[0].reshape(CMP_BLOCK * HD_B, HD_B).astype(BF16)
    w1v = w_c1[1].reshape(CMP_BLOCK * HD_B, HD_B).astype(BF16)
    g = kc_g[None, :]
    kc = _compress(_cmp_inputs(kc_rows, b, length), pe_k, w1k, w_c2[0].astype(BF16), g, norm=True)
    vc = _compress(_cmp_inputs(vc_rows, b, length), pe_v, w1v, w_c2[1].astype(BF16), g, norm=False)
    return kc.reshape(b, nb, N_KV_B, HD_B), vc.reshape(b, nb, N_KV_B, HD_B)


def _nsa_prompt(x, g_mix, b, t, w_in, q_g, k_g, pe, w_c1, w_c2, w_out):
    wts = _nsa_weights(w_in, q_g, k_g, w_out)
    w = N_KV_B * HD_B
    tm = 256
    pos = jnp.arange(t)
    qcat, gates, kv_rows, win_rows, ks_s, vs_s, kw_s, vw_s = _nsa_proj(
        x, g_mix, wts, _rope_tables(pos, LANE), _rope_tables(pos, HD_B), t // tm, tm)
    kc_blk, vc_blk = _cmp_blocks(kv_rows[:, :w], kv_rows[:, w:2 * w], b, t, pe, w_c1, w_c2, k_g[0])
    nb = t // CMP_BLOCK
    assert nb <= N_CMP_PAD and t % SEL_BLOCK == 0

    def blocks(a, lo):
        a = jnp.pad(a.transpose(0, 2, 1, 3), ((0, 0), (0, 0), (0, N_CMP_PAD - nb), (lo, LANE - HD_B - lo)))
        return a.astype(BF16)

    kc = blocks(kc_blk, HD_B)
    kct = jnp.concatenate([kc[:, :, 0::2], kc[:, :, 1::2]], axis=2)
    vc = blocks(vc_blk, 0)
    tq = 256
    o_cmp, mnot = _cmp_select(qcat, kc, kct, vc, gates, b, t, tq)
    qmax = jnp.max(jnp.abs(q_g))
    o_sel = _flash(qcat, mnot, ks_s, vs_s, gates, qmax * jnp.max(jnp.abs(k_g[1])) * math.sqrt(HD_B), b, t,
                   min(FLASH_SEL_TILE, t), True)
    o_win = _flash(qcat, mnot, kw_s, vw_s, gates, qmax * jnp.max(jnp.abs(k_g[2])) * math.sqrt(HD_B), b, t,
                   min(FLASH_WIN_TILE, t), False)
    y = _sum_proj(o_cmp, o_sel, o_win, x, wts["wo"])
    wb = min(WINDOW, t)
    kv_out = kv_rows.reshape(b, t, N_KV_SLOTS, N_KV_B, HD_B)
    win_out = win_rows.reshape(b, t, 2, N_KV_B, HD_B)[:, t - wb:]
    return y, kv_out, win_out


def _gmlp_layer(x, g, w_in, ln_g, ln_b, w_s, b_s, w_out, *, single):
    gw = w_out.shape[0] // N_GROUPS_A
    if single:
        ws = jnp.repeat(w_s[:, 0, 0], gw)[None, :]
        bs = jnp.repeat(b_s[:, 0], gw)[None, :]
    else:
        ws = w_s
        bs = jnp.repeat(b_s.T, gw, axis=1)
    return _gmlp(x, g[None, :], w_in.astype(BF16), ln_g[None, :], ln_b[None, :], ws, bs, w_out.astype(BF16),
                 single=single)


def _mlstm_proj_body(x_ref, g_ref, wq_ref, wk_ref, wv_ref, wgi_ref, wo_ref, bif_ref,
                     q_ref, k_ref, v_ref, gi_ref, og_ref):
    xb = _rms(x_ref[...], g_ref[...]).astype(BF16)
    q_ref[...] = _dot(xb, wq_ref[...]).astype(BF16)
    k_ref[...] = _dot(xb, wk_ref[...]).astype(BF16)
    v_ref[...] = _dot(xb, wv_ref[...]).astype(BF16)
    gi_ref[...] = _dot(xb, wgi_ref[...]) + bif_ref[...]
    og_ref[...] = jax.nn.sigmoid(_dot(xb, wo_ref[...]))


def _mlstm_proj(x, g, wts):
    m, d = x.shape
    hv = N_HEADS_C * DV_C
    tm = _row_tile(m, 512)
    consts = [g, wts["wq"], wts["wk"], wts["wv"], wts["wgi"], wts["wo"], wts["bif"]]
    tile = lambda n: pl.BlockSpec((tm, n), lambda i: (i, 0))
    return pl.pallas_call(
        _mlstm_proj_body,
        out_shape=(jax.ShapeDtypeStruct((m, N_HEADS_C * LANE), BF16), jax.ShapeDtypeStruct((m, N_HEADS_C * LANE), BF16),
                   jax.ShapeDtypeStruct((m, hv), BF16), jax.ShapeDtypeStruct((m, LANE), F32),
                   jax.ShapeDtypeStruct((m, hv), F32)),
        grid=(m // tm,),
        in_specs=[tile(d)] + [_const_spec(c.shape) for c in consts],
        out_specs=(tile(N_HEADS_C * LANE), tile(N_HEADS_C * LANE), tile(hv), tile(LANE), tile(hv)),
        compiler_params=_params("parallel"),
        name="mlstm_proj",
    )(x, *consts)


def _mlstm_scan_body(q_ref, k_ref, v_ref, gi_ref, git_ref, hs_ref, c_out, n_out, m_out, c_s, n_s, m_s):
    c = pl.program_id(1)
    L = q_ref.shape[0]

    @pl.when(c == 0)
    def _():
        c_s[...] = jnp.zeros(c_s.shape, F32)
        n_s[...] = jnp.zeros(n_s.shape, F32)
        m_s[...] = jnp.zeros(m_s.shape, F32)

    row = lax.broadcasted_iota(jnp.int32, (L, L), 0)
    col = lax.broadcasted_iota(jnp.int32, (L, L), 1)
    causal = col <= row
    tril = causal.astype(BF16)
    gi = gi_ref[...]
    git = git_ref[...]
    fcol = jax.nn.log_sigmoid(gi)
    frow = jax.nn.log_sigmoid(git[N_HEADS_C:, :])
    bcol_all = sum(_dot(tril, part) for part in _split3(fcol))
    brow_all = sum(_dot_nt(part, tril) for part in _split3(frow))
    for h in range(N_HEADS_C):
        sl = slice(h * LANE, (h + 1) * LANE)
        q = q_ref[:, sl]
        k = k_ref[:, sl]
        v = v_ref[:, sl]
        bcol = bcol_all[:, N_HEADS_C + h:N_HEADS_C + h + 1]
        icol = gi[:, h:h + 1]
        brow = brow_all[h:h + 1, :]
        irow = git[h:h + 1, :]
        m_prev = m_s[h:h + 1, 0:1]
        dlog = jnp.where(causal, bcol - brow + irow, -jnp.inf)
        inter = bcol + m_prev
        m_t = jnp.maximum(inter, jnp.max(dlog, axis=1, keepdims=True))
        w = jnp.exp(dlog - m_t)
        a = jnp.exp(inter - m_t)
        s = _dot_nt(q, k) * w
        cq = _dot_nt(q, c_s[h].astype(BF16))
        num = a * cq + _dot(s.astype(BF16), v)
        nq = jnp.sum(q.astype(F32) * n_s[h:h + 1, :], axis=1, keepdims=True)
        den = a * nq + jnp.sum(s, axis=1, keepdims=True)
        hs_ref[:, sl] = num / jnp.maximum(jnp.abs(den), jnp.exp(-m_t))
        b_end = bcol[L - 1:L, :]
        wlog = b_end - bcol + icol
        m_new = jnp.maximum(b_end + m_prev, jnp.max(wlog, axis=0, keepdims=True))
        wk = jnp.exp(wlog - m_new)
        decay = jnp.exp(b_end + m_prev - m_new)
        vw = (v.astype(F32) * wk).astype(BF16)
        c_s[h] = decay * c_s[h] + _dot_tn(vw, k)
        n_s[h:h + 1, :] = decay * n_s[h:h + 1, :] + jnp.sum(k.astype(F32) * wk, axis=0, keepdims=True)
        m_s[h:h + 1, :] = jnp.broadcast_to(m_new, (1, LANE))

    @pl.when(c == pl.num_programs(1) - 1)
    def _():
        c_out[...] = c_s[...]
        n_out[...] = n_s[...]
        m_out[...] = m_s[...]


def _mlstm_scan(q, k, v, gi, git, b, t):
    L = math.gcd(t, CHUNK_C)
    nc = t // L
    hv = N_HEADS_C * DV_C
    tile = lambda n: pl.BlockSpec((L, n), lambda bi, c: (bi * nc + c, 0))
    return pl.pallas_call(
        _mlstm_scan_body,
        out_shape=(jax.ShapeDtypeStruct((b * t, hv), F32),
                   jax.ShapeDtypeStruct((b, N_HEADS_C, DV_C, LANE), F32),
                   jax.ShapeDtypeStruct((b, N_HEADS_C, LANE), F32), jax.ShapeDtypeStruct((b, N_HEADS_C, LANE), F32)),
        grid=(b, nc),
        in_specs=[tile(N_HEADS_C * LANE), tile(N_HEADS_C * LANE), tile(hv), tile(LANE),
                  pl.BlockSpec((None, 2 * N_HEADS_C, L), lambda bi, c: (bi, 0, c))],
        out_specs=(tile(hv), pl.BlockSpec((None, N_HEADS_C, DV_C, LANE), lambda bi, c: (bi, 0, 0, 0)),
                   pl.BlockSpec((None, N_HEADS_C, LANE), lambda bi, c: (bi, 0, 0)),
                   pl.BlockSpec((None, N_HEADS_C, LANE), lambda bi, c: (bi, 0, 0))),
        scratch_shapes=[pltpu.VMEM((N_HEADS_C, DV_C, LANE), F32), pltpu.VMEM((N_HEADS_C, LANE), F32),
                        pltpu.VMEM((N_HEADS_C, LANE), F32)],
        compiler_params=_params("parallel", "arbitrary"),
        name="mlstm_scan",
    )(q, k, v, gi, git)


def _mlstm_out_body(hs_ref, og_ref, hg_ref, x_ref, w_ref, o_ref):
    parts = []
    for h in range(N_HEADS_C):
        sl = slice(h * DV_C, (h + 1) * DV_C)
        parts.append((og_ref[:, sl] * _rms(hs_ref[:, sl], hg_ref[:, sl])).astype(BF16))
    o_ref[...] = x_ref[...] + _dot(jnp.concatenate(parts, axis=1), w_ref[...])


def _mlstm_out(hs, og, hg, x, w):
    m, d = x.shape
    hv = hs.shape[1]
    tm = _row_tile(m, 512)
    wide = pl.BlockSpec((tm, hv), lambda i: (i, 0))
    row = pl.BlockSpec((tm, d), lambda i: (i, 0))
    return pl.pallas_call(
        _mlstm_out_body,
        out_shape=jax.ShapeDtypeStruct((m, d), F32),
        grid=(m // tm,),
        in_specs=[wide, wide, _const_spec(hg.shape), row, _const_spec(w.shape)],
        out_specs=row,
        compiler_params=_params("parallel"),
        name="mlstm_out",
    )(hs, og, hg, x, w)


def _mlstm_weights(w_in, b_if):
    d = w_in.shape[0]
    hk, hv = N_HEADS_C * DK_C, N_HEADS_C * DV_C

    def spread(w):
        w = w.reshape(d, N_HEADS_C, DK_C)
        return jnp.pad(w, ((0, 0), (0, 0), (0, LANE - DK_C))).reshape(d, N_HEADS_C * LANE)

    wgi = jnp.pad(w_in[:, 2 * hk + hv:2 * hk + hv + 2 * N_HEADS_C], ((0, 0), (0, LANE - 2 * N_HEADS_C)))
    return {
        "wq": spread(w_in[:, :hk]).astype(BF16),
        "wk": (spread(w_in[:, hk:2 * hk]) * (DK_C ** -0.5)).astype(BF16),
        "wv": w_in[:, 2 * hk:2 * hk + hv].astype(BF16),
        "wgi": wgi.astype(BF16),
        "wo": w_in[:, 2 * hk + hv + 2 * N_HEADS_C:].astype(BF16),
        "bif": jnp.pad(b_if, (0, LANE - 2 * N_HEADS_C))[None, :],
    }


def _mlstm_prompt(x, g_mix, b, t, w_in, b_if, h_g, w_out):
    wts = _mlstm_weights(w_in, b_if)
    q, k, v, gi, og = _mlstm_proj(x, g_mix, wts)
    git = gi[:, :2 * N_HEADS_C].reshape(b, t, 2 * N_HEADS_C).transpose(0, 2, 1)
    hs, c, n, m = _mlstm_scan(q, k, v, gi, git, b, t)
    y = _mlstm_out(hs, og, h_g[None, :], x, w_out.astype(BF16))
    return y, c[..., :DK_C], n[..., :DK_C], m[..., 0]


def _proj_add_body(o_ref, x_ref, w_ref, out_ref):
    out_ref[...] = x_ref[...] + _dot(o_ref[...].astype(BF16), w_ref[...])


def _proj_add(o, x, w):
    m, d = x.shape
    tm = _row_tile(m, 512)
    return pl.pallas_call(
        _proj_add_body,
        out_shape=jax.ShapeDtypeStruct((m, d), F32),
        grid=(m // tm,),
        in_specs=[pl.BlockSpec((tm, o.shape[1]), lambda i: (i, 0)), pl.BlockSpec((tm, d), lambda i: (i, 0)),
                  _const_spec(w.shape)],
        out_specs=pl.BlockSpec((tm, d), lambda i: (i, 0)),
        compiler_params=_params("parallel"),
        name="proj_add",
    )(o, x, w)


def _nsa_step_body(pt_ref, *refs, n_pages, page, past_len):
    pages = refs[:n_pages]
    (win_ref, qr_ref, qn_ref, gates_ref, nkv_ref, nwin_ref, pe_ref, w1_ref, w2_ref, kcg_ref,
     o_ref, c_s, ks_s, vs_s, x_s) = refs[n_pages:]
    w = N_KV_B * HD_B
    length = n_pages * page
    nb = length // CMP_BLOCK
    t = past_len
    for p in range(n_pages):
        rows = slice(p * page, (p + 1) * page)
        for c in range(2 * w // LANE):
            c_s[c, rows, :] = pages[p][:, c * LANE:(c + 1) * LANE]
        ks_s[rows, :] = pages[p][:, 2 * w:3 * w].astype(BF16)
        vs_s[rows, :] = pages[p][:, 3 * w:].astype(BF16)
    for ng in range(nb // 8):
        for l in range(CMP_BLOCK):
            for c in range(2 * w // LANE):
                rows = c_s[c, pl.ds(ng * 8 * CMP_BLOCK + l, 8, stride=CMP_BLOCK), :]
                for half in range(LANE // HD_B):
                    x_s[c * (LANE // HD_B) + half, ng * 8:(ng + 1) * 8, l * HD_B:(l + 1) * HD_B] = (
                        rows[:, half * HD_B:(half + 1) * HD_B])
    lane_w = lax.broadcasted_iota(jnp.int32, (HD_B, w), 1)
    row_w = lax.broadcasted_iota(jnp.int32, (HD_B, w), 0)
    cmp_nat = []
    for slot in range(2):
        xs = x_s[slot * N_KV_B:(slot + 1) * N_KV_B].reshape(N_KV_B * nb, CMP_BLOCK * HD_B)
        hid = jax.nn.gelu(_dot((xs + pe_ref[slot]).astype(BF16), w1_ref[slot]))
        y = _dot(hid.astype(BF16), w2_ref[slot])
        if slot == 0:
            y = _rms(y, kcg_ref[...])
        nat = jnp.zeros((nb, w), F32)
        for g in range(N_KV_B):
            place = (lane_w == row_w + g * HD_B).astype(BF16)
            nat = nat + _dot(y[g * nb:(g + 1) * nb].astype(BF16), place)
        cmp_nat.append(nat.astype(BF16))
    kc, vc = cmp_nat
    qr = qr_ref[...]
    qn = qn_ref[...]
    gates = gates_ref[...]
    nh = N_HEADS_B
    blk = lax.broadcasted_iota(jnp.int32, (nh, nb), 1)
    p_c = _masked_softmax(_dot_nt(qn, kc), (blk + 1) * CMP_BLOCK - 1 <= t, 1)
    o_c = _dot(p_c.astype(BF16), vc)
    blk_t = lax.broadcasted_iota(jnp.int32, (nb, nh), 0)
    p_t = _masked_softmax(_dot_nt(kc, qn), (blk_t + 1) * CMP_BLOCK - 1 <= t, 0)
    gsum = (lax.broadcasted_iota(jnp.int32, (nh, LANE), 0) // REP_B
            == lax.broadcasted_iota(jnp.int32, (nh, LANE), 1)).astype(BF16)
    pair = (lax.broadcasted_iota(jnp.int32, (nb, nb), 1) // (SEL_BLOCK // CMP_BLOCK)
            == lax.broadcasted_iota(jnp.int32, (nb, nb), 0)).astype(BF16)
    imp = sum(_dot(part, gsum) for part in _split3(p_t))
    imp = sum(_dot(pair, part) for part in _split3(imp))
    sblk = lax.broadcasted_iota(jnp.int32, (nb, LANE), 0)
    cur = t // SEL_BLOCK
    forced = (sblk == 0) | (sblk == cur) | (sblk == cur - 1)
    score = jnp.where(forced, jnp.inf, jnp.where(sblk * SEL_BLOCK <= t, imp, -jnp.inf))
    sblk_f = sblk.astype(F32)
    pickable = score > -jnp.inf
    for _ in range(N_SEL):
        mx = jnp.max(score, axis=0, keepdims=True)
        first = jnp.min(jnp.where(score == mx, sblk_f, float(nb)), axis=0, keepdims=True)
        score = jnp.where(sblk_f == first, -jnp.inf, score)
    notsel = jnp.where(pickable, jnp.where(score > -jnp.inf, -MASK_BIG, 0.0), -MASK_BIG)
    bias = _dot_nt(gsum, notsel.astype(BF16)).astype(BF16)
    expand = (lax.broadcasted_iota(jnp.int32, (nb, length), 1) // SEL_BLOCK
              == lax.broadcasted_iota(jnp.int32, (nb, length), 0)).astype(BF16)
    nkv = nkv_ref[...]
    nwin = nwin_ref[...]

    def attend(s, k_new, v_cache, v_new):
        s_new = jnp.sum(qr.astype(F32) * k_new.astype(BF16).astype(F32), axis=1, keepdims=True)
        m = jnp.maximum(jnp.max(s, axis=1, keepdims=True), s_new)
        e = jnp.exp(s - m)
        e_new = jnp.exp(s_new - m)
        den = jnp.sum(e, axis=1, keepdims=True) + e_new
        num = _dot(e.astype(BF16), v_cache) + e_new.astype(BF16).astype(F32) * v_new.astype(BF16).astype(F32)
        return num / den

    o_s = attend(_dot_nt(qr, ks_s[...]) + _dot(bias, expand), nkv[:, 2 * w:3 * w], vs_s[...], nkv[:, 3 * w:])
    wb = win_ref.shape[0]
    pos_w = t - wb + lax.broadcasted_iota(jnp.int32, (nh, wb), 1)
    ok_w = (pos_w >= 0) & (t - pos_w <= WINDOW)
    s_w = jnp.where(ok_w, _dot_nt(qr, win_ref[:, :w].astype(BF16)), -MASK_BIG)
    o_w = attend(s_w, nwin[:, :w], win_ref[:, w:].astype(BF16), nwin[:, w:])
    o_ref[...] = gates[:, 0:1] * o_c + gates[:, 1:2] * o_s + gates[:, 2:3] * o_w


def _nsa_step(page_table, cache, win_cache, qr, qn, gates, new_kv, new_win, pe, w1, w2, kc_g, past_len):
    bsz, n_pages = page_table.shape
    page = cache.shape[1]
    w = N_KV_B * HD_B
    nb = n_pages * page // CMP_BLOCK
    wb = win_cache.shape[1]
    per = lambda shape: pl.BlockSpec((None,) + shape, lambda b, pt: (b,) + (0,) * len(shape))
    const = lambda a: pl.BlockSpec(a.shape, lambda b, pt: (0,) * a.ndim)
    page_specs = [pl.BlockSpec((None, page, 4 * w), lambda b, pt, p=p: (pt[b, p], 0, 0)) for p in range(n_pages)]
    return pl.pallas_call(
        functools.partial(_nsa_step_body, n_pages=n_pages, page=page, past_len=past_len),
        out_shape=jax.ShapeDtypeStruct((bsz, N_HEADS_B, w), F32),
        grid_spec=pltpu.PrefetchScalarGridSpec(
            num_scalar_prefetch=1,
            grid=(bsz,),
            in_specs=page_specs + [per((wb, 2 * w)), per((N_HEADS_B, w)), per((N_HEADS_B, w)), per((N_HEADS_B, LANE)),
                                   per((1, 4 * w)), per((1, 2 * w)), const(pe), const(w1), const(w2), const(kc_g)],
            out_specs=per((N_HEADS_B, w)),
            scratch_shapes=[pltpu.VMEM((2 * w // LANE, n_pages * page, LANE), F32),
                            pltpu.VMEM((n_pages * page, w), BF16),
                            pltpu.VMEM((n_pages * page, w), BF16),
                            pltpu.VMEM((2 * N_KV_B, nb, CMP_BLOCK * HD_B), F32)]),
        compiler_params=_params("parallel"),
        name="nsa_step",
    )(page_table, *([cache] * n_pages), win_cache, qr, qn, gates, new_kv, new_win, pe, w1, w2, kc_g)


def _nsa_sample_step(x, g_mix, past_len, kv_cache, win_cache, page_table, w_in, q_g, k_g, pe, w_c1, w_c2, w_out):
    bsz, d = x.shape
    w = N_KV_B * HD_B
    assert past_len % CMP_BLOCK == 0 and past_len // SEL_BLOCK + 1 <= past_len // CMP_BLOCK
    wts = _nsa_weights(w_in, q_g, k_g, w_out)
    pos = jnp.full((bsz,), past_len, jnp.int32)
    qcat, gates, kv_rows, win_rows, _, _, _, _ = _nsa_proj(
        x, g_mix, wts, _rope_tables(pos, LANE), _rope_tables(pos, HD_B), 1, bsz)
    q5 = qcat.reshape(bsz, N_KV_B, REP_B, 2, HD_B)
    eye = jnp.eye(N_KV_B, dtype=BF16)
    qrows = (q5[:, :, :, :, None, :] * eye[None, :, None, None, :, None])
    qr = qrows[:, :, :, 0].reshape(bsz, N_HEADS_B, w)
    qn = qrows[:, :, :, 1].reshape(bsz, N_HEADS_B, w)
    gts = gates.reshape(bsz, N_KV_B, LANE)[:, :, :REP_B * 3].reshape(bsz, N_HEADS_B, 3)
    gts = jnp.pad(gts, ((0, 0), (0, 0), (0, LANE - 3)))
    pool, page = kv_cache.shape[:2]
    o = _nsa_step(page_table, kv_cache.reshape(pool, page, 4 * w), win_cache.reshape(bsz, -1, 2 * w), qr, qn, gts,
                  kv_rows.reshape(bsz, 1, 4 * w), win_rows.reshape(bsz, 1, 2 * w),
                  pe.reshape(2, 1, CMP_BLOCK * HD_B), w_c1.reshape(2, CMP_BLOCK * HD_B, HD_B).astype(BF16),
                  w_c2.astype(BF16), k_g[0][None, :], past_len)
    own = (jnp.arange(N_HEADS_B)[:, None] // REP_B == jnp.arange(N_KV_B)[None, :]).astype(F32)
    w_exp = own[:, :, None, None] * w_out.reshape(N_HEADS_B, 1, HD_B, d)
    y = _proj_add(o.reshape(bsz, N_HEADS_B * w), x, w_exp.reshape(N_HEADS_B * w, d).astype(BF16))
    return y, kv_rows.reshape(bsz, 1, N_KV_SLOTS, N_KV_B, HD_B), win_rows.reshape(bsz, 1, 2, N_KV_B, HD_B)


def _mlstm_step_body(q_ref, k_ref, vt_ref, gi_ref, c_ref, n_ref, m_ref, ht_ref, co_ref, no_ref, mo_ref, *, sb):
    lane8 = lax.broadcasted_iota(jnp.int32, (DV_C, N_HEADS_C), 1)
    gi = gi_ref[...]
    logf = jax.nn.log_sigmoid(gi)
    m_all = m_ref[...]
    for s in range(sb):
        ht = jnp.zeros((DV_C, N_HEADS_C), F32)
        m_new_row = jnp.zeros((1, N_HEADS_C), F32)
        lane_m = lax.broadcasted_iota(jnp.int32, (1, N_HEADS_C), 1)
        for h in range(N_HEADS_C):
            q = q_ref[s:s + 1, h * LANE:h * LANE + DK_C].astype(F32)
            k = k_ref[s:s + 1, h * LANE:h * LANE + DK_C].astype(F32)
            v = vt_ref[s, :, h:h + 1]
            c = c_ref[s, h]
            n = n_ref[s, h:h + 1, :]
            it = gi[s:s + 1, h:h + 1]
            b = logf[s:s + 1, N_HEADS_C + h:N_HEADS_C + h + 1]
            m0 = m_all[s:s + 1, h:h + 1]
            inter = b + m0
            m_t = jnp.maximum(inter, it)
            wgt = jnp.exp(it - m_t)
            a = jnp.exp(inter - m_t)
            sc = jnp.sum(q * k, axis=1, keepdims=True) * wgt
            num = a * jnp.sum(c * q, axis=1, keepdims=True) + sc * v
            den = a * jnp.sum(n * q, axis=1, keepdims=True) + sc
            hcol = num / jnp.maximum(jnp.abs(den), jnp.exp(-m_t))
            ht = jnp.where(lane8 == h, hcol, ht)
            co_ref[s, h] = a * c + (wgt * v) * k
            no_ref[s, h:h + 1, :] = a * n + wgt * k
            m_new_row = jnp.where(lane_m == h, m_t, m_new_row)
        ht_ref[s] = ht
        mo_ref[s:s + 1, :] = m_new_row


def _mlstm_step(q, k, vt, gi, c0, n0, m0):
    bsz = q.shape[0]
    sb = 8
    row = lambda n: pl.BlockSpec((sb, n), lambda i: (i, 0))
    c_spec = pl.BlockSpec((sb, N_HEADS_C, DV_C, DK_C), lambda i: (i, 0, 0, 0))
    n_spec = pl.BlockSpec((sb, N_HEADS_C, DK_C), lambda i: (i, 0, 0))
    vt_spec = pl.BlockSpec((sb, DV_C, N_HEADS_C), lambda i: (i, 0, 0))
    return pl.pallas_call(
        functools.partial(_mlstm_step_body, sb=sb),
        out_shape=(jax.ShapeDtypeStruct((bsz, DV_C, N_HEADS_C), F32), jax.ShapeDtypeStruct(c0.shape, F32),
                   jax.ShapeDtypeStruct(n0.shape, F32), jax.ShapeDtypeStruct(m0.shape, F32)),
        grid=(bsz // sb,),
        in_specs=[row(N_HEADS_C * LANE), row(N_HEADS_C * LANE), vt_spec, row(LANE), c_spec, n_spec, row(N_HEADS_C)],
        out_specs=(vt_spec, c_spec, n_spec, row(N_HEADS_C)),
        compiler_params=_params("parallel"),
        name="mlstm_step",
    )(q, k, vt, gi, c0, n0, m0)


def _mlstm_sample_step(x, g_mix, c0, n0, m0, w_in, b_if, h_g, w_out):
    bsz = x.shape[0]
    wts = _mlstm_weights(w_in, b_if)
    q, k, v, gi, og = _mlstm_proj(x, g_mix, wts)
    vt = v.astype(F32).reshape(bsz, N_HEADS_C, DV_C).transpose(0, 2, 1)
    ht, c, n, m = _mlstm_step(q, k, vt, gi, c0, n0, m0)
    hs = ht.transpose(0, 2, 1).reshape(bsz, N_HEADS_C * DV_C)
    y = _mlstm_out(hs, og, h_g[None, :], x, w_out.astype(BF16))
    return y, c, n, m


def kernel(x_prompt, x_sample, cache_nsa_kv, cache_nsa_win, state_mlstm_C, state_mlstm_n, state_mlstm_m, page_table,
           norm_mix_g, norm_ffn_g, ffn_w1, ffn_w2, a_w_in, a_ln_g, a_ln_b, a_w_s, a_b_s, a_w_out,
           b_w_in, b_q_g, b_k_g, b_pe, b_w_c1, b_w_c2, b_w_out, c_w_in, c_b_if, c_h_g, c_w_out):
    bp, t, d = x_prompt.shape
    bs, ts, _ = x_sample.shape
    assert ts == 1
    past_len = page_table.shape[1] * cache_nsa_kv.shape[2]
    xp = x_prompt.reshape(bp * t, d)
    xs = x_sample.reshape(bs, d)
    out = {k: [] for k in ("v_s", "kv_p", "win_p", "kv_s", "win_s", "C_p", "n_p", "m_p", "C_s", "n_s", "m_s")}
    for layer in range(norm_mix_g.shape[0]):
        kind, j = layer % 3, layer // 3
        gm = norm_mix_g[layer]
        if kind == 0:
            args = (a_w_in[j], a_ln_g[j], a_ln_b[j], a_w_s[j], a_b_s[j], a_w_out[j])
            xp = _gmlp_layer(xp, gm, *args, single=False)[0]
            xs, v = _gmlp_layer(xs, gm, *args, single=True)
            out["v_s"].append(v.reshape(bs, ts, -1))
        elif kind == 1:
            args = (b_w_in[j], b_q_g[j], b_k_g[j], b_pe[j], b_w_c1[j], b_w_c2[j], b_w_out[j])
            xp, kv, win = _nsa_prompt(xp, gm[None, :], bp, t, *args)
            out["kv_p"].append(kv)
            out["win_p"].append(win)
            xs, kv, win = _nsa_sample_step(xs, gm[None, :], past_len, cache_nsa_kv[j], cache_nsa_win[j], page_table,
                                           *args)
            out["kv_s"].append(kv)
            out["win_s"].append(win)
        else:
            args = (c_w_in[j], c_b_if[j], c_h_g[j], c_w_out[j])
            xp, c, n, m = _mlstm_prompt(xp, gm[None, :], bp, t, *args)
            out["C_p"].append(c)
            out["n_p"].append(n)
            out["m_p"].append(m)
            xs, c, n, m = _mlstm_sample_step(xs, gm[None, :], state_mlstm_C[j], state_mlstm_n[j], state_mlstm_m[j],
                                             *args)
            out["C_s"].append(c)
            out["n_s"].append(n)
            out["m_s"].append(m)
        gf = norm_ffn_g[layer][None, :]
        w1, w2 = ffn_w1[layer].astype(BF16), ffn_w2[layer].astype(BF16)
        xp = _ffn(xp, gf, w1, w2)
        xs = _ffn(xs, gf, w1, w2)
    st = {k: jnp.stack(v) for k, v in out.items()}
    return (xp.reshape(bp, t, d), xs.reshape(bs, ts, d), st["v_s"], st["kv_p"], st["win_p"], st["kv_s"], st["win_s"],
            st["C_p"], st["n_p"], st["m_p"], st["C_s"], st["n_s"], st["m_s"])
```

```python
import functools
import math

import jax
import jax.numpy as jnp
from jax import lax
from jax.experimental import pallas as pl
from jax.experimental.pallas import tpu as pltpu

F32 = jnp.float32
BF16 = jnp.bfloat16

EPS = 1e-6
CHUNK_A = 128
N_GROUPS_A = 8
N_HEADS_B = 16
N_KV_B = 4
REP_B = N_HEADS_B // N_KV_B
HD_B = 64
ROT_DIM = 16
ROPE_THETA = 500000.0
CMP_BLOCK = 32
SEL_BLOCK = 64
N_SEL = 16
WINDOW = 512
N_KV_SLOTS = 4
N_HEADS_C = 8
DK_C = 64
DV_C = 128
CHUNK_C = 128
SCALE_B = HD_B ** -0.5

LANE = 128
VMEM_LIMIT_BYTES = 56 * 1024 * 1024
MASK_BIG = 1e30
M_INIT = -1e20
N_SEL_PAD = 128
N_CMP_PAD = 2 * N_SEL_PAD
SHIFT_MAX = 40.0
FLASH_SEL_TILE = 512
FLASH_WIN_TILE = 512


def _params(*sem):
    return pltpu.CompilerParams(dimension_semantics=sem, vmem_limit_bytes=VMEM_LIMIT_BYTES)


def _dot(a, b):
    return jnp.dot(a, b, preferred_element_type=F32)


def _dot_nt(a, b):
    return lax.dot_general(a, b, (((1,), (1,)), ((), ())), preferred_element_type=F32)


def _dot_tn(a, b):
    return lax.dot_general(a, b, (((0,), (0,)), ((), ())), preferred_element_type=F32)


def _rms(x, g):
    return x * lax.rsqrt(jnp.mean(x * x, axis=-1, keepdims=True) + EPS) * g


def _split3(x):
    a = x.astype(BF16)
    r = x - a.astype(F32)
    b = r.astype(BF16)
    c = (r - b.astype(F32)).astype(BF16)
    return a, b, c


def _const_spec(shape):
    n = len(shape)
    return pl.BlockSpec(shape, lambda *_: (0,) * n)


def _row_tile(m, pref):
    t = min(pref, m)
    while m % t:
        t //= 2
    return t


def _ffn_body(x_ref, g_ref, w1_ref, w2_ref, o_ref, *, ck):
    x = x_ref[...]
    xb = _rms(x, g_ref[...]).astype(BF16)
    acc = x
    for j in range(w1_ref.shape[1] // ck):
        h = jnp.maximum(_dot(xb, w1_ref[:, j * ck:(j + 1) * ck]), 0.0)
        acc = acc + _dot((h * h).astype(BF16), w2_ref[j * ck:(j + 1) * ck, :])
    o_ref[...] = acc


def _ffn(x, g, w1, w2):
    m, d = x.shape
    tm = _row_tile(m, 512)
    return pl.pallas_call(
        functools.partial(_ffn_body, ck=1024),
        out_shape=jax.ShapeDtypeStruct((m, d), F32),
        grid=(m // tm,),
        in_specs=[pl.BlockSpec((tm, d), lambda i: (i, 0)), _const_spec(g.shape),
                  _const_spec(w1.shape), _const_spec(w2.shape)],
        out_specs=pl.BlockSpec((tm, d), lambda i: (i, 0)),
        compiler_params=_params("parallel"),
        name="ffn",
    )(x, g, w1, w2)


def _gmlp_body(x_ref, g_ref, win_ref, lng_ref, lnb_ref, ws_ref, bs_ref, wout_ref, o_ref, *maybe_v_ref, single):
    x = x_ref[...]
    dg = lng_ref.shape[1]
    xb = _rms(x, g_ref[...]).astype(BF16)
    u = jax.nn.gelu(_dot(xb, win_ref[:, :dg]))
    v = jax.nn.gelu(_dot(xb, win_ref[:, dg:]))
    mu = jnp.mean(v, axis=-1, keepdims=True)
    vc = v - mu
    var = jnp.mean(vc * vc, axis=-1, keepdims=True)
    v = vc * lax.rsqrt(var + EPS) * lng_ref[...] + lnb_ref[...]
    if single:
        maybe_v_ref[0][...] = v
        gate = v * ws_ref[...] + bs_ref[...]
    else:
        gw = dg // N_GROUPS_A
        row = lax.broadcasted_iota(jnp.int32, (CHUNK_A, CHUNK_A), 0)
        col = lax.broadcasted_iota(jnp.int32, (CHUNK_A, CHUNK_A), 1)
        causal = col <= row
        vb = v.astype(BF16)
        chunks = []
        for c in range(x.shape[0] // CHUNK_A):
            parts = []
            for gi in range(N_GROUPS_A):
                w = jnp.where(causal, ws_ref[gi], 0.0).astype(BF16)
                parts.append(_dot(w, vb[c * CHUNK_A:(c + 1) * CHUNK_A, gi * gw:(gi + 1) * gw]))
            chunks.append(jnp.concatenate(parts, axis=1) + bs_ref[...])
        gate = jnp.concatenate(chunks, axis=0)
    o_ref[...] = x + _dot((u * gate).astype(BF16), wout_ref[...])


def _gmlp(x, g, w_in, ln_g, ln_b, ws, bs, w_out, *, single):
    m, d = x.shape
    dg = w_out.shape[0]
    tm = _row_tile(m, 256)
    n_out = 2 if single else 1
    outs = pl.pallas_call(
        functools.partial(_gmlp_body, single=single),
        out_shape=(jax.ShapeDtypeStruct((m, d), F32), jax.ShapeDtypeStruct((m, dg), F32))[:n_out],
        grid=(m // tm,),
        in_specs=[pl.BlockSpec((tm, d), lambda i: (i, 0)), _const_spec(g.shape), _const_spec(w_in.shape),
                  _const_spec(ln_g.shape), _const_spec(ln_b.shape), _const_spec(ws.shape), _const_spec(bs.shape),
                  _const_spec(w_out.shape)],
        out_specs=(pl.BlockSpec((tm, d), lambda i: (i, 0)), pl.BlockSpec((tm, dg), lambda i: (i, 0)))[:n_out],
        compiler_params=_params("parallel"),
        name="gmlp_single" if single else "gmlp",
    )(x, g, w_in, ln_g, ln_b, ws, bs, w_out)
    return outs if single else (outs[0], None)


def _rope_tables(pos, seg):
    half = ROT_DIM // 2
    freq = jnp.power(ROPE_THETA, -jnp.arange(half, dtype=F32) * 2.0 / ROT_DIM)
    ang = pos.astype(F32)[:, None] * freq[None, :]
    cos, sin = jnp.cos(ang), jnp.sin(ang)
    t = pos.shape[0]
    one = jnp.ones((t, seg - ROT_DIM), F32)
    zero = jnp.zeros((t, seg - ROT_DIM), F32)
    z8 = jnp.zeros((t, half), F32)
    tabs = [jnp.concatenate([cos, cos, one], 1), jnp.concatenate([-sin, z8, zero], 1),
            jnp.concatenate([z8, sin, zero], 1)]
    return jnp.stack([jnp.tile(a, (1, LANE // seg)) for a in tabs])


def _rope128(x, tab):
    return x * tab[0] + pltpu.roll(x, LANE - ROT_DIM // 2, 1) * tab[1] + pltpu.roll(x, ROT_DIM // 2, 1) * tab[2]


def _nsa_proj_body(x_ref, g_ref, wq_ref, wg_ref, wkv_ref, seg_ref, spread_ref, qg_ref, kg_ref, tq_ref, tk_ref,
                   qcat_ref, gates_ref, kv_ref, win_ref, ks_ref, vs_ref, kw_ref, vw_ref):
    x = x_ref[...]
    xb = _rms(x, g_ref[...]).astype(BF16)
    tq = tq_ref[...]
    tk = tk_ref[...]
    qg = qg_ref[...]
    for h in range(N_HEADS_B):
        q = _dot(xb, wq_ref[:, h * LANE:(h + 1) * LANE])
        ms = jnp.sum(q * q, axis=-1, keepdims=True) * (1.0 / LANE)
        qn = q * lax.rsqrt(ms + EPS) * qg
        qcat_ref[:, h * LANE:(h + 1) * LANE] = (_rope128(qn, tq) * SCALE_B).astype(BF16)
    gates_ref[...] = jax.nn.sigmoid(_dot(xb, wg_ref[...]))
    kv = _dot(xb, wkv_ref[...])
    w = N_KV_B * HD_B
    seg = seg_ref[...]
    spread = spread_ref[...]

    def head_norm(k, gain):
        k2 = k * k
        hi = k2.astype(BF16)
        lo = (k2 - hi.astype(F32)).astype(BF16)
        ss = _dot(hi, seg) + _dot(lo, seg)
        return k * lax.rsqrt(ss * (1.0 / HD_B) + EPS) * gain

    def rope(k):
        return jnp.concatenate([_rope128(k[:, j * LANE:(j + 1) * LANE], tk) for j in range(w // LANE)], axis=1)

    ks = rope(head_norm(kv[:, 2 * w:3 * w], kg_ref[0:1, :]))
    kw = rope(head_norm(kv[:, 4 * w:5 * w], kg_ref[1:2, :]))
    vs = kv[:, 3 * w:4 * w]
    vw = kv[:, 5 * w:6 * w]
    kv_ref[:, :2 * w] = kv[:, :2 * w]
    kv_ref[:, 2 * w:3 * w] = ks
    kv_ref[:, 3 * w:] = vs
    win_ref[:, :w] = kw
    win_ref[:, w:] = vw
    lane = lax.broadcasted_iota(jnp.int32, (1, N_KV_B * LANE), 1)
    ones_hi = ((lane & HD_B) != 0).astype(F32)
    ks_ref[...] = _dot(ks.astype(BF16), spread).astype(BF16)
    kw_ref[...] = _dot(kw.astype(BF16), spread).astype(BF16)
    vs_ref[...] = (_dot(vs.astype(BF16), spread) + ones_hi).astype(BF16)
    vw_ref[...] = (_dot(vw.astype(BF16), spread) + ones_hi).astype(BF16)


def _nsa_proj(x, g, wts, tab_q, tab_k, n_tab_tiles, tm):
    m, d = x.shape
    w = N_KV_B * HD_B
    ws = N_KV_B * LANE
    tile = lambda n: pl.BlockSpec((tm, n), lambda i: (i, 0))
    tab = pl.BlockSpec((3, tm, LANE), lambda i: (0, i % n_tab_tiles, 0))
    consts = [g, wts["wq"], wts["wg"], wts["wkv"], wts["seg"], wts["spread"], wts["qg"], wts["kg"]]
    return pl.pallas_call(
        _nsa_proj_body,
        out_shape=(jax.ShapeDtypeStruct((m, N_HEADS_B * LANE), BF16), jax.ShapeDtypeStruct((m, ws), F32),
                   jax.ShapeDtypeStruct((m, 4 * w), F32), jax.ShapeDtypeStruct((m, 2 * w), F32),
                   jax.ShapeDtypeStruct((m, ws), BF16), jax.ShapeDtypeStruct((m, ws), BF16),
                   jax.ShapeDtypeStruct((m, ws), BF16), jax.ShapeDtypeStruct((m, ws), BF16)),
        grid=(m // tm,),
        in_specs=[tile(d)] + [_const_spec(c.shape) for c in consts] + [tab, tab],
        out_specs=(tile(N_HEADS_B * LANE), tile(ws), tile(4 * w), tile(2 * w), tile(ws), tile(ws), tile(ws), tile(ws)),
        compiler_params=_params("parallel"),
        name="nsa_proj",
    )(x, *consts, tab_q, tab_k)


def _compress_body(x_ref, pe_ref, w1_ref, w2_ref, g_ref, o_ref, *, norm):
    xb = (x_ref[...] + pe_ref[...]).astype(BF16)
    hid = jax.nn.gelu(_dot(xb, w1_ref[...]))
    y = _dot(hid.astype(BF16), w2_ref[...])
    if norm:
        y = _rms(y, g_ref[...])
    o_ref[...] = y


def _compress(x, pe, w1, w2, g, *, norm):
    r, kdim = x.shape
    tr = _row_tile(r, 512)
    return pl.pallas_call(
        functools.partial(_compress_body, norm=norm),
        out_shape=jax.ShapeDtypeStruct((r, HD_B), F32),
        grid=(r // tr,),
        in_specs=[pl.BlockSpec((tr, kdim), lambda i: (i, 0)), _const_spec(pe.shape), _const_spec(w1.shape),
                  _const_spec(w2.shape), _const_spec(g.shape)],
        out_specs=pl.BlockSpec((tr, HD_B), lambda i: (i, 0)),
        compiler_params=_params("parallel"),
        name="nsa_compress_k" if norm else "nsa_compress_v",
    )(x, pe, w1, w2, g)


def _masked_softmax(s, mask, axis):
    sm = jnp.where(mask, s, -jnp.inf)
    mx = jnp.max(sm, axis=axis, keepdims=True)
    mx = jnp.where(mx > -jnp.inf, mx, 0.0)
    e = jnp.where(mask, jnp.exp(s - mx), 0.0)
    return e / jnp.maximum(jnp.sum(e, axis=axis, keepdims=True), 1e-30)


def _cmp_select_body(q_ref, kc_ref, kct_ref, vc_ref, gates_ref, o_ref, mnot_ref, *, tq):
    q0 = pl.program_id(2) * tq
    kc = kc_ref[...]
    kct = kct_ref[...]
    vc = vc_ref[...]
    gates = gates_ref[...]
    t_r = q0 + lax.broadcasted_iota(jnp.int32, (tq, N_CMP_PAD), 0)
    blk_r = lax.broadcasted_iota(jnp.int32, (tq, N_CMP_PAD), 1)
    mask_r = (blk_r + 1) * CMP_BLOCK - 1 <= t_r
    row_c = lax.broadcasted_iota(jnp.int32, (N_CMP_PAD, tq), 0)
    t_c = q0 + lax.broadcasted_iota(jnp.int32, (N_CMP_PAD, tq), 1)
    blk_c = jnp.where(row_c < N_SEL_PAD, 2 * row_c, 2 * (row_c - N_SEL_PAD) + 1)
    mask_c = (blk_c + 1) * CMP_BLOCK - 1 <= t_c
    imp = jnp.zeros((N_SEL_PAD, tq), F32)
    for r in range(REP_B):
        qh = q_ref[:, r * LANE:(r + 1) * LANE]
        p = _masked_softmax(_dot_nt(qh, kc), mask_r, 1)
        o = _dot(p.astype(BF16), vc)
        o_ref[:, r * LANE:(r + 1) * LANE] = o * gates[:, 3 * r:3 * r + 1]
        pt = _masked_softmax(_dot_nt(kct, qh), mask_c, 0)
        imp = imp + pt[:N_SEL_PAD] + pt[N_SEL_PAD:]
    blk = lax.broadcasted_iota(jnp.int32, (N_SEL_PAD, tq), 0)
    t_s = q0 + lax.broadcasted_iota(jnp.int32, (N_SEL_PAD, tq), 1)
    cur = t_s // SEL_BLOCK
    forced = (blk == 0) | (blk == cur) | (blk == cur - 1)
    valid = blk * SEL_BLOCK <= t_s
    score = jnp.where(forced, jnp.inf, jnp.where(valid, imp, -jnp.inf))
    blk_f = blk.astype(F32)
    pickable = score > -jnp.inf
    for _ in range(N_SEL):
        mx = jnp.max(score, axis=0, keepdims=True)
        first = jnp.min(jnp.where(score == mx, blk_f, float(N_SEL_PAD)), axis=0, keepdims=True)
        score = jnp.where(blk_f == first, -jnp.inf, score)
    mnot_ref[...] = jnp.where(pickable, jnp.where(score > -jnp.inf, 1.0, 0.0), 1.0).T.astype(BF16)


def _cmp_select(qcat, kc, kct, vc, gates, b, t, tq):
    m = b * t
    nq = t // tq
    blk = pl.BlockSpec((None, None, N_CMP_PAD, LANE), lambda bi, g, i: (bi, g, 0, 0))
    return pl.pallas_call(
        functools.partial(_cmp_select_body, tq=tq),
        out_shape=(jax.ShapeDtypeStruct((m, N_HEADS_B * LANE), F32), jax.ShapeDtypeStruct((m, N_KV_B * LANE), BF16)),
        grid=(b, N_KV_B, nq),
        in_specs=[pl.BlockSpec((tq, REP_B * LANE), lambda bi, g, i: (bi * nq + i, g)), blk, blk, blk,
                  pl.BlockSpec((tq, LANE), lambda bi, g, i: (bi * nq + i, g))],
        out_specs=(pl.BlockSpec((tq, REP_B * LANE), lambda bi, g, i: (bi * nq + i, g)),
                   pl.BlockSpec((tq, LANE), lambda bi, g, i: (bi * nq + i, g))),
        compiler_params=_params("parallel", "parallel", "parallel"),
        name="nsa_cmp_select",
    )(qcat, kc, kct, vc, gates)


def _flash_body(tab_ref, bound_ref, q_ref, mnot_ref, k_ref, v_ref, gates_ref, o_ref, qs_ref, ks_ref, m_ref, acc_ref, *,
                tq, tk, sel, gate_col, fixed):
    step_id = pl.program_id(2)
    i = tab_ref[0, step_id]
    j = tab_ref[1, step_id]
    rows = REP_B * tq
    first = j == (0 if sel else jnp.maximum(i - WINDOW // tk, 0))

    @pl.when(first)
    def _():
        if not fixed:
            m_ref[...] = jnp.full(m_ref.shape, M_INIT, F32)
        acc_ref[...] = jnp.zeros(acc_ref.shape, F32)
        lane = lax.broadcasted_iota(jnp.int32, (tq, LANE), 1)
        for r in range(REP_B):
            qh = q_ref[:, r * LANE:(r + 1) * LANE]
            if fixed:
                qh = jnp.where(lane < HD_B, qh, jnp.ones_like(qh))
            if sel:
                qs_ref[r * tq:(r + 1) * tq, :LANE] = mnot_ref[...]
                qs_ref[r * tq:(r + 1) * tq, LANE:] = qh
            else:
                qs_ref[r * tq:(r + 1) * tq, :] = qh

    def step(masked):
        kt = k_ref[...]
        if fixed:
            lane = lax.broadcasted_iota(jnp.int32, (tk, LANE), 1)
            kt = jnp.where(lane == HD_B, -bound_ref[0], kt.astype(F32)).astype(BF16)
        if sel:
            kpos = j * tk + lax.broadcasted_iota(jnp.int32, (tk, LANE), 0)
            lane = lax.broadcasted_iota(jnp.int32, (tk, LANE), 1)
            ks_ref[:, :LANE] = jnp.where(kpos // SEL_BLOCK == lane, -MASK_BIG, 0.0).astype(BF16)
            ks_ref[:, LANE:] = kt
            kmat = ks_ref[...]
        else:
            kmat = kt
        s = _dot_nt(qs_ref[...], kmat)
        if masked is not None:
            t = i * tq + lax.broadcasted_iota(jnp.int32, (rows, tk), 0) % tq
            key = j * tk + lax.broadcasted_iota(jnp.int32, (rows, tk), 1)
            ok = key <= t if masked == "causal" else t - key <= WINDOW
            s = jnp.where(ok, s, -MASK_BIG)
        if fixed:
            acc_ref[...] += _dot(jnp.exp(s).astype(BF16), v_ref[...])
        else:
            m_old = m_ref[...]
            m_new = jnp.maximum(m_old, jnp.max(s, axis=-1, keepdims=True))
            p = jnp.exp(s - m_new[:, :1])
            acc_ref[...] = jnp.exp(m_old - m_new) * acc_ref[...] + _dot(p.astype(BF16), v_ref[...])
            m_ref[...] = m_new

    if sel:
        pl.when(j < i)(lambda: step(None))
    else:
        pl.when(j == i - WINDOW // tk)(lambda: step("band"))
        pl.when((j < i) & (j > i - WINDOW // tk))(lambda: step(None))

    @pl.when(j == i)
    def _():
        step("causal")
        gates = gates_ref[...]
        lane = lax.broadcasted_iota(jnp.int32, (tq, LANE), 1)
        for r in range(REP_B):
            a = acc_ref[r * tq:(r + 1) * tq, :]
            o = a / a[:, HD_B:HD_B + 1]
            g = gates[:, 3 * r + gate_col:3 * r + gate_col + 1]
            o_ref[:, r * LANE:(r + 1) * LANE] = jnp.where(lane < HD_B, o * g, 0.0)


def _flash(qcat, mnot, k, v, gates, bound, b, t, tq, sel):
    m = b * t
    nq = t // tq
    tk = tq
    assert WINDOW % tk == 0
    lo = (lambda i: 0) if sel else (lambda i: max(i - WINDOW // tk, 0))
    pairs = [(i, j) for i in range(nq) for j in range(lo(i), i + 1)]
    tab = jnp.asarray(pairs, jnp.int32).T
    kdim = 2 * LANE if sel else LANE
    qidx = lambda bi, g, p, *pf: (bi * nq + pf[0][0, p], g)
    kidx = lambda bi, g, p, *pf: (bi * nq + pf[0][1, p], g)

    def call(fixed):
        name = ("nsa_flash_sel" if sel else "nsa_flash_win") + ("" if fixed else "_online")
        return pl.pallas_call(
            functools.partial(_flash_body, tq=tq, tk=tk, sel=sel, gate_col=1 if sel else 2, fixed=fixed),
            out_shape=jax.ShapeDtypeStruct((m, N_HEADS_B * LANE), F32),
            grid_spec=pltpu.PrefetchScalarGridSpec(
                num_scalar_prefetch=2,
                grid=(b, N_KV_B, len(pairs)),
                in_specs=[pl.BlockSpec((tq, REP_B * LANE), qidx), pl.BlockSpec((tq, LANE), qidx),
                          pl.BlockSpec((tk, LANE), kidx), pl.BlockSpec((tk, LANE), kidx),
                          pl.BlockSpec((tq, LANE), qidx)],
                out_specs=pl.BlockSpec((tq, REP_B * LANE), qidx),
                scratch_shapes=[pltpu.VMEM((REP_B * tq, kdim), BF16), pltpu.VMEM((tk, 2 * LANE), BF16),
                                pltpu.VMEM((REP_B * tq, LANE), F32), pltpu.VMEM((REP_B * tq, LANE), F32)]),
            compiler_params=_params("parallel", "parallel", "arbitrary"),
            name=name,
        )(tab, bound.reshape(1), qcat, mnot, k, v, gates)

    return lax.cond(bound <= SHIFT_MAX, lambda: call(True), lambda: call(False))


def _sum_proj_body(a_ref, b_ref, c_ref, x_ref, w_ref, o_ref):
    o = (a_ref[...] + b_ref[...] + c_ref[...]).astype(BF16)
    o_ref[...] = x_ref[...] + _dot(o, w_ref[...])


def _sum_proj(a, b, c, x, w):
    m, d = x.shape
    kdim = a.shape[1]
    tm = _row_tile(m, 512)
    big = pl.BlockSpec((tm, kdim), lambda i: (i, 0))
    row = pl.BlockSpec((tm, d), lambda i: (i, 0))
    return pl.pallas_call(
        _sum_proj_body,
        out_shape=jax.ShapeDtypeStruct((m, d), F32),
        grid=(m // tm,),
        in_specs=[big, big, big, row, _const_spec(w.shape)],
        out_specs=row,
        compiler_params=_params("parallel"),
        name="nsa_out_proj",
    )(a, b, c, x, w)


def _nsa_weights(w_in, q_g, k_g, w_out):
    d = w_in.shape[0]
    nq = N_HEADS_B * HD_B
    w = N_KV_B * HD_B
    wq = w_in[:, :nq].reshape(d, N_HEADS_B, 1, HD_B)
    wq = jnp.broadcast_to(wq, (d, N_HEADS_B, 2, HD_B)).reshape(d, N_HEADS_B * LANE)
    wg = w_in[:, nq:nq + 3 * N_HEADS_B].reshape(d, N_KV_B, REP_B * 3)
    wg = jnp.pad(wg, ((0, 0), (0, 0), (0, LANE - REP_B * 3))).reshape(d, N_KV_B * LANE)
    wkv = w_in[:, nq + 3 * N_HEADS_B:]
    lane = jnp.arange(w)
    seg = (lane[:, None] // HD_B == lane[None, :] // HD_B).astype(BF16)
    spread = (lane[:, None] // HD_B * LANE + lane[:, None] % HD_B == jnp.arange(N_KV_B * LANE)[None, :]).astype(BF16)
    wo = jnp.pad(w_out.reshape(N_HEADS_B, HD_B, -1), ((0, 0), (0, LANE - HD_B), (0, 0)))
    return {
        "wq": wq.astype(BF16), "wg": wg.astype(BF16), "wkv": wkv.astype(BF16), "seg": seg, "spread": spread,
        "qg": jnp.tile(q_g, 2)[None, :], "kg": jnp.stack([jnp.tile(k_g[1], N_KV_B), jnp.tile(k_g[2], N_KV_B)]),
        "wo": wo.reshape(N_HEADS_B * LANE, -1).astype(BF16),
    }


def _cmp_inputs(rows, b, length):
    nb = length // CMP_BLOCK
    x = rows.reshape(b, length, N_KV_B, HD_B)[:, :nb * CMP_BLOCK].reshape(b, nb, CMP_BLOCK, N_KV_B, HD_B)
    return x.transpose(0, 1, 3, 2, 4).reshape(b * nb * N_KV_B, CMP_BLOCK * HD_B)


def _cmp_blocks(kc_rows, vc_rows, b, length, pe, w_c1, w_c2, kc_g):
    nb = length // CMP_BLOCK
    pe_k, pe_v = pe[0].reshape(1, -1), pe[1].reshape(1, -1)
    w1k = w_c1[0].reshape(CMP_BLOCK * HD_B, HD_B).astype(BF16)
    w1v = w_c1[1].reshape(CMP_BLOCK * HD_B, HD_B).astype(BF16)
    g = kc_g[None, :]
    kc = _compress(_cmp_inputs(kc_rows, b, length), pe_k, w1k, w_c2[0].astype(BF16), g, norm=True)
    vc = _compress(_cmp_inputs(vc_rows, b, length), pe_v, w1v, w_c2[1].astype(BF16), g, norm=False)
    return kc.reshape(b, nb, N_KV_B, HD_B), vc.reshape(b, nb, N_KV_B, HD_B)


def _nsa_prompt(x, g_mix, b, t, w_in, q_g, k_g, pe, w_c1, w_c2, w_out):
    wts = _nsa_weights(w_in, q_g, k_g, w_out)
    w = N_KV_B * HD_B
    tm = 256
    pos = jnp.arange(t)
    qcat, gates, kv_rows, win_rows, ks_s, vs_s, kw_s, vw_s = _nsa_proj(
        x, g_mix, wts, _rope_tables(pos, LANE), _rope_tables(pos, HD_B), t // tm, tm)
    kc_blk, vc_blk = _cmp_blocks(kv_rows[:, :w], kv_rows[:, w:2 * w], b, t, pe, w_c1, w_c2, k_g[0])
    nb = t // CMP_BLOCK
    assert nb <= N_CMP_PAD and t % SEL_BLOCK == 0

    def blocks(a, lo):
        a = jnp.pad(a.transpose(0, 2, 1, 3), ((0, 0), (0, 0), (0, N_CMP_PAD - nb), (lo, LANE - HD_B - lo)))
        return a.astype(BF16)

    kc = blocks(kc_blk, HD_B)
    kct = jnp.concatenate([kc[:, :, 0::2], kc[:, :, 1::2]], axis=2)
    vc = blocks(vc_blk, 0)
    tq = 256
    o_cmp, mnot = _cmp_select(qcat, kc, kct, vc, gates, b, t, tq)
    qmax = jnp.max(jnp.abs(q_g))
    o_sel = _flash(qcat, mnot, ks_s, vs_s, gates, qmax * jnp.max(jnp.abs(k_g[1])) * math.sqrt(HD_B), b, t,
                   min(FLASH_SEL_TILE, t), True)
    o_win = _flash(qcat, mnot, kw_s, vw_s, gates, qmax * jnp.max(jnp.abs(k_g[2])) * math.sqrt(HD_B), b, t,
                   min(FLASH_WIN_TILE, t), False)
    y = _sum_proj(o_cmp, o_sel, o_win, x, wts["wo"])
    wb = min(WINDOW, t)
    kv_out = kv_rows.reshape(b, t, N_KV_SLOTS, N_KV_B, HD_B)
    win_out = win_rows.reshape(b, t, 2, N_KV_B, HD_B)[:, t - wb:]
    return y, kv_out, win_out


def _gmlp_layer(x, g, w_in, ln_g, ln_b, w_s, b_s, w_out, *, single):
    gw = w_out.shape[0] // N_GROUPS_A
    if single:
        ws = jnp.repeat(w_s[:, 0, 0], gw)[None, :]
        bs = jnp.repeat(b_s[:, 0], gw)[None, :]
    else:
        ws = w_s
        bs = jnp.repeat(b_s.T, gw, axis=1)
    return _gmlp(x, g[None, :], w_in.astype(BF16), ln_g[None, :], ln_b[None, :], ws, bs, w_out.astype(BF16),
                 single=single)


def _mlstm_proj_body(x_ref, g_ref, wq_ref, wk_ref, wv_ref, wgi_ref, wo_ref, bif_ref,
                     q_ref, k_ref, v_ref, gi_ref, og_ref):
    xb = _rms(x_ref[...], g_ref[...]).astype(BF16)
    q_ref[...] = _dot(xb, wq_ref[...]).astype(BF16)
    k_ref[...] = _dot(xb, wk_ref[...]).astype(BF16)
    v_ref[...] = _dot(xb, wv_ref[...]).astype(BF16)
    gi_ref[...] = _dot(xb, wgi_ref[...]) + bif_ref[...]
    og_ref[...] = jax.nn.sigmoid(_dot(xb, wo_ref[...]))


def _mlstm_proj(x, g, wts):
    m, d = x.shape
    hv = N_HEADS_C * DV_C
    tm = _row_tile(m, 512)
    consts = [g, wts["wq"], wts["wk"], wts["wv"], wts["wgi"], wts["wo"], wts["bif"]]
    tile = lambda n: pl.BlockSpec((tm, n), lambda i: (i, 0))
    return pl.pallas_call(
        _mlstm_proj_body,
        out_shape=(jax.ShapeDtypeStruct((m, N_HEADS_C * LANE), BF16), jax.ShapeDtypeStruct((m, N_HEADS_C * LANE), BF16),
                   jax.ShapeDtypeStruct((m, hv), BF16), jax.ShapeDtypeStruct((m, LANE), F32),
                   jax.ShapeDtypeStruct((m, hv), F32)),
        grid=(m // tm,),
        in_specs=[tile(d)] + [_const_spec(c.shape) for c in consts],
        out_specs=(tile(N_HEADS_C * LANE), tile(N_HEADS_C * LANE), tile(hv), tile(LANE), tile(hv)),
        compiler_params=_params("parallel"),
        name="mlstm_proj",
    )(x, *consts)


def _mlstm_scan_body(q_ref, k_ref, v_ref, gi_ref, git_ref, hs_ref, c_out, n_out, m_out, c_s, n_s, m_s):
    c = pl.program_id(1)
    L = q_ref.shape[0]

    @pl.when(c == 0)
    def _():
        c_s[...] = jnp.zeros(c_s.shape, F32)
        n_s[...] = jnp.zeros(n_s.shape, F32)
        m_s[...] = jnp.zeros(m_s.shape, F32)

    row = lax.broadcasted_iota(jnp.int32, (L, L), 0)
    col = lax.broadcasted_iota(jnp.int32, (L, L), 1)
    causal = col <= row
    tril = causal.astype(BF16)
    gi = gi_ref[...]
    git = git_ref[...]
    fcol = jax.nn.log_sigmoid(gi)
    frow = jax.nn.log_sigmoid(git[N_HEADS_C:, :])
    bcol_all = sum(_dot(tril, part) for part in _split3(fcol))
    brow_all = sum(_dot_nt(part, tril) for part in _split3(frow))
    for h in range(N_HEADS_C):
        sl = slice(h * LANE, (h + 1) * LANE)
        q = q_ref[:, sl]
        k = k_ref[:, sl]
        v = v_ref[:, sl]
        bcol = bcol_all[:, N_HEADS_C + h:N_HEADS_C + h + 1]
        icol = gi[:, h:h + 1]
        brow = brow_all[h:h + 1, :]
        irow = git[h:h + 1, :]
        m_prev = m_s[h:h + 1, 0:1]
        dlog = jnp.where(causal, bcol - brow + irow, -jnp.inf)
        inter = bcol + m_prev
        m_t = jnp.maximum(inter, jnp.max(dlog, axis=1, keepdims=True))
        w = jnp.exp(dlog - m_t)
        a = jnp.exp(inter - m_t)
        s = _dot_nt(q, k) * w
        cq = _dot_nt(q, c_s[h].astype(BF16))
        num = a * cq + _dot(s.astype(BF16), v)
        nq = jnp.sum(q.astype(F32) * n_s[h:h + 1, :], axis=1, keepdims=True)
        den = a * nq + jnp.sum(s, axis=1, keepdims=True)
        hs_ref[:, sl] = num / jnp.maximum(jnp.abs(den), jnp.exp(-m_t))
        b_end = bcol[L - 1:L, :]
        wlog = b_end - bcol + icol
        m_new = jnp.maximum(b_end + m_prev, jnp.max(wlog, axis=0, keepdims=True))
        wk = jnp.exp(wlog - m_new)
        decay = jnp.exp(b_end + m_prev - m_new)
        vw = (v.astype(F32) * wk).astype(BF16)
        c_s[h] = decay * c_s[h] + _dot_tn(vw, k)
        n_s[h:h + 1, :] = decay * n_s[h:h + 1, :] + jnp.sum(k.astype(F32) * wk, axis=0, keepdims=True)
        m_s[h:h + 1, :] = jnp.broadcast_to(m_new, (1, LANE))

    @pl.when(c == pl.num_programs(1) - 1)
    def _():
        c_out[...] = c_s[...]
        n_out[...] = n_s[...]
        m_out[...] = m_s[...]


def _mlstm_scan(q, k, v, gi, git, b, t):
    L = math.gcd(t, CHUNK_C)
    nc = t // L
    hv = N_HEADS_C * DV_C
    tile = lambda n: pl.BlockSpec((L, n), lambda bi, c: (bi * nc + c, 0))
    return pl.pallas_call(
        _mlstm_scan_body,
        out_shape=(jax.ShapeDtypeStruct((b * t, hv), F32),
                   jax.ShapeDtypeStruct((b, N_HEADS_C, DV_C, LANE), F32),
                   jax.ShapeDtypeStruct((b, N_HEADS_C, LANE), F32), jax.ShapeDtypeStruct((b, N_HEADS_C, LANE), F32)),
        grid=(b, nc),
        in_specs=[tile(N_HEADS_C * LANE), tile(N_HEADS_C * LANE), tile(hv), tile(LANE),
                  pl.BlockSpec((None, 2 * N_HEADS_C, L), lambda bi, c: (bi, 0, c))],
        out_specs=(tile(hv), pl.BlockSpec((None, N_HEADS_C, DV_C, LANE), lambda bi, c: (bi, 0, 0, 0)),
                   pl.BlockSpec((None, N_HEADS_C, LANE), lambda bi, c: (bi, 0, 0)),
                   pl.BlockSpec((None, N_HEADS_C, LANE), lambda bi, c: (bi, 0, 0))),
        scratch_shapes=[pltpu.VMEM((N_HEADS_C, DV_C, LANE), F32), pltpu.VMEM((N_HEADS_C, LANE), F32),
                        pltpu.VMEM((N_HEADS_C, LANE), F32)],
        compiler_params=_params("parallel", "arbitrary"),
        name="mlstm_scan",
    )(q, k, v, gi, git)


def _mlstm_out_body(hs_ref, og_ref, hg_ref, x_ref, w_ref, o_ref):
    parts = []
    for h in range(N_HEADS_C):
        sl = slice(h * DV_C, (h + 1) * DV_C)
        parts.append((og_ref[:, sl] * _rms(hs_ref[:, sl], hg_ref[:, sl])).astype(BF16))
    o_ref[...] = x_ref[...] + _dot(jnp.concatenate(parts, axis=1), w_ref[...])


def _mlstm_out(hs, og, hg, x, w):
    m, d = x.shape
    hv = hs.shape[1]
    tm = _row_tile(m, 512)
    wide = pl.BlockSpec((tm, hv), lambda i: (i, 0))
    row = pl.BlockSpec((tm, d), lambda i: (i, 0))
    return pl.pallas_call(
        _mlstm_out_body,
        out_shape=jax.ShapeDtypeStruct((m, d), F32),
        grid=(m // tm,),
        in_specs=[wide, wide, _const_spec(hg.shape), row, _const_spec(w.shape)],
        out_specs=row,
        compiler_params=_params("parallel"),
        name="mlstm_out",
    )(hs, og, hg, x, w)


def _mlstm_weights(w_in, b_if):
    d = w_in.shape[0]
    hk, hv = N_HEADS_C * DK_C, N_HEADS_C * DV_C

    def spread(w):
        w = w.reshape(d, N_HEADS_C, DK_C)
        return jnp.pad(w, ((0, 0), (0, 0), (0, LANE - DK_C))).reshape(d, N_HEADS_C * LANE)

    wgi = jnp.pad(w_in[:, 2 * hk + hv:2 * hk + hv + 2 * N_HEADS_C], ((0, 0), (0, LANE - 2 * N_HEADS_C)))
    return {
        "wq": spread(w_in[:, :hk]).astype(BF16),
        "wk": (spread(w_in[:, hk:2 * hk]) * (DK_C ** -0.5)).astype(BF16),
        "wv": w_in[:, 2 * hk:2 * hk + hv].astype(BF16),
        "wgi": wgi.astype(BF16),
        "wo": w_in[:, 2 * hk + hv + 2 * N_HEADS_C:].astype(BF16),
        "bif": jnp.pad(b_if, (0, LANE - 2 * N_HEADS_C))[None, :],
    }


def _mlstm_prompt(x, g_mix, b, t, w_in, b_if, h_g, w_out):
    wts = _mlstm_weights(w_in, b_if)
    q, k, v, gi, og = _mlstm_proj(x, g_mix, wts)
    git = gi[:, :2 * N_HEADS_C].reshape(b, t, 2 * N_HEADS_C).transpose(0, 2, 1)
    hs, c, n, m = _mlstm_scan(q, k, v, gi, git, b, t)
    y = _mlstm_out(hs, og, h_g[None, :], x, w_out.astype(BF16))
    return y, c[..., :DK_C], n[..., :DK_C], m[..., 0]


def _proj_add_body(o_ref, x_ref, w_ref, out_ref):
    out_ref[...] = x_ref[...] + _dot(o_ref[...].astype(BF16), w_ref[...])


def _proj_add(o, x, w):
    m, d = x.shape
    tm = _row_tile(m, 512)
    return pl.pallas_call(
        _proj_add_body,
        out_shape=jax.ShapeDtypeStruct((m, d), F32),
        grid=(m // tm,),
        in_specs=[pl.BlockSpec((tm, o.shape[1]), lambda i: (i, 0)), pl.BlockSpec((tm, d), lambda i: (i, 0)),
                  _const_spec(w.shape)],
        out_specs=pl.BlockSpec((tm, d), lambda i: (i, 0)),
        compiler_params=_params("parallel"),
        name="proj_add",
    )(o, x, w)


def _nsa_step_body(pt_ref, *refs, n_pages, page, past_len):
    pages = refs[:n_pages]
    (win_ref, qr_ref, qn_ref, gates_ref, nkv_ref, nwin_ref, pe_ref, w1_ref, w2_ref, kcg_ref,
     o_ref, c_s, x_s) = refs[n_pages:]
    w = N_KV_B * HD_B
    length = n_pages * page
    nb = length // CMP_BLOCK
    t = past_len
    for p in range(n_pages):
        for c in range(2 * w // LANE):
            c_s[c, p * page:(p + 1) * page, :] = pages[p][c * LANE:(c + 1) * LANE, :].T
    for ng in range(nb // 8):
        for l in range(CMP_BLOCK):
            for c in range(2 * w // LANE):
                rows = c_s[c, pl.ds(ng * 8 * CMP_BLOCK + l, 8, stride=CMP_BLOCK), :]
                for half in range(LANE // HD_B):
                    x_s[c * (LANE // HD_B) + half, ng * 8:(ng + 1) * 8, l * HD_B:(l + 1) * HD_B] = (
                        rows[:, half * HD_B:(half + 1) * HD_B])
    lane_w = lax.broadcasted_iota(jnp.int32, (HD_B, w), 1)
    row_w = lax.broadcasted_iota(jnp.int32, (HD_B, w), 0)
    cmp_nat = []
    for slot in range(2):
        xs = x_s[slot * N_KV_B:(slot + 1) * N_KV_B].reshape(N_KV_B * nb, CMP_BLOCK * HD_B)
        hid = jax.nn.gelu(_dot((xs + pe_ref[slot]).astype(BF16), w1_ref[slot]))
        y = _dot(hid.astype(BF16), w2_ref[slot])
        if slot == 0:
            y = _rms(y, kcg_ref[...])
        nat = jnp.zeros((nb, w), F32)
        for g in range(N_KV_B):
            place = (lane_w == row_w + g * HD_B).astype(BF16)
            nat = nat + _dot(y[g * nb:(g + 1) * nb].astype(BF16), place)
        cmp_nat.append(nat.astype(BF16))
    kc, vc = cmp_nat
    qr = qr_ref[...]
    qn = qn_ref[...]
    gates = gates_ref[...]
    nh = N_HEADS_B
    blk = lax.broadcasted_iota(jnp.int32, (nh, nb), 1)
    p_c = _masked_softmax(_dot_nt(qn, kc), (blk + 1) * CMP_BLOCK - 1 <= t, 1)
    o_c = _dot(p_c.astype(BF16), vc)
    blk_t = lax.broadcasted_iota(jnp.int32, (nb, nh), 0)
    p_t = _masked_softmax(_dot_nt(kc, qn), (blk_t + 1) * CMP_BLOCK - 1 <= t, 0)
    gsum = (lax.broadcasted_iota(jnp.int32, (nh, LANE), 0) // REP_B
            == lax.broadcasted_iota(jnp.int32, (nh, LANE), 1)).astype(BF16)
    pair = (lax.broadcasted_iota(jnp.int32, (nb, nb), 1) // (SEL_BLOCK // CMP_BLOCK)
            == lax.broadcasted_iota(jnp.int32, (nb, nb), 0)).astype(BF16)
    imp = sum(_dot(part, gsum) for part in _split3(p_t))
    imp = sum(_dot(pair, part) for part in _split3(imp))
    sblk = lax.broadcasted_iota(jnp.int32, (nb, LANE), 0)
    cur = t // SEL_BLOCK
    forced = (sblk == 0) | (sblk == cur) | (sblk == cur - 1)
    score = jnp.where(forced, jnp.inf, jnp.where(sblk * SEL_BLOCK <= t, imp, -jnp.inf))
    sblk_f = sblk.astype(F32)
    pickable = score > -jnp.inf
    for _ in range(N_SEL):
        mx = jnp.max(score, axis=0, keepdims=True)
        first = jnp.min(jnp.where(score == mx, sblk_f, float(nb)), axis=0, keepdims=True)
        score = jnp.where(sblk_f == first, -jnp.inf, score)
    notsel = jnp.where(pickable, jnp.where(score > -jnp.inf, -MASK_BIG, 0.0), -MASK_BIG)
    bias = _dot_nt(gsum, notsel.astype(BF16)).astype(BF16)
    expand = (lax.broadcasted_iota(jnp.int32, (nb, length), 1) // SEL_BLOCK
              == lax.broadcasted_iota(jnp.int32, (nb, length), 0)).astype(BF16)
    nkv = nkv_ref[...]
    nwin = nwin_ref[...]

    def attend(s, k_new, v_new, weighted_values):
        s_new = jnp.sum(qr.astype(F32) * k_new.astype(BF16).astype(F32), axis=1, keepdims=True)
        m = jnp.maximum(jnp.max(s, axis=1, keepdims=True), s_new)
        e = jnp.exp(s - m)
        e_new = jnp.exp(s_new - m)
        den = jnp.sum(e, axis=1, keepdims=True) + e_new
        num = weighted_values(e.astype(BF16)) + e_new.astype(BF16).astype(F32) * v_new.astype(BF16).astype(F32)
        return num / den

    s_sel = jnp.concatenate([_dot(qr, pages[p][2 * w:3 * w, :].astype(BF16)) for p in range(n_pages)], axis=1)
    o_s = attend(s_sel + _dot(bias, expand), nkv[:, 2 * w:3 * w], nkv[:, 3 * w:],
                 lambda e: sum(_dot_nt(e[:, p * page:(p + 1) * page], pages[p][3 * w:, :].astype(BF16))
                               for p in range(n_pages)))
    wb = win_ref.shape[1]
    pos_w = t - wb + lax.broadcasted_iota(jnp.int32, (nh, wb), 1)
    ok_w = (pos_w >= 0) & (t - pos_w <= WINDOW)
    s_w = jnp.where(ok_w, _dot(qr, win_ref[:w, :].astype(BF16)), -MASK_BIG)
    o_w = attend(s_w, nwin[:, :w], nwin[:, w:], lambda e: _dot_nt(e, win_ref[w:, :].astype(BF16)))
    o_ref[...] = gates[:, 0:1] * o_c + gates[:, 1:2] * o_s + gates[:, 2:3] * o_w


def _nsa_step(page_table, cache, win_cache, qr, qn, gates, new_kv, new_win, pe, w1, w2, kc_g, past_len):
    bsz, n_pages = page_table.shape
    page = cache.shape[2]
    assert page == LANE
    w = N_KV_B * HD_B
    nb = n_pages * page // CMP_BLOCK
    wb = win_cache.shape[2]
    per = lambda shape: pl.BlockSpec((None,) + shape, lambda b, pt: (b,) + (0,) * len(shape))
    const = lambda a: pl.BlockSpec(a.shape, lambda b, pt: (0,) * a.ndim)
    page_specs = [pl.BlockSpec((None, 4 * w, page), lambda b, pt, p=p: (pt[b, p], 0, 0)) for p in range(n_pages)]
    return pl.pallas_call(
        functools.partial(_nsa_step_body, n_pages=n_pages, page=page, past_len=past_len),
        out_shape=jax.ShapeDtypeStruct((bsz, N_HEADS_B, w), F32),
        grid_spec=pltpu.PrefetchScalarGridSpec(
            num_scalar_prefetch=1,
            grid=(bsz,),
            in_specs=page_specs + [per((2 * w, wb)), per((N_HEADS_B, w)), per((N_HEADS_B, w)), per((N_HEADS_B, LANE)),
                                   per((1, 4 * w)), per((1, 2 * w)), const(pe), const(w1), const(w2), const(kc_g)],
            out_specs=per((N_HEADS_B, w)),
            scratch_shapes=[pltpu.VMEM((2 * w // LANE, n_pages * page, LANE), F32),
                            pltpu.VMEM((2 * N_KV_B, nb, CMP_BLOCK * HD_B), F32)]),
        compiler_params=_params("parallel"),
        name="nsa_step",
    )(page_table, *([cache] * n_pages), win_cache, qr, qn, gates, new_kv, new_win, pe, w1, w2, kc_g)


def _nsa_sample_step(x, g_mix, past_len, kv_cache, win_cache, page_table, w_in, q_g, k_g, pe, w_c1, w_c2, w_out):
    bsz, d = x.shape
    w = N_KV_B * HD_B
    assert past_len % CMP_BLOCK == 0 and past_len // SEL_BLOCK + 1 <= past_len // CMP_BLOCK
    wts = _nsa_weights(w_in, q_g, k_g, w_out)
    pos = jnp.full((bsz,), past_len, jnp.int32)
    qcat, gates, kv_rows, win_rows, _, _, _, _ = _nsa_proj(
        x, g_mix, wts, _rope_tables(pos, LANE), _rope_tables(pos, HD_B), 1, bsz)
    q5 = qcat.reshape(bsz, N_KV_B, REP_B, 2, HD_B)
    eye = jnp.eye(N_KV_B, dtype=BF16)
    qrows = (q5[:, :, :, :, None, :] * eye[None, :, None, None, :, None])
    qr = qrows[:, :, :, 0].reshape(bsz, N_HEADS_B, w)
    qn = qrows[:, :, :, 1].reshape(bsz, N_HEADS_B, w)
    gts = gates.reshape(bsz, N_KV_B, LANE)[:, :, :REP_B * 3].reshape(bsz, N_HEADS_B, 3)
    gts = jnp.pad(gts, ((0, 0), (0, 0), (0, LANE - 3)))
    pool, page = kv_cache.shape[:2]
    cache_t = kv_cache.reshape(pool, page, 4 * w).transpose(0, 2, 1)
    win_t = win_cache.reshape(bsz, -1, 2 * w).transpose(0, 2, 1)
    o = _nsa_step(page_table, cache_t, win_t, qr, qn, gts,
                  kv_rows.reshape(bsz, 1, 4 * w), win_rows.reshape(bsz, 1, 2 * w),
                  pe.reshape(2, 1, CMP_BLOCK * HD_B), w_c1.reshape(2, CMP_BLOCK * HD_B, HD_B).astype(BF16),
                  w_c2.astype(BF16), k_g[0][None, :], past_len)
    own = (jnp.arange(N_HEADS_B)[:, None] // REP_B == jnp.arange(N_KV_B)[None, :]).astype(F32)
    w_exp = own[:, :, None, None] * w_out.reshape(N_HEADS_B, 1, HD_B, d)
    y = _proj_add(o.reshape(bsz, N_HEADS_B * w), x, w_exp.reshape(N_HEADS_B * w, d).astype(BF16))
    return y, kv_rows.reshape(bsz, 1, N_KV_SLOTS, N_KV_B, HD_B), win_rows.reshape(bsz, 1, 2, N_KV_B, HD_B)


def _mlstm_step_body(q_ref, k_ref, vt_ref, gi_ref, c_ref, n_ref, m_ref, ht_ref, co_ref, no_ref, mo_ref, *, sb):
    lane8 = lax.broadcasted_iota(jnp.int32, (DV_C, N_HEADS_C), 1)
    gi = gi_ref[...]
    logf = jax.nn.log_sigmoid(gi)
    m_all = m_ref[...]
    for s in range(sb):
        ht = jnp.zeros((DV_C, N_HEADS_C), F32)
        m_new_row = jnp.zeros((1, N_HEADS_C), F32)
        lane_m = lax.broadcasted_iota(jnp.int32, (1, N_HEADS_C), 1)
        for h in range(N_HEADS_C):
            q = q_ref[s:s + 1, h * LANE:h * LANE + DK_C].astype(F32)
            k = k_ref[s:s + 1, h * LANE:h * LANE + DK_C].astype(F32)
            v = vt_ref[s, :, h:h + 1]
            c = c_ref[s, h]
            n = n_ref[s, h:h + 1, :]
            it = gi[s:s + 1, h:h + 1]
            b = logf[s:s + 1, N_HEADS_C + h:N_HEADS_C + h + 1]
            m0 = m_all[s:s + 1, h:h + 1]
            inter = b + m0
            m_t = jnp.maximum(inter, it)
            wgt = jnp.exp(it - m_t)
            a = jnp.exp(inter - m_t)
            sc = jnp.sum(q * k, axis=1, keepdims=True) * wgt
            num = a * jnp.sum(c * q, axis=1, keepdims=True) + sc * v
            den = a * jnp.sum(n * q, axis=1, keepdims=True) + sc
            hcol = num / jnp.maximum(jnp.abs(den), jnp.exp(-m_t))
            ht = jnp.where(lane8 == h, hcol, ht)
            co_ref[s, h] = a * c + (wgt * v) * k
            no_ref[s, h:h + 1, :] = a * n + wgt * k
            m_new_row = jnp.where(lane_m == h, m_t, m_new_row)
        ht_ref[s] = ht
        mo_ref[s:s + 1, :] = m_new_row


def _mlstm_step(q, k, vt, gi, c0, n0, m0):
    bsz = q.shape[0]
    sb = 8
    row = lambda n: pl.BlockSpec((sb, n), lambda i: (i, 0))
    c_spec = pl.BlockSpec((sb, N_HEADS_C, DV_C, DK_C), lambda i: (i, 0, 0, 0))
    n_spec = pl.BlockSpec((sb, N_HEADS_C, DK_C), lambda i: (i, 0, 0))
    vt_spec = pl.BlockSpec((sb, DV_C, N_HEADS_C), lambda i: (i, 0, 0))
    return pl.pallas_call(
        functools.partial(_mlstm_step_body, sb=sb),
        out_shape=(jax.ShapeDtypeStruct((bsz, DV_C, N_HEADS_C), F32), jax.ShapeDtypeStruct(c0.shape, F32),
                   jax.ShapeDtypeStruct(n0.shape, F32), jax.ShapeDtypeStruct(m0.shape, F32)),
        grid=(bsz // sb,),
        in_specs=[row(N_HEADS_C * LANE), row(N_HEADS_C * LANE), vt_spec, row(LANE), c_spec, n_spec, row(N_HEADS_C)],
        out_specs=(vt_spec, c_spec, n_spec, row(N_HEADS_C)),
        compiler_params=_params("parallel"),
        name="mlstm_step",
    )(q, k, vt, gi, c0, n0, m0)


def _mlstm_sample_step(x, g_mix, c0, n0, m0, w_in, b_if, h_g, w_out):
    bsz = x.shape[0]
    wts = _mlstm_weights(w_in, b_if)
    q, k, v, gi, og = _mlstm_proj(x, g_mix, wts)
    vt = v.astype(F32).reshape(bsz, N_HEADS_C, DV_C).transpose(0, 2, 1)
    ht, c, n, m = _mlstm_step(q, k, vt, gi, c0, n0, m0)
    hs = ht.transpose(0, 2, 1).reshape(bsz, N_HEADS_C * DV_C)
    y = _mlstm_out(hs, og, h_g[None, :], x, w_out.astype(BF16))
    return y, c, n, m


def kernel(x_prompt, x_sample, cache_nsa_kv, cache_nsa_win, state_mlstm_C, state_mlstm_n, state_mlstm_m, page_table,
           norm_mix_g, norm_ffn_g, ffn_w1, ffn_w2, a_w_in, a_ln_g, a_ln_b, a_w_s, a_b_s, a_w_out,
           b_w_in, b_q_g, b_k_g, b_pe, b_w_c1, b_w_c2, b_w_out, c_w_in, c_b_if, c_h_g, c_w_out):
    bp, t, d = x_prompt.shape
    bs, ts, _ = x_sample.shape
    assert ts == 1
    past_len = page_table.shape[1] * cache_nsa_kv.shape[2]
    xp = x_prompt.reshape(bp * t, d)
    xs = x_sample.reshape(bs, d)
    out = {k: [] for k in ("v_s", "kv_p", "win_p", "kv_s", "win_s", "C_p", "n_p", "m_p", "C_s", "n_s", "m_s")}
    for layer in range(norm_mix_g.shape[0]):
        kind, j = layer % 3, layer // 3
        gm = norm_mix_g[layer]
        if kind == 0:
            args = (a_w_in[j], a_ln_g[j], a_ln_b[j], a_w_s[j], a_b_s[j], a_w_out[j])
            xp = _gmlp_layer(xp, gm, *args, single=False)[0]
            xs, v = _gmlp_layer(xs, gm, *args, single=True)
            out["v_s"].append(v.reshape(bs, ts, -1))
        elif kind == 1:
            args = (b_w_in[j], b_q_g[j], b_k_g[j], b_pe[j], b_w_c1[j], b_w_c2[j], b_w_out[j])
            xp, kv, win = _nsa_prompt(xp, gm[None, :], bp, t, *args)
            out["kv_p"].append(kv)
            out["win_p"].append(win)
            xs, kv, win = _nsa_sample_step(xs, gm[None, :], past_len, cache_nsa_kv[j], cache_nsa_win[j], page_table,
                                           *args)
            out["kv_s"].append(kv)
            out["win_s"].append(win)
        else:
            args = (c_w_in[j], c_b_if[j], c_h_g[j], c_w_out[j])
            xp, c, n, m = _mlstm_prompt(xp, gm[None, :], bp, t, *args)
            out["C_p"].append(c)
            out["n_p"].append(n)
            out["m_p"].append(m)
            xs, c, n, m = _mlstm_sample_step(xs, gm[None, :], state_mlstm_C[j], state_mlstm_n[j], state_mlstm_m[j],
                                             *args)
            out["C_s"].append(c)
            out["n_s"].append(n)
            out["m_s"].append(m)
        gf = norm_ffn_g[layer][None, :]
        w1, w2 = ffn_w1[layer].astype(BF16), ffn_w2[layer].astype(BF16)
        xp = _ffn(xp, gf, w1, w2)
        xs = _ffn(xs, gf, w1, w2)
    st = {k: jnp.stack(v) for k, v in out.items()}
    return (xp.reshape(bp, t, d), xs.reshape(bs, ts, d), st["v_s"], st["kv_p"], st["win_p"], st["kv_s"], st["win_s"],
            st["C_p"], st["n_p"], st["m_p"], st["C_s"], st["n_s"], st["m_s"])
```

```python
import functools
import math

import jax
import jax.numpy as jnp
from jax import lax
from jax.experimental import pallas as pl
from jax.experimental.pallas import tpu as pltpu

F32 = jnp.float32
BF16 = jnp.bfloat16

EPS = 1e-6
CHUNK_A = 128
N_GROUPS_A = 8
N_HEADS_B = 16
N_KV_B = 4
REP_B = N_HEADS_B // N_KV_B
HD_B = 64
ROT_DIM = 16
ROPE_THETA = 500000.0
CMP_BLOCK = 32
SEL_BLOCK = 64
N_SEL = 16
WINDOW = 512
N_KV_SLOTS = 4
N_HEADS_C = 8
DK_C = 64
DV_C = 128
CHUNK_C = 128
SCALE_B = HD_B ** -0.5

LANE = 128
VMEM_LIMIT_BYTES = 56 * 1024 * 1024
MASK_BIG = 1e30
M_INIT = -1e20
N_SEL_PAD = 128
N_CMP_PAD = 2 * N_SEL_PAD
SHIFT_MAX = 40.0
FLASH_SEL_TILES = (1024, 512)
FLASH_WIN_TILES = (512, 512)
MASK_CAUSAL = 1
MASK_BAND = 2


def _params(*sem):
    return pltpu.CompilerParams(dimension_semantics=sem, vmem_limit_bytes=VMEM_LIMIT_BYTES)


def _dot(a, b):
    return jnp.dot(a, b, preferred_element_type=F32)


def _dot_nt(a, b):
    return lax.dot_general(a, b, (((1,), (1,)), ((), ())), preferred_element_type=F32)


def _dot_tn(a, b):
    return lax.dot_general(a, b, (((0,), (0,)), ((), ())), preferred_element_type=F32)


def _rms(x, g):
    return x * lax.rsqrt(jnp.mean(x * x, axis=-1, keepdims=True) + EPS) * g


def _split3(x):
    a = x.astype(BF16)
    r = x - a.astype(F32)
    b = r.astype(BF16)
    c = (r - b.astype(F32)).astype(BF16)
    return a, b, c


def _const_spec(shape):
    n = len(shape)
    return pl.BlockSpec(shape, lambda *_: (0,) * n)


def _row_tile(m, pref):
    t = min(pref, m)
    while m % t:
        t //= 2
    return t


def _ffn_body(x_ref, g_ref, w1_ref, w2_ref, o_ref, *, ck):
    x = x_ref[...]
    xb = _rms(x, g_ref[...]).astype(BF16)
    acc = x
    for j in range(w1_ref.shape[1] // ck):
        h = jnp.maximum(_dot(xb, w1_ref[:, j * ck:(j + 1) * ck]), 0.0)
        acc = acc + _dot((h * h).astype(BF16), w2_ref[j * ck:(j + 1) * ck, :])
    o_ref[...] = acc


def _ffn(x, g, w1, w2):
    m, d = x.shape
    tm = _row_tile(m, 512)
    return pl.pallas_call(
        functools.partial(_ffn_body, ck=1024),
        out_shape=jax.ShapeDtypeStruct((m, d), F32),
        grid=(m // tm,),
        in_specs=[pl.BlockSpec((tm, d), lambda i: (i, 0)), _const_spec(g.shape),
                  _const_spec(w1.shape), _const_spec(w2.shape)],
        out_specs=pl.BlockSpec((tm, d), lambda i: (i, 0)),
        compiler_params=_params("parallel"),
        name="ffn",
    )(x, g, w1, w2)


def _gmlp_body(x_ref, g_ref, win_ref, lng_ref, lnb_ref, ws_ref, bs_ref, wout_ref, o_ref, *maybe_v_ref, single):
    x = x_ref[...]
    dg = lng_ref.shape[1]
    xb = _rms(x, g_ref[...]).astype(BF16)
    u = jax.nn.gelu(_dot(xb, win_ref[:, :dg]))
    v = jax.nn.gelu(_dot(xb, win_ref[:, dg:]))
    mu = jnp.mean(v, axis=-1, keepdims=True)
    vc = v - mu
    var = jnp.mean(vc * vc, axis=-1, keepdims=True)
    v = vc * lax.rsqrt(var + EPS) * lng_ref[...] + lnb_ref[...]
    if single:
        maybe_v_ref[0][...] = v
        gate = v * ws_ref[...] + bs_ref[...]
    else:
        gw = dg // N_GROUPS_A
        row = lax.broadcasted_iota(jnp.int32, (CHUNK_A, CHUNK_A), 0)
        col = lax.broadcasted_iota(jnp.int32, (CHUNK_A, CHUNK_A), 1)
        causal = col <= row
        vb = v.astype(BF16)
        chunks = []
        for c in range(x.shape[0] // CHUNK_A):
            parts = []
            for gi in range(N_GROUPS_A):
                w = jnp.where(causal, ws_ref[gi], 0.0).astype(BF16)
                parts.append(_dot(w, vb[c * CHUNK_A:(c + 1) * CHUNK_A, gi * gw:(gi + 1) * gw]))
            chunks.append(jnp.concatenate(parts, axis=1) + bs_ref[...])
        gate = jnp.concatenate(chunks, axis=0)
    o_ref[...] = x + _dot((u * gate).astype(BF16), wout_ref[...])


def _gmlp(x, g, w_in, ln_g, ln_b, ws, bs, w_out, *, single):
    m, d = x.shape
    dg = w_out.shape[0]
    tm = _row_tile(m, 256)
    n_out = 2 if single else 1
    outs = pl.pallas_call(
        functools.partial(_gmlp_body, single=single),
        out_shape=(jax.ShapeDtypeStruct((m, d), F32), jax.ShapeDtypeStruct((m, dg), F32))[:n_out],
        grid=(m // tm,),
        in_specs=[pl.BlockSpec((tm, d), lambda i: (i, 0)), _const_spec(g.shape), _const_spec(w_in.shape),
                  _const_spec(ln_g.shape), _const_spec(ln_b.shape), _const_spec(ws.shape), _const_spec(bs.shape),
                  _const_spec(w_out.shape)],
        out_specs=(pl.BlockSpec((tm, d), lambda i: (i, 0)), pl.BlockSpec((tm, dg), lambda i: (i, 0)))[:n_out],
        compiler_params=_params("parallel"),
        name="gmlp_single" if single else "gmlp",
    )(x, g, w_in, ln_g, ln_b, ws, bs, w_out)
    return outs if single else (outs[0], None)


def _rope_tables(pos, seg):
    half = ROT_DIM // 2
    freq = jnp.power(ROPE_THETA, -jnp.arange(half, dtype=F32) * 2.0 / ROT_DIM)
    ang = pos.astype(F32)[:, None] * freq[None, :]
    cos, sin = jnp.cos(ang), jnp.sin(ang)
    t = pos.shape[0]
    one = jnp.ones((t, seg - ROT_DIM), F32)
    zero = jnp.zeros((t, seg - ROT_DIM), F32)
    z8 = jnp.zeros((t, half), F32)
    tabs = [jnp.concatenate([cos, cos, one], 1), jnp.concatenate([-sin, z8, zero], 1),
            jnp.concatenate([z8, sin, zero], 1)]
    return jnp.stack([jnp.tile(a, (1, LANE // seg)) for a in tabs])


def _rope128(x, tab):
    return x * tab[0] + pltpu.roll(x, LANE - ROT_DIM // 2, 1) * tab[1] + pltpu.roll(x, ROT_DIM // 2, 1) * tab[2]


def _nsa_proj_body(x_ref, g_ref, wq_ref, wg_ref, wkv_ref, seg_ref, spread_ref, qg_ref, kg_ref, tq_ref, tk_ref,
                   qcat_ref, gates_ref, kv_ref, win_ref, ks_ref, vs_ref, kw_ref, vw_ref):
    x = x_ref[...]
    xb = _rms(x, g_ref[...]).astype(BF16)
    tq = tq_ref[...]
    tk = tk_ref[...]
    qg = qg_ref[...]
    for h in range(N_HEADS_B):
        q = _dot(xb, wq_ref[:, h * LANE:(h + 1) * LANE])
        ms = jnp.sum(q * q, axis=-1, keepdims=True) * (1.0 / LANE)
        qn = q * lax.rsqrt(ms + EPS) * qg
        qcat_ref[:, h * LANE:(h + 1) * LANE] = (_rope128(qn, tq) * SCALE_B).astype(BF16)
    gates_ref[...] = jax.nn.sigmoid(_dot(xb, wg_ref[...]))
    kv = _dot(xb, wkv_ref[...])
    w = N_KV_B * HD_B
    seg = seg_ref[...]
    spread = spread_ref[...]

    def head_norm(k, gain):
        k2 = k * k
        hi = k2.astype(BF16)
        lo = (k2 - hi.astype(F32)).astype(BF16)
        ss = _dot(hi, seg) + _dot(lo, seg)
        return k * lax.rsqrt(ss * (1.0 / HD_B) + EPS) * gain

    def rope(k):
        return jnp.concatenate([_rope128(k[:, j * LANE:(j + 1) * LANE], tk) for j in range(w // LANE)], axis=1)

    ks = rope(head_norm(kv[:, 2 * w:3 * w], kg_ref[0:1, :]))
    kw = rope(head_norm(kv[:, 4 * w:5 * w], kg_ref[1:2, :]))
    vs = kv[:, 3 * w:4 * w]
    vw = kv[:, 5 * w:6 * w]
    kv_ref[:, :2 * w] = kv[:, :2 * w]
    kv_ref[:, 2 * w:3 * w] = ks
    kv_ref[:, 3 * w:] = vs
    win_ref[:, :w] = kw
    win_ref[:, w:] = vw
    lane = lax.broadcasted_iota(jnp.int32, (1, N_KV_B * LANE), 1)
    ones_hi = ((lane & HD_B) != 0).astype(F32)
    ks_ref[...] = _dot(ks.astype(BF16), spread).astype(BF16)
    kw_ref[...] = _dot(kw.astype(BF16), spread).astype(BF16)
    vs_ref[...] = (_dot(vs.astype(BF16), spread) + ones_hi).astype(BF16)
    vw_ref[...] = (_dot(vw.astype(BF16), spread) + ones_hi).astype(BF16)


def _nsa_proj(x, g, wts, tab_q, tab_k, n_tab_tiles, tm):
    m, d = x.shape
    w = N_KV_B * HD_B
    ws = N_KV_B * LANE
    tile = lambda n: pl.BlockSpec((tm, n), lambda i: (i, 0))
    tab = pl.BlockSpec((3, tm, LANE), lambda i: (0, i % n_tab_tiles, 0))
    consts = [g, wts["wq"], wts["wg"], wts["wkv"], wts["seg"], wts["spread"], wts["qg"], wts["kg"]]
    return pl.pallas_call(
        _nsa_proj_body,
        out_shape=(jax.ShapeDtypeStruct((m, N_HEADS_B * LANE), BF16), jax.ShapeDtypeStruct((m, ws), F32),
                   jax.ShapeDtypeStruct((m, 4 * w), F32), jax.ShapeDtypeStruct((m, 2 * w), F32),
                   jax.ShapeDtypeStruct((m, ws), BF16), jax.ShapeDtypeStruct((m, ws), BF16),
                   jax.ShapeDtypeStruct((m, ws), BF16), jax.ShapeDtypeStruct((m, ws), BF16)),
        grid=(m // tm,),
        in_specs=[tile(d)] + [_const_spec(c.shape) for c in consts] + [tab, tab],
        out_specs=(tile(N_HEADS_B * LANE), tile(ws), tile(4 * w), tile(2 * w), tile(ws), tile(ws), tile(ws), tile(ws)),
        compiler_params=_params("parallel"),
        name="nsa_proj",
    )(x, *consts, tab_q, tab_k)


def _compress_body(x_ref, pe_ref, w1_ref, w2_ref, g_ref, o_ref, *, norm):
    xb = (x_ref[...] + pe_ref[...]).astype(BF16)
    hid = jax.nn.gelu(_dot(xb, w1_ref[...]))
    y = _dot(hid.astype(BF16), w2_ref[...])
    if norm:
        y = _rms(y, g_ref[...])
    o_ref[...] = y


def _compress(x, pe, w1, w2, g, *, norm):
    r, kdim = x.shape
    tr = _row_tile(r, 512)
    return pl.pallas_call(
        functools.partial(_compress_body, norm=norm),
        out_shape=jax.ShapeDtypeStruct((r, HD_B), F32),
        grid=(r // tr,),
        in_specs=[pl.BlockSpec((tr, kdim), lambda i: (i, 0)), _const_spec(pe.shape), _const_spec(w1.shape),
                  _const_spec(w2.shape), _const_spec(g.shape)],
        out_specs=pl.BlockSpec((tr, HD_B), lambda i: (i, 0)),
        compiler_params=_params("parallel"),
        name="nsa_compress_k" if norm else "nsa_compress_v",
    )(x, pe, w1, w2, g)


def _masked_softmax(s, mask, axis):
    sm = jnp.where(mask, s, -jnp.inf)
    mx = jnp.max(sm, axis=axis, keepdims=True)
    mx = jnp.where(mx > -jnp.inf, mx, 0.0)
    e = jnp.where(mask, jnp.exp(s - mx), 0.0)
    return e / jnp.maximum(jnp.sum(e, axis=axis, keepdims=True), 1e-30)


CMP_SEGMENTS = (64, 64, 128)


def _cmp_block_order():
    order, base = [], 0
    for size in CMP_SEGMENTS:
        order += list(range(base, base + size, 2)) + list(range(base + 1, base + size, 2))
        base += size
    assert base == N_CMP_PAD
    return order


def _cmp_select_body(q_ref, kct_ref, vct_ref, gates_ref, o_ref, mnot_ref, *, tq):
    i = pl.program_id(2)
    q0 = i * tq
    gates = gates_ref[...]

    def run(nv):
        hv = nv // 2
        row = lax.broadcasted_iota(jnp.int32, (nv, tq), 0)
        tok = q0 + lax.broadcasted_iota(jnp.int32, (nv, tq), 1)
        blk_c = jnp.zeros((nv, tq), jnp.int32)
        base = 0
        for size in CMP_SEGMENTS:
            if base < nv:
                local = row - base
                seg_blk = base + jnp.where(local < size // 2, 2 * local, 2 * (local - size // 2) + 1)
                blk_c = jnp.where((row >= base) & (row < base + size), seg_blk, blk_c)
            base += size
        mask = (blk_c + 1) * CMP_BLOCK - 1 <= tok
        kct = kct_ref[:nv, :]
        vct = vct_ref[:, :nv]
        imp_parts = None
        for r in range(REP_B):
            qh = q_ref[:, r * LANE:(r + 1) * LANE]
            pt = _masked_softmax(_dot_nt(kct, qh), mask, 0)
            o_ref[:, r * LANE:(r + 1) * LANE] = _dot(vct, pt.astype(BF16)).T * gates[:, 3 * r:3 * r + 1]
            parts, base = [], 0
            for size in CMP_SEGMENTS:
                if base < nv:
                    parts.append(pt[base:base + size // 2] + pt[base + size // 2:base + size])
                base += size
            imp_parts = parts if imp_parts is None else [a + b for a, b in zip(imp_parts, parts)]
        imp = jnp.concatenate(imp_parts, axis=0)
        blk = lax.broadcasted_iota(jnp.int32, (hv, tq), 0)
        t_s = q0 + lax.broadcasted_iota(jnp.int32, (hv, tq), 1)
        cur = t_s // SEL_BLOCK
        forced = (blk == 0) | (blk == cur) | (blk == cur - 1)
        score = jnp.where(forced, jnp.inf, jnp.where(blk * SEL_BLOCK <= t_s, imp, -jnp.inf))
        blk_f = blk.astype(F32)
        pickable = score > -jnp.inf
        for _ in range(N_SEL):
            mx = jnp.max(score, axis=0, keepdims=True)
            first = jnp.min(jnp.where(score == mx, blk_f, float(N_SEL_PAD)), axis=0, keepdims=True)
            score = jnp.where(blk_f == first, -jnp.inf, score)
        mnot = jnp.where(pickable, jnp.where(score > -jnp.inf, 1.0, 0.0), 1.0)
        if hv < N_SEL_PAD:
            mnot = jnp.concatenate([mnot, jnp.ones((N_SEL_PAD - hv, tq), F32)], axis=0)
        mnot_ref[...] = mnot.T.astype(BF16)

    need = (q0 + tq) // CMP_BLOCK
    bounds, base = [], 0
    for size in CMP_SEGMENTS:
        base += size
        bounds.append(base)
    lo = 0
    for nv in bounds:
        pl.when((need > lo) & (need <= nv))(functools.partial(run, nv))
        lo = nv


def _cmp_select(qcat, kct, vct, gates, b, t, tq):
    m = b * t
    nq = t // tq
    return pl.pallas_call(
        functools.partial(_cmp_select_body, tq=tq),
        out_shape=(jax.ShapeDtypeStruct((m, N_HEADS_B * LANE), F32), jax.ShapeDtypeStruct((m, N_KV_B * LANE), BF16)),
        grid=(b, N_KV_B, nq),
        in_specs=[pl.BlockSpec((tq, REP_B * LANE), lambda bi, g, i: (bi * nq + i, g)),
                  pl.BlockSpec((None, None, N_CMP_PAD, LANE), lambda bi, g, i: (bi, g, 0, 0)),
                  pl.BlockSpec((None, None, LANE, N_CMP_PAD), lambda bi, g, i: (bi, g, 0, 0)),
                  pl.BlockSpec((tq, LANE), lambda bi, g, i: (bi * nq + i, g))],
        out_specs=(pl.BlockSpec((tq, REP_B * LANE), lambda bi, g, i: (bi * nq + i, g)),
                   pl.BlockSpec((tq, LANE), lambda bi, g, i: (bi * nq + i, g))),
        compiler_params=_params("parallel", "parallel", "parallel"),
        name="nsa_cmp_select",
    )(qcat, kct, vct, gates)


def _flash_body(tab_ref, bound_ref, q_ref, mnot_ref, k_ref, v_ref, gates_ref, o_ref, qs_ref, ks_ref, m_ref, acc_ref, *,
                tq, tk, sel, gate_col, fixed, modes):
    step_id = pl.program_id(2)
    i = tab_ref[0, step_id]
    j = tab_ref[1, step_id]
    rows = REP_B * tq

    @pl.when(tab_ref[3, step_id] == 1)
    def _():
        if not fixed:
            m_ref[...] = jnp.full(m_ref.shape, M_INIT, F32)
        acc_ref[...] = jnp.zeros(acc_ref.shape, F32)
        lane = lax.broadcasted_iota(jnp.int32, (tq, LANE), 1)
        for r in range(REP_B):
            qh = q_ref[:, r * LANE:(r + 1) * LANE]
            if fixed:
                qh = jnp.where(lane < HD_B, qh, jnp.ones_like(qh))
            if sel:
                qs_ref[r * tq:(r + 1) * tq, :LANE] = mnot_ref[...]
                qs_ref[r * tq:(r + 1) * tq, LANE:] = qh
            else:
                qs_ref[r * tq:(r + 1) * tq, :] = qh

    def step(masked):
        kt = k_ref[...]
        if fixed:
            lane = lax.broadcasted_iota(jnp.int32, (tk, LANE), 1)
            kt = jnp.where(lane == HD_B, -bound_ref[0], kt.astype(F32)).astype(BF16)
        if sel:
            kpos = j * tk + lax.broadcasted_iota(jnp.int32, (tk, LANE), 0)
            lane = lax.broadcasted_iota(jnp.int32, (tk, LANE), 1)
            ks_ref[:, :LANE] = jnp.where(kpos // SEL_BLOCK == lane, -MASK_BIG, 0.0).astype(BF16)
            ks_ref[:, LANE:] = kt
            kmat = ks_ref[...]
        else:
            kmat = kt
        s = _dot_nt(qs_ref[...], kmat)
        if masked:
            t = i * tq + lax.broadcasted_iota(jnp.int32, (rows, tk), 0) % tq
            key = j * tk + lax.broadcasted_iota(jnp.int32, (rows, tk), 1)
            if masked & MASK_CAUSAL:
                s = jnp.where(key <= t, s, -MASK_BIG)
            if masked & MASK_BAND:
                s = jnp.where(t - key <= WINDOW, s, -MASK_BIG)
        if fixed:
            acc_ref[...] += _dot(jnp.exp(s).astype(BF16), v_ref[...])
        else:
            m_old = m_ref[...]
            m_new = jnp.maximum(m_old, jnp.max(s, axis=-1, keepdims=True))
            p = jnp.exp(s - m_new[:, :1])
            acc_ref[...] = jnp.exp(m_old - m_new) * acc_ref[...] + _dot(p.astype(BF16), v_ref[...])
            m_ref[...] = m_new

    for mode in modes:
        pl.when(tab_ref[2, step_id] == mode)(functools.partial(step, mode))

    @pl.when(tab_ref[4, step_id] == 1)
    def _():
        gates = gates_ref[...]
        lane = lax.broadcasted_iota(jnp.int32, (tq, LANE), 1)
        for r in range(REP_B):
            a = acc_ref[r * tq:(r + 1) * tq, :]
            o = a / a[:, HD_B:HD_B + 1]
            g = gates[:, 3 * r + gate_col:3 * r + gate_col + 1]
            o_ref[:, r * LANE:(r + 1) * LANE] = jnp.where(lane < HD_B, o * g, 0.0)


def _flash_steps(t, tq, tk, sel):
    steps = []
    for i in range(t // tq):
        q_lo, q_hi = i * tq, (i + 1) * tq - 1
        k_lo = 0 if sel else max(q_lo - WINDOW, 0)
        js = list(range(k_lo // tk, q_hi // tk + 1))
        for j in js:
            causal = (j + 1) * tk - 1 > q_lo
            band = not sel and q_hi - j * tk > WINDOW
            steps.append((i, j, MASK_CAUSAL * causal + MASK_BAND * band, j == js[0], j == js[-1]))
    return steps


def _flash(qcat, mnot, k, v, gates, bound, b, t, tq, tk, sel):
    m = b * t
    nq, nk = t // tq, t // tk
    steps = _flash_steps(t, tq, tk, sel)
    tab = jnp.asarray(steps, jnp.int32).T
    modes = tuple(sorted({s[2] for s in steps}))
    kdim = 2 * LANE if sel else LANE
    qidx = lambda bi, g, p, *pf: (bi * nq + pf[0][0, p], g)
    kidx = lambda bi, g, p, *pf: (bi * nk + pf[0][1, p], g)

    def call(fixed):
        name = ("nsa_flash_sel" if sel else "nsa_flash_win") + ("" if fixed else "_online")
        return pl.pallas_call(
            functools.partial(_flash_body, tq=tq, tk=tk, sel=sel, gate_col=1 if sel else 2, fixed=fixed,
                              modes=modes),
            out_shape=jax.ShapeDtypeStruct((m, N_HEADS_B * LANE), F32),
            grid_spec=pltpu.PrefetchScalarGridSpec(
                num_scalar_prefetch=2,
                grid=(b, N_KV_B, len(steps)),
                in_specs=[pl.BlockSpec((tq, REP_B * LANE), qidx), pl.BlockSpec((tq, LANE), qidx),
                          pl.BlockSpec((tk, LANE), kidx), pl.BlockSpec((tk, LANE), kidx),
                          pl.BlockSpec((tq, LANE), qidx)],
                out_specs=pl.BlockSpec((tq, REP_B * LANE), qidx),
                scratch_shapes=[pltpu.VMEM((REP_B * tq, kdim), BF16), pltpu.VMEM((tk, 2 * LANE), BF16),
                                pltpu.VMEM((REP_B * tq, LANE), F32), pltpu.VMEM((REP_B * tq, LANE), F32)]),
            compiler_params=_params("parallel", "parallel", "arbitrary"),
            name=name,
        )(tab, bound.reshape(1), qcat, mnot, k, v, gates)

    return lax.cond(bound <= SHIFT_MAX, lambda: call(True), lambda: call(False))


def _sum_proj_body(a_ref, b_ref, c_ref, x_ref, w_ref, o_ref):
    o = (a_ref[...] + b_ref[...] + c_ref[...]).astype(BF16)
    o_ref[...] = x_ref[...] + _dot(o, w_ref[...])


def _sum_proj(a, b, c, x, w):
    m, d = x.shape
    kdim = a.shape[1]
    tm = _row_tile(m, 512)
    big = pl.BlockSpec((tm, kdim), lambda i: (i, 0))
    row = pl.BlockSpec((tm, d), lambda i: (i, 0))
    return pl.pallas_call(
        _sum_proj_body,
        out_shape=jax.ShapeDtypeStruct((m, d), F32),
        grid=(m // tm,),
        in_specs=[big, big, big, row, _const_spec(w.shape)],
        out_specs=row,
        compiler_params=_params("parallel"),
        name="nsa_out_proj",
    )(a, b, c, x, w)


def _nsa_weights(w_in, q_g, k_g, w_out):
    d = w_in.shape[0]
    nq = N_HEADS_B * HD_B
    w = N_KV_B * HD_B
    wq = w_in[:, :nq].reshape(d, N_HEADS_B, 1, HD_B)
    wq = jnp.broadcast_to(wq, (d, N_HEADS_B, 2, HD_B)).reshape(d, N_HEADS_B * LANE)
    wg = w_in[:, nq:nq + 3 * N_HEADS_B].reshape(d, N_KV_B, REP_B * 3)
    wg = jnp.pad(wg, ((0, 0), (0, 0), (0, LANE - REP_B * 3))).reshape(d, N_KV_B * LANE)
    wkv = w_in[:, nq + 3 * N_HEADS_B:]
    lane = jnp.arange(w)
    seg = (lane[:, None] // HD_B == lane[None, :] // HD_B).astype(BF16)
    spread = (lane[:, None] // HD_B * LANE + lane[:, None] % HD_B == jnp.arange(N_KV_B * LANE)[None, :]).astype(BF16)
    wo = jnp.pad(w_out.reshape(N_HEADS_B, HD_B, -1), ((0, 0), (0, LANE - HD_B), (0, 0)))
    return {
        "wq": wq.astype(BF16), "wg": wg.astype(BF16), "wkv": wkv.astype(BF16), "seg": seg, "spread": spread,
        "qg": jnp.tile(q_g, 2)[None, :], "kg": jnp.stack([jnp.tile(k_g[1], N_KV_B), jnp.tile(k_g[2], N_KV_B)]),
        "wo": wo.reshape(N_HEADS_B * LANE, -1).astype(BF16),
    }


def _cmp_inputs(rows, b, length):
    nb = length // CMP_BLOCK
    x = rows.reshape(b, length, N_KV_B, HD_B)[:, :nb * CMP_BLOCK].reshape(b, nb, CMP_BLOCK, N_KV_B, HD_B)
    return x.transpose(0, 1, 3, 2, 4).reshape(b * nb * N_KV_B, CMP_BLOCK * HD_B)


def _cmp_blocks(kc_rows, vc_rows, b, length, pe, w_c1, w_c2, kc_g):
    nb = length // CMP_BLOCK
    pe_k, pe_v = pe[0].reshape(1, -1), pe[1].reshape(1, -1)
    w1k = w_c1[0].reshape(CMP_BLOCK * HD_B, HD_B).astype(BF16)
    w1v = w_c1[1].reshape(CMP_BLOCK * HD_B, HD_B).astype(BF16)
    g = kc_g[None, :]
    kc = _compress(_cmp_inputs(kc_rows, b, length), pe_k, w1k, w_c2[0].astype(BF16), g, norm=True)
    vc = _compress(_cmp_inputs(vc_rows, b, length), pe_v, w1v, w_c2[1].astype(BF16), g, norm=False)
    return kc.reshape(b, nb, N_KV_B, HD_B), vc.reshape(b, nb, N_KV_B, HD_B)


def _nsa_prompt(x, g_mix, b, t, w_in, q_g, k_g, pe, w_c1, w_c2, w_out):
    wts = _nsa_weights(w_in, q_g, k_g, w_out)
    w = N_KV_B * HD_B
    tm = 256
    pos = jnp.arange(t)
    qcat, gates, kv_rows, win_rows, ks_s, vs_s, kw_s, vw_s = _nsa_proj(
        x, g_mix, wts, _rope_tables(pos, LANE), _rope_tables(pos, HD_B), t // tm, tm)
    kc_blk, vc_blk = _cmp_blocks(kv_rows[:, :w], kv_rows[:, w:2 * w], b, t, pe, w_c1, w_c2, k_g[0])
    nb = t // CMP_BLOCK
    assert nb <= N_CMP_PAD and t % SEL_BLOCK == 0

    order = jnp.asarray(_cmp_block_order(), jnp.int32)

    def blocks(a, lo):
        a = jnp.pad(a.transpose(0, 2, 1, 3), ((0, 0), (0, 0), (0, N_CMP_PAD - nb), (lo, LANE - HD_B - lo)))
        return a[:, :, order].astype(BF16)

    kct = blocks(kc_blk, HD_B)
    vct = blocks(vc_blk, 0).transpose(0, 1, 3, 2)
    tq = 256
    o_cmp, mnot = _cmp_select(qcat, kct, vct, gates, b, t, tq)
    qmax = jnp.max(jnp.abs(q_g))
    o_sel = _flash(qcat, mnot, ks_s, vs_s, gates, qmax * jnp.max(jnp.abs(k_g[1])) * math.sqrt(HD_B), b, t,
                   *(min(n, t) for n in FLASH_SEL_TILES), True)
    o_win = _flash(qcat, mnot, kw_s, vw_s, gates, qmax * jnp.max(jnp.abs(k_g[2])) * math.sqrt(HD_B), b, t,
                   *(min(n, t) for n in FLASH_WIN_TILES), False)
    y = _sum_proj(o_cmp, o_sel, o_win, x, wts["wo"])
    wb = min(WINDOW, t)
    kv_out = kv_rows.reshape(b, t, N_KV_SLOTS, N_KV_B, HD_B)
    win_out = win_rows.reshape(b, t, 2, N_KV_B, HD_B)[:, t - wb:]
    return y, kv_out, win_out


def _gmlp_layer(x, g, w_in, ln_g, ln_b, w_s, b_s, w_out, *, single):
    gw = w_out.shape[0] // N_GROUPS_A
    if single:
        ws = jnp.repeat(w_s[:, 0, 0], gw)[None, :]
        bs = jnp.repeat(b_s[:, 0], gw)[None, :]
    else:
        ws = w_s
        bs = jnp.repeat(b_s.T, gw, axis=1)
    return _gmlp(x, g[None, :], w_in.astype(BF16), ln_g[None, :], ln_b[None, :], ws, bs, w_out.astype(BF16),
                 single=single)


def _mlstm_proj_body(x_ref, g_ref, wq_ref, wk_ref, wv_ref, wgi_ref, wo_ref, bif_ref,
                     q_ref, k_ref, v_ref, gi_ref, og_ref):
    xb = _rms(x_ref[...], g_ref[...]).astype(BF16)
    q_ref[...] = _dot(xb, wq_ref[...]).astype(BF16)
    k_ref[...] = _dot(xb, wk_ref[...]).astype(BF16)
    v_ref[...] = _dot(xb, wv_ref[...]).astype(BF16)
    gi_ref[...] = _dot(xb, wgi_ref[...]) + bif_ref[...]
    og_ref[...] = jax.nn.sigmoid(_dot(xb, wo_ref[...]))


def _mlstm_proj(x, g, wts):
    m, d = x.shape
    hv = N_HEADS_C * DV_C
    tm = _row_tile(m, 512)
    consts = [g, wts["wq"], wts["wk"], wts["wv"], wts["wgi"], wts["wo"], wts["bif"]]
    tile = lambda n: pl.BlockSpec((tm, n), lambda i: (i, 0))
    return pl.pallas_call(
        _mlstm_proj_body,
        out_shape=(jax.ShapeDtypeStruct((m, N_HEADS_C * LANE), BF16), jax.ShapeDtypeStruct((m, N_HEADS_C * LANE), BF16),
                   jax.ShapeDtypeStruct((m, hv), BF16), jax.ShapeDtypeStruct((m, LANE), F32),
                   jax.ShapeDtypeStruct((m, hv), F32)),
        grid=(m // tm,),
        in_specs=[tile(d)] + [_const_spec(c.shape) for c in consts],
        out_specs=(tile(N_HEADS_C * LANE), tile(N_HEADS_C * LANE), tile(hv), tile(LANE), tile(hv)),
        compiler_params=_params("parallel"),
        name="mlstm_proj",
    )(x, *consts)


def _mlstm_scan_body(q_ref, k_ref, v_ref, gi_ref, git_ref, hs_ref, c_out, n_out, m_out, c_s, n_s, m_s):
    c = pl.program_id(1)
    L = q_ref.shape[0]

    @pl.when(c == 0)
    def _():
        c_s[...] = jnp.zeros(c_s.shape, F32)
        n_s[...] = jnp.zeros(n_s.shape, F32)
        m_s[...] = jnp.zeros(m_s.shape, F32)

    row = lax.broadcasted_iota(jnp.int32, (L, L), 0)
    col = lax.broadcasted_iota(jnp.int32, (L, L), 1)
    causal = col <= row
    tril = causal.astype(BF16)
    gi = gi_ref[...]
    git = git_ref[...]
    fcol = jax.nn.log_sigmoid(gi)
    frow = jax.nn.log_sigmoid(git[N_HEADS_C:, :])
    bcol_all = sum(_dot(tril, part) for part in _split3(fcol))
    brow_all = sum(_dot_nt(part, tril) for part in _split3(frow))
    heads = range(N_HEADS_C)
    sl = [slice(h * LANE, (h + 1) * LANE) for h in heads]
    q = [q_ref[:, sl[h]] for h in heads]
    k = [k_ref[:, sl[h]] for h in heads]
    v = [v_ref[:, sl[h]] for h in heads]
    qk = [_dot_nt(q[h], k[h]) for h in heads]
    cq = [_dot_nt(q[h], c_s[h].astype(BF16)) for h in heads]
    bcol = [bcol_all[:, N_HEADS_C + h:N_HEADS_C + h + 1] for h in heads]
    m_prev = [m_s[h:h + 1, 0:1] for h in heads]
    s, a, m_t = [], [], []
    for h in heads:
        dlog = jnp.where(causal, bcol[h] - brow_all[h:h + 1, :] + git[h:h + 1, :], -jnp.inf)
        inter = bcol[h] + m_prev[h]
        m_t.append(jnp.maximum(inter, jnp.max(dlog, axis=1, keepdims=True)))
        s.append(qk[h] * jnp.exp(dlog - m_t[h]))
        a.append(jnp.exp(inter - m_t[h]))
    sv = [_dot(s[h].astype(BF16), v[h]) for h in heads]
    wk, decay, m_new = [], [], []
    for h in heads:
        nq = jnp.sum(q[h].astype(F32) * n_s[h:h + 1, :], axis=1, keepdims=True)
        den = a[h] * nq + jnp.sum(s[h], axis=1, keepdims=True)
        hs_ref[:, sl[h]] = (a[h] * cq[h] + sv[h]) / jnp.maximum(jnp.abs(den), jnp.exp(-m_t[h]))
        b_end = bcol[h][L - 1:L, :]
        wlog = b_end - bcol[h] + gi[:, h:h + 1]
        m_new.append(jnp.maximum(b_end + m_prev[h], jnp.max(wlog, axis=0, keepdims=True)))
        wk.append(jnp.exp(wlog - m_new[h]))
        decay.append(jnp.exp(b_end + m_prev[h] - m_new[h]))
    upd = [_dot_tn((v[h].astype(F32) * wk[h]).astype(BF16), k[h]) for h in heads]
    for h in heads:
        c_s[h] = decay[h] * c_s[h] + upd[h]
        n_s[h:h + 1, :] = decay[h] * n_s[h:h + 1, :] + jnp.sum(k[h].astype(F32) * wk[h], axis=0, keepdims=True)
        m_s[h:h + 1, :] = jnp.broadcast_to(m_new[h], (1, LANE))

    @pl.when(c == pl.num_programs(1) - 1)
    def _():
        c_out[...] = c_s[...]
        n_out[...] = n_s[...]
        m_out[...] = m_s[...]


def _mlstm_scan(q, k, v, gi, git, b, t):
    L = math.gcd(t, CHUNK_C)
    nc = t // L
    hv = N_HEADS_C * DV_C
    tile = lambda n: pl.BlockSpec((L, n), lambda bi, c: (bi * nc + c, 0))
    return pl.pallas_call(
        _mlstm_scan_body,
        out_shape=(jax.ShapeDtypeStruct((b * t, hv), F32),
                   jax.ShapeDtypeStruct((b, N_HEADS_C, DV_C, LANE), F32),
                   jax.ShapeDtypeStruct((b, N_HEADS_C, LANE), F32), jax.ShapeDtypeStruct((b, N_HEADS_C, LANE), F32)),
        grid=(b, nc),
        in_specs=[tile(N_HEADS_C * LANE), tile(N_HEADS_C * LANE), tile(hv), tile(LANE),
                  pl.BlockSpec((None, 2 * N_HEADS_C, L), lambda bi, c: (bi, 0, c))],
        out_specs=(tile(hv), pl.BlockSpec((None, N_HEADS_C, DV_C, LANE), lambda bi, c: (bi, 0, 0, 0)),
                   pl.BlockSpec((None, N_HEADS_C, LANE), lambda bi, c: (bi, 0, 0)),
                   pl.BlockSpec((None, N_HEADS_C, LANE), lambda bi, c: (bi, 0, 0))),
        scratch_shapes=[pltpu.VMEM((N_HEADS_C, DV_C, LANE), F32), pltpu.VMEM((N_HEADS_C, LANE), F32),
                        pltpu.VMEM((N_HEADS_C, LANE), F32)],
        compiler_params=_params("parallel", "arbitrary"),
        name="mlstm_scan",
    )(q, k, v, gi, git)


def _mlstm_out_body(hs_ref, og_ref, hg_ref, x_ref, w_ref, o_ref):
    parts = []
    for h in range(N_HEADS_C):
        sl = slice(h * DV_C, (h + 1) * DV_C)
        parts.append((og_ref[:, sl] * _rms(hs_ref[:, sl], hg_ref[:, sl])).astype(BF16))
    o_ref[...] = x_ref[...] + _dot(jnp.concatenate(parts, axis=1), w_ref[...])


def _mlstm_out(hs, og, hg, x, w):
    m, d = x.shape
    hv = hs.shape[1]
    tm = _row_tile(m, 512)
    wide = pl.BlockSpec((tm, hv), lambda i: (i, 0))
    row = pl.BlockSpec((tm, d), lambda i: (i, 0))
    return pl.pallas_call(
        _mlstm_out_body,
        out_shape=jax.ShapeDtypeStruct((m, d), F32),
        grid=(m // tm,),
        in_specs=[wide, wide, _const_spec(hg.shape), row, _const_spec(w.shape)],
        out_specs=row,
        compiler_params=_params("parallel"),
        name="mlstm_out",
    )(hs, og, hg, x, w)


def _mlstm_weights(w_in, b_if):
    d = w_in.shape[0]
    hk, hv = N_HEADS_C * DK_C, N_HEADS_C * DV_C

    def spread(w):
        w = w.reshape(d, N_HEADS_C, DK_C)
        return jnp.pad(w, ((0, 0), (0, 0), (0, LANE - DK_C))).reshape(d, N_HEADS_C * LANE)

    wgi = jnp.pad(w_in[:, 2 * hk + hv:2 * hk + hv + 2 * N_HEADS_C], ((0, 0), (0, LANE - 2 * N_HEADS_C)))
    return {
        "wq": spread(w_in[:, :hk]).astype(BF16),
        "wk": (spread(w_in[:, hk:2 * hk]) * (DK_C ** -0.5)).astype(BF16),
        "wv": w_in[:, 2 * hk:2 * hk + hv].astype(BF16),
        "wgi": wgi.astype(BF16),
        "wo": w_in[:, 2 * hk + hv + 2 * N_HEADS_C:].astype(BF16),
        "bif": jnp.pad(b_if, (0, LANE - 2 * N_HEADS_C))[None, :],
    }


def _mlstm_prompt(x, g_mix, b, t, w_in, b_if, h_g, w_out):
    wts = _mlstm_weights(w_in, b_if)
    q, k, v, gi, og = _mlstm_proj(x, g_mix, wts)
    git = gi[:, :2 * N_HEADS_C].reshape(b, t, 2 * N_HEADS_C).transpose(0, 2, 1)
    hs, c, n, m = _mlstm_scan(q, k, v, gi, git, b, t)
    y = _mlstm_out(hs, og, h_g[None, :], x, w_out.astype(BF16))
    return y, c[..., :DK_C], n[..., :DK_C], m[..., 0]


def _proj_add_body(o_ref, x_ref, w_ref, out_ref):
    out_ref[...] = x_ref[...] + _dot(o_ref[...].astype(BF16), w_ref[...])


def _proj_add(o, x, w):
    m, d = x.shape
    tm = _row_tile(m, 512)
    return pl.pallas_call(
        _proj_add_body,
        out_shape=jax.ShapeDtypeStruct((m, d), F32),
        grid=(m // tm,),
        in_specs=[pl.BlockSpec((tm, o.shape[1]), lambda i: (i, 0)), pl.BlockSpec((tm, d), lambda i: (i, 0)),
                  _const_spec(w.shape)],
        out_specs=pl.BlockSpec((tm, d), lambda i: (i, 0)),
        compiler_params=_params("parallel"),
        name="proj_add",
    )(o, x, w)


def _nsa_step_body(pt_ref, *refs, n_pages, page, past_len):
    pages = refs[:n_pages]
    (win_ref, qr_ref, qn_ref, gates_ref, nkv_ref, nwin_ref, pe_ref, w1_ref, w2_ref, kcg_ref,
     o_ref, c_s, x_s) = refs[n_pages:]
    w = N_KV_B * HD_B
    length = n_pages * page
    nb = length // CMP_BLOCK
    t = past_len
    for p in range(n_pages):
        for c in range(2 * w // LANE):
            c_s[c, p * page:(p + 1) * page, :] = pages[p][c * LANE:(c + 1) * LANE, :].T
    for ng in range(nb // 8):
        for l in range(CMP_BLOCK):
            for c in range(2 * w // LANE):
                rows = c_s[c, pl.ds(ng * 8 * CMP_BLOCK + l, 8, stride=CMP_BLOCK), :]
                for half in range(LANE // HD_B):
                    x_s[c * (LANE // HD_B) + half, ng * 8:(ng + 1) * 8, l * HD_B:(l + 1) * HD_B] = (
                        rows[:, half * HD_B:(half + 1) * HD_B])
    lane_w = lax.broadcasted_iota(jnp.int32, (HD_B, w), 1)
    row_w = lax.broadcasted_iota(jnp.int32, (HD_B, w), 0)
    cmp_nat = []
    for slot in range(2):
        xs = x_s[slot * N_KV_B:(slot + 1) * N_KV_B].reshape(N_KV_B * nb, CMP_BLOCK * HD_B)
        hid = jax.nn.gelu(_dot((xs + pe_ref[slot]).astype(BF16), w1_ref[slot]))
        y = _dot(hid.astype(BF16), w2_ref[slot])
        if slot == 0:
            y = _rms(y, kcg_ref[...])
        nat = jnp.zeros((nb, w), F32)
        for g in range(N_KV_B):
            place = (lane_w == row_w + g * HD_B).astype(BF16)
            nat = nat + _dot(y[g * nb:(g + 1) * nb].astype(BF16), place)
        cmp_nat.append(nat.astype(BF16))
    kc, vc = cmp_nat
    qr = qr_ref[...]
    qn = qn_ref[...]
    gates = gates_ref[...]
    nh = N_HEADS_B
    blk = lax.broadcasted_iota(jnp.int32, (nh, nb), 1)
    p_c = _masked_softmax(_dot_nt(qn, kc), (blk + 1) * CMP_BLOCK - 1 <= t, 1)
    o_c = _dot(p_c.astype(BF16), vc)
    blk_t = lax.broadcasted_iota(jnp.int32, (nb, nh), 0)
    p_t = _masked_softmax(_dot_nt(kc, qn), (blk_t + 1) * CMP_BLOCK - 1 <= t, 0)
    gsum = (lax.broadcasted_iota(jnp.int32, (nh, LANE), 0) // REP_B
            == lax.broadcasted_iota(jnp.int32, (nh, LANE), 1)).astype(BF16)
    pair = (lax.broadcasted_iota(jnp.int32, (nb, nb), 1) // (SEL_BLOCK // CMP_BLOCK)
            == lax.broadcasted_iota(jnp.int32, (nb, nb), 0)).astype(BF16)
    imp = sum(_dot(part, gsum) for part in _split3(p_t))
    imp = sum(_dot(pair, part) for part in _split3(imp))
    sblk = lax.broadcasted_iota(jnp.int32, (nb, LANE), 0)
    cur = t // SEL_BLOCK
    forced = (sblk == 0) | (sblk == cur) | (sblk == cur - 1)
    score = jnp.where(forced, jnp.inf, jnp.where(sblk * SEL_BLOCK <= t, imp, -jnp.inf))
    sblk_f = sblk.astype(F32)
    pickable = score > -jnp.inf
    for _ in range(N_SEL):
        mx = jnp.max(score, axis=0, keepdims=True)
        first = jnp.min(jnp.where(score == mx, sblk_f, float(nb)), axis=0, keepdims=True)
        score = jnp.where(sblk_f == first, -jnp.inf, score)
    notsel = jnp.where(pickable, jnp.where(score > -jnp.inf, -MASK_BIG, 0.0), -MASK_BIG)
    bias = _dot_nt(gsum, notsel.astype(BF16)).astype(BF16)
    expand = (lax.broadcasted_iota(jnp.int32, (nb, length), 1) // SEL_BLOCK
              == lax.broadcasted_iota(jnp.int32, (nb, length), 0)).astype(BF16)
    nkv = nkv_ref[...]
    nwin = nwin_ref[...]

    def attend(s, k_new, v_new, weighted_values):
        s_new = jnp.sum(qr.astype(F32) * k_new.astype(BF16).astype(F32), axis=1, keepdims=True)
        m = jnp.maximum(jnp.max(s, axis=1, keepdims=True), s_new)
        e = jnp.exp(s - m)
        e_new = jnp.exp(s_new - m)
        den = jnp.sum(e, axis=1, keepdims=True) + e_new
        num = weighted_values(e.astype(BF16)) + e_new.astype(BF16).astype(F32) * v_new.astype(BF16).astype(F32)
        return num / den

    s_sel = jnp.concatenate([_dot(qr, pages[p][2 * w:3 * w, :].astype(BF16)) for p in range(n_pages)], axis=1)
    o_s = attend(s_sel + _dot(bias, expand), nkv[:, 2 * w:3 * w], nkv[:, 3 * w:],
                 lambda e: sum(_dot_nt(e[:, p * page:(p + 1) * page], pages[p][3 * w:, :].astype(BF16))
                               for p in range(n_pages)))
    wb = win_ref.shape[1]
    pos_w = t - wb + lax.broadcasted_iota(jnp.int32, (nh, wb), 1)
    ok_w = (pos_w >= 0) & (t - pos_w <= WINDOW)
    s_w = jnp.where(ok_w, _dot(qr, win_ref[:w, :].astype(BF16)), -MASK_BIG)
    o_w = attend(s_w, nwin[:, :w], nwin[:, w:], lambda e: _dot_nt(e, win_ref[w:, :].astype(BF16)))
    o_ref[...] = gates[:, 0:1] * o_c + gates[:, 1:2] * o_s + gates[:, 2:3] * o_w


def _nsa_step(page_table, cache, win_cache, qr, qn, gates, new_kv, new_win, pe, w1, w2, kc_g, past_len):
    bsz, n_pages = page_table.shape
    page = cache.shape[2]
    assert page == LANE
    w = N_KV_B * HD_B
    nb = n_pages * page // CMP_BLOCK
    wb = win_cache.shape[2]
    per = lambda shape: pl.BlockSpec((None,) + shape, lambda b, pt: (b,) + (0,) * len(shape))
    const = lambda a: pl.BlockSpec(a.shape, lambda b, pt: (0,) * a.ndim)
    page_specs = [pl.BlockSpec((None, 4 * w, page), lambda b, pt, p=p: (pt[b, p], 0, 0)) for p in range(n_pages)]
    return pl.pallas_call(
        functools.partial(_nsa_step_body, n_pages=n_pages, page=page, past_len=past_len),
        out_shape=jax.ShapeDtypeStruct((bsz, N_HEADS_B, w), F32),
        grid_spec=pltpu.PrefetchScalarGridSpec(
            num_scalar_prefetch=1,
            grid=(bsz,),
            in_specs=page_specs + [per((2 * w, wb)), per((N_HEADS_B, w)), per((N_HEADS_B, w)), per((N_HEADS_B, LANE)),
                                   per((1, 4 * w)), per((1, 2 * w)), const(pe), const(w1), const(w2), const(kc_g)],
            out_specs=per((N_HEADS_B, w)),
            scratch_shapes=[pltpu.VMEM((2 * w // LANE, n_pages * page, LANE), F32),
                            pltpu.VMEM((2 * N_KV_B, nb, CMP_BLOCK * HD_B), F32)]),
        compiler_params=_params("parallel"),
        name="nsa_step",
    )(page_table, *([cache] * n_pages), win_cache, qr, qn, gates, new_kv, new_win, pe, w1, w2, kc_g)


def _nsa_sample_step(x, g_mix, past_len, kv_cache, win_cache, page_table, w_in, q_g, k_g, pe, w_c1, w_c2, w_out):
    bsz, d = x.shape
    w = N_KV_B * HD_B
    assert past_len % CMP_BLOCK == 0 and past_len // SEL_BLOCK + 1 <= past_len // CMP_BLOCK
    wts = _nsa_weights(w_in, q_g, k_g, w_out)
    pos = jnp.full((bsz,), past_len, jnp.int32)
    qcat, gates, kv_rows, win_rows, _, _, _, _ = _nsa_proj(
        x, g_mix, wts, _rope_tables(pos, LANE), _rope_tables(pos, HD_B), 1, bsz)
    q5 = qcat.reshape(bsz, N_KV_B, REP_B, 2, HD_B)
    eye = jnp.eye(N_KV_B, dtype=BF16)
    qrows = (q5[:, :, :, :, None, :] * eye[None, :, None, None, :, None])
    qr = qrows[:, :, :, 0].reshape(bsz, N_HEADS_B, w)
    qn = qrows[:, :, :, 1].reshape(bsz, N_HEADS_B, w)
    gts = gates.reshape(bsz, N_KV_B, LANE)[:, :, :REP_B * 3].reshape(bsz, N_HEADS_B, 3)
    gts = jnp.pad(gts, ((0, 0), (0, 0), (0, LANE - 3)))
    pool, page = kv_cache.shape[:2]
    cache_t = kv_cache.reshape(pool, page, 4 * w).transpose(0, 2, 1)
    win_t = win_cache.reshape(bsz, -1, 2 * w).transpose(0, 2, 1)
    o = _nsa_step(page_table, cache_t, win_t, qr, qn, gts,
                  kv_rows.reshape(bsz, 1, 4 * w), win_rows.reshape(bsz, 1, 2 * w),
                  pe.reshape(2, 1, CMP_BLOCK * HD_B), w_c1.reshape(2, CMP_BLOCK * HD_B, HD_B).astype(BF16),
                  w_c2.astype(BF16), k_g[0][None, :], past_len)
    own = (jnp.arange(N_HEADS_B)[:, None] // REP_B == jnp.arange(N_KV_B)[None, :]).astype(F32)
    w_exp = own[:, :, None, None] * w_out.reshape(N_HEADS_B, 1, HD_B, d)
    y = _proj_add(o.reshape(bsz, N_HEADS_B * w), x, w_exp.reshape(N_HEADS_B * w, d).astype(BF16))
    return y, kv_rows.reshape(bsz, 1, N_KV_SLOTS, N_KV_B, HD_B), win_rows.reshape(bsz, 1, 2, N_KV_B, HD_B)


def _mlstm_step_body(q_ref, k_ref, vt_ref, gi_ref, c_ref, n_ref, m_ref, ht_ref, co_ref, no_ref, mo_ref, *, sb):
    lane8 = lax.broadcasted_iota(jnp.int32, (DV_C, N_HEADS_C), 1)
    gi = gi_ref[...]
    logf = jax.nn.log_sigmoid(gi)
    m_all = m_ref[...]
    for s in range(sb):
        ht = jnp.zeros((DV_C, N_HEADS_C), F32)
        m_new_row = jnp.zeros((1, N_HEADS_C), F32)
        lane_m = lax.broadcasted_iota(jnp.int32, (1, N_HEADS_C), 1)
        for h in range(N_HEADS_C):
            q = q_ref[s:s + 1, h * LANE:h * LANE + DK_C].astype(F32)
            k = k_ref[s:s + 1, h * LANE:h * LANE + DK_C].astype(F32)
            v = vt_ref[s, :, h:h + 1]
            c = c_ref[s, h]
            n = n_ref[s, h:h + 1, :]
            it = gi[s:s + 1, h:h + 1]
            b = logf[s:s + 1, N_HEADS_C + h:N_HEADS_C + h + 1]
            m0 = m_all[s:s + 1, h:h + 1]
            inter = b + m0
            m_t = jnp.maximum(inter, it)
            wgt = jnp.exp(it - m_t)
            a = jnp.exp(inter - m_t)
            sc = jnp.sum(q * k, axis=1, keepdims=True) * wgt
            num = a * jnp.sum(c * q, axis=1, keepdims=True) + sc * v
            den = a * jnp.sum(n * q, axis=1, keepdims=True) + sc
            hcol = num / jnp.maximum(jnp.abs(den), jnp.exp(-m_t))
            ht = jnp.where(lane8 == h, hcol, ht)
            co_ref[s, h] = a * c + (wgt * v) * k
            no_ref[s, h:h + 1, :] = a * n + wgt * k
            m_new_row = jnp.where(lane_m == h, m_t, m_new_row)
        ht_ref[s] = ht
        mo_ref[s:s + 1, :] = m_new_row


def _mlstm_step(q, k, vt, gi, c0, n0, m0):
    bsz = q.shape[0]
    sb = 8
    row = lambda n: pl.BlockSpec((sb, n), lambda i: (i, 0))
    c_spec = pl.BlockSpec((sb, N_HEADS_C, DV_C, DK_C), lambda i: (i, 0, 0, 0))
    n_spec = pl.BlockSpec((sb, N_HEADS_C, DK_C), lambda i: (i, 0, 0))
    vt_spec = pl.BlockSpec((sb, DV_C, N_HEADS_C), lambda i: (i, 0, 0))
    return pl.pallas_call(
        functools.partial(_mlstm_step_body, sb=sb),
        out_shape=(jax.ShapeDtypeStruct((bsz, DV_C, N_HEADS_C), F32), jax.ShapeDtypeStruct(c0.shape, F32),
                   jax.ShapeDtypeStruct(n0.shape, F32), jax.ShapeDtypeStruct(m0.shape, F32)),
        grid=(bsz // sb,),
        in_specs=[row(N_HEADS_C * LANE), row(N_HEADS_C * LANE), vt_spec, row(LANE), c_spec, n_spec, row(N_HEADS_C)],
        out_specs=(vt_spec, c_spec, n_spec, row(N_HEADS_C)),
        compiler_params=_params("parallel"),
        name="mlstm_step",
    )(q, k, vt, gi, c0, n0, m0)


def _mlstm_sample_step(x, g_mix, c0, n0, m0, w_in, b_if, h_g, w_out):
    bsz = x.shape[0]
    wts = _mlstm_weights(w_in, b_if)
    q, k, v, gi, og = _mlstm_proj(x, g_mix, wts)
    vt = v.astype(F32).reshape(bsz, N_HEADS_C, DV_C).transpose(0, 2, 1)
    ht, c, n, m = _mlstm_step(q, k, vt, gi, c0, n0, m0)
    hs = ht.transpose(0, 2, 1).reshape(bsz, N_HEADS_C * DV_C)
    y = _mlstm_out(hs, og, h_g[None, :], x, w_out.astype(BF16))
    return y, c, n, m


def kernel(x_prompt, x_sample, cache_nsa_kv, cache_nsa_win, state_mlstm_C, state_mlstm_n, state_mlstm_m, page_table,
           norm_mix_g, norm_ffn_g, ffn_w1, ffn_w2, a_w_in, a_ln_g, a_ln_b, a_w_s, a_b_s, a_w_out,
           b_w_in, b_q_g, b_k_g, b_pe, b_w_c1, b_w_c2, b_w_out, c_w_in, c_b_if, c_h_g, c_w_out):
    bp, t, d = x_prompt.shape
    bs, ts, _ = x_sample.shape
    assert ts == 1
    past_len = page_table.shape[1] * cache_nsa_kv.shape[2]
    xp = x_prompt.reshape(bp * t, d)
    xs = x_sample.reshape(bs, d)
    out = {k: [] for k in ("v_s", "kv_p", "win_p", "kv_s", "win_s", "C_p", "n_p", "m_p", "C_s", "n_s", "m_s")}
    for layer in range(norm_mix_g.shape[0]):
        kind, j = layer % 3, layer // 3
        gm = norm_mix_g[layer]
        if kind == 0:
            args = (a_w_in[j], a_ln_g[j], a_ln_b[j], a_w_s[j], a_b_s[j], a_w_out[j])
            xp = _gmlp_layer(xp, gm, *args, single=False)[0]
            xs, v = _gmlp_layer(xs, gm, *args, single=True)
            out["v_s"].append(v.reshape(bs, ts, -1))
        elif kind == 1:
            args = (b_w_in[j], b_q_g[j], b_k_g[j], b_pe[j], b_w_c1[j], b_w_c2[j], b_w_out[j])
            xp, kv, win = _nsa_prompt(xp, gm[None, :], bp, t, *args)
            out["kv_p"].append(kv)
            out["win_p"].append(win)
            xs, kv, win = _nsa_sample_step(xs, gm[None, :], past_len, cache_nsa_kv[j], cache_nsa_win[j], page_table,
                                           *args)
            out["kv_s"].append(kv)
            out["win_s"].append(win)
        else:
            args = (c_w_in[j], c_b_if[j], c_h_g[j], c_w_out[j])
            xp, c, n, m = _mlstm_prompt(xp, gm[None, :], bp, t, *args)
            out["C_p"].append(c)
            out["n_p"].append(n)
            out["m_p"].append(m)
            xs, c, n, m = _mlstm_sample_step(xs, gm[None, :], state_mlstm_C[j], state_mlstm_n[j], state_mlstm_m[j],
                                             *args)
            out["C_s"].append(c)
            out["n_s"].append(n)
            out["m_s"].append(m)
        gf = norm_ffn_g[layer][None, :]
        w1, w2 = ffn_w1[layer].astype(BF16), ffn_w2[layer].astype(BF16)
        xp = _ffn(xp, gf, w1, w2)
        xs = _ffn(xs, gf, w1, w2)
    st = {k: jnp.stack(v) for k, v in out.items()}
    return (xp.reshape(bp, t, d), xs.reshape(bs, ts, d), st["v_s"], st["kv_p"], st["win_p"], st["kv_s"], st["win_s"],
            st["C_p"], st["n_p"], st["m_p"], st["C_s"], st["n_s"], st["m_s"])
```

```python
import functools
import math

import jax
import jax.numpy as jnp
from jax import lax
from jax.experimental import pallas as pl
from jax.experimental.pallas import tpu as pltpu

F32 = jnp.float32
BF16 = jnp.bfloat16

EPS = 1e-6
CHUNK_A = 128
N_GROUPS_A = 8
N_HEADS_B = 16
N_KV_B = 4
REP_B = N_HEADS_B // N_KV_B
HD_B = 64
ROT_DIM = 16
ROPE_THETA = 500000.0
CMP_BLOCK = 32
SEL_BLOCK = 64
N_SEL = 16
WINDOW = 512
N_KV_SLOTS = 4
N_HEADS_C = 8
DK_C = 64
DV_C = 128
CHUNK_C = 128
SCALE_B = HD_B ** -0.5

LANE = 128
VMEM_LIMIT_BYTES = 56 * 1024 * 1024
MASK_BIG = 1e30
M_INIT = -1e20
N_SEL_PAD = 128
N_CMP_PAD = 2 * N_SEL_PAD
SHIFT_MAX = 40.0
FLASH_SEL_TILES = (1024, 512)
FLASH_WIN_TILES = (512, 512)
MASK_CAUSAL = 1
MASK_BAND = 2


def _params(*sem):
    return pltpu.CompilerParams(dimension_semantics=sem, vmem_limit_bytes=VMEM_LIMIT_BYTES)


def _dot(a, b):
    return jnp.dot(a, b, preferred_element_type=F32)


def _dot_nt(a, b):
    return lax.dot_general(a, b, (((1,), (1,)), ((), ())), preferred_element_type=F32)


def _dot_tn(a, b):
    return lax.dot_general(a, b, (((0,), (0,)), ((), ())), preferred_element_type=F32)


def _rms(x, g):
    return x * lax.rsqrt(jnp.mean(x * x, axis=-1, keepdims=True) + EPS) * g


def _split3(x):
    a = x.astype(BF16)
    r = x - a.astype(F32)
    b = r.astype(BF16)
    c = (r - b.astype(F32)).astype(BF16)
    return a, b, c


def _const_spec(shape):
    n = len(shape)
    return pl.BlockSpec(shape, lambda *_: (0,) * n)


def _row_tile(m, pref):
    t = min(pref, m)
    while m % t:
        t //= 2
    return t


def _ffn_body(x_ref, g_ref, w1_ref, w2_ref, o_ref, *, ck):
    x = x_ref[...]
    xb = _rms(x, g_ref[...]).astype(BF16)
    acc = x
    for j in range(w1_ref.shape[1] // ck):
        h = jnp.maximum(_dot(xb, w1_ref[:, j * ck:(j + 1) * ck]), 0.0)
        acc = acc + _dot((h * h).astype(BF16), w2_ref[j * ck:(j + 1) * ck, :])
    o_ref[...] = acc


def _layer_spec(stacked, layer):
    n = stacked.ndim - 1
    return pl.BlockSpec((None,) + stacked.shape[1:], lambda *_: (layer,) + (0,) * n)


def _ffn(x, g, w1s, w2s, layer):
    m, d = x.shape
    tm = _row_tile(m, 512)
    return pl.pallas_call(
        functools.partial(_ffn_body, ck=1024),
        out_shape=jax.ShapeDtypeStruct((m, d), F32),
        grid=(m // tm,),
        in_specs=[pl.BlockSpec((tm, d), lambda i: (i, 0)), _const_spec(g.shape),
                  _layer_spec(w1s, layer), _layer_spec(w2s, layer)],
        out_specs=pl.BlockSpec((tm, d), lambda i: (i, 0)),
        compiler_params=_params("parallel"),
        name="ffn",
    )(x, g, w1s, w2s)


def _gmlp_body(x_ref, g_ref, win_ref, lng_ref, lnb_ref, ws_ref, bs_ref, wout_ref, o_ref, *maybe_v_ref, single):
    x = x_ref[...]
    dg = lng_ref.shape[1]
    xb = _rms(x, g_ref[...]).astype(BF16)
    u = jax.nn.gelu(_dot(xb, win_ref[:, :dg]))
    v = jax.nn.gelu(_dot(xb, win_ref[:, dg:]))
    mu = jnp.mean(v, axis=-1, keepdims=True)
    vc = v - mu
    var = jnp.mean(vc * vc, axis=-1, keepdims=True)
    v = vc * lax.rsqrt(var + EPS) * lng_ref[...] + lnb_ref[...]
    if single:
        maybe_v_ref[0][...] = v
        gate = v * ws_ref[...] + bs_ref[...]
    else:
        gw = dg // N_GROUPS_A
        row = lax.broadcasted_iota(jnp.int32, (CHUNK_A, CHUNK_A), 0)
        col = lax.broadcasted_iota(jnp.int32, (CHUNK_A, CHUNK_A), 1)
        causal = col <= row
        vb = v.astype(BF16)
        chunks = []
        for c in range(x.shape[0] // CHUNK_A):
            parts = []
            for gi in range(N_GROUPS_A):
                w = jnp.where(causal, ws_ref[gi], 0.0).astype(BF16)
                parts.append(_dot(w, vb[c * CHUNK_A:(c + 1) * CHUNK_A, gi * gw:(gi + 1) * gw]))
            chunks.append(jnp.concatenate(parts, axis=1) + bs_ref[...])
        gate = jnp.concatenate(chunks, axis=0)
    o_ref[...] = x + _dot((u * gate).astype(BF16), wout_ref[...])


def _gmlp(x, g, w_ins, ln_g, ln_b, ws, bs, w_outs, layer, *, single):
    m, d = x.shape
    dg = w_outs.shape[1]
    tm = _row_tile(m, 256)
    n_out = 2 if single else 1
    outs = pl.pallas_call(
        functools.partial(_gmlp_body, single=single),
        out_shape=(jax.ShapeDtypeStruct((m, d), F32), jax.ShapeDtypeStruct((m, dg), F32))[:n_out],
        grid=(m // tm,),
        in_specs=[pl.BlockSpec((tm, d), lambda i: (i, 0)), _const_spec(g.shape), _layer_spec(w_ins, layer),
                  _const_spec(ln_g.shape), _const_spec(ln_b.shape), _const_spec(ws.shape), _const_spec(bs.shape),
                  _layer_spec(w_outs, layer)],
        out_specs=(pl.BlockSpec((tm, d), lambda i: (i, 0)), pl.BlockSpec((tm, dg), lambda i: (i, 0)))[:n_out],
        compiler_params=_params("parallel"),
        name="gmlp_single" if single else "gmlp",
    )(x, g, w_ins, ln_g, ln_b, ws, bs, w_outs)
    return outs if single else (outs[0], None)


def _rope_tables(pos, seg):
    half = ROT_DIM // 2
    freq = jnp.power(ROPE_THETA, -jnp.arange(half, dtype=F32) * 2.0 / ROT_DIM)
    ang = pos.astype(F32)[:, None] * freq[None, :]
    cos, sin = jnp.cos(ang), jnp.sin(ang)
    t = pos.shape[0]
    one = jnp.ones((t, seg - ROT_DIM), F32)
    zero = jnp.zeros((t, seg - ROT_DIM), F32)
    z8 = jnp.zeros((t, half), F32)
    tabs = [jnp.concatenate([cos, cos, one], 1), jnp.concatenate([-sin, z8, zero], 1),
            jnp.concatenate([z8, sin, zero], 1)]
    return jnp.stack([jnp.tile(a, (1, LANE // seg)) for a in tabs])


def _rope128(x, tab):
    return x * tab[0] + pltpu.roll(x, LANE - ROT_DIM // 2, 1) * tab[1] + pltpu.roll(x, ROT_DIM // 2, 1) * tab[2]


def _nsa_proj_body(x_ref, g_ref, wq_ref, wg_ref, wkv_ref, seg_ref, spread_ref, qg_ref, kg_ref, tq_ref, tk_ref,
                   qcat_ref, gates_ref, kv_ref, win_ref, ks_ref, vs_ref, kw_ref, vw_ref):
    x = x_ref[...]
    xb = _rms(x, g_ref[...]).astype(BF16)
    tq = tq_ref[...]
    tk = tk_ref[...]
    qg = qg_ref[...]
    for h in range(N_HEADS_B):
        q = _dot(xb, wq_ref[:, h * LANE:(h + 1) * LANE])
        ms = jnp.sum(q * q, axis=-1, keepdims=True) * (1.0 / LANE)
        qn = q * lax.rsqrt(ms + EPS) * qg
        qcat_ref[:, h * LANE:(h + 1) * LANE] = (_rope128(qn, tq) * SCALE_B).astype(BF16)
    gates_ref[...] = jax.nn.sigmoid(_dot(xb, wg_ref[...]))
    kv = _dot(xb, wkv_ref[...])
    w = N_KV_B * HD_B
    seg = seg_ref[...]
    spread = spread_ref[...]

    def head_norm(k, gain):
        k2 = k * k
        hi = k2.astype(BF16)
        lo = (k2 - hi.astype(F32)).astype(BF16)
        ss = _dot(hi, seg) + _dot(lo, seg)
        return k * lax.rsqrt(ss * (1.0 / HD_B) + EPS) * gain

    def rope(k):
        return jnp.concatenate([_rope128(k[:, j * LANE:(j + 1) * LANE], tk) for j in range(w // LANE)], axis=1)

    ks = rope(head_norm(kv[:, 2 * w:3 * w], kg_ref[0:1, :]))
    kw = rope(head_norm(kv[:, 4 * w:5 * w], kg_ref[1:2, :]))
    vs = kv[:, 3 * w:4 * w]
    vw = kv[:, 5 * w:6 * w]
    kv_ref[:, :2 * w] = kv[:, :2 * w]
    kv_ref[:, 2 * w:3 * w] = ks
    kv_ref[:, 3 * w:] = vs
    win_ref[:, :w] = kw
    win_ref[:, w:] = vw
    lane = lax.broadcasted_iota(jnp.int32, (1, N_KV_B * LANE), 1)
    ones_hi = ((lane & HD_B) != 0).astype(F32)
    ks_ref[...] = _dot(ks.astype(BF16), spread).astype(BF16)
    kw_ref[...] = _dot(kw.astype(BF16), spread).astype(BF16)
    vs_ref[...] = (_dot(vs.astype(BF16), spread) + ones_hi).astype(BF16)
    vw_ref[...] = (_dot(vw.astype(BF16), spread) + ones_hi).astype(BF16)


def _nsa_proj(x, g, wts, tab_q, tab_k, n_tab_tiles, tm):
    m, d = x.shape
    w = N_KV_B * HD_B
    ws = N_KV_B * LANE
    tile = lambda n: pl.BlockSpec((tm, n), lambda i: (i, 0))
    tab = pl.BlockSpec((3, tm, LANE), lambda i: (0, i % n_tab_tiles, 0))
    consts = [g, wts["wq"], wts["wg"], wts["wkv"], wts["seg"], wts["spread"], wts["qg"], wts["kg"]]
    return pl.pallas_call(
        _nsa_proj_body,
        out_shape=(jax.ShapeDtypeStruct((m, N_HEADS_B * LANE), BF16), jax.ShapeDtypeStruct((m, ws), F32),
                   jax.ShapeDtypeStruct((m, 4 * w), F32), jax.ShapeDtypeStruct((m, 2 * w), F32),
                   jax.ShapeDtypeStruct((m, ws), BF16), jax.ShapeDtypeStruct((m, ws), BF16),
                   jax.ShapeDtypeStruct((m, ws), BF16), jax.ShapeDtypeStruct((m, ws), BF16)),
        grid=(m // tm,),
        in_specs=[tile(d)] + [_const_spec(c.shape) for c in consts] + [tab, tab],
        out_specs=(tile(N_HEADS_B * LANE), tile(ws), tile(4 * w), tile(2 * w), tile(ws), tile(ws), tile(ws), tile(ws)),
        compiler_params=_params("parallel"),
        name="nsa_proj",
    )(x, *consts, tab_q, tab_k)


CMP_PLANES = 2 * N_KV_B * HD_B // LANE


def _flatten_cmp_blocks(load_rows, x_s, n_blocks):
    for ng in range(n_blocks // 8):
        for l in range(CMP_BLOCK):
            for c in range(CMP_PLANES):
                rows = load_rows(c, ng * 8 * CMP_BLOCK + l)
                for half in range(LANE // HD_B):
                    x_s[c * (LANE // HD_B) + half, ng * 8:(ng + 1) * 8, l * HD_B:(l + 1) * HD_B] = (
                        rows[:, half * HD_B:(half + 1) * HD_B])


def _compress_slot(x_s, slot, pe_ref, w1_ref, w2_ref, kcg_ref):
    n_blocks = x_s.shape[1]
    xs = x_s[slot * N_KV_B:(slot + 1) * N_KV_B].reshape(N_KV_B * n_blocks, CMP_BLOCK * HD_B)
    hid = jax.nn.gelu(_dot((xs + pe_ref[slot]).astype(BF16), w1_ref[slot]))
    y = _dot(hid.astype(BF16), w2_ref[slot])
    return _rms(y, kcg_ref[...]) if slot == 0 else y


def _cmp_prompt_body(*refs, n_blocks):
    planes = refs[:CMP_PLANES]
    pe_ref, w1_ref, w2_ref, kcg_ref, kc_ref, vc_ref, x_s = refs[CMP_PLANES:]
    _flatten_cmp_blocks(lambda c, start: planes[c][pl.ds(start, 8, stride=CMP_BLOCK), :], x_s, n_blocks)
    kc_ref[...] = _compress_slot(x_s, 0, pe_ref, w1_ref, w2_ref, kcg_ref).reshape(N_KV_B, n_blocks, HD_B)
    vc_ref[...] = _compress_slot(x_s, 1, pe_ref, w1_ref, w2_ref, kcg_ref).reshape(N_KV_B, n_blocks, HD_B)


def _cmp_prompt(kv_rows, b, t, pe, w1, w2, kc_g):
    tt = min(t, 2048)
    nbk = tt // CMP_BLOCK
    assert t % tt == 0 and nbk % 8 == 0
    steps = t // tt
    out = jax.ShapeDtypeStruct((b, N_KV_B, t // CMP_BLOCK, HD_B), F32)
    out_spec = pl.BlockSpec((None, N_KV_B, nbk, HD_B), lambda bi, i: (bi, 0, i, 0))
    plane_specs = [pl.BlockSpec((tt, LANE), lambda bi, i, c=c: (bi * steps + i, c)) for c in range(CMP_PLANES)]
    return pl.pallas_call(
        functools.partial(_cmp_prompt_body, n_blocks=nbk),
        out_shape=(out, out),
        grid=(b, steps),
        in_specs=plane_specs + [_const_spec(pe.shape), _const_spec(w1.shape), _const_spec(w2.shape),
                                _const_spec(kc_g.shape)],
        out_specs=(out_spec, out_spec),
        scratch_shapes=[pltpu.VMEM((2 * N_KV_B, nbk, CMP_BLOCK * HD_B), F32)],
        compiler_params=_params("parallel", "parallel"),
        name="nsa_compress",
    )(*([kv_rows] * CMP_PLANES), pe, w1, w2, kc_g)


def _masked_softmax(s, mask, axis):
    sm = jnp.where(mask, s, -jnp.inf)
    mx = jnp.max(sm, axis=axis, keepdims=True)
    mx = jnp.where(mx > -jnp.inf, mx, 0.0)
    e = jnp.where(mask, jnp.exp(s - mx), 0.0)
    return e / jnp.maximum(jnp.sum(e, axis=axis, keepdims=True), 1e-30)


CMP_SEGMENTS = (64, 64, 128)


def _cmp_block_order():
    order, base = [], 0
    for size in CMP_SEGMENTS:
        order += list(range(base, base + size, 2)) + list(range(base + 1, base + size, 2))
        base += size
    assert base == N_CMP_PAD
    return order


def _cmp_select_body(q_ref, kct_ref, vct_ref, gates_ref, o_ref, mnot_ref, *, tq):
    i = pl.program_id(2)
    q0 = i * tq
    gates = gates_ref[...]

    def run(nv):
        hv = nv // 2
        row = lax.broadcasted_iota(jnp.int32, (nv, tq), 0)
        tok = q0 + lax.broadcasted_iota(jnp.int32, (nv, tq), 1)
        blk_c = jnp.zeros((nv, tq), jnp.int32)
        base = 0
        for size in CMP_SEGMENTS:
            if base < nv:
                local = row - base
                seg_blk = base + jnp.where(local < size // 2, 2 * local, 2 * (local - size // 2) + 1)
                blk_c = jnp.where((row >= base) & (row < base + size), seg_blk, blk_c)
            base += size
        mask = (blk_c + 1) * CMP_BLOCK - 1 <= tok
        kct = kct_ref[:nv, :]
        vct = vct_ref[:, :nv]
        imp_parts = None
        for r in range(REP_B):
            qh = q_ref[:, r * LANE:(r + 1) * LANE]
            pt = _masked_softmax(_dot_nt(kct, qh), mask, 0)
            o_ref[:, r * LANE:(r + 1) * LANE] = _dot(vct, pt.astype(BF16)).T * gates[:, 3 * r:3 * r + 1]
            parts, base = [], 0
            for size in CMP_SEGMENTS:
                if base < nv:
                    parts.append(pt[base:base + size // 2] + pt[base + size // 2:base + size])
                base += size
            imp_parts = parts if imp_parts is None else [a + b for a, b in zip(imp_parts, parts)]
        imp = jnp.concatenate(imp_parts, axis=0)
        blk = lax.broadcasted_iota(jnp.int32, (hv, tq), 0)
        t_s = q0 + lax.broadcasted_iota(jnp.int32, (hv, tq), 1)
        cur = t_s // SEL_BLOCK
        forced = (blk == 0) | (blk == cur) | (blk == cur - 1)
        score = jnp.where(forced, jnp.inf, jnp.where(blk * SEL_BLOCK <= t_s, imp, -jnp.inf))
        blk_f = blk.astype(F32)
        pickable = score > -jnp.inf
        for _ in range(N_SEL):
            mx = jnp.max(score, axis=0, keepdims=True)
            first = jnp.min(jnp.where(score == mx, blk_f, float(N_SEL_PAD)), axis=0, keepdims=True)
            score = jnp.where(blk_f == first, -jnp.inf, score)
        mnot = jnp.where(pickable, jnp.where(score > -jnp.inf, 1.0, 0.0), 1.0)
        if hv < N_SEL_PAD:
            mnot = jnp.concatenate([mnot, jnp.ones((N_SEL_PAD - hv, tq), F32)], axis=0)
        mnot_ref[...] = mnot.T.astype(BF16)

    need = (q0 + tq) // CMP_BLOCK
    bounds, base = [], 0
    for size in CMP_SEGMENTS:
        base += size
        bounds.append(base)
    lo = 0
    for nv in bounds:
        pl.when((need > lo) & (need <= nv))(functools.partial(run, nv))
        lo = nv


def _cmp_select(qcat, kct, vct, gates, b, t, tq):
    m = b * t
    nq = t // tq
    return pl.pallas_call(
        functools.partial(_cmp_select_body, tq=tq),
        out_shape=(jax.ShapeDtypeStruct((m, N_HEADS_B * LANE), F32), jax.ShapeDtypeStruct((m, N_KV_B * LANE), BF16)),
        grid=(b, N_KV_B, nq),
        in_specs=[pl.BlockSpec((tq, REP_B * LANE), lambda bi, g, i: (bi * nq + i, g)),
                  pl.BlockSpec((None, None, N_CMP_PAD, LANE), lambda bi, g, i: (bi, g, 0, 0)),
                  pl.BlockSpec((None, None, LANE, N_CMP_PAD), lambda bi, g, i: (bi, g, 0, 0)),
                  pl.BlockSpec((tq, LANE), lambda bi, g, i: (bi * nq + i, g))],
        out_specs=(pl.BlockSpec((tq, REP_B * LANE), lambda bi, g, i: (bi * nq + i, g)),
                   pl.BlockSpec((tq, LANE), lambda bi, g, i: (bi * nq + i, g))),
        compiler_params=_params("parallel", "parallel", "parallel"),
        name="nsa_cmp_select",
    )(qcat, kct, vct, gates)


def _flash_body(tab_ref, bound_ref, q_ref, mnot_ref, k_ref, v_ref, gates_ref, o_ref, qs_ref, ks_ref, m_ref, acc_ref, *,
                tq, tk, sel, gate_col, fixed, modes):
    step_id = pl.program_id(2)
    i = tab_ref[0, step_id]
    j = tab_ref[1, step_id]
    rows = REP_B * tq

    @pl.when(tab_ref[3, step_id] == 1)
    def _():
        if not fixed:
            m_ref[...] = jnp.full(m_ref.shape, M_INIT, F32)
        acc_ref[...] = jnp.zeros(acc_ref.shape, F32)
        lane = lax.broadcasted_iota(jnp.int32, (tq, LANE), 1)
        for r in range(REP_B):
            qh = q_ref[:, r * LANE:(r + 1) * LANE]
            if fixed:
                qh = jnp.where(lane < HD_B, qh, jnp.ones_like(qh))
            if sel:
                qs_ref[r * tq:(r + 1) * tq, :LANE] = mnot_ref[...]
                qs_ref[r * tq:(r + 1) * tq, LANE:] = qh
            else:
                qs_ref[r * tq:(r + 1) * tq, :] = qh

    def step(masked):
        kt = k_ref[...]
        if fixed:
            lane = lax.broadcasted_iota(jnp.int32, (tk, LANE), 1)
            kt = jnp.where(lane == HD_B, -bound_ref[0], kt.astype(F32)).astype(BF16)
        if sel:
            kpos = j * tk + lax.broadcasted_iota(jnp.int32, (tk, LANE), 0)
            lane = lax.broadcasted_iota(jnp.int32, (tk, LANE), 1)
            ks_ref[:, :LANE] = jnp.where(kpos // SEL_BLOCK == lane, -MASK_BIG, 0.0).astype(BF16)
            ks_ref[:, LANE:] = kt
            kmat = ks_ref[...]
        else:
            kmat = kt
        s = _dot_nt(qs_ref[...], kmat)
        if masked:
            t = i * tq + lax.broadcasted_iota(jnp.int32, (rows, tk), 0) % tq
            key = j * tk + lax.broadcasted_iota(jnp.int32, (rows, tk), 1)
            if masked & MASK_CAUSAL:
                s = jnp.where(key <= t, s, -MASK_BIG)
            if masked & MASK_BAND:
                s = jnp.where(t - key <= WINDOW, s, -MASK_BIG)
        if fixed:
            acc_ref[...] += _dot(jnp.exp(s).astype(BF16), v_ref[...])
        else:
            m_old = m_ref[...]
            m_new = jnp.maximum(m_old, jnp.max(s, axis=-1, keepdims=True))
            p = jnp.exp(s - m_new[:, :1])
            acc_ref[...] = jnp.exp(m_old - m_new) * acc_ref[...] + _dot(p.astype(BF16), v_ref[...])
            m_ref[...] = m_new

    for mode in modes:
        pl.when(tab_ref[2, step_id] == mode)(functools.partial(step, mode))

    @pl.when(tab_ref[4, step_id] == 1)
    def _():
        gates = gates_ref[...]
        lane = lax.broadcasted_iota(jnp.int32, (tq, LANE), 1)
        for r in range(REP_B):
            a = acc_ref[r * tq:(r + 1) * tq, :]
            o = a / a[:, HD_B:HD_B + 1]
            g = gates[:, 3 * r + gate_col:3 * r + gate_col + 1]
            o_ref[:, r * LANE:(r + 1) * LANE] = jnp.where(lane < HD_B, o * g, 0.0)


def _flash_steps(t, tq, tk, sel):
    steps = []
    for i in range(t // tq):
        q_lo, q_hi = i * tq, (i + 1) * tq - 1
        k_lo = 0 if sel else max(q_lo - WINDOW, 0)
        js = list(range(k_lo // tk, q_hi // tk + 1))
        for j in js:
            causal = (j + 1) * tk - 1 > q_lo
            band = not sel and q_hi - j * tk > WINDOW
            steps.append((i, j, MASK_CAUSAL * causal + MASK_BAND * band, j == js[0], j == js[-1]))
    return steps


def _flash(qcat, mnot, k, v, gates, bound, b, t, tq, tk, sel):
    m = b * t
    nq, nk = t // tq, t // tk
    steps = _flash_steps(t, tq, tk, sel)
    tab = jnp.asarray(steps, jnp.int32).T
    modes = tuple(sorted({s[2] for s in steps}))
    kdim = 2 * LANE if sel else LANE
    qidx = lambda bi, g, p, *pf: (bi * nq + pf[0][0, p], g)
    kidx = lambda bi, g, p, *pf: (bi * nk + pf[0][1, p], g)

    def call(fixed):
        name = ("nsa_flash_sel" if sel else "nsa_flash_win") + ("" if fixed else "_online")
        return pl.pallas_call(
            functools.partial(_flash_body, tq=tq, tk=tk, sel=sel, gate_col=1 if sel else 2, fixed=fixed,
                              modes=modes),
            out_shape=jax.ShapeDtypeStruct((m, N_HEADS_B * LANE), F32),
            grid_spec=pltpu.PrefetchScalarGridSpec(
                num_scalar_prefetch=2,
                grid=(b, N_KV_B, len(steps)),
                in_specs=[pl.BlockSpec((tq, REP_B * LANE), qidx), pl.BlockSpec((tq, LANE), qidx),
                          pl.BlockSpec((tk, LANE), kidx), pl.BlockSpec((tk, LANE), kidx),
                          pl.BlockSpec((tq, LANE), qidx)],
                out_specs=pl.BlockSpec((tq, REP_B * LANE), qidx),
                scratch_shapes=[pltpu.VMEM((REP_B * tq, kdim), BF16), pltpu.VMEM((tk, 2 * LANE), BF16),
                                pltpu.VMEM((REP_B * tq, LANE), F32), pltpu.VMEM((REP_B * tq, LANE), F32)]),
            compiler_params=_params("parallel", "parallel", "arbitrary"),
            name=name,
        )(tab, bound.reshape(1), qcat, mnot, k, v, gates)

    return lax.cond(bound <= SHIFT_MAX, lambda: call(True), lambda: call(False))


def _sum_proj_body(a_ref, b_ref, c_ref, x_ref, w_ref, o_ref):
    o = (a_ref[...] + b_ref[...] + c_ref[...]).astype(BF16)
    o_ref[...] = x_ref[...] + _dot(o, w_ref[...])


def _sum_proj(a, b, c, x, w):
    m, d = x.shape
    kdim = a.shape[1]
    tm = _row_tile(m, 512)
    big = pl.BlockSpec((tm, kdim), lambda i: (i, 0))
    row = pl.BlockSpec((tm, d), lambda i: (i, 0))
    return pl.pallas_call(
        _sum_proj_body,
        out_shape=jax.ShapeDtypeStruct((m, d), F32),
        grid=(m // tm,),
        in_specs=[big, big, big, row, _const_spec(w.shape)],
        out_specs=row,
        compiler_params=_params("parallel"),
        name="nsa_out_proj",
    )(a, b, c, x, w)


def _nsa_weights(w_in, q_g, k_g, w_out):
    d = w_in.shape[0]
    nq = N_HEADS_B * HD_B
    w = N_KV_B * HD_B
    wq = w_in[:, :nq].reshape(d, N_HEADS_B, 1, HD_B)
    wq = jnp.broadcast_to(wq, (d, N_HEADS_B, 2, HD_B)).reshape(d, N_HEADS_B * LANE)
    wg = w_in[:, nq:nq + 3 * N_HEADS_B].reshape(d, N_KV_B, REP_B * 3)
    wg = jnp.pad(wg, ((0, 0), (0, 0), (0, LANE - REP_B * 3))).reshape(d, N_KV_B * LANE)
    wkv = w_in[:, nq + 3 * N_HEADS_B:]
    lane = jnp.arange(w)
    seg = (lane[:, None] // HD_B == lane[None, :] // HD_B).astype(BF16)
    spread = (lane[:, None] // HD_B * LANE + lane[:, None] % HD_B == jnp.arange(N_KV_B * LANE)[None, :]).astype(BF16)
    wo = jnp.pad(w_out.reshape(N_HEADS_B, HD_B, -1), ((0, 0), (0, LANE - HD_B), (0, 0)))
    return {
        "wq": wq.astype(BF16), "wg": wg.astype(BF16), "wkv": wkv.astype(BF16), "seg": seg, "spread": spread,
        "qg": jnp.tile(q_g, 2)[None, :], "kg": jnp.stack([jnp.tile(k_g[1], N_KV_B), jnp.tile(k_g[2], N_KV_B)]),
        "wo": wo.reshape(N_HEADS_B * LANE, -1).astype(BF16),
    }


def _cmp_weights(pe, w_c1, w_c2, kc_g):
    return (pe.reshape(2, 1, CMP_BLOCK * HD_B), w_c1.reshape(2, CMP_BLOCK * HD_B, HD_B).astype(BF16),
            w_c2.astype(BF16), kc_g[None, :])


def _nsa_prompt(x, g_mix, b, t, w_in, q_g, k_g, pe, w_c1, w_c2, w_out):
    wts = _nsa_weights(w_in, q_g, k_g, w_out)
    tm = 256
    pos = jnp.arange(t)
    qcat, gates, kv_rows, win_rows, ks_s, vs_s, kw_s, vw_s = _nsa_proj(
        x, g_mix, wts, _rope_tables(pos, LANE), _rope_tables(pos, HD_B), t // tm, tm)
    kc_blk, vc_blk = _cmp_prompt(kv_rows, b, t, *_cmp_weights(pe, w_c1, w_c2, k_g[0]))
    nb = t // CMP_BLOCK
    assert nb <= N_CMP_PAD and t % SEL_BLOCK == 0

    order = jnp.asarray(_cmp_block_order(), jnp.int32)

    def blocks(a, lo):
        a = jnp.pad(a, ((0, 0), (0, 0), (0, N_CMP_PAD - nb), (lo, LANE - HD_B - lo)))
        return a[:, :, order].astype(BF16)

    kct = blocks(kc_blk, HD_B)
    vct = blocks(vc_blk, 0).transpose(0, 1, 3, 2)
    tq = 256
    o_cmp, mnot = _cmp_select(qcat, kct, vct, gates, b, t, tq)
    qmax = jnp.max(jnp.abs(q_g))
    o_sel = _flash(qcat, mnot, ks_s, vs_s, gates, qmax * jnp.max(jnp.abs(k_g[1])) * math.sqrt(HD_B), b, t,
                   *(min(n, t) for n in FLASH_SEL_TILES), True)
    o_win = _flash(qcat, mnot, kw_s, vw_s, gates, qmax * jnp.max(jnp.abs(k_g[2])) * math.sqrt(HD_B), b, t,
                   *(min(n, t) for n in FLASH_WIN_TILES), False)
    y = _sum_proj(o_cmp, o_sel, o_win, x, wts["wo"])
    wb = min(WINDOW, t)
    kv_out = kv_rows.reshape(b, t, N_KV_SLOTS, N_KV_B, HD_B)
    win_out = win_rows.reshape(b, t, 2, N_KV_B, HD_B)[:, t - wb:]
    return y, kv_out, win_out


def _gmlp_layer(x, g, w_ins, ln_g, ln_b, w_s, b_s, w_outs, layer, *, single):
    gw = w_outs.shape[1] // N_GROUPS_A
    if single:
        ws = jnp.repeat(w_s[:, 0, 0], gw)[None, :]
        bs = jnp.repeat(b_s[:, 0], gw)[None, :]
    else:
        ws = w_s
        bs = jnp.repeat(b_s.T, gw, axis=1)
    return _gmlp(x, g[None, :], w_ins, ln_g[None, :], ln_b[None, :], ws, bs, w_outs, layer, single=single)


def _mlstm_proj_body(x_ref, g_ref, wq_ref, wk_ref, wv_ref, wgi_ref, wo_ref, bif_ref,
                     q_ref, k_ref, v_ref, gi_ref, og_ref):
    xb = _rms(x_ref[...], g_ref[...]).astype(BF16)
    q_ref[...] = _dot(xb, wq_ref[...]).astype(BF16)
    k_ref[...] = _dot(xb, wk_ref[...]).astype(BF16)
    v_ref[...] = _dot(xb, wv_ref[...]).astype(BF16)
    gi_ref[...] = _dot(xb, wgi_ref[...]) + bif_ref[...]
    og_ref[...] = jax.nn.sigmoid(_dot(xb, wo_ref[...]))


def _mlstm_proj(x, g, wts):
    m, d = x.shape
    hv = N_HEADS_C * DV_C
    tm = _row_tile(m, 512)
    consts = [g, wts["wq"], wts["wk"], wts["wv"], wts["wgi"], wts["wo"], wts["bif"]]
    tile = lambda n: pl.BlockSpec((tm, n), lambda i: (i, 0))
    return pl.pallas_call(
        _mlstm_proj_body,
        out_shape=(jax.ShapeDtypeStruct((m, N_HEADS_C * LANE), BF16), jax.ShapeDtypeStruct((m, N_HEADS_C * LANE), BF16),
                   jax.ShapeDtypeStruct((m, hv), BF16), jax.ShapeDtypeStruct((m, LANE), F32),
                   jax.ShapeDtypeStruct((m, hv), F32)),
        grid=(m // tm,),
        in_specs=[tile(d)] + [_const_spec(c.shape) for c in consts],
        out_specs=(tile(N_HEADS_C * LANE), tile(N_HEADS_C * LANE), tile(hv), tile(LANE), tile(hv)),
        compiler_params=_params("parallel"),
        name="mlstm_proj",
    )(x, *consts)


def _mlstm_scan_body(q_ref, k_ref, v_ref, gi_ref, git_ref, hs_ref, c_out, n_out, m_out, c_s, n_s, m_s):
    c = pl.program_id(1)
    L = q_ref.shape[0]

    @pl.when(c == 0)
    def _():
        c_s[...] = jnp.zeros(c_s.shape, F32)
        n_s[...] = jnp.zeros(n_s.shape, F32)
        m_s[...] = jnp.zeros(m_s.shape, F32)

    row = lax.broadcasted_iota(jnp.int32, (L, L), 0)
    col = lax.broadcasted_iota(jnp.int32, (L, L), 1)
    causal = col <= row
    tril = causal.astype(BF16)
    gi = gi_ref[...]
    git = git_ref[...]
    fcol = jax.nn.log_sigmoid(gi)
    frow = jax.nn.log_sigmoid(git[N_HEADS_C:, :])
    bcol_all = sum(_dot(tril, part) for part in _split3(fcol))
    brow_all = sum(_dot_nt(part, tril) for part in _split3(frow))
    heads = range(N_HEADS_C)
    sl = [slice(h * LANE, (h + 1) * LANE) for h in heads]
    q = [q_ref[:, sl[h]] for h in heads]
    k = [k_ref[:, sl[h]] for h in heads]
    v = [v_ref[:, sl[h]] for h in heads]
    qk = [_dot_nt(q[h], k[h]) for h in heads]
    cq = [_dot_nt(q[h], c_s[h].astype(BF16)) for h in heads]
    bcol = [bcol_all[:, N_HEADS_C + h:N_HEADS_C + h + 1] for h in heads]
    m_prev = [m_s[h:h + 1, 0:1] for h in heads]
    s, a, m_t = [], [], []
    for h in heads:
        dlog = jnp.where(causal, bcol[h] - brow_all[h:h + 1, :] + git[h:h + 1, :], -jnp.inf)
        inter = bcol[h] + m_prev[h]
        m_t.append(jnp.maximum(inter, jnp.max(dlog, axis=1, keepdims=True)))
        s.append(qk[h] * jnp.exp(dlog - m_t[h]))
        a.append(jnp.exp(inter - m_t[h]))
    sv = [_dot(s[h].astype(BF16), v[h]) for h in heads]
    wk, decay, m_new = [], [], []
    for h in heads:
        nq = jnp.sum(q[h].astype(F32) * n_s[h:h + 1, :], axis=1, keepdims=True)
        den = a[h] * nq + jnp.sum(s[h], axis=1, keepdims=True)
        hs_ref[:, sl[h]] = (a[h] * cq[h] + sv[h]) / jnp.maximum(jnp.abs(den), jnp.exp(-m_t[h]))
        b_end = bcol[h][L - 1:L, :]
        wlog = b_end - bcol[h] + gi[:, h:h + 1]
        m_new.append(jnp.maximum(b_end + m_prev[h], jnp.max(wlog, axis=0, keepdims=True)))
        wk.append(jnp.exp(wlog - m_new[h]))
        decay.append(jnp.exp(b_end + m_prev[h] - m_new[h]))
    upd = [_dot_tn((v[h].astype(F32) * wk[h]).astype(BF16), k[h]) for h in heads]
    for h in heads:
        c_s[h] = decay[h] * c_s[h] + upd[h]
        n_s[h:h + 1, :] = decay[h] * n_s[h:h + 1, :] + jnp.sum(k[h].astype(F32) * wk[h], axis=0, keepdims=True)
        m_s[h:h + 1, :] = jnp.broadcast_to(m_new[h], (1, LANE))

    @pl.when(c == pl.num_programs(1) - 1)
    def _():
        c_out[...] = c_s[...]
        n_out[...] = n_s[...]
        m_out[...] = m_s[...]


def _mlstm_scan(q, k, v, gi, git, b, t):
    L = math.gcd(t, CHUNK_C)
    nc = t // L
    hv = N_HEADS_C * DV_C
    tile = lambda n: pl.BlockSpec((L, n), lambda bi, c: (bi * nc + c, 0))
    return pl.pallas_call(
        _mlstm_scan_body,
        out_shape=(jax.ShapeDtypeStruct((b * t, hv), F32),
                   jax.ShapeDtypeStruct((b, N_HEADS_C, DV_C, LANE), F32),
                   jax.ShapeDtypeStruct((b, N_HEADS_C, LANE), F32), jax.ShapeDtypeStruct((b, N_HEADS_C, LANE), F32)),
        grid=(b, nc),
        in_specs=[tile(N_HEADS_C * LANE), tile(N_HEADS_C * LANE), tile(hv), tile(LANE),
                  pl.BlockSpec((None, 2 * N_HEADS_C, L), lambda bi, c: (bi, 0, c))],
        out_specs=(tile(hv), pl.BlockSpec((None, N_HEADS_C, DV_C, LANE), lambda bi, c: (bi, 0, 0, 0)),
                   pl.BlockSpec((None, N_HEADS_C, LANE), lambda bi, c: (bi, 0, 0)),
                   pl.BlockSpec((None, N_HEADS_C, LANE), lambda bi, c: (bi, 0, 0))),
        scratch_shapes=[pltpu.VMEM((N_HEADS_C, DV_C, LANE), F32), pltpu.VMEM((N_HEADS_C, LANE), F32),
                        pltpu.VMEM((N_HEADS_C, LANE), F32)],
        compiler_params=_params("parallel", "arbitrary"),
        name="mlstm_scan",
    )(q, k, v, gi, git)


def _mlstm_out_body(hs_ref, og_ref, hg_ref, x_ref, w_ref, o_ref):
    parts = []
    for h in range(N_HEADS_C):
        sl = slice(h * DV_C, (h + 1) * DV_C)
        parts.append((og_ref[:, sl] * _rms(hs_ref[:, sl], hg_ref[:, sl])).astype(BF16))
    o_ref[...] = x_ref[...] + _dot(jnp.concatenate(parts, axis=1), w_ref[...])


def _mlstm_out(hs, og, hg, x, w):
    m, d = x.shape
    hv = hs.shape[1]
    tm = _row_tile(m, 512)
    wide = pl.BlockSpec((tm, hv), lambda i: (i, 0))
    row = pl.BlockSpec((tm, d), lambda i: (i, 0))
    return pl.pallas_call(
        _mlstm_out_body,
        out_shape=jax.ShapeDtypeStruct((m, d), F32),
        grid=(m // tm,),
        in_specs=[wide, wide, _const_spec(hg.shape), row, _const_spec(w.shape)],
        out_specs=row,
        compiler_params=_params("parallel"),
        name="mlstm_out",
    )(hs, og, hg, x, w)


def _mlstm_weights(w_in, b_if):
    d = w_in.shape[0]
    hk, hv = N_HEADS_C * DK_C, N_HEADS_C * DV_C

    def spread(w):
        w = w.reshape(d, N_HEADS_C, DK_C)
        return jnp.pad(w, ((0, 0), (0, 0), (0, LANE - DK_C))).reshape(d, N_HEADS_C * LANE)

    wgi = jnp.pad(w_in[:, 2 * hk + hv:2 * hk + hv + 2 * N_HEADS_C], ((0, 0), (0, LANE - 2 * N_HEADS_C)))
    return {
        "wq": spread(w_in[:, :hk]).astype(BF16),
        "wk": (spread(w_in[:, hk:2 * hk]) * (DK_C ** -0.5)).astype(BF16),
        "wv": w_in[:, 2 * hk:2 * hk + hv].astype(BF16),
        "wgi": wgi.astype(BF16),
        "wo": w_in[:, 2 * hk + hv + 2 * N_HEADS_C:].astype(BF16),
        "bif": jnp.pad(b_if, (0, LANE - 2 * N_HEADS_C))[None, :],
    }


def _mlstm_prompt(x, g_mix, b, t, w_in, b_if, h_g, w_out):
    wts = _mlstm_weights(w_in, b_if)
    q, k, v, gi, og = _mlstm_proj(x, g_mix, wts)
    git = gi[:, :2 * N_HEADS_C].reshape(b, t, 2 * N_HEADS_C).transpose(0, 2, 1)
    hs, c, n, m = _mlstm_scan(q, k, v, gi, git, b, t)
    y = _mlstm_out(hs, og, h_g[None, :], x, w_out.astype(BF16))
    return y, c[..., :DK_C], n[..., :DK_C], m[..., 0]


def _proj_add_body(o_ref, x_ref, w_ref, out_ref):
    out_ref[...] = x_ref[...] + _dot(o_ref[...].astype(BF16), w_ref[...])


def _proj_add(o, x, w):
    m, d = x.shape
    tm = _row_tile(m, 512)
    return pl.pallas_call(
        _proj_add_body,
        out_shape=jax.ShapeDtypeStruct((m, d), F32),
        grid=(m // tm,),
        in_specs=[pl.BlockSpec((tm, o.shape[1]), lambda i: (i, 0)), pl.BlockSpec((tm, d), lambda i: (i, 0)),
                  _const_spec(w.shape)],
        out_specs=pl.BlockSpec((tm, d), lambda i: (i, 0)),
        compiler_params=_params("parallel"),
        name="proj_add",
    )(o, x, w)


def _nsa_step_body(pt_ref, *refs, n_pages, page, past_len):
    pages = refs[:n_pages]
    (win_ref, qr_ref, qn_ref, gates_ref, nkv_ref, nwin_ref, pe_ref, w1_ref, w2_ref, kcg_ref,
     o_ref, c_s, x_s) = refs[n_pages:]
    w = N_KV_B * HD_B
    length = n_pages * page
    nb = length // CMP_BLOCK
    t = past_len
    for p in range(n_pages):
        for c in range(CMP_PLANES):
            c_s[c, p * page:(p + 1) * page, :] = pages[p][c * LANE:(c + 1) * LANE, :].T
    _flatten_cmp_blocks(lambda c, start: c_s[c, pl.ds(start, 8, stride=CMP_BLOCK), :], x_s, nb)
    lane_w = lax.broadcasted_iota(jnp.int32, (HD_B, w), 1)
    row_w = lax.broadcasted_iota(jnp.int32, (HD_B, w), 0)
    cmp_nat = []
    for slot in range(2):
        y = _compress_slot(x_s, slot, pe_ref, w1_ref, w2_ref, kcg_ref)
        nat = jnp.zeros((nb, w), F32)
        for g in range(N_KV_B):
            place = (lane_w == row_w + g * HD_B).astype(BF16)
            nat = nat + _dot(y[g * nb:(g + 1) * nb].astype(BF16), place)
        cmp_nat.append(nat.astype(BF16))
    kc, vc = cmp_nat
    qr = qr_ref[...]
    qn = qn_ref[...]
    gates = gates_ref[...]
    nh = N_HEADS_B
    blk = lax.broadcasted_iota(jnp.int32, (nh, nb), 1)
    p_c = _masked_softmax(_dot_nt(qn, kc), (blk + 1) * CMP_BLOCK - 1 <= t, 1)
    o_c = _dot(p_c.astype(BF16), vc)
    blk_t = lax.broadcasted_iota(jnp.int32, (nb, nh), 0)
    p_t = _masked_softmax(_dot_nt(kc, qn), (blk_t + 1) * CMP_BLOCK - 1 <= t, 0)
    gsum = (lax.broadcasted_iota(jnp.int32, (nh, LANE), 0) // REP_B
            == lax.broadcasted_iota(jnp.int32, (nh, LANE), 1)).astype(BF16)
    pair = (lax.broadcasted_iota(jnp.int32, (nb, nb), 1) // (SEL_BLOCK // CMP_BLOCK)
            == lax.broadcasted_iota(jnp.int32, (nb, nb), 0)).astype(BF16)
    imp = sum(_dot(part, gsum) for part in _split3(p_t))
    imp = sum(_dot(pair, part) for part in _split3(imp))
    sblk = lax.broadcasted_iota(jnp.int32, (nb, LANE), 0)
    cur = t // SEL_BLOCK
    forced = (sblk == 0) | (sblk == cur) | (sblk == cur - 1)
    score = jnp.where(forced, jnp.inf, jnp.where(sblk * SEL_BLOCK <= t, imp, -jnp.inf))
    sblk_f = sblk.astype(F32)
    pickable = score > -jnp.inf
    for _ in range(N_SEL):
        mx = jnp.max(score, axis=0, keepdims=True)
        first = jnp.min(jnp.where(score == mx, sblk_f, float(nb)), axis=0, keepdims=True)
        score = jnp.where(sblk_f == first, -jnp.inf, score)
    notsel = jnp.where(pickable, jnp.where(score > -jnp.inf, -MASK_BIG, 0.0), -MASK_BIG)
    bias = _dot_nt(gsum, notsel.astype(BF16)).astype(BF16)
    expand = (lax.broadcasted_iota(jnp.int32, (nb, length), 1) // SEL_BLOCK
              == lax.broadcasted_iota(jnp.int32, (nb, length), 0)).astype(BF16)
    nkv = nkv_ref[...]
    nwin = nwin_ref[...]

    def attend(s, k_new, v_new, weighted_values):
        s_new = jnp.sum(qr.astype(F32) * k_new.astype(BF16).astype(F32), axis=1, keepdims=True)
        m = jnp.maximum(jnp.max(s, axis=1, keepdims=True), s_new)
        e = jnp.exp(s - m)
        e_new = jnp.exp(s_new - m)
        den = jnp.sum(e, axis=1, keepdims=True) + e_new
        num = weighted_values(e.astype(BF16)) + e_new.astype(BF16).astype(F32) * v_new.astype(BF16).astype(F32)
        return num / den

    s_sel = jnp.concatenate([_dot(qr, pages[p][2 * w:3 * w, :].astype(BF16)) for p in range(n_pages)], axis=1)
    o_s = attend(s_sel + _dot(bias, expand), nkv[:, 2 * w:3 * w], nkv[:, 3 * w:],
                 lambda e: sum(_dot_nt(e[:, p * page:(p + 1) * page], pages[p][3 * w:, :].astype(BF16))
                               for p in range(n_pages)))
    wb = win_ref.shape[1]
    pos_w = t - wb + lax.broadcasted_iota(jnp.int32, (nh, wb), 1)
    ok_w = (pos_w >= 0) & (t - pos_w <= WINDOW)
    s_w = jnp.where(ok_w, _dot(qr, win_ref[:w, :].astype(BF16)), -MASK_BIG)
    o_w = attend(s_w, nwin[:, :w], nwin[:, w:], lambda e: _dot_nt(e, win_ref[w:, :].astype(BF16)))
    o_ref[...] = gates[:, 0:1] * o_c + gates[:, 1:2] * o_s + gates[:, 2:3] * o_w


def _nsa_step(page_table, cache, win_cache, qr, qn, gates, new_kv, new_win, pe, w1, w2, kc_g, past_len):
    bsz, n_pages = page_table.shape
    page = cache.shape[2]
    assert page == LANE
    w = N_KV_B * HD_B
    nb = n_pages * page // CMP_BLOCK
    wb = win_cache.shape[2]
    per = lambda shape: pl.BlockSpec((None,) + shape, lambda b, pt: (b,) + (0,) * len(shape))
    const = lambda a: pl.BlockSpec(a.shape, lambda b, pt: (0,) * a.ndim)
    page_specs = [pl.BlockSpec((None, 4 * w, page), lambda b, pt, p=p: (pt[b, p], 0, 0)) for p in range(n_pages)]
    return pl.pallas_call(
        functools.partial(_nsa_step_body, n_pages=n_pages, page=page, past_len=past_len),
        out_shape=jax.ShapeDtypeStruct((bsz, N_HEADS_B, w), F32),
        grid_spec=pltpu.PrefetchScalarGridSpec(
            num_scalar_prefetch=1,
            grid=(bsz,),
            in_specs=page_specs + [per((2 * w, wb)), per((N_HEADS_B, w)), per((N_HEADS_B, w)), per((N_HEADS_B, LANE)),
                                   per((1, 4 * w)), per((1, 2 * w)), const(pe), const(w1), const(w2), const(kc_g)],
            out_specs=per((N_HEADS_B, w)),
            scratch_shapes=[pltpu.VMEM((2 * w // LANE, n_pages * page, LANE), F32),
                            pltpu.VMEM((2 * N_KV_B, nb, CMP_BLOCK * HD_B), F32)]),
        compiler_params=_params("parallel"),
        name="nsa_step",
    )(page_table, *([cache] * n_pages), win_cache, qr, qn, gates, new_kv, new_win, pe, w1, w2, kc_g)


def _nsa_sample_step(x, g_mix, past_len, kv_cache, win_cache, page_table, w_in, q_g, k_g, pe, w_c1, w_c2, w_out):
    bsz, d = x.shape
    w = N_KV_B * HD_B
    assert past_len % CMP_BLOCK == 0 and past_len // SEL_BLOCK + 1 <= past_len // CMP_BLOCK
    wts = _nsa_weights(w_in, q_g, k_g, w_out)
    pos = jnp.full((bsz,), past_len, jnp.int32)
    qcat, gates, kv_rows, win_rows, _, _, _, _ = _nsa_proj(
        x, g_mix, wts, _rope_tables(pos, LANE), _rope_tables(pos, HD_B), 1, bsz)
    q5 = qcat.reshape(bsz, N_KV_B, REP_B, 2, HD_B)
    eye = jnp.eye(N_KV_B, dtype=BF16)
    qrows = (q5[:, :, :, :, None, :] * eye[None, :, None, None, :, None])
    qr = qrows[:, :, :, 0].reshape(bsz, N_HEADS_B, w)
    qn = qrows[:, :, :, 1].reshape(bsz, N_HEADS_B, w)
    gts = gates.reshape(bsz, N_KV_B, LANE)[:, :, :REP_B * 3].reshape(bsz, N_HEADS_B, 3)
    gts = jnp.pad(gts, ((0, 0), (0, 0), (0, LANE - 3)))
    pool, page = kv_cache.shape[:2]
    cache_t = kv_cache.reshape(pool, page, 4 * w).transpose(0, 2, 1)
    win_t = win_cache.reshape(bsz, -1, 2 * w).transpose(0, 2, 1)
    o = _nsa_step(page_table, cache_t, win_t, qr, qn, gts,
                  kv_rows.reshape(bsz, 1, 4 * w), win_rows.reshape(bsz, 1, 2 * w),
                  *_cmp_weights(pe, w_c1, w_c2, k_g[0]), past_len)
    own = (jnp.arange(N_HEADS_B)[:, None] // REP_B == jnp.arange(N_KV_B)[None, :]).astype(F32)
    w_exp = own[:, :, None, None] * w_out.reshape(N_HEADS_B, 1, HD_B, d)
    y = _proj_add(o.reshape(bsz, N_HEADS_B * w), x, w_exp.reshape(N_HEADS_B * w, d).astype(BF16))
    return y, kv_rows.reshape(bsz, 1, N_KV_SLOTS, N_KV_B, HD_B), win_rows.reshape(bsz, 1, 2, N_KV_B, HD_B)


def _mlstm_step_body(q_ref, k_ref, qt_ref, kt_ref, v_ref, gi_ref, ct_ref, n_ref, m_ref,
                     h_ref, cto_ref, no_ref, mo_ref, *, sb):
    gi = gi_ref[...]
    logf = jax.nn.log_sigmoid(gi)
    m_all = m_ref[...]
    lane_m = lax.broadcasted_iota(jnp.int32, (1, N_HEADS_C), 1)
    for s in range(sb):
        m_new_row = jnp.zeros((1, N_HEADS_C), F32)
        for h in range(N_HEADS_C):
            q = q_ref[s:s + 1, h * LANE:h * LANE + DK_C].astype(F32)
            k = k_ref[s:s + 1, h * LANE:h * LANE + DK_C].astype(F32)
            qc = qt_ref[s, :, h:h + 1]
            kc = kt_ref[s, :, h:h + 1]
            v = v_ref[s:s + 1, h * DV_C:(h + 1) * DV_C].astype(F32)
            ct = ct_ref[s, h]
            n = n_ref[s, h:h + 1, :]
            it = gi[s:s + 1, h:h + 1]
            b = logf[s:s + 1, N_HEADS_C + h:N_HEADS_C + h + 1]
            m0 = m_all[s:s + 1, h:h + 1]
            inter = b + m0
            m_t = jnp.maximum(inter, it)
            wgt = jnp.exp(it - m_t)
            a = jnp.exp(inter - m_t)
            sc = jnp.sum(q * k, axis=1, keepdims=True) * wgt
            num = a * jnp.sum(ct * qc, axis=0, keepdims=True) + sc * v
            den = a * jnp.sum(n * q, axis=1, keepdims=True) + sc
            h_ref[s:s + 1, h * DV_C:(h + 1) * DV_C] = num / jnp.maximum(jnp.abs(den), jnp.exp(-m_t))
            cto_ref[s, h] = a * ct + (wgt * kc) * v
            no_ref[s, h:h + 1, :] = a * n + wgt * k
            m_new_row = jnp.where(lane_m == h, m_t, m_new_row)
        mo_ref[s:s + 1, :] = m_new_row


def _mlstm_step(q, k, qt, kt, v, gi, ct0, n0, m0):
    bsz = q.shape[0]
    sb = 8
    row = lambda n: pl.BlockSpec((sb, n), lambda i: (i, 0))
    c_spec = pl.BlockSpec((sb, N_HEADS_C, DK_C, DV_C), lambda i: (i, 0, 0, 0))
    n_spec = pl.BlockSpec((sb, N_HEADS_C, DK_C), lambda i: (i, 0, 0))
    col_spec = pl.BlockSpec((sb, DK_C, N_HEADS_C), lambda i: (i, 0, 0))
    hv = N_HEADS_C * DV_C
    return pl.pallas_call(
        functools.partial(_mlstm_step_body, sb=sb),
        out_shape=(jax.ShapeDtypeStruct((bsz, hv), F32), jax.ShapeDtypeStruct(ct0.shape, F32),
                   jax.ShapeDtypeStruct(n0.shape, F32), jax.ShapeDtypeStruct(m0.shape, F32)),
        grid=(bsz // sb,),
        in_specs=[row(N_HEADS_C * LANE), row(N_HEADS_C * LANE), col_spec, col_spec, row(hv), row(LANE),
                  c_spec, n_spec, row(N_HEADS_C)],
        out_specs=(row(hv), c_spec, n_spec, row(N_HEADS_C)),
        compiler_params=_params("parallel"),
        name="mlstm_step",
    )(q, k, qt, kt, v, gi, ct0, n0, m0)


def _mlstm_sample_step(x, g_mix, c0, n0, m0, w_in, b_if, h_g, w_out):
    bsz = x.shape[0]
    wts = _mlstm_weights(w_in, b_if)
    q, k, v, gi, og = _mlstm_proj(x, g_mix, wts)
    cols = lambda a: a.astype(F32).reshape(bsz, N_HEADS_C, LANE)[:, :, :DK_C].transpose(0, 2, 1)
    hs, ct, n, m = _mlstm_step(q, k, cols(q), cols(k), v, gi, c0.transpose(0, 1, 3, 2), n0, m0)
    y = _mlstm_out(hs, og, h_g[None, :], x, w_out.astype(BF16))
    return y, ct.transpose(0, 1, 3, 2), n, m


def kernel(x_prompt, x_sample, cache_nsa_kv, cache_nsa_win, state_mlstm_C, state_mlstm_n, state_mlstm_m, page_table,
           norm_mix_g, norm_ffn_g, ffn_w1, ffn_w2, a_w_in, a_ln_g, a_ln_b, a_w_s, a_b_s, a_w_out,
           b_w_in, b_q_g, b_k_g, b_pe, b_w_c1, b_w_c2, b_w_out, c_w_in, c_b_if, c_h_g, c_w_out):
    bp, t, d = x_prompt.shape
    bs, ts, _ = x_sample.shape
    assert ts == 1
    past_len = page_table.shape[1] * cache_nsa_kv.shape[2]
    xp = x_prompt.reshape(bp * t, d)
    xs = x_sample.reshape(bs, d)
    out = {k: [] for k in ("v_s", "kv_p", "win_p", "kv_s", "win_s", "C_p", "n_p", "m_p", "C_s", "n_s", "m_s")}
    ffn_w1_b, ffn_w2_b = ffn_w1.astype(BF16), ffn_w2.astype(BF16)
    a_w_in_b, a_w_out_b = a_w_in.astype(BF16), a_w_out.astype(BF16)
    for layer in range(norm_mix_g.shape[0]):
        kind, j = layer % 3, layer // 3
        gm = norm_mix_g[layer]
        if kind == 0:
            args = (a_w_in_b, a_ln_g[j], a_ln_b[j], a_w_s[j], a_b_s[j], a_w_out_b, j)
            xp = _gmlp_layer(xp, gm, *args, single=False)[0]
            xs, v = _gmlp_layer(xs, gm, *args, single=True)
            out["v_s"].append(v.reshape(bs, ts, -1))
        elif kind == 1:
            args = (b_w_in[j], b_q_g[j], b_k_g[j], b_pe[j], b_w_c1[j], b_w_c2[j], b_w_out[j])
            xp, kv, win = _nsa_prompt(xp, gm[None, :], bp, t, *args)
            out["kv_p"].append(kv)
            out["win_p"].append(win)
            xs, kv, win = _nsa_sample_step(xs, gm[None, :], past_len, cache_nsa_kv[j], cache_nsa_win[j], page_table,
                                           *args)
            out["kv_s"].append(kv)
            out["win_s"].append(win)
        else:
            args = (c_w_in[j], c_b_if[j], c_h_g[j], c_w_out[j])
            xp, c, n, m = _mlstm_prompt(xp, gm[None, :], bp, t, *args)
            out["C_p"].append(c)
            out["n_p"].append(n)
            out["m_p"].append(m)
            xs, c, n, m = _mlstm_sample_step(xs, gm[None, :], state_mlstm_C[j], state_mlstm_n[j], state_mlstm_m[j],
                                             *args)
            out["C_s"].append(c)
            out["n_s"].append(n)
            out["m_s"].append(m)
        gf = norm_ffn_g[layer][None, :]
        xp = _ffn(xp, gf, ffn_w1_b, ffn_w2_b, layer)
        xs = _ffn(xs, gf, ffn_w1_b, ffn_w2_b, layer)
    st = {k: jnp.stack(v) for k, v in out.items()}
    return (xp.reshape(bp, t, d), xs.reshape(bs, ts, d), st["v_s"], st["kv_p"], st["win_p"], st["kv_s"], st["win_s"],
            st["C_p"], st["n_p"], st["m_p"], st["C_s"], st["n_s"], st["m_s"])
```

```python
import functools
import math

import jax
import jax.numpy as jnp
from jax import lax
from jax.experimental import pallas as pl
from jax.experimental.pallas import tpu as pltpu

F32 = jnp.float32
BF16 = jnp.bfloat16

EPS = 1e-6
CHUNK_A = 128
N_GROUPS_A = 8
N_HEADS_B = 16
N_KV_B = 4
REP_B = N_HEADS_B // N_KV_B
HD_B = 64
ROT_DIM = 16
ROPE_THETA = 500000.0
CMP_BLOCK = 32
SEL_BLOCK = 64
N_SEL = 16
WINDOW = 512
N_KV_SLOTS = 4
N_HEADS_C = 8
DK_C = 64
DV_C = 128
CHUNK_C = 128
SCALE_B = HD_B ** -0.5

LANE = 128
VMEM_LIMIT_BYTES = 56 * 1024 * 1024
MASK_BIG = 1e30
M_INIT = -1e20
N_SEL_PAD = 128
N_CMP_PAD = 2 * N_SEL_PAD
SHIFT_MAX = 40.0
FLASH_SEL_TILES = (1024, 512)
FLASH_WIN_TILES = (512, 512)
CMP_SELECT_TILE = 512


def _params(*sem):
    return pltpu.CompilerParams(dimension_semantics=sem, vmem_limit_bytes=VMEM_LIMIT_BYTES)


def _dot(a, b):
    return jnp.dot(a, b, preferred_element_type=F32)


def _dot_nt(a, b):
    return lax.dot_general(a, b, (((1,), (1,)), ((), ())), preferred_element_type=F32)


def _dot_tn(a, b):
    return lax.dot_general(a, b, (((0,), (0,)), ((), ())), preferred_element_type=F32)


def _rms(x, g):
    return x * lax.rsqrt(jnp.mean(x * x, axis=-1, keepdims=True) + EPS) * g


def _split3(x):
    a = x.astype(BF16)
    r = x - a.astype(F32)
    b = r.astype(BF16)
    c = (r - b.astype(F32)).astype(BF16)
    return a, b, c


def _const_spec(shape):
    n = len(shape)
    return pl.BlockSpec(shape, lambda *_: (0,) * n)


def _row_tile(m, pref):
    t = min(pref, m)
    while m % t:
        t //= 2
    return t


def _ffn_body(x_ref, g_ref, w1_ref, w2_ref, o_ref, *, ck):
    x = x_ref[...]
    xb = _rms(x, g_ref[...]).astype(BF16)
    acc = x
    for j in range(w1_ref.shape[1] // ck):
        h = jnp.maximum(_dot(xb, w1_ref[:, j * ck:(j + 1) * ck]), 0.0)
        acc = acc + _dot((h * h).astype(BF16), w2_ref[j * ck:(j + 1) * ck, :])
    o_ref[...] = acc


def _layer_spec(stacked, layer):
    n = stacked.ndim - 1
    return pl.BlockSpec((None,) + stacked.shape[1:], lambda *_: (layer,) + (0,) * n)


def _ffn(x, g, w1s, w2s, layer):
    m, d = x.shape
    tm = _row_tile(m, 512)
    return pl.pallas_call(
        functools.partial(_ffn_body, ck=1024),
        out_shape=jax.ShapeDtypeStruct((m, d), F32),
        grid=(m // tm,),
        in_specs=[pl.BlockSpec((tm, d), lambda i: (i, 0)), _const_spec(g.shape),
                  _layer_spec(w1s, layer), _layer_spec(w2s, layer)],
        out_specs=pl.BlockSpec((tm, d), lambda i: (i, 0)),
        compiler_params=_params("parallel"),
        name="ffn",
    )(x, g, w1s, w2s)


def _gmlp_body(x_ref, g_ref, win_ref, lng_ref, lnb_ref, ws_ref, bs_ref, wout_ref, o_ref, *maybe_v_ref, single):
    x = x_ref[...]
    dg = lng_ref.shape[1]
    xb = _rms(x, g_ref[...]).astype(BF16)
    u = jax.nn.gelu(_dot(xb, win_ref[:, :dg]))
    v = jax.nn.gelu(_dot(xb, win_ref[:, dg:]))
    mu = jnp.mean(v, axis=-1, keepdims=True)
    vc = v - mu
    var = jnp.mean(vc * vc, axis=-1, keepdims=True)
    v = vc * lax.rsqrt(var + EPS) * lng_ref[...] + lnb_ref[...]
    if single:
        maybe_v_ref[0][...] = v
        gate = v * ws_ref[...] + bs_ref[...]
    else:
        gw = dg // N_GROUPS_A
        row = lax.broadcasted_iota(jnp.int32, (CHUNK_A, CHUNK_A), 0)
        col = lax.broadcasted_iota(jnp.int32, (CHUNK_A, CHUNK_A), 1)
        causal = col <= row
        vb = v.astype(BF16)
        chunks = []
        for c in range(x.shape[0] // CHUNK_A):
            parts = []
            for gi in range(N_GROUPS_A):
                w = jnp.where(causal, ws_ref[gi], 0.0).astype(BF16)
                parts.append(_dot(w, vb[c * CHUNK_A:(c + 1) * CHUNK_A, gi * gw:(gi + 1) * gw]))
            chunks.append(jnp.concatenate(parts, axis=1) + bs_ref[...])
        gate = jnp.concatenate(chunks, axis=0)
    o_ref[...] = x + _dot((u * gate).astype(BF16), wout_ref[...])


def _gmlp(x, g, w_ins, ln_g, ln_b, ws, bs, w_outs, layer, *, single):
    m, d = x.shape
    dg = w_outs.shape[1]
    tm = _row_tile(m, 256)
    n_out = 2 if single else 1
    outs = pl.pallas_call(
        functools.partial(_gmlp_body, single=single),
        out_shape=(jax.ShapeDtypeStruct((m, d), F32), jax.ShapeDtypeStruct((m, dg), F32))[:n_out],
        grid=(m // tm,),
        in_specs=[pl.BlockSpec((tm, d), lambda i: (i, 0)), _const_spec(g.shape), _layer_spec(w_ins, layer),
                  _const_spec(ln_g.shape), _const_spec(ln_b.shape), _const_spec(ws.shape), _const_spec(bs.shape),
                  _layer_spec(w_outs, layer)],
        out_specs=(pl.BlockSpec((tm, d), lambda i: (i, 0)), pl.BlockSpec((tm, dg), lambda i: (i, 0)))[:n_out],
        compiler_params=_params("parallel"),
        name="gmlp_single" if single else "gmlp",
    )(x, g, w_ins, ln_g, ln_b, ws, bs, w_outs)
    return outs if single else (outs[0], None)


def _rope_tables(pos, seg):
    half = ROT_DIM // 2
    freq = jnp.power(ROPE_THETA, -jnp.arange(half, dtype=F32) * 2.0 / ROT_DIM)
    ang = pos.astype(F32)[:, None] * freq[None, :]
    cos, sin = jnp.cos(ang), jnp.sin(ang)
    t = pos.shape[0]
    one = jnp.ones((t, seg - ROT_DIM), F32)
    zero = jnp.zeros((t, seg - ROT_DIM), F32)
    z8 = jnp.zeros((t, half), F32)
    tabs = [jnp.concatenate([cos, cos, one], 1), jnp.concatenate([-sin, z8, zero], 1),
            jnp.concatenate([z8, sin, zero], 1)]
    return jnp.stack([jnp.tile(a, (1, LANE // seg)) for a in tabs])


def _rope128(x, tab):
    return x * tab[0] + pltpu.roll(x, LANE - ROT_DIM // 2, 1) * tab[1] + pltpu.roll(x, ROT_DIM // 2, 1) * tab[2]


def _nsa_proj_body(x_ref, g_ref, wq_ref, wg_ref, wkv_ref, seg_ref, spread_ref, qg_ref, kg_ref, tq_ref, tk_ref,
                   qcat_ref, gates_ref, kv_ref, win_ref, ks_ref, vs_ref, kw_ref, vw_ref):
    x = x_ref[...]
    xb = _rms(x, g_ref[...]).astype(BF16)
    tq = tq_ref[...]
    tk = tk_ref[...]
    qg = qg_ref[...]
    for h in range(N_HEADS_B):
        q = _dot(xb, wq_ref[:, h * LANE:(h + 1) * LANE])
        ms = jnp.sum(q * q, axis=-1, keepdims=True) * (1.0 / LANE)
        qn = q * lax.rsqrt(ms + EPS) * qg
        qcat_ref[:, h * LANE:(h + 1) * LANE] = (_rope128(qn, tq) * SCALE_B).astype(BF16)
    gates_ref[...] = jax.nn.sigmoid(_dot(xb, wg_ref[...]))
    kv = _dot(xb, wkv_ref[...])
    w = N_KV_B * HD_B
    seg = seg_ref[...]
    spread = spread_ref[...]

    def head_norm(k, gain):
        k2 = k * k
        hi = k2.astype(BF16)
        lo = (k2 - hi.astype(F32)).astype(BF16)
        ss = _dot(hi, seg) + _dot(lo, seg)
        return k * lax.rsqrt(ss * (1.0 / HD_B) + EPS) * gain

    def rope(k):
        return jnp.concatenate([_rope128(k[:, j * LANE:(j + 1) * LANE], tk) for j in range(w // LANE)], axis=1)

    ks = rope(head_norm(kv[:, 2 * w:3 * w], kg_ref[0:1, :]))
    kw = rope(head_norm(kv[:, 4 * w:5 * w], kg_ref[1:2, :]))
    vs = kv[:, 3 * w:4 * w]
    vw = kv[:, 5 * w:6 * w]
    kv_ref[:, :2 * w] = kv[:, :2 * w]
    kv_ref[:, 2 * w:3 * w] = ks
    kv_ref[:, 3 * w:] = vs
    win_ref[:, :w] = kw
    win_ref[:, w:] = vw
    lane = lax.broadcasted_iota(jnp.int32, (1, N_KV_B * LANE), 1)
    ones_hi = ((lane & HD_B) != 0).astype(F32)
    ks_ref[...] = _dot(ks.astype(BF16), spread).astype(BF16)
    kw_ref[...] = _dot(kw.astype(BF16), spread).astype(BF16)
    vs_ref[...] = (_dot(vs.astype(BF16), spread) + ones_hi).astype(BF16)
    vw_ref[...] = (_dot(vw.astype(BF16), spread) + ones_hi).astype(BF16)


def _nsa_proj(x, g, wts, tab_q, tab_k, n_tab_tiles, tm):
    m, d = x.shape
    w = N_KV_B * HD_B
    ws = N_KV_B * LANE
    tile = lambda n: pl.BlockSpec((tm, n), lambda i: (i, 0))
    tab = pl.BlockSpec((3, tm, LANE), lambda i: (0, i % n_tab_tiles, 0))
    consts = [g, wts["wq"], wts["wg"], wts["wkv"], wts["seg"], wts["spread"], wts["qg"], wts["kg"]]
    return pl.pallas_call(
        _nsa_proj_body,
        out_shape=(jax.ShapeDtypeStruct((m, N_HEADS_B * LANE), BF16), jax.ShapeDtypeStruct((m, ws), F32),
                   jax.ShapeDtypeStruct((m, 4 * w), F32), jax.ShapeDtypeStruct((m, 2 * w), F32),
                   jax.ShapeDtypeStruct((m, ws), BF16), jax.ShapeDtypeStruct((m, ws), BF16),
                   jax.ShapeDtypeStruct((m, ws), BF16), jax.ShapeDtypeStruct((m, ws), BF16)),
        grid=(m // tm,),
        in_specs=[tile(d)] + [_const_spec(c.shape) for c in consts] + [tab, tab],
        out_specs=(tile(N_HEADS_B * LANE), tile(ws), tile(4 * w), tile(2 * w), tile(ws), tile(ws), tile(ws), tile(ws)),
        compiler_params=_params("parallel"),
        name="nsa_proj",
    )(x, *consts, tab_q, tab_k)


CMP_PLANES = 2 * N_KV_B * HD_B // LANE


def _flatten_cmp_blocks(load_rows, x_s, n_blocks):
    for ng in range(n_blocks // 8):
        for l in range(CMP_BLOCK):
            for c in range(CMP_PLANES):
                rows = load_rows(c, ng * 8 * CMP_BLOCK + l)
                for half in range(LANE // HD_B):
                    x_s[c * (LANE // HD_B) + half, ng * 8:(ng + 1) * 8, l * HD_B:(l + 1) * HD_B] = (
                        rows[:, half * HD_B:(half + 1) * HD_B])


def _compress_slot(x_s, slot, pe_ref, w1_ref, w2_ref, kcg_ref):
    n_blocks = x_s.shape[1]
    xs = x_s[slot * N_KV_B:(slot + 1) * N_KV_B].reshape(N_KV_B * n_blocks, CMP_BLOCK * HD_B)
    hid = jax.nn.gelu(_dot((xs + pe_ref[slot]).astype(BF16), w1_ref[slot]))
    y = _dot(hid.astype(BF16), w2_ref[slot])
    return _rms(y, kcg_ref[...]) if slot == 0 else y


def _cmp_prompt_body(*refs, n_blocks):
    planes = refs[:CMP_PLANES]
    pe_ref, w1_ref, w2_ref, kcg_ref, kc_ref, vc_ref, x_s = refs[CMP_PLANES:]
    _flatten_cmp_blocks(lambda c, start: planes[c][pl.ds(start, 8, stride=CMP_BLOCK), :], x_s, n_blocks)
    kc_ref[...] = _compress_slot(x_s, 0, pe_ref, w1_ref, w2_ref, kcg_ref).reshape(N_KV_B, n_blocks, HD_B)
    vc_ref[...] = _compress_slot(x_s, 1, pe_ref, w1_ref, w2_ref, kcg_ref).reshape(N_KV_B, n_blocks, HD_B)


def _cmp_prompt(kv_rows, b, t, pe, w1, w2, kc_g):
    tt = min(t, 2048)
    nbk = tt // CMP_BLOCK
    assert t % tt == 0 and nbk % 8 == 0
    steps = t // tt
    out = jax.ShapeDtypeStruct((b, N_KV_B, t // CMP_BLOCK, HD_B), F32)
    out_spec = pl.BlockSpec((None, N_KV_B, nbk, HD_B), lambda bi, i: (bi, 0, i, 0))
    plane_specs = [pl.BlockSpec((tt, LANE), lambda bi, i, c=c: (bi * steps + i, c)) for c in range(CMP_PLANES)]
    return pl.pallas_call(
        functools.partial(_cmp_prompt_body, n_blocks=nbk),
        out_shape=(out, out),
        grid=(b, steps),
        in_specs=plane_specs + [_const_spec(pe.shape), _const_spec(w1.shape), _const_spec(w2.shape),
                                _const_spec(kc_g.shape)],
        out_specs=(out_spec, out_spec),
        scratch_shapes=[pltpu.VMEM((2 * N_KV_B, nbk, CMP_BLOCK * HD_B), F32)],
        compiler_params=_params("parallel", "parallel"),
        name="nsa_compress",
    )(*([kv_rows] * CMP_PLANES), pe, w1, w2, kc_g)


def _masked_softmax(s, mask, axis):
    sm = jnp.where(mask, s, -jnp.inf)
    mx = jnp.max(sm, axis=axis, keepdims=True)
    mx = jnp.where(mx > -jnp.inf, mx, 0.0)
    e = jnp.where(mask, jnp.exp(s - mx), 0.0)
    return e / jnp.maximum(jnp.sum(e, axis=axis, keepdims=True), 1e-30)


CMP_SEGMENTS = (64, 64, 128)


def _cmp_block_order():
    order, base = [], 0
    for size in CMP_SEGMENTS:
        order += list(range(base, base + size, 2)) + list(range(base + 1, base + size, 2))
        base += size
    assert base == N_CMP_PAD
    return order


def _cmp_select_body(q_ref, kct_ref, vct_ref, gates_ref, o_ref, mnot_ref, *, tq):
    i = pl.program_id(2)
    q0 = i * tq
    gates = gates_ref[...]

    def run(nv):
        hv = nv // 2
        row = lax.broadcasted_iota(jnp.int32, (nv, tq), 0)
        tok = q0 + lax.broadcasted_iota(jnp.int32, (nv, tq), 1)
        blk_c = jnp.zeros((nv, tq), jnp.int32)
        base = 0
        for size in CMP_SEGMENTS:
            if base < nv:
                local = row - base
                seg_blk = base + jnp.where(local < size // 2, 2 * local, 2 * (local - size // 2) + 1)
                blk_c = jnp.where((row >= base) & (row < base + size), seg_blk, blk_c)
            base += size
        mask = (blk_c + 1) * CMP_BLOCK - 1 <= tok
        kct = kct_ref[:nv, :]
        vct = vct_ref[:, :nv]
        imp_parts = None
        for r in range(REP_B):
            qh = q_ref[:, r * LANE:(r + 1) * LANE]
            pt = _masked_softmax(_dot_nt(kct, qh), mask, 0)
            o_ref[:, r * LANE:(r + 1) * LANE] = _dot(vct, pt.astype(BF16)).T * gates[:, 3 * r:3 * r + 1]
            parts, base = [], 0
            for size in CMP_SEGMENTS:
                if base < nv:
                    parts.append(pt[base:base + size // 2] + pt[base + size // 2:base + size])
                base += size
            imp_parts = parts if imp_parts is None else [a + b for a, b in zip(imp_parts, parts)]
        imp = jnp.concatenate(imp_parts, axis=0)
        blk = lax.broadcasted_iota(jnp.int32, (hv, tq), 0)
        t_s = q0 + lax.broadcasted_iota(jnp.int32, (hv, tq), 1)
        cur = t_s // SEL_BLOCK
        forced = (blk == 0) | (blk == cur) | (blk == cur - 1)
        score = jnp.where(forced, jnp.inf, jnp.where(blk * SEL_BLOCK <= t_s, imp, -jnp.inf))
        blk_f = blk.astype(F32)
        pickable = score > -jnp.inf
        for _ in range(N_SEL):
            mx = jnp.max(score, axis=0, keepdims=True)
            first = jnp.min(jnp.where(score == mx, blk_f, float(N_SEL_PAD)), axis=0, keepdims=True)
            score = jnp.where(blk_f == first, -jnp.inf, score)
        mnot = jnp.where(pickable, jnp.where(score > -jnp.inf, 1.0, 0.0), 1.0)
        if hv < N_SEL_PAD:
            mnot = jnp.concatenate([mnot, jnp.ones((N_SEL_PAD - hv, tq), F32)], axis=0)
        mnot_ref[...] = mnot.T.astype(BF16)

    need = (q0 + tq) // CMP_BLOCK
    bounds, base = [], 0
    for size in CMP_SEGMENTS:
        base += size
        bounds.append(base)
    lo = 0
    for nv in bounds:
        pl.when((need > lo) & (need <= nv))(functools.partial(run, nv))
        lo = nv


def _cmp_select(qcat, kct, vct, gates, b, t, tq):
    m = b * t
    nq = t // tq
    return pl.pallas_call(
        functools.partial(_cmp_select_body, tq=tq),
        out_shape=(jax.ShapeDtypeStruct((m, N_HEADS_B * LANE), F32), jax.ShapeDtypeStruct((m, N_KV_B * LANE), BF16)),
        grid=(b, N_KV_B, nq),
        in_specs=[pl.BlockSpec((tq, REP_B * LANE), lambda bi, g, i: (bi * nq + i, g)),
                  pl.BlockSpec((None, None, N_CMP_PAD, LANE), lambda bi, g, i: (bi, g, 0, 0)),
                  pl.BlockSpec((None, None, LANE, N_CMP_PAD), lambda bi, g, i: (bi, g, 0, 0)),
                  pl.BlockSpec((tq, LANE), lambda bi, g, i: (bi * nq + i, g))],
        out_specs=(pl.BlockSpec((tq, REP_B * LANE), lambda bi, g, i: (bi * nq + i, g)),
                   pl.BlockSpec((tq, LANE), lambda bi, g, i: (bi * nq + i, g))),
        compiler_params=_params("parallel", "parallel", "parallel"),
        name="nsa_cmp_select",
    )(qcat, kct, vct, gates)


def _flash_body(tab_ref, bound_ref, q_ref, mnot_ref, k_ref, v_ref, gates_ref, mask_ref, o_ref,
                qs_ref, ks_ref, m_ref, acc_ref, *, tq, tk, sel, gate_col, fixed):
    step_id = pl.program_id(2)
    j = tab_ref[1, step_id]
    rows = REP_B * tq

    @pl.when(tab_ref[3, step_id] == 1)
    def _():
        if not fixed:
            m_ref[...] = jnp.full(m_ref.shape, M_INIT, F32)
        acc_ref[...] = jnp.zeros(acc_ref.shape, F32)
        lane = lax.broadcasted_iota(jnp.int32, (tq, LANE), 1)
        for r in range(REP_B):
            qh = q_ref[:, r * LANE:(r + 1) * LANE]
            if fixed:
                qh = jnp.where(lane < HD_B, qh, jnp.ones_like(qh))
            if sel:
                qs_ref[r * tq:(r + 1) * tq, :LANE] = mnot_ref[...]
                qs_ref[r * tq:(r + 1) * tq, LANE:] = qh
            else:
                qs_ref[r * tq:(r + 1) * tq, :] = qh

    def step(masked):
        kt = k_ref[...]
        if fixed:
            lane = lax.broadcasted_iota(jnp.int32, (tk, LANE), 1)
            kt = jnp.where(lane == HD_B, -bound_ref[0], kt.astype(F32)).astype(BF16)
        if sel:
            kpos = j * tk + lax.broadcasted_iota(jnp.int32, (tk, LANE), 0)
            lane = lax.broadcasted_iota(jnp.int32, (tk, LANE), 1)
            ks_ref[:, :LANE] = jnp.where(kpos // SEL_BLOCK == lane, -MASK_BIG, 0.0).astype(BF16)
            ks_ref[:, LANE:] = kt
            kmat = ks_ref[...]
        else:
            kmat = kt
        s = _dot_nt(qs_ref[...], kmat)
        if masked:
            s = (s.reshape(REP_B, tq, tk) + mask_ref[...][None]).reshape(rows, tk)
        if fixed:
            acc_ref[...] += _dot(jnp.exp(s).astype(BF16), v_ref[...])
        else:
            m_old = m_ref[...]
            m_new = jnp.maximum(m_old, jnp.max(s, axis=-1, keepdims=True))
            p = jnp.exp(s - m_new[:, :1])
            acc_ref[...] = jnp.exp(m_old - m_new) * acc_ref[...] + _dot(p.astype(BF16), v_ref[...])
            m_ref[...] = m_new

    pl.when(tab_ref[2, step_id] == 0)(functools.partial(step, False))
    pl.when(tab_ref[2, step_id] == 1)(functools.partial(step, True))

    @pl.when(tab_ref[4, step_id] == 1)
    def _():
        gates = gates_ref[...]
        lane = lax.broadcasted_iota(jnp.int32, (tq, LANE), 1)
        for r in range(REP_B):
            a = acc_ref[r * tq:(r + 1) * tq, :]
            o = a / a[:, HD_B:HD_B + 1]
            g = gates[:, 3 * r + gate_col:3 * r + gate_col + 1]
            o_ref[:, r * LANE:(r + 1) * LANE] = jnp.where(lane < HD_B, o * g, 0.0)


def _flash_steps(t, tq, tk, sel):
    steps, offsets = [], []
    for i in range(t // tq):
        q_lo, q_hi = i * tq, (i + 1) * tq - 1
        k_lo = 0 if sel else max(q_lo - WINDOW, 0)
        js = list(range(k_lo // tk, q_hi // tk + 1))
        for j in js:
            causal = (j + 1) * tk - 1 > q_lo
            band = not sel and q_hi - j * tk > WINDOW
            masked = causal or band
            if masked and q_lo - j * tk not in offsets:
                offsets.append(q_lo - j * tk)
            pattern = offsets.index(q_lo - j * tk) if masked else (steps[-1][5] if steps else 0)
            steps.append((i, j, int(masked), j == js[0], j == js[-1], pattern))
    return steps, offsets


def _flash(qcat, mnot, k, v, gates, bound, b, t, tq, tk, sel):
    m = b * t
    nq, nk = t // tq, t // tk
    steps, offsets = _flash_steps(t, tq, tk, sel)
    tab = jnp.asarray(steps, jnp.int32).T
    dist = (jnp.asarray(offsets, jnp.int32)[:, None, None] + jnp.arange(tq, dtype=jnp.int32)[None, :, None]
            - jnp.arange(tk, dtype=jnp.int32)[None, None, :])
    allowed = (dist >= 0) if sel else ((dist >= 0) & (dist <= WINDOW))
    masks = jnp.where(allowed, 0.0, -MASK_BIG).astype(F32)
    kdim = 2 * LANE if sel else LANE
    qidx = lambda bi, g, p, *pf: (bi * nq + pf[0][0, p], g)
    kidx = lambda bi, g, p, *pf: (bi * nk + pf[0][1, p], g)
    midx = lambda bi, g, p, *pf: (pf[0][5, p], 0, 0)

    def call(fixed):
        name = ("nsa_flash_sel" if sel else "nsa_flash_win") + ("" if fixed else "_online")
        return pl.pallas_call(
            functools.partial(_flash_body, tq=tq, tk=tk, sel=sel, gate_col=1 if sel else 2, fixed=fixed),
            out_shape=jax.ShapeDtypeStruct((m, N_HEADS_B * LANE), F32),
            grid_spec=pltpu.PrefetchScalarGridSpec(
                num_scalar_prefetch=2,
                grid=(b, N_KV_B, len(steps)),
                in_specs=[pl.BlockSpec((tq, REP_B * LANE), qidx), pl.BlockSpec((tq, LANE), qidx),
                          pl.BlockSpec((tk, LANE), kidx), pl.BlockSpec((tk, LANE), kidx),
                          pl.BlockSpec((tq, LANE), qidx), pl.BlockSpec((None, tq, tk), midx)],
                out_specs=pl.BlockSpec((tq, REP_B * LANE), qidx),
                scratch_shapes=[pltpu.VMEM((REP_B * tq, kdim), BF16), pltpu.VMEM((tk, 2 * LANE), BF16),
                                pltpu.VMEM((REP_B * tq, LANE), F32), pltpu.VMEM((REP_B * tq, LANE), F32)]),
            compiler_params=_params("parallel", "parallel", "arbitrary"),
            name=name,
        )(tab, bound.reshape(1), qcat, mnot, k, v, gates, masks)

    return lax.cond(bound <= SHIFT_MAX, lambda: call(True), lambda: call(False))


def _sum_proj_body(a_ref, b_ref, c_ref, x_ref, w_ref, o_ref):
    o = (a_ref[...] + b_ref[...] + c_ref[...]).astype(BF16)
    o_ref[...] = x_ref[...] + _dot(o, w_ref[...])


def _sum_proj(a, b, c, x, w):
    m, d = x.shape
    kdim = a.shape[1]
    tm = _row_tile(m, 512)
    big = pl.BlockSpec((tm, kdim), lambda i: (i, 0))
    row = pl.BlockSpec((tm, d), lambda i: (i, 0))
    return pl.pallas_call(
        _sum_proj_body,
        out_shape=jax.ShapeDtypeStruct((m, d), F32),
        grid=(m // tm,),
        in_specs=[big, big, big, row, _const_spec(w.shape)],
        out_specs=row,
        compiler_params=_params("parallel"),
        name="nsa_out_proj",
    )(a, b, c, x, w)


def _nsa_weights(w_in, q_g, k_g, w_out):
    d = w_in.shape[0]
    nq = N_HEADS_B * HD_B
    w = N_KV_B * HD_B
    wq = w_in[:, :nq].reshape(d, N_HEADS_B, 1, HD_B)
    wq = jnp.broadcast_to(wq, (d, N_HEADS_B, 2, HD_B)).reshape(d, N_HEADS_B * LANE)
    wg = w_in[:, nq:nq + 3 * N_HEADS_B].reshape(d, N_KV_B, REP_B * 3)
    wg = jnp.pad(wg, ((0, 0), (0, 0), (0, LANE - REP_B * 3))).reshape(d, N_KV_B * LANE)
    wkv = w_in[:, nq + 3 * N_HEADS_B:]
    lane = jnp.arange(w)
    seg = (lane[:, None] // HD_B == lane[None, :] // HD_B).astype(BF16)
    spread = (lane[:, None] // HD_B * LANE + lane[:, None] % HD_B == jnp.arange(N_KV_B * LANE)[None, :]).astype(BF16)
    wo = jnp.pad(w_out.reshape(N_HEADS_B, HD_B, -1), ((0, 0), (0, LANE - HD_B), (0, 0)))
    return {
        "wq": wq.astype(BF16), "wg": wg.astype(BF16), "wkv": wkv.astype(BF16), "seg": seg, "spread": spread,
        "qg": jnp.tile(q_g, 2)[None, :], "kg": jnp.stack([jnp.tile(k_g[1], N_KV_B), jnp.tile(k_g[2], N_KV_B)]),
        "wo": wo.reshape(N_HEADS_B * LANE, -1).astype(BF16),
    }


def _cmp_weights(pe, w_c1, w_c2, kc_g):
    return (pe.reshape(2, 1, CMP_BLOCK * HD_B), w_c1.reshape(2, CMP_BLOCK * HD_B, HD_B).astype(BF16),
            w_c2.astype(BF16), kc_g[None, :])


def _nsa_prompt(x, g_mix, b, t, w_in, q_g, k_g, pe, w_c1, w_c2, w_out):
    wts = _nsa_weights(w_in, q_g, k_g, w_out)
    tm = 256
    pos = jnp.arange(t)
    qcat, gates, kv_rows, win_rows, ks_s, vs_s, kw_s, vw_s = _nsa_proj(
        x, g_mix, wts, _rope_tables(pos, LANE), _rope_tables(pos, HD_B), t // tm, tm)
    kc_blk, vc_blk = _cmp_prompt(kv_rows, b, t, *_cmp_weights(pe, w_c1, w_c2, k_g[0]))
    nb = t // CMP_BLOCK
    assert nb <= N_CMP_PAD and t % SEL_BLOCK == 0

    order = jnp.asarray(_cmp_block_order(), jnp.int32)

    def blocks(a, lo):
        a = jnp.pad(a, ((0, 0), (0, 0), (0, N_CMP_PAD - nb), (lo, LANE - HD_B - lo)))
        return a[:, :, order].astype(BF16)

    kct = blocks(kc_blk, HD_B)
    vct = blocks(vc_blk, 0).transpose(0, 1, 3, 2)
    o_cmp, mnot = _cmp_select(qcat, kct, vct, gates, b, t, min(CMP_SELECT_TILE, t))
    qmax = jnp.max(jnp.abs(q_g))
    o_sel = _flash(qcat, mnot, ks_s, vs_s, gates, qmax * jnp.max(jnp.abs(k_g[1])) * math.sqrt(HD_B), b, t,
                   *(min(n, t) for n in FLASH_SEL_TILES), True)
    o_win = _flash(qcat, mnot, kw_s, vw_s, gates, qmax * jnp.max(jnp.abs(k_g[2])) * math.sqrt(HD_B), b, t,
                   *(min(n, t) for n in FLASH_WIN_TILES), False)
    y = _sum_proj(o_cmp, o_sel, o_win, x, wts["wo"])
    wb = min(WINDOW, t)
    kv_out = kv_rows.reshape(b, t, N_KV_SLOTS, N_KV_B, HD_B)
    win_out = win_rows.reshape(b, t, 2, N_KV_B, HD_B)[:, t - wb:]
    return y, kv_out, win_out


def _gmlp_layer(x, g, w_ins, ln_g, ln_b, w_s, b_s, w_outs, layer, *, single):
    gw = w_outs.shape[1] // N_GROUPS_A
    if single:
        ws = jnp.repeat(w_s[:, 0, 0], gw)[None, :]
        bs = jnp.repeat(b_s[:, 0], gw)[None, :]
    else:
        ws = w_s
        bs = jnp.repeat(b_s.T, gw, axis=1)
    return _gmlp(x, g[None, :], w_ins, ln_g[None, :], ln_b[None, :], ws, bs, w_outs, layer, single=single)


def _mlstm_proj_body(x_ref, g_ref, wq_ref, wk_ref, wv_ref, wgi_ref, wo_ref, bif_ref,
                     q_ref, k_ref, v_ref, gi_ref, og_ref):
    xb = _rms(x_ref[...], g_ref[...]).astype(BF16)
    q_ref[...] = _dot(xb, wq_ref[...]).astype(BF16)
    k_ref[...] = _dot(xb, wk_ref[...]).astype(BF16)
    v_ref[...] = _dot(xb, wv_ref[...]).astype(BF16)
    gi_ref[...] = _dot(xb, wgi_ref[...]) + bif_ref[...]
    og_ref[...] = jax.nn.sigmoid(_dot(xb, wo_ref[...]))


def _mlstm_proj(x, g, wts):
    m, d = x.shape
    hv = N_HEADS_C * DV_C
    tm = _row_tile(m, 512)
    consts = [g, wts["wq"], wts["wk"], wts["wv"], wts["wgi"], wts["wo"], wts["bif"]]
    tile = lambda n: pl.BlockSpec((tm, n), lambda i: (i, 0))
    return pl.pallas_call(
        _mlstm_proj_body,
        out_shape=(jax.ShapeDtypeStruct((m, N_HEADS_C * LANE), BF16), jax.ShapeDtypeStruct((m, N_HEADS_C * LANE), BF16),
                   jax.ShapeDtypeStruct((m, hv), BF16), jax.ShapeDtypeStruct((m, LANE), F32),
                   jax.ShapeDtypeStruct((m, hv), F32)),
        grid=(m // tm,),
        in_specs=[tile(d)] + [_const_spec(c.shape) for c in consts],
        out_specs=(tile(N_HEADS_C * LANE), tile(N_HEADS_C * LANE), tile(hv), tile(LANE), tile(hv)),
        compiler_params=_params("parallel"),
        name="mlstm_proj",
    )(x, *consts)


def _mlstm_scan_body(q_ref, k_ref, v_ref, gi_ref, git_ref, hs_ref, c_out, n_out, m_out, c_s, n_s, m_s):
    c = pl.program_id(1)
    L = q_ref.shape[0]

    @pl.when(c == 0)
    def _():
        c_s[...] = jnp.zeros(c_s.shape, F32)
        n_s[...] = jnp.zeros(n_s.shape, F32)
        m_s[...] = jnp.zeros(m_s.shape, F32)

    row = lax.broadcasted_iota(jnp.int32, (L, L), 0)
    col = lax.broadcasted_iota(jnp.int32, (L, L), 1)
    causal = col <= row
    tril = causal.astype(BF16)
    gi = gi_ref[...]
    git = git_ref[...]
    fcol = jax.nn.log_sigmoid(gi)
    frow = jax.nn.log_sigmoid(git[N_HEADS_C:, :])
    bcol_all = sum(_dot(tril, part) for part in _split3(fcol))
    brow_all = sum(_dot_nt(part, tril) for part in _split3(frow))
    heads = range(N_HEADS_C)
    sl = [slice(h * LANE, (h + 1) * LANE) for h in heads]
    q = [q_ref[:, sl[h]] for h in heads]
    k = [k_ref[:, sl[h]] for h in heads]
    v = [v_ref[:, sl[h]] for h in heads]
    qk = [_dot_nt(q[h], k[h]) for h in heads]
    cq = [_dot_nt(q[h], c_s[h].astype(BF16)) for h in heads]
    bcol = [bcol_all[:, N_HEADS_C + h:N_HEADS_C + h + 1] for h in heads]
    m_prev = [m_s[h:h + 1, 0:1] for h in heads]
    s, a, m_t = [], [], []
    for h in heads:
        dlog = jnp.where(causal, bcol[h] - brow_all[h:h + 1, :] + git[h:h + 1, :], -jnp.inf)
        inter = bcol[h] + m_prev[h]
        m_t.append(jnp.maximum(inter, jnp.max(dlog, axis=1, keepdims=True)))
        s.append(qk[h] * jnp.exp(dlog - m_t[h]))
        a.append(jnp.exp(inter - m_t[h]))
    sv = [_dot(s[h].astype(BF16), v[h]) for h in heads]
    wk, decay, m_new = [], [], []
    for h in heads:
        nq = jnp.sum(q[h].astype(F32) * n_s[h:h + 1, :], axis=1, keepdims=True)
        den = a[h] * nq + jnp.sum(s[h], axis=1, keepdims=True)
        hs_ref[:, sl[h]] = (a[h] * cq[h] + sv[h]) / jnp.maximum(jnp.abs(den), jnp.exp(-m_t[h]))
        b_end = bcol[h][L - 1:L, :]
        wlog = b_end - bcol[h] + gi[:, h:h + 1]
        m_new.append(jnp.maximum(b_end + m_prev[h], jnp.max(wlog, axis=0, keepdims=True)))
        wk.append(jnp.exp(wlog - m_new[h]))
        decay.append(jnp.exp(b_end + m_prev[h] - m_new[h]))
    upd = [_dot_tn((v[h].astype(F32) * wk[h]).astype(BF16), k[h]) for h in heads]
    for h in heads:
        c_s[h] = decay[h] * c_s[h] + upd[h]
        n_s[h:h + 1, :] = decay[h] * n_s[h:h + 1, :] + jnp.sum(k[h].astype(F32) * wk[h], axis=0, keepdims=True)
        m_s[h:h + 1, :] = jnp.broadcast_to(m_new[h], (1, LANE))

    @pl.when(c == pl.num_programs(1) - 1)
    def _():
        c_out[...] = c_s[...]
        n_out[...] = n_s[...]
        m_out[...] = m_s[...]


def _mlstm_scan(q, k, v, gi, git, b, t):
    L = math.gcd(t, CHUNK_C)
    nc = t // L
    hv = N_HEADS_C * DV_C
    tile = lambda n: pl.BlockSpec((L, n), lambda bi, c: (bi * nc + c, 0))
    return pl.pallas_call(
        _mlstm_scan_body,
        out_shape=(jax.ShapeDtypeStruct((b * t, hv), F32),
                   jax.ShapeDtypeStruct((b, N_HEADS_C, DV_C, LANE), F32),
                   jax.ShapeDtypeStruct((b, N_HEADS_C, LANE), F32), jax.ShapeDtypeStruct((b, N_HEADS_C, LANE), F32)),
        grid=(b, nc),
        in_specs=[tile(N_HEADS_C * LANE), tile(N_HEADS_C * LANE), tile(hv), tile(LANE),
                  pl.BlockSpec((None, 2 * N_HEADS_C, L), lambda bi, c: (bi, 0, c))],
        out_specs=(tile(hv), pl.BlockSpec((None, N_HEADS_C, DV_C, LANE), lambda bi, c: (bi, 0, 0, 0)),
                   pl.BlockSpec((None, N_HEADS_C, LANE), lambda bi, c: (bi, 0, 0)),
                   pl.BlockSpec((None, N_HEADS_C, LANE), lambda bi, c: (bi, 0, 0))),
        scratch_shapes=[pltpu.VMEM((N_HEADS_C, DV_C, LANE), F32), pltpu.VMEM((N_HEADS_C, LANE), F32),
                        pltpu.VMEM((N_HEADS_C, LANE), F32)],
        compiler_params=_params("parallel", "arbitrary"),
        name="mlstm_scan",
    )(q, k, v, gi, git)


def _mlstm_out_body(hs_ref, og_ref, hg_ref, x_ref, w_ref, o_ref):
    parts = []
    for h in range(N_HEADS_C):
        sl = slice(h * DV_C, (h + 1) * DV_C)
        parts.append((og_ref[:, sl] * _rms(hs_ref[:, sl], hg_ref[:, sl])).astype(BF16))
    o_ref[...] = x_ref[...] + _dot(jnp.concatenate(parts, axis=1), w_ref[...])


def _mlstm_out(hs, og, hg, x, w):
    m, d = x.shape
    hv = hs.shape[1]
    tm = _row_tile(m, 512)
    wide = pl.BlockSpec((tm, hv), lambda i: (i, 0))
    row = pl.BlockSpec((tm, d), lambda i: (i, 0))
    return pl.pallas_call(
        _mlstm_out_body,
        out_shape=jax.ShapeDtypeStruct((m, d), F32),
        grid=(m // tm,),
        in_specs=[wide, wide, _const_spec(hg.shape), row, _const_spec(w.shape)],
        out_specs=row,
        compiler_params=_params("parallel"),
        name="mlstm_out",
    )(hs, og, hg, x, w)


def _mlstm_weights(w_in, b_if):
    d = w_in.shape[0]
    hk, hv = N_HEADS_C * DK_C, N_HEADS_C * DV_C

    def spread(w):
        w = w.reshape(d, N_HEADS_C, DK_C)
        return jnp.pad(w, ((0, 0), (0, 0), (0, LANE - DK_C))).reshape(d, N_HEADS_C * LANE)

    wgi = jnp.pad(w_in[:, 2 * hk + hv:2 * hk + hv + 2 * N_HEADS_C], ((0, 0), (0, LANE - 2 * N_HEADS_C)))
    return {
        "wq": spread(w_in[:, :hk]).astype(BF16),
        "wk": (spread(w_in[:, hk:2 * hk]) * (DK_C ** -0.5)).astype(BF16),
        "wv": w_in[:, 2 * hk:2 * hk + hv].astype(BF16),
        "wgi": wgi.astype(BF16),
        "wo": w_in[:, 2 * hk + hv + 2 * N_HEADS_C:].astype(BF16),
        "bif": jnp.pad(b_if, (0, LANE - 2 * N_HEADS_C))[None, :],
    }


def _mlstm_prompt(x, g_mix, b, t, w_in, b_if, h_g, w_out):
    wts = _mlstm_weights(w_in, b_if)
    q, k, v, gi, og = _mlstm_proj(x, g_mix, wts)
    git = gi[:, :2 * N_HEADS_C].reshape(b, t, 2 * N_HEADS_C).transpose(0, 2, 1)
    hs, c, n, m = _mlstm_scan(q, k, v, gi, git, b, t)
    y = _mlstm_out(hs, og, h_g[None, :], x, w_out.astype(BF16))
    return y, c[..., :DK_C], n[..., :DK_C], m[..., 0]


def _proj_add_body(o_ref, x_ref, w_ref, out_ref):
    out_ref[...] = x_ref[...] + _dot(o_ref[...].astype(BF16), w_ref[...])


def _proj_add(o, x, w):
    m, d = x.shape
    tm = _row_tile(m, 512)
    return pl.pallas_call(
        _proj_add_body,
        out_shape=jax.ShapeDtypeStruct((m, d), F32),
        grid=(m // tm,),
        in_specs=[pl.BlockSpec((tm, o.shape[1]), lambda i: (i, 0)), pl.BlockSpec((tm, d), lambda i: (i, 0)),
                  _const_spec(w.shape)],
        out_specs=pl.BlockSpec((tm, d), lambda i: (i, 0)),
        compiler_params=_params("parallel"),
        name="proj_add",
    )(o, x, w)


def _nsa_step_body(pt_ref, *refs, n_pages, page, past_len):
    pages = refs[:n_pages]
    (win_ref, qr_ref, qn_ref, gates_ref, nkv_ref, nwin_ref, pe_ref, w1_ref, w2_ref, kcg_ref,
     o_ref, c_s, x_s) = refs[n_pages:]
    w = N_KV_B * HD_B
    length = n_pages * page
    nb = length // CMP_BLOCK
    t = past_len
    for p in range(n_pages):
        for c in range(CMP_PLANES):
            c_s[c, p * page:(p + 1) * page, :] = pages[p][c * LANE:(c + 1) * LANE, :].T
    _flatten_cmp_blocks(lambda c, start: c_s[c, pl.ds(start, 8, stride=CMP_BLOCK), :], x_s, nb)
    lane_w = lax.broadcasted_iota(jnp.int32, (HD_B, w), 1)
    row_w = lax.broadcasted_iota(jnp.int32, (HD_B, w), 0)
    cmp_nat = []
    for slot in range(2):
        y = _compress_slot(x_s, slot, pe_ref, w1_ref, w2_ref, kcg_ref)
        nat = jnp.zeros((nb, w), F32)
        for g in range(N_KV_B):
            place = (lane_w == row_w + g * HD_B).astype(BF16)
            nat = nat + _dot(y[g * nb:(g + 1) * nb].astype(BF16), place)
        cmp_nat.append(nat.astype(BF16))
    kc, vc = cmp_nat
    qr = qr_ref[...]
    qn = qn_ref[...]
    gates = gates_ref[...]
    nh = N_HEADS_B
    blk = lax.broadcasted_iota(jnp.int32, (nh, nb), 1)
    p_c = _masked_softmax(_dot_nt(qn, kc), (blk + 1) * CMP_BLOCK - 1 <= t, 1)
    o_c = _dot(p_c.astype(BF16), vc)
    blk_t = lax.broadcasted_iota(jnp.int32, (nb, nh), 0)
    p_t = _masked_softmax(_dot_nt(kc, qn), (blk_t + 1) * CMP_BLOCK - 1 <= t, 0)
    gsum = (lax.broadcasted_iota(jnp.int32, (nh, LANE), 0) // REP_B
            == lax.broadcasted_iota(jnp.int32, (nh, LANE), 1)).astype(BF16)
    pair = (lax.broadcasted_iota(jnp.int32, (nb, nb), 1) // (SEL_BLOCK // CMP_BLOCK)
            == lax.broadcasted_iota(jnp.int32, (nb, nb), 0)).astype(BF16)
    imp = sum(_dot(part, gsum) for part in _split3(p_t))
    imp = sum(_dot(pair, part) for part in _split3(imp))
    sblk = lax.broadcasted_iota(jnp.int32, (nb, LANE), 0)
    cur = t // SEL_BLOCK
    forced = (sblk == 0) | (sblk == cur) | (sblk == cur - 1)
    score = jnp.where(forced, jnp.inf, jnp.where(sblk * SEL_BLOCK <= t, imp, -jnp.inf))
    sblk_f = sblk.astype(F32)
    pickable = score > -jnp.inf
    for _ in range(N_SEL):
        mx = jnp.max(score, axis=0, keepdims=True)
        first = jnp.min(jnp.where(score == mx, sblk_f, float(nb)), axis=0, keepdims=True)
        score = jnp.where(sblk_f == first, -jnp.inf, score)
    notsel = jnp.where(pickable, jnp.where(score > -jnp.inf, -MASK_BIG, 0.0), -MASK_BIG)
    bias = _dot_nt(gsum, notsel.astype(BF16)).astype(BF16)
    expand = (lax.broadcasted_iota(jnp.int32, (nb, length), 1) // SEL_BLOCK
              == lax.broadcasted_iota(jnp.int32, (nb, length), 0)).astype(BF16)
    nkv = nkv_ref[...]
    nwin = nwin_ref[...]

    def attend(s, k_new, v_new, weighted_values):
        s_new = jnp.sum(qr.astype(F32) * k_new.astype(BF16).astype(F32), axis=1, keepdims=True)
        m = jnp.maximum(jnp.max(s, axis=1, keepdims=True), s_new)
        e = jnp.exp(s - m)
        e_new = jnp.exp(s_new - m)
        den = jnp.sum(e, axis=1, keepdims=True) + e_new
        num = weighted_values(e.astype(BF16)) + e_new.astype(BF16).astype(F32) * v_new.astype(BF16).astype(F32)
        return num / den

    s_sel = jnp.concatenate([_dot(qr, pages[p][2 * w:3 * w, :].astype(BF16)) for p in range(n_pages)], axis=1)
    o_s = attend(s_sel + _dot(bias, expand), nkv[:, 2 * w:3 * w], nkv[:, 3 * w:],
                 lambda e: sum(_dot_nt(e[:, p * page:(p + 1) * page], pages[p][3 * w:, :].astype(BF16))
                               for p in range(n_pages)))
    wb = win_ref.shape[1]
    pos_w = t - wb + lax.broadcasted_iota(jnp.int32, (nh, wb), 1)
    ok_w = (pos_w >= 0) & (t - pos_w <= WINDOW)
    s_w = jnp.where(ok_w, _dot(qr, win_ref[:w, :].astype(BF16)), -MASK_BIG)
    o_w = attend(s_w, nwin[:, :w], nwin[:, w:], lambda e: _dot_nt(e, win_ref[w:, :].astype(BF16)))
    o_ref[...] = gates[:, 0:1] * o_c + gates[:, 1:2] * o_s + gates[:, 2:3] * o_w


def _nsa_step(page_table, cache, win_cache, qr, qn, gates, new_kv, new_win, pe, w1, w2, kc_g, past_len):
    bsz, n_pages = page_table.shape
    page = cache.shape[2]
    assert page == LANE
    w = N_KV_B * HD_B
    nb = n_pages * page // CMP_BLOCK
    wb = win_cache.shape[2]
    per = lambda shape: pl.BlockSpec((None,) + shape, lambda b, pt: (b,) + (0,) * len(shape))
    const = lambda a: pl.BlockSpec(a.shape, lambda b, pt: (0,) * a.ndim)
    page_specs = [pl.BlockSpec((None, 4 * w, page), lambda b, pt, p=p: (pt[b, p], 0, 0)) for p in range(n_pages)]
    return pl.pallas_call(
        functools.partial(_nsa_step_body, n_pages=n_pages, page=page, past_len=past_len),
        out_shape=jax.ShapeDtypeStruct((bsz, N_HEADS_B, w), F32),
        grid_spec=pltpu.PrefetchScalarGridSpec(
            num_scalar_prefetch=1,
            grid=(bsz,),
            in_specs=page_specs + [per((2 * w, wb)), per((N_HEADS_B, w)), per((N_HEADS_B, w)), per((N_HEADS_B, LANE)),
                                   per((1, 4 * w)), per((1, 2 * w)), const(pe), const(w1), const(w2), const(kc_g)],
            out_specs=per((N_HEADS_B, w)),
            scratch_shapes=[pltpu.VMEM((2 * w // LANE, n_pages * page, LANE), F32),
                            pltpu.VMEM((2 * N_KV_B, nb, CMP_BLOCK * HD_B), F32)]),
        compiler_params=_params("parallel"),
        name="nsa_step",
    )(page_table, *([cache] * n_pages), win_cache, qr, qn, gates, new_kv, new_win, pe, w1, w2, kc_g)


def _nsa_sample_step(x, g_mix, past_len, kv_cache, win_cache, page_table, w_in, q_g, k_g, pe, w_c1, w_c2, w_out):
    bsz, d = x.shape
    w = N_KV_B * HD_B
    assert past_len % CMP_BLOCK == 0 and past_len // SEL_BLOCK + 1 <= past_len // CMP_BLOCK
    wts = _nsa_weights(w_in, q_g, k_g, w_out)
    pos = jnp.full((bsz,), past_len, jnp.int32)
    qcat, gates, kv_rows, win_rows, _, _, _, _ = _nsa_proj(
        x, g_mix, wts, _rope_tables(pos, LANE), _rope_tables(pos, HD_B), 1, bsz)
    q5 = qcat.reshape(bsz, N_KV_B, REP_B, 2, HD_B)
    eye = jnp.eye(N_KV_B, dtype=BF16)
    qrows = (q5[:, :, :, :, None, :] * eye[None, :, None, None, :, None])
    qr = qrows[:, :, :, 0].reshape(bsz, N_HEADS_B, w)
    qn = qrows[:, :, :, 1].reshape(bsz, N_HEADS_B, w)
    gts = gates.reshape(bsz, N_KV_B, LANE)[:, :, :REP_B * 3].reshape(bsz, N_HEADS_B, 3)
    gts = jnp.pad(gts, ((0, 0), (0, 0), (0, LANE - 3)))
    pool, page = kv_cache.shape[:2]
    cache_t = kv_cache.reshape(pool, page, 4 * w).transpose(0, 2, 1)
    win_t = win_cache.reshape(bsz, -1, 2 * w).transpose(0, 2, 1)
    o = _nsa_step(page_table, cache_t, win_t, qr, qn, gts,
                  kv_rows.reshape(bsz, 1, 4 * w), win_rows.reshape(bsz, 1, 2 * w),
                  *_cmp_weights(pe, w_c1, w_c2, k_g[0]), past_len)
    own = (jnp.arange(N_HEADS_B)[:, None] // REP_B == jnp.arange(N_KV_B)[None, :]).astype(F32)
    w_exp = own[:, :, None, None] * w_out.reshape(N_HEADS_B, 1, HD_B, d)
    y = _proj_add(o.reshape(bsz, N_HEADS_B * w), x, w_exp.reshape(N_HEADS_B * w, d).astype(BF16))
    return y, kv_rows.reshape(bsz, 1, N_KV_SLOTS, N_KV_B, HD_B), win_rows.reshape(bsz, 1, 2, N_KV_B, HD_B)


def _mlstm_step_body(q_ref, k_ref, qt_ref, kt_ref, v_ref, gi_ref, ct_ref, n_ref, m_ref,
                     h_ref, cto_ref, no_ref, mo_ref, *, sb):
    gi = gi_ref[...]
    logf = jax.nn.log_sigmoid(gi)
    m_all = m_ref[...]
    lane_m = lax.broadcasted_iota(jnp.int32, (1, N_HEADS_C), 1)
    for s in range(sb):
        m_new_row = jnp.zeros((1, N_HEADS_C), F32)
        for h in range(N_HEADS_C):
            q = q_ref[s:s + 1, h * LANE:h * LANE + DK_C].astype(F32)
            k = k_ref[s:s + 1, h * LANE:h * LANE + DK_C].astype(F32)
            qc = qt_ref[s, :, h:h + 1]
            kc = kt_ref[s, :, h:h + 1]
            v = v_ref[s:s + 1, h * DV_C:(h + 1) * DV_C].astype(F32)
            ct = ct_ref[s, h]
            n = n_ref[s, h:h + 1, :]
            it = gi[s:s + 1, h:h + 1]
            b = logf[s:s + 1, N_HEADS_C + h:N_HEADS_C + h + 1]
            m0 = m_all[s:s + 1, h:h + 1]
            inter = b + m0
            m_t = jnp.maximum(inter, it)
            wgt = jnp.exp(it - m_t)
            a = jnp.exp(inter - m_t)
            sc = jnp.sum(q * k, axis=1, keepdims=True) * wgt
            num = a * jnp.sum(ct * qc, axis=0, keepdims=True) + sc * v
            den = a * jnp.sum(n * q, axis=1, keepdims=True) + sc
            h_ref[s:s + 1, h * DV_C:(h + 1) * DV_C] = num / jnp.maximum(jnp.abs(den), jnp.exp(-m_t))
            cto_ref[s, h] = a * ct + (wgt * kc) * v
            no_ref[s, h:h + 1, :] = a * n + wgt * k
            m_new_row = jnp.where(lane_m == h, m_t, m_new_row)
        mo_ref[s:s + 1, :] = m_new_row


def _mlstm_step(q, k, qt, kt, v, gi, ct0, n0, m0):
    bsz = q.shape[0]
    sb = 8
    row = lambda n: pl.BlockSpec((sb, n), lambda i: (i, 0))
    c_spec = pl.BlockSpec((sb, N_HEADS_C, DK_C, DV_C), lambda i: (i, 0, 0, 0))
    n_spec = pl.BlockSpec((sb, N_HEADS_C, DK_C), lambda i: (i, 0, 0))
    col_spec = pl.BlockSpec((sb, DK_C, N_HEADS_C), lambda i: (i, 0, 0))
    hv = N_HEADS_C * DV_C
    return pl.pallas_call(
        functools.partial(_mlstm_step_body, sb=sb),
        out_shape=(jax.ShapeDtypeStruct((bsz, hv), F32), jax.ShapeDtypeStruct(ct0.shape, F32),
                   jax.ShapeDtypeStruct(n0.shape, F32), jax.ShapeDtypeStruct(m0.shape, F32)),
        grid=(bsz // sb,),
        in_specs=[row(N_HEADS_C * LANE), row(N_HEADS_C * LANE), col_spec, col_spec, row(hv), row(LANE),
                  c_spec, n_spec, row(N_HEADS_C)],
        out_specs=(row(hv), c_spec, n_spec, row(N_HEADS_C)),
        compiler_params=_params("parallel"),
        name="mlstm_step",
    )(q, k, qt, kt, v, gi, ct0, n0, m0)


def _mlstm_sample_step(x, g_mix, c0, n0, m0, w_in, b_if, h_g, w_out):
    bsz = x.shape[0]
    wts = _mlstm_weights(w_in, b_if)
    q, k, v, gi, og = _mlstm_proj(x, g_mix, wts)
    cols = lambda a: a.astype(F32).reshape(bsz, N_HEADS_C, LANE)[:, :, :DK_C].transpose(0, 2, 1)
    hs, ct, n, m = _mlstm_step(q, k, cols(q), cols(k), v, gi, c0.transpose(0, 1, 3, 2), n0, m0)
    y = _mlstm_out(hs, og, h_g[None, :], x, w_out.astype(BF16))
    return y, ct.transpose(0, 1, 3, 2), n, m


def kernel(x_prompt, x_sample, cache_nsa_kv, cache_nsa_win, state_mlstm_C, state_mlstm_n, state_mlstm_m, page_table,
           norm_mix_g, norm_ffn_g, ffn_w1, ffn_w2, a_w_in, a_ln_g, a_ln_b, a_w_s, a_b_s, a_w_out,
           b_w_in, b_q_g, b_k_g, b_pe, b_w_c1, b_w_c2, b_w_out, c_w_in, c_b_if, c_h_g, c_w_out):
    bp, t, d = x_prompt.shape
    bs, ts, _ = x_sample.shape
    assert ts == 1
    past_len = page_table.shape[1] * cache_nsa_kv.shape[2]
    xp = x_prompt.reshape(bp * t, d)
    xs = x_sample.reshape(bs, d)
    out = {k: [] for k in ("v_s", "kv_p", "win_p", "kv_s", "win_s", "C_p", "n_p", "m_p", "C_s", "n_s", "m_s")}
    ffn_w1_b, ffn_w2_b = ffn_w1.astype(BF16), ffn_w2.astype(BF16)
    a_w_in_b, a_w_out_b = a_w_in.astype(BF16), a_w_out.astype(BF16)
    for layer in range(norm_mix_g.shape[0]):
        kind, j = layer % 3, layer // 3
        gm = norm_mix_g[layer]
        if kind == 0:
            args = (a_w_in_b, a_ln_g[j], a_ln_b[j], a_w_s[j], a_b_s[j], a_w_out_b, j)
            xp = _gmlp_layer(xp, gm, *args, single=False)[0]
            xs, v = _gmlp_layer(xs, gm, *args, single=True)
            out["v_s"].append(v.reshape(bs, ts, -1))
        elif kind == 1:
            args = (b_w_in[j], b_q_g[j], b_k_g[j], b_pe[j], b_w_c1[j], b_w_c2[j], b_w_out[j])
            xp, kv, win = _nsa_prompt(xp, gm[None, :], bp, t, *args)
            out["kv_p"].append(kv)
            out["win_p"].append(win)
            xs, kv, win = _nsa_sample_step(xs, gm[None, :], past_len, cache_nsa_kv[j], cache_nsa_win[j], page_table,
                                           *args)
            out["kv_s"].append(kv)
            out["win_s"].append(win)
        else:
            args = (c_w_in[j], c_b_if[j], c_h_g[j], c_w_out[j])
            xp, c, n, m = _mlstm_prompt(xp, gm[None, :], bp, t, *args)
            out["C_p"].append(c)
            out["n_p"].append(n)
            out["m_p"].append(m)
            xs, c, n, m = _mlstm_sample_step(xs, gm[None, :], state_mlstm_C[j], state_mlstm_n[j], state_mlstm_m[j],
                                             *args)
            out["C_s"].append(c)
            out["n_s"].append(n)
            out["m_s"].append(m)
        gf = norm_ffn_g[layer][None, :]
        xp = _ffn(xp, gf, ffn_w1_b, ffn_w2_b, layer)
        xs = _ffn(xs, gf, ffn_w1_b, ffn_w2_b, layer)
    st = {k: jnp.stack(v) for k, v in out.items()}
    return (xp.reshape(bp, t, d), xs.reshape(bs, ts, d), st["v_s"], st["kv_p"], st["win_p"], st["kv_s"], st["win_s"],
            st["C_p"], st["n_p"], st["m_p"], st["C_s"], st["n_s"], st["m_s"])
```

```python
import functools
import math

import jax
import jax.numpy as jnp
from jax import lax
from jax.experimental import pallas as pl
from jax.experimental.pallas import tpu as pltpu

F32 = jnp.float32
BF16 = jnp.bfloat16

EPS = 1e-6
CHUNK_A = 128
N_GROUPS_A = 8
N_HEADS_B = 16
N_KV_B = 4
REP_B = N_HEADS_B // N_KV_B
HD_B = 64
ROT_DIM = 16
ROPE_THETA = 500000.0
CMP_BLOCK = 32
SEL_BLOCK = 64
N_SEL = 16
WINDOW = 512
N_KV_SLOTS = 4
N_HEADS_C = 8
DK_C = 64
DV_C = 128
CHUNK_C = 128
SCALE_B = HD_B ** -0.5

LANE = 128
VMEM_LIMIT_BYTES = 56 * 1024 * 1024
MASK_BIG = 1e30
M_INIT = -1e20
N_SEL_PAD = 128
N_CMP_PAD = 2 * N_SEL_PAD
SHIFT_MAX = 40.0
FLASH_SEL_TILES = (1024, 512)
WINDOW_TILE = 256
CMP_SELECT_TILE = 512


def _params(*sem):
    return pltpu.CompilerParams(dimension_semantics=sem, vmem_limit_bytes=VMEM_LIMIT_BYTES)


def _dot(a, b):
    return jnp.dot(a, b, preferred_element_type=F32)


def _dot_nt(a, b):
    return lax.dot_general(a, b, (((1,), (1,)), ((), ())), preferred_element_type=F32)


def _dot_tn(a, b):
    return lax.dot_general(a, b, (((0,), (0,)), ((), ())), preferred_element_type=F32)


def _rms(x, g):
    return x * lax.rsqrt(jnp.mean(x * x, axis=-1, keepdims=True) + EPS) * g


def _split3(x):
    a = x.astype(BF16)
    r = x - a.astype(F32)
    b = r.astype(BF16)
    c = (r - b.astype(F32)).astype(BF16)
    return a, b, c


def _const_spec(shape):
    n = len(shape)
    return pl.BlockSpec(shape, lambda *_: (0,) * n)


ROW_TILE = 512


def _row_tile(m, pref=ROW_TILE):
    t = min(pref, m)
    while m % t:
        t //= 2
    return t


def _ffn_body(x_ref, g_ref, w1_ref, w2_ref, o_ref, *, ck):
    x = x_ref[...]
    xb = _rms(x, g_ref[...]).astype(BF16)
    acc = x
    for j in range(w1_ref.shape[1] // ck):
        h = jnp.maximum(_dot(xb, w1_ref[:, j * ck:(j + 1) * ck]), 0.0)
        acc = acc + _dot((h * h).astype(BF16), w2_ref[j * ck:(j + 1) * ck, :])
    o_ref[...] = acc


def _layer_spec(stacked, layer):
    n = stacked.ndim - 1
    return pl.BlockSpec((None,) + stacked.shape[1:], lambda *_: (layer,) + (0,) * n)


def _ffn(x, g, w1s, w2s, layer):
    m, d = x.shape
    tm = _row_tile(m)
    return pl.pallas_call(
        functools.partial(_ffn_body, ck=1024),
        out_shape=jax.ShapeDtypeStruct((m, d), F32),
        grid=(m // tm,),
        in_specs=[pl.BlockSpec((tm, d), lambda i: (i, 0)), _const_spec(g.shape),
                  _layer_spec(w1s, layer), _layer_spec(w2s, layer)],
        out_specs=pl.BlockSpec((tm, d), lambda i: (i, 0)),
        compiler_params=_params("parallel"),
        name="ffn",
    )(x, g, w1s, w2s)


def _gmlp_body(x_ref, g_ref, win_ref, lng_ref, lnb_ref, ws_ref, bs_ref, wout_ref, o_ref, *maybe_v_ref, single):
    x = x_ref[...]
    dg = lng_ref.shape[1]
    xb = _rms(x, g_ref[...]).astype(BF16)
    u = jax.nn.gelu(_dot(xb, win_ref[:, :dg]))
    v = jax.nn.gelu(_dot(xb, win_ref[:, dg:]))
    mu = jnp.mean(v, axis=-1, keepdims=True)
    vc = v - mu
    var = jnp.mean(vc * vc, axis=-1, keepdims=True)
    v = vc * lax.rsqrt(var + EPS) * lng_ref[...] + lnb_ref[...]
    if single:
        maybe_v_ref[0][...] = v
        gate = v * ws_ref[...] + bs_ref[...]
    else:
        gw = dg // N_GROUPS_A
        row = lax.broadcasted_iota(jnp.int32, (CHUNK_A, CHUNK_A), 0)
        col = lax.broadcasted_iota(jnp.int32, (CHUNK_A, CHUNK_A), 1)
        causal = col <= row
        vb = v.astype(BF16)
        chunks = []
        for c in range(x.shape[0] // CHUNK_A):
            parts = []
            for gi in range(N_GROUPS_A):
                w = jnp.where(causal, ws_ref[gi], 0.0).astype(BF16)
                parts.append(_dot(w, vb[c * CHUNK_A:(c + 1) * CHUNK_A, gi * gw:(gi + 1) * gw]))
            chunks.append(jnp.concatenate(parts, axis=1) + bs_ref[...])
        gate = jnp.concatenate(chunks, axis=0)
    o_ref[...] = x + _dot((u * gate).astype(BF16), wout_ref[...])


def _gmlp(x, g, w_ins, ln_g, ln_b, ws, bs, w_outs, layer, *, single):
    m, d = x.shape
    dg = w_outs.shape[1]
    tm = _row_tile(m)
    n_out = 2 if single else 1
    outs = pl.pallas_call(
        functools.partial(_gmlp_body, single=single),
        out_shape=(jax.ShapeDtypeStruct((m, d), F32), jax.ShapeDtypeStruct((m, dg), F32))[:n_out],
        grid=(m // tm,),
        in_specs=[pl.BlockSpec((tm, d), lambda i: (i, 0)), _const_spec(g.shape), _layer_spec(w_ins, layer),
                  _const_spec(ln_g.shape), _const_spec(ln_b.shape), _const_spec(ws.shape), _const_spec(bs.shape),
                  _layer_spec(w_outs, layer)],
        out_specs=(pl.BlockSpec((tm, d), lambda i: (i, 0)), pl.BlockSpec((tm, dg), lambda i: (i, 0)))[:n_out],
        compiler_params=_params("parallel"),
        name="gmlp_single" if single else "gmlp",
    )(x, g, w_ins, ln_g, ln_b, ws, bs, w_outs)
    return outs if single else (outs[0], None)


def _rope_tables(pos, seg):
    half = ROT_DIM // 2
    freq = jnp.power(ROPE_THETA, -jnp.arange(half, dtype=F32) * 2.0 / ROT_DIM)
    ang = pos.astype(F32)[:, None] * freq[None, :]
    cos, sin = jnp.cos(ang), jnp.sin(ang)
    t = pos.shape[0]
    one = jnp.ones((t, seg - ROT_DIM), F32)
    zero = jnp.zeros((t, seg - ROT_DIM), F32)
    z8 = jnp.zeros((t, half), F32)
    tabs = [jnp.concatenate([cos, cos, one], 1), jnp.concatenate([-sin, z8, zero], 1),
            jnp.concatenate([z8, sin, zero], 1)]
    return jnp.stack([jnp.tile(a, (1, LANE // seg)) for a in tabs])


def _rope128(x, tab):
    return x * tab[0] + pltpu.roll(x, LANE - ROT_DIM // 2, 1) * tab[1] + pltpu.roll(x, ROT_DIM // 2, 1) * tab[2]


def _nsa_proj_body(x_ref, g_ref, wq_ref, wg_ref, wkv_ref, seg_ref, spread_ref, qg_ref, kg_ref, tq_ref, tk_ref,
                   qcat_ref, gates_ref, kv_ref, win_ref, ks_ref, vs_ref, kw_ref, vw_ref):
    x = x_ref[...]
    xb = _rms(x, g_ref[...]).astype(BF16)
    tq = tq_ref[...]
    tk = tk_ref[...]
    qg = qg_ref[...]
    for h in range(N_HEADS_B):
        q = _dot(xb, wq_ref[:, h * LANE:(h + 1) * LANE])
        ms = jnp.sum(q * q, axis=-1, keepdims=True) * (1.0 / LANE)
        qn = q * lax.rsqrt(ms + EPS) * qg
        qcat_ref[:, h * LANE:(h + 1) * LANE] = (_rope128(qn, tq) * SCALE_B).astype(BF16)
    gates_ref[...] = jax.nn.sigmoid(_dot(xb, wg_ref[...]))
    kv = _dot(xb, wkv_ref[...])
    w = N_KV_B * HD_B
    seg = seg_ref[...]
    spread = spread_ref[...]

    def head_norm(k, gain):
        k2 = k * k
        hi = k2.astype(BF16)
        lo = (k2 - hi.astype(F32)).astype(BF16)
        ss = _dot(hi, seg) + _dot(lo, seg)
        return k * lax.rsqrt(ss * (1.0 / HD_B) + EPS) * gain

    def rope(k):
        return jnp.concatenate([_rope128(k[:, j * LANE:(j + 1) * LANE], tk) for j in range(w // LANE)], axis=1)

    ks = rope(head_norm(kv[:, 2 * w:3 * w], kg_ref[0:1, :]))
    kw = rope(head_norm(kv[:, 4 * w:5 * w], kg_ref[1:2, :]))
    vs = kv[:, 3 * w:4 * w]
    vw = kv[:, 5 * w:6 * w]
    kv_ref[:, :2 * w] = kv[:, :2 * w]
    kv_ref[:, 2 * w:3 * w] = ks
    kv_ref[:, 3 * w:] = vs
    win_ref[:, :w] = kw
    win_ref[:, w:] = vw
    lane = lax.broadcasted_iota(jnp.int32, (1, N_KV_B * LANE), 1)
    ones_hi = ((lane & HD_B) != 0).astype(F32)
    ks_ref[...] = _dot(ks.astype(BF16), spread).astype(BF16)
    kw_ref[...] = _dot(kw.astype(BF16), spread).astype(BF16)
    vs_ref[...] = (_dot(vs.astype(BF16), spread) + ones_hi).astype(BF16)
    vw_ref[...] = (_dot(vw.astype(BF16), spread) + ones_hi).astype(BF16)


def _nsa_proj(x, g, wts, tab_q, tab_k, n_tab_tiles, tm):
    m, d = x.shape
    w = N_KV_B * HD_B
    ws = N_KV_B * LANE
    tile = lambda n: pl.BlockSpec((tm, n), lambda i: (i, 0))
    tab = pl.BlockSpec((3, tm, LANE), lambda i: (0, i % n_tab_tiles, 0))
    consts = [g, wts["wq"], wts["wg"], wts["wkv"], wts["seg"], wts["spread"], wts["qg"], wts["kg"]]
    return pl.pallas_call(
        _nsa_proj_body,
        out_shape=(jax.ShapeDtypeStruct((m, N_HEADS_B * LANE), BF16), jax.ShapeDtypeStruct((m, ws), F32),
                   jax.ShapeDtypeStruct((m, 4 * w), F32), jax.ShapeDtypeStruct((m, 2 * w), F32),
                   jax.ShapeDtypeStruct((m, ws), BF16), jax.ShapeDtypeStruct((m, ws), BF16),
                   jax.ShapeDtypeStruct((m, ws), BF16), jax.ShapeDtypeStruct((m, ws), BF16)),
        grid=(m // tm,),
        in_specs=[tile(d)] + [_const_spec(c.shape) for c in consts] + [tab, tab],
        out_specs=(tile(N_HEADS_B * LANE), tile(ws), tile(4 * w), tile(2 * w), tile(ws), tile(ws), tile(ws), tile(ws)),
        compiler_params=_params("parallel"),
        name="nsa_proj",
    )(x, *consts, tab_q, tab_k)


CMP_PLANES = 2 * N_KV_B * HD_B // LANE


def _flatten_cmp_blocks(load_rows, x_s, n_blocks):
    for ng in range(n_blocks // 8):
        for l in range(CMP_BLOCK):
            for c in range(CMP_PLANES):
                rows = load_rows(c, ng * 8 * CMP_BLOCK + l)
                for half in range(LANE // HD_B):
                    x_s[c * (LANE // HD_B) + half, ng * 8:(ng + 1) * 8, l * HD_B:(l + 1) * HD_B] = (
                        rows[:, half * HD_B:(half + 1) * HD_B])


def _compress_slot(x_s, slot, pe_ref, w1_ref, w2_ref, kcg_ref):
    n_blocks = x_s.shape[1]
    xs = x_s[slot * N_KV_B:(slot + 1) * N_KV_B].reshape(N_KV_B * n_blocks, CMP_BLOCK * HD_B)
    hid = jax.nn.gelu(_dot((xs + pe_ref[slot]).astype(BF16), w1_ref[slot]))
    y = _dot(hid.astype(BF16), w2_ref[slot])
    return _rms(y, kcg_ref[...]) if slot == 0 else y


def _cmp_prompt_body(*refs, n_blocks):
    planes = refs[:CMP_PLANES]
    pe_ref, w1_ref, w2_ref, kcg_ref, kc_ref, vc_ref, x_s = refs[CMP_PLANES:]
    _flatten_cmp_blocks(lambda c, start: planes[c][pl.ds(start, 8, stride=CMP_BLOCK), :], x_s, n_blocks)
    kc_ref[...] = _compress_slot(x_s, 0, pe_ref, w1_ref, w2_ref, kcg_ref).reshape(N_KV_B, n_blocks, HD_B)
    vc_ref[...] = _compress_slot(x_s, 1, pe_ref, w1_ref, w2_ref, kcg_ref).reshape(N_KV_B, n_blocks, HD_B)


def _cmp_prompt(kv_rows, b, t, pe, w1, w2, kc_g):
    tt = min(t, 2048)
    nbk = tt // CMP_BLOCK
    assert t % tt == 0 and nbk % 8 == 0
    steps = t // tt
    out = jax.ShapeDtypeStruct((b, N_KV_B, t // CMP_BLOCK, HD_B), F32)
    out_spec = pl.BlockSpec((None, N_KV_B, nbk, HD_B), lambda bi, i: (bi, 0, i, 0))
    plane_specs = [pl.BlockSpec((tt, LANE), lambda bi, i, c=c: (bi * steps + i, c)) for c in range(CMP_PLANES)]
    return pl.pallas_call(
        functools.partial(_cmp_prompt_body, n_blocks=nbk),
        out_shape=(out, out),
        grid=(b, steps),
        in_specs=plane_specs + [_const_spec(pe.shape), _const_spec(w1.shape), _const_spec(w2.shape),
                                _const_spec(kc_g.shape)],
        out_specs=(out_spec, out_spec),
        scratch_shapes=[pltpu.VMEM((2 * N_KV_B, nbk, CMP_BLOCK * HD_B), F32)],
        compiler_params=_params("parallel", "parallel"),
        name="nsa_compress",
    )(*([kv_rows] * CMP_PLANES), pe, w1, w2, kc_g)


def _masked_softmax(s, mask, axis):
    sm = jnp.where(mask, s, -jnp.inf)
    mx = jnp.max(sm, axis=axis, keepdims=True)
    mx = jnp.where(mx > -jnp.inf, mx, 0.0)
    e = jnp.where(mask, jnp.exp(s - mx), 0.0)
    return e / jnp.maximum(jnp.sum(e, axis=axis, keepdims=True), 1e-30)


CMP_SEGMENTS = (64, 64, 128)


def _cmp_block_order():
    order, base = [], 0
    for size in CMP_SEGMENTS:
        order += list(range(base, base + size, 2)) + list(range(base + 1, base + size, 2))
        base += size
    assert base == N_CMP_PAD
    return order


def _cmp_select_body(q_ref, kct_ref, vct_ref, gates_ref, o_ref, mnot_ref, *, tq):
    i = pl.program_id(2)
    q0 = i * tq
    gates = gates_ref[...]

    def run(nv):
        hv = nv // 2
        row = lax.broadcasted_iota(jnp.int32, (nv, tq), 0)
        tok = q0 + lax.broadcasted_iota(jnp.int32, (nv, tq), 1)
        blk_c = jnp.zeros((nv, tq), jnp.int32)
        base = 0
        for size in CMP_SEGMENTS:
            if base < nv:
                local = row - base
                seg_blk = base + jnp.where(local < size // 2, 2 * local, 2 * (local - size // 2) + 1)
                blk_c = jnp.where((row >= base) & (row < base + size), seg_blk, blk_c)
            base += size
        mask = (blk_c + 1) * CMP_BLOCK - 1 <= tok
        kct = kct_ref[:nv, :]
        vct = vct_ref[:, :nv]
        imp_parts = None
        for r in range(REP_B):
            qh = q_ref[:, r * LANE:(r + 1) * LANE]
            pt = _masked_softmax(_dot_nt(kct, qh), mask, 0)
            o_ref[:, r * LANE:(r + 1) * LANE] = _dot(vct, pt.astype(BF16)).T * gates[:, 3 * r:3 * r + 1]
            parts, base = [], 0
            for size in CMP_SEGMENTS:
                if base < nv:
                    parts.append(pt[base:base + size // 2] + pt[base + size // 2:base + size])
                base += size
            imp_parts = parts if imp_parts is None else [a + b for a, b in zip(imp_parts, parts)]
        imp = jnp.concatenate(imp_parts, axis=0)
        blk = lax.broadcasted_iota(jnp.int32, (hv, tq), 0)
        t_s = q0 + lax.broadcasted_iota(jnp.int32, (hv, tq), 1)
        cur = t_s // SEL_BLOCK
        forced = (blk == 0) | (blk == cur) | (blk == cur - 1)
        score = jnp.where(forced, jnp.inf, jnp.where(blk * SEL_BLOCK <= t_s, imp, -jnp.inf))
        blk_f = blk.astype(F32)
        pickable = score > -jnp.inf
        for _ in range(N_SEL):
            mx = jnp.max(score, axis=0, keepdims=True)
            first = jnp.min(jnp.where(score == mx, blk_f, float(N_SEL_PAD)), axis=0, keepdims=True)
            score = jnp.where(blk_f == first, -jnp.inf, score)
        mnot = jnp.where(pickable, jnp.where(score > -jnp.inf, 1.0, 0.0), 1.0)
        if hv < N_SEL_PAD:
            mnot = jnp.concatenate([mnot, jnp.ones((N_SEL_PAD - hv, tq), F32)], axis=0)
        mnot_ref[...] = mnot.T.astype(BF16)

    need = (q0 + tq) // CMP_BLOCK
    bounds, base = [], 0
    for size in CMP_SEGMENTS:
        base += size
        bounds.append(base)
    lo = 0
    for nv in bounds:
        pl.when((need > lo) & (need <= nv))(functools.partial(run, nv))
        lo = nv


def _cmp_select(qcat, kct, vct, gates, b, t, tq):
    m = b * t
    nq = t // tq
    return pl.pallas_call(
        functools.partial(_cmp_select_body, tq=tq),
        out_shape=(jax.ShapeDtypeStruct((m, N_HEADS_B * LANE), F32), jax.ShapeDtypeStruct((m, N_KV_B * LANE), BF16)),
        grid=(b, N_KV_B, nq),
        in_specs=[pl.BlockSpec((tq, REP_B * LANE), lambda bi, g, i: (bi * nq + i, g)),
                  pl.BlockSpec((None, None, N_CMP_PAD, LANE), lambda bi, g, i: (bi, g, 0, 0)),
                  pl.BlockSpec((None, None, LANE, N_CMP_PAD), lambda bi, g, i: (bi, g, 0, 0)),
                  pl.BlockSpec((tq, LANE), lambda bi, g, i: (bi * nq + i, g))],
        out_specs=(pl.BlockSpec((tq, REP_B * LANE), lambda bi, g, i: (bi * nq + i, g)),
                   pl.BlockSpec((tq, LANE), lambda bi, g, i: (bi * nq + i, g))),
        compiler_params=_params("parallel", "parallel", "parallel"),
        name="nsa_cmp_select",
    )(qcat, kct, vct, gates)


def _flash_body(tab_ref, bound_ref, q_ref, mnot_ref, k_ref, v_ref, gates_ref, mask_ref, o_ref,
                qs_ref, ks_ref, m_ref, acc_ref, *, tq, tk, fixed):
    step_id = pl.program_id(2)
    j = tab_ref[1, step_id]
    rows = REP_B * tq

    @pl.when(tab_ref[3, step_id] == 1)
    def _():
        if not fixed:
            m_ref[...] = jnp.full(m_ref.shape, M_INIT, F32)
        acc_ref[...] = jnp.zeros(acc_ref.shape, F32)
        lane = lax.broadcasted_iota(jnp.int32, (tq, LANE), 1)
        for r in range(REP_B):
            qh = q_ref[:, r * LANE:(r + 1) * LANE]
            if fixed:
                qh = jnp.where(lane < HD_B, qh, jnp.ones_like(qh))
            qs_ref[r * tq:(r + 1) * tq, :LANE] = mnot_ref[...]
            qs_ref[r * tq:(r + 1) * tq, LANE:] = qh

    def step(masked):
        kt = k_ref[...]
        lane = lax.broadcasted_iota(jnp.int32, (tk, LANE), 1)
        if fixed:
            kt = jnp.where(lane == HD_B, -bound_ref[0], kt.astype(F32)).astype(BF16)
        kpos = j * tk + lax.broadcasted_iota(jnp.int32, (tk, LANE), 0)
        ks_ref[:, :LANE] = jnp.where(kpos // SEL_BLOCK == lane, -MASK_BIG, 0.0).astype(BF16)
        ks_ref[:, LANE:] = kt
        s = _dot_nt(qs_ref[...], ks_ref[...])
        if masked:
            s = (s.reshape(REP_B, tq, tk) + mask_ref[...][None]).reshape(rows, tk)
        if fixed:
            acc_ref[...] += _dot(jnp.exp(s).astype(BF16), v_ref[...])
        else:
            m_old = m_ref[...]
            m_new = jnp.maximum(m_old, jnp.max(s, axis=-1, keepdims=True))
            p = jnp.exp(s - m_new[:, :1])
            acc_ref[...] = jnp.exp(m_old - m_new) * acc_ref[...] + _dot(p.astype(BF16), v_ref[...])
            m_ref[...] = m_new

    pl.when(tab_ref[2, step_id] == 0)(functools.partial(step, False))
    pl.when(tab_ref[2, step_id] == 1)(functools.partial(step, True))

    @pl.when(tab_ref[4, step_id] == 1)
    def _():
        gates = gates_ref[...]
        lane = lax.broadcasted_iota(jnp.int32, (tq, LANE), 1)
        for r in range(REP_B):
            a = acc_ref[r * tq:(r + 1) * tq, :]
            o = a / a[:, HD_B:HD_B + 1]
            g = gates[:, 3 * r + 1:3 * r + 2]
            o_ref[:, r * LANE:(r + 1) * LANE] = jnp.where(lane < HD_B, o * g, 0.0)


def _flash_steps(t, tq, tk):
    steps, offsets = [], []
    for i in range(t // tq):
        q_lo, q_hi = i * tq, (i + 1) * tq - 1
        js = list(range(q_hi // tk + 1))
        for j in js:
            masked = (j + 1) * tk - 1 > q_lo
            if masked and q_lo - j * tk not in offsets:
                offsets.append(q_lo - j * tk)
            pattern = offsets.index(q_lo - j * tk) if masked else (steps[-1][5] if steps else 0)
            steps.append((i, j, int(masked), j == js[0], j == js[-1], pattern))
    return steps, offsets


def _flash(qcat, mnot, k, v, gates, bound, b, t, tq, tk):
    m = b * t
    nq, nk = t // tq, t // tk
    steps, offsets = _flash_steps(t, tq, tk)
    tab = jnp.asarray(steps, jnp.int32).T
    dist = (jnp.asarray(offsets, jnp.int32)[:, None, None] + jnp.arange(tq, dtype=jnp.int32)[None, :, None]
            - jnp.arange(tk, dtype=jnp.int32)[None, None, :])
    masks = jnp.where(dist >= 0, 0.0, -MASK_BIG).astype(F32)
    kdim = 2 * LANE
    qidx = lambda bi, g, p, *pf: (bi * nq + pf[0][0, p], g)
    kidx = lambda bi, g, p, *pf: (bi * nk + pf[0][1, p], g)
    midx = lambda bi, g, p, *pf: (pf[0][5, p], 0, 0)

    def call(fixed):
        name = "nsa_flash_sel" + ("" if fixed else "_online")
        return pl.pallas_call(
            functools.partial(_flash_body, tq=tq, tk=tk, fixed=fixed),
            out_shape=jax.ShapeDtypeStruct((m, N_HEADS_B * LANE), F32),
            grid_spec=pltpu.PrefetchScalarGridSpec(
                num_scalar_prefetch=2,
                grid=(b, N_KV_B, len(steps)),
                in_specs=[pl.BlockSpec((tq, REP_B * LANE), qidx), pl.BlockSpec((tq, LANE), qidx),
                          pl.BlockSpec((tk, LANE), kidx), pl.BlockSpec((tk, LANE), kidx),
                          pl.BlockSpec((tq, LANE), qidx), pl.BlockSpec((None, tq, tk), midx)],
                out_specs=pl.BlockSpec((tq, REP_B * LANE), qidx),
                scratch_shapes=[pltpu.VMEM((REP_B * tq, kdim), BF16), pltpu.VMEM((tk, 2 * LANE), BF16),
                                pltpu.VMEM((REP_B * tq, LANE), F32), pltpu.VMEM((REP_B * tq, LANE), F32)]),
            compiler_params=_params("parallel", "parallel", "arbitrary"),
            name=name,
        )(tab, bound.reshape(1), qcat, mnot, k, v, gates, masks)

    return lax.cond(bound <= SHIFT_MAX, lambda: call(True), lambda: call(False))


def _window_body(bound_ref, q_ref, *refs, tile, n_key_tiles, fixed):
    k_refs = refs[:n_key_tiles]
    v_refs = refs[n_key_tiles:2 * n_key_tiles]
    gates_ref, mask_ref, o_ref = refs[2 * n_key_tiles:]
    keys = jnp.concatenate([r[...] for r in k_refs], axis=0)
    vals = jnp.concatenate([r[...] for r in v_refs], axis=0)
    q = jnp.concatenate([q_ref[:, r * LANE:(r + 1) * LANE] for r in range(REP_B)], axis=0)
    if fixed:
        lane_k = lax.broadcasted_iota(jnp.int32, keys.shape, 1)
        keys = jnp.where(lane_k == HD_B, -bound_ref[0], keys.astype(F32)).astype(BF16)
        lane_q = lax.broadcasted_iota(jnp.int32, q.shape, 1)
        q = jnp.where(lane_q < HD_B, q, jnp.ones_like(q))
    else:
        lane_q = lax.broadcasted_iota(jnp.int32, q.shape, 1)
        q = jnp.where(lane_q < HD_B, q, jnp.zeros_like(q))
    span = n_key_tiles * tile
    s = (_dot_nt(q, keys).reshape(REP_B, tile, span) + mask_ref[...][None]).reshape(REP_B * tile, span)
    if not fixed:
        s = s - jnp.max(s, axis=-1, keepdims=True)
    acc = _dot(jnp.exp(s).astype(BF16), vals)
    gates = gates_ref[...]
    lane = lax.broadcasted_iota(jnp.int32, (tile, LANE), 1)
    for r in range(REP_B):
        a = acc[r * tile:(r + 1) * tile]
        g = gates[:, 3 * r + 2:3 * r + 3]
        o_ref[:, r * LANE:(r + 1) * LANE] = jnp.where(lane < HD_B, a / a[:, HD_B:HD_B + 1] * g, 0.0)


def _window_attn(qcat, k, v, gates, bound, b, t, tile):
    assert WINDOW % tile == 0 and t % tile == 0
    m = b * t
    nq = t // tile
    back = WINDOW // tile
    nkt = back + 1
    first_tile = jnp.arange(back + 1, dtype=jnp.int32)[:, None, None] - back
    col = jnp.arange(nkt * tile, dtype=jnp.int32)[None, None, :]
    key_tile = first_tile + col // tile
    dist = (jnp.arange(tile, dtype=jnp.int32)[None, :, None] + back * tile) - col
    masks = jnp.where((dist >= 0) & (dist <= WINDOW) & (key_tile >= 0), 0.0, -MASK_BIG).astype(F32)
    qidx = lambda bi, g, i, *_: (bi * nq + i, g)
    kspecs = [pl.BlockSpec((tile, LANE), lambda bi, g, i, *_, c=c: (bi * nq + jnp.maximum(i - back + c, 0), g))
              for c in range(nkt)]

    def call(fixed):
        return pl.pallas_call(
            functools.partial(_window_body, tile=tile, n_key_tiles=nkt, fixed=fixed),
            out_shape=jax.ShapeDtypeStruct((m, N_HEADS_B * LANE), F32),
            grid_spec=pltpu.PrefetchScalarGridSpec(
                num_scalar_prefetch=1,
                grid=(b, N_KV_B, nq),
                in_specs=[pl.BlockSpec((tile, REP_B * LANE), qidx)] + kspecs + kspecs
                         + [pl.BlockSpec((tile, LANE), qidx),
                            pl.BlockSpec((None, tile, nkt * tile), lambda bi, g, i, *_: (jnp.minimum(i, back), 0, 0))],
                out_specs=pl.BlockSpec((tile, REP_B * LANE), qidx)),
            compiler_params=_params("parallel", "parallel", "parallel"),
            name="nsa_window" if fixed else "nsa_window_rowmax",
        )(bound.reshape(1), qcat, *([k] * nkt), *([v] * nkt), gates, masks)

    return lax.cond(bound <= SHIFT_MAX, lambda: call(True), lambda: call(False))


def _sum_proj_body(a_ref, b_ref, c_ref, x_ref, w_ref, o_ref):
    o = (a_ref[...] + b_ref[...] + c_ref[...]).astype(BF16)
    o_ref[...] = x_ref[...] + _dot(o, w_ref[...])


def _sum_proj(a, b, c, x, w):
    m, d = x.shape
    kdim = a.shape[1]
    tm = _row_tile(m)
    big = pl.BlockSpec((tm, kdim), lambda i: (i, 0))
    row = pl.BlockSpec((tm, d), lambda i: (i, 0))
    return pl.pallas_call(
        _sum_proj_body,
        out_shape=jax.ShapeDtypeStruct((m, d), F32),
        grid=(m // tm,),
        in_specs=[big, big, big, row, _const_spec(w.shape)],
        out_specs=row,
        compiler_params=_params("parallel"),
        name="nsa_out_proj",
    )(a, b, c, x, w)


def _nsa_weights(w_in, q_g, k_g, w_out):
    d = w_in.shape[0]
    nq = N_HEADS_B * HD_B
    w = N_KV_B * HD_B
    wq = w_in[:, :nq].reshape(d, N_HEADS_B, 1, HD_B)
    wq = jnp.broadcast_to(wq, (d, N_HEADS_B, 2, HD_B)).reshape(d, N_HEADS_B * LANE)
    wg = w_in[:, nq:nq + 3 * N_HEADS_B].reshape(d, N_KV_B, REP_B * 3)
    wg = jnp.pad(wg, ((0, 0), (0, 0), (0, LANE - REP_B * 3))).reshape(d, N_KV_B * LANE)
    wkv = w_in[:, nq + 3 * N_HEADS_B:]
    lane = jnp.arange(w)
    seg = (lane[:, None] // HD_B == lane[None, :] // HD_B).astype(BF16)
    spread = (lane[:, None] // HD_B * LANE + lane[:, None] % HD_B == jnp.arange(N_KV_B * LANE)[None, :]).astype(BF16)
    wo = jnp.pad(w_out.reshape(N_HEADS_B, HD_B, -1), ((0, 0), (0, LANE - HD_B), (0, 0)))
    return {
        "wq": wq.astype(BF16), "wg": wg.astype(BF16), "wkv": wkv.astype(BF16), "seg": seg, "spread": spread,
        "qg": jnp.tile(q_g, 2)[None, :], "kg": jnp.stack([jnp.tile(k_g[1], N_KV_B), jnp.tile(k_g[2], N_KV_B)]),
        "wo": wo.reshape(N_HEADS_B * LANE, -1).astype(BF16),
    }


def _cmp_weights(pe, w_c1, w_c2, kc_g):
    return (pe.reshape(2, 1, CMP_BLOCK * HD_B), w_c1.reshape(2, CMP_BLOCK * HD_B, HD_B).astype(BF16),
            w_c2.astype(BF16), kc_g[None, :])


def _nsa_prompt(x, g_mix, b, t, w_in, q_g, k_g, pe, w_c1, w_c2, w_out):
    wts = _nsa_weights(w_in, q_g, k_g, w_out)
    tm = _row_tile(t)
    pos = jnp.arange(t)
    qcat, gates, kv_rows, win_rows, ks_s, vs_s, kw_s, vw_s = _nsa_proj(
        x, g_mix, wts, _rope_tables(pos, LANE), _rope_tables(pos, HD_B), t // tm, tm)
    kc_blk, vc_blk = _cmp_prompt(kv_rows, b, t, *_cmp_weights(pe, w_c1, w_c2, k_g[0]))
    nb = t // CMP_BLOCK
    assert nb <= N_CMP_PAD and t % SEL_BLOCK == 0

    order = jnp.asarray(_cmp_block_order(), jnp.int32)

    def blocks(a, lo):
        a = jnp.pad(a, ((0, 0), (0, 0), (0, N_CMP_PAD - nb), (lo, LANE - HD_B - lo)))
        return a[:, :, order].astype(BF16)

    kct = blocks(kc_blk, HD_B)
    vct = blocks(vc_blk, 0).transpose(0, 1, 3, 2)
    o_cmp, mnot = _cmp_select(qcat, kct, vct, gates, b, t, min(CMP_SELECT_TILE, t))
    qmax = jnp.max(jnp.abs(q_g))
    o_sel = _flash(qcat, mnot, ks_s, vs_s, gates, qmax * jnp.max(jnp.abs(k_g[1])) * math.sqrt(HD_B), b, t,
                   *(min(n, t) for n in FLASH_SEL_TILES))
    o_win = _window_attn(qcat, kw_s, vw_s, gates, qmax * jnp.max(jnp.abs(k_g[2])) * math.sqrt(HD_B), b, t,
                         min(WINDOW_TILE, t))
    y = _sum_proj(o_cmp, o_sel, o_win, x, wts["wo"])
    wb = min(WINDOW, t)
    kv_out = kv_rows.reshape(b, t, N_KV_SLOTS, N_KV_B, HD_B)
    win_out = win_rows.reshape(b, t, 2, N_KV_B, HD_B)[:, t - wb:]
    return y, kv_out, win_out


def _gmlp_layer(x, g, w_ins, ln_g, ln_b, w_s, b_s, w_outs, layer, *, single):
    gw = w_outs.shape[1] // N_GROUPS_A
    if single:
        ws = jnp.repeat(w_s[:, 0, 0], gw)[None, :]
        bs = jnp.repeat(b_s[:, 0], gw)[None, :]
    else:
        ws = w_s
        bs = jnp.repeat(b_s.T, gw, axis=1)
    return _gmlp(x, g[None, :], w_ins, ln_g[None, :], ln_b[None, :], ws, bs, w_outs, layer, single=single)


def _mlstm_proj_body(x_ref, g_ref, wq_ref, wk_ref, wv_ref, wgi_ref, wo_ref, bif_ref,
                     q_ref, k_ref, v_ref, gi_ref, og_ref):
    xb = _rms(x_ref[...], g_ref[...]).astype(BF16)
    q_ref[...] = _dot(xb, wq_ref[...]).astype(BF16)
    k_ref[...] = _dot(xb, wk_ref[...]).astype(BF16)
    v_ref[...] = _dot(xb, wv_ref[...]).astype(BF16)
    gi_ref[...] = _dot(xb, wgi_ref[...]) + bif_ref[...]
    og_ref[...] = jax.nn.sigmoid(_dot(xb, wo_ref[...]))


def _mlstm_proj(x, g, wts):
    m, d = x.shape
    hv = N_HEADS_C * DV_C
    tm = _row_tile(m)
    consts = [g, wts["wq"], wts["wk"], wts["wv"], wts["wgi"], wts["wo"], wts["bif"]]
    tile = lambda n: pl.BlockSpec((tm, n), lambda i: (i, 0))
    return pl.pallas_call(
        _mlstm_proj_body,
        out_shape=(jax.ShapeDtypeStruct((m, N_HEADS_C * LANE), BF16), jax.ShapeDtypeStruct((m, N_HEADS_C * LANE), BF16),
                   jax.ShapeDtypeStruct((m, hv), BF16), jax.ShapeDtypeStruct((m, LANE), F32),
                   jax.ShapeDtypeStruct((m, hv), F32)),
        grid=(m // tm,),
        in_specs=[tile(d)] + [_const_spec(c.shape) for c in consts],
        out_specs=(tile(N_HEADS_C * LANE), tile(N_HEADS_C * LANE), tile(hv), tile(LANE), tile(hv)),
        compiler_params=_params("parallel"),
        name="mlstm_proj",
    )(x, *consts)


def _mlstm_scan_body(q_ref, k_ref, v_ref, gi_ref, git_ref, hs_ref, c_out, n_out, m_out, c_s, n_s, m_s):
    c = pl.program_id(1)
    L = q_ref.shape[0]

    @pl.when(c == 0)
    def _():
        c_s[...] = jnp.zeros(c_s.shape, F32)
        n_s[...] = jnp.zeros(n_s.shape, F32)
        m_s[...] = jnp.zeros(m_s.shape, F32)

    row = lax.broadcasted_iota(jnp.int32, (L, L), 0)
    col = lax.broadcasted_iota(jnp.int32, (L, L), 1)
    causal = col <= row
    tril = causal.astype(BF16)
    gi = gi_ref[...]
    git = git_ref[...]
    fcol = jax.nn.log_sigmoid(gi)
    frow = jax.nn.log_sigmoid(git[N_HEADS_C:, :])
    bcol_all = sum(_dot(tril, part) for part in _split3(fcol))
    brow_all = sum(_dot_nt(part, tril) for part in _split3(frow))
    heads = range(N_HEADS_C)
    sl = [slice(h * LANE, (h + 1) * LANE) for h in heads]
    q = [q_ref[:, sl[h]] for h in heads]
    k = [k_ref[:, sl[h]] for h in heads]
    v = [v_ref[:, sl[h]] for h in heads]
    qk = [_dot_nt(q[h], k[h]) for h in heads]
    cq = [_dot_nt(q[h], c_s[h].astype(BF16)) for h in heads]
    bcol = [bcol_all[:, N_HEADS_C + h:N_HEADS_C + h + 1] for h in heads]
    m_prev = [m_s[h:h + 1, 0:1] for h in heads]
    s, a, m_t = [], [], []
    for h in heads:
        dlog = jnp.where(causal, bcol[h] - brow_all[h:h + 1, :] + git[h:h + 1, :], -jnp.inf)
        inter = bcol[h] + m_prev[h]
        m_t.append(jnp.maximum(inter, jnp.max(dlog, axis=1, keepdims=True)))
        s.append(qk[h] * jnp.exp(dlog - m_t[h]))
        a.append(jnp.exp(inter - m_t[h]))
    sv = [_dot(s[h].astype(BF16), v[h]) for h in heads]
    wk, decay, m_new = [], [], []
    for h in heads:
        nq = jnp.sum(q[h].astype(F32) * n_s[h:h + 1, :], axis=1, keepdims=True)
        den = a[h] * nq + jnp.sum(s[h], axis=1, keepdims=True)
        hs_ref[:, sl[h]] = (a[h] * cq[h] + sv[h]) / jnp.maximum(jnp.abs(den), jnp.exp(-m_t[h]))
        b_end = bcol[h][L - 1:L, :]
        wlog = b_end - bcol[h] + gi[:, h:h + 1]
        m_new.append(jnp.maximum(b_end + m_prev[h], jnp.max(wlog, axis=0, keepdims=True)))
        wk.append(jnp.exp(wlog - m_new[h]))
        decay.append(jnp.exp(b_end + m_prev[h] - m_new[h]))
    upd = [_dot_tn((v[h].astype(F32) * wk[h]).astype(BF16), k[h]) for h in heads]
    for h in heads:
        c_s[h] = decay[h] * c_s[h] + upd[h]
        n_s[h:h + 1, :] = decay[h] * n_s[h:h + 1, :] + jnp.sum(k[h].astype(F32) * wk[h], axis=0, keepdims=True)
        m_s[h:h + 1, :] = jnp.broadcast_to(m_new[h], (1, LANE))

    @pl.when(c == pl.num_programs(1) - 1)
    def _():
        c_out[...] = c_s[...]
        n_out[...] = n_s[...]
        m_out[...] = m_s[...]


def _mlstm_scan(q, k, v, gi, git, b, t):
    L = math.gcd(t, CHUNK_C)
    nc = t // L
    hv = N_HEADS_C * DV_C
    tile = lambda n: pl.BlockSpec((L, n), lambda bi, c: (bi * nc + c, 0))
    return pl.pallas_call(
        _mlstm_scan_body,
        out_shape=(jax.ShapeDtypeStruct((b * t, hv), F32),
                   jax.ShapeDtypeStruct((b, N_HEADS_C, DV_C, LANE), F32),
                   jax.ShapeDtypeStruct((b, N_HEADS_C, LANE), F32), jax.ShapeDtypeStruct((b, N_HEADS_C, LANE), F32)),
        grid=(b, nc),
        in_specs=[tile(N_HEADS_C * LANE), tile(N_HEADS_C * LANE), tile(hv), tile(LANE),
                  pl.BlockSpec((None, 2 * N_HEADS_C, L), lambda bi, c: (bi, 0, c))],
        out_specs=(tile(hv), pl.BlockSpec((None, N_HEADS_C, DV_C, LANE), lambda bi, c: (bi, 0, 0, 0)),
                   pl.BlockSpec((None, N_HEADS_C, LANE), lambda bi, c: (bi, 0, 0)),
                   pl.BlockSpec((None, N_HEADS_C, LANE), lambda bi, c: (bi, 0, 0))),
        scratch_shapes=[pltpu.VMEM((N_HEADS_C, DV_C, LANE), F32), pltpu.VMEM((N_HEADS_C, LANE), F32),
                        pltpu.VMEM((N_HEADS_C, LANE), F32)],
        compiler_params=_params("parallel", "arbitrary"),
        name="mlstm_scan",
    )(q, k, v, gi, git)


def _mlstm_out_body(hs_ref, og_ref, hg_ref, x_ref, w_ref, o_ref):
    parts = []
    for h in range(N_HEADS_C):
        sl = slice(h * DV_C, (h + 1) * DV_C)
        parts.append((og_ref[:, sl] * _rms(hs_ref[:, sl], hg_ref[:, sl])).astype(BF16))
    o_ref[...] = x_ref[...] + _dot(jnp.concatenate(parts, axis=1), w_ref[...])


def _mlstm_out(hs, og, hg, x, w):
    m, d = x.shape
    hv = hs.shape[1]
    tm = _row_tile(m)
    wide = pl.BlockSpec((tm, hv), lambda i: (i, 0))
    row = pl.BlockSpec((tm, d), lambda i: (i, 0))
    return pl.pallas_call(
        _mlstm_out_body,
        out_shape=jax.ShapeDtypeStruct((m, d), F32),
        grid=(m // tm,),
        in_specs=[wide, wide, _const_spec(hg.shape), row, _const_spec(w.shape)],
        out_specs=row,
        compiler_params=_params("parallel"),
        name="mlstm_out",
    )(hs, og, hg, x, w)


def _mlstm_weights(w_in, b_if):
    d = w_in.shape[0]
    hk, hv = N_HEADS_C * DK_C, N_HEADS_C * DV_C

    def spread(w):
        w = w.reshape(d, N_HEADS_C, DK_C)
        return jnp.pad(w, ((0, 0), (0, 0), (0, LANE - DK_C))).reshape(d, N_HEADS_C * LANE)

    wgi = jnp.pad(w_in[:, 2 * hk + hv:2 * hk + hv + 2 * N_HEADS_C], ((0, 0), (0, LANE - 2 * N_HEADS_C)))
    return {
        "wq": spread(w_in[:, :hk]).astype(BF16),
        "wk": (spread(w_in[:, hk:2 * hk]) * (DK_C ** -0.5)).astype(BF16),
        "wv": w_in[:, 2 * hk:2 * hk + hv].astype(BF16),
        "wgi": wgi.astype(BF16),
        "wo": w_in[:, 2 * hk + hv + 2 * N_HEADS_C:].astype(BF16),
        "bif": jnp.pad(b_if, (0, LANE - 2 * N_HEADS_C))[None, :],
    }


def _mlstm_prompt(x, g_mix, b, t, w_in, b_if, h_g, w_out):
    wts = _mlstm_weights(w_in, b_if)
    q, k, v, gi, og = _mlstm_proj(x, g_mix, wts)
    git = gi[:, :2 * N_HEADS_C].reshape(b, t, 2 * N_HEADS_C).transpose(0, 2, 1)
    hs, c, n, m = _mlstm_scan(q, k, v, gi, git, b, t)
    y = _mlstm_out(hs, og, h_g[None, :], x, w_out.astype(BF16))
    return y, c[..., :DK_C], n[..., :DK_C], m[..., 0]


def _proj_add_body(o_ref, x_ref, w_ref, out_ref):
    out_ref[...] = x_ref[...] + _dot(o_ref[...].astype(BF16), w_ref[...])


def _proj_add(o, x, w):
    m, d = x.shape
    tm = _row_tile(m)
    return pl.pallas_call(
        _proj_add_body,
        out_shape=jax.ShapeDtypeStruct((m, d), F32),
        grid=(m // tm,),
        in_specs=[pl.BlockSpec((tm, o.shape[1]), lambda i: (i, 0)), pl.BlockSpec((tm, d), lambda i: (i, 0)),
                  _const_spec(w.shape)],
        out_specs=pl.BlockSpec((tm, d), lambda i: (i, 0)),
        compiler_params=_params("parallel"),
        name="proj_add",
    )(o, x, w)


def _nsa_step_body(pt_ref, *refs, n_pages, page, past_len):
    pages = refs[:n_pages]
    (win_ref, qr_ref, qn_ref, gates_ref, nkv_ref, nwin_ref, pe_ref, w1_ref, w2_ref, kcg_ref,
     o_ref, c_s, x_s) = refs[n_pages:]
    w = N_KV_B * HD_B
    length = n_pages * page
    nb = length // CMP_BLOCK
    t = past_len
    for p in range(n_pages):
        for c in range(CMP_PLANES):
            c_s[c, p * page:(p + 1) * page, :] = pages[p][c * LANE:(c + 1) * LANE, :].T
    _flatten_cmp_blocks(lambda c, start: c_s[c, pl.ds(start, 8, stride=CMP_BLOCK), :], x_s, nb)
    lane_w = lax.broadcasted_iota(jnp.int32, (HD_B, w), 1)
    row_w = lax.broadcasted_iota(jnp.int32, (HD_B, w), 0)
    cmp_nat = []
    for slot in range(2):
        y = _compress_slot(x_s, slot, pe_ref, w1_ref, w2_ref, kcg_ref)
        nat = jnp.zeros((nb, w), F32)
        for g in range(N_KV_B):
            place = (lane_w == row_w + g * HD_B).astype(BF16)
            nat = nat + _dot(y[g * nb:(g + 1) * nb].astype(BF16), place)
        cmp_nat.append(nat.astype(BF16))
    kc, vc = cmp_nat
    qr = qr_ref[...]
    qn = qn_ref[...]
    gates = gates_ref[...]
    nh = N_HEADS_B
    blk = lax.broadcasted_iota(jnp.int32, (nh, nb), 1)
    p_c = _masked_softmax(_dot_nt(qn, kc), (blk + 1) * CMP_BLOCK - 1 <= t, 1)
    o_c = _dot(p_c.astype(BF16), vc)
    blk_t = lax.broadcasted_iota(jnp.int32, (nb, nh), 0)
    p_t = _masked_softmax(_dot_nt(kc, qn), (blk_t + 1) * CMP_BLOCK - 1 <= t, 0)
    gsum = (lax.broadcasted_iota(jnp.int32, (nh, LANE), 0) // REP_B
            == lax.broadcasted_iota(jnp.int32, (nh, LANE), 1)).astype(BF16)
    pair = (lax.broadcasted_iota(jnp.int32, (nb, nb), 1) // (SEL_BLOCK // CMP_BLOCK)
            == lax.broadcasted_iota(jnp.int32, (nb, nb), 0)).astype(BF16)
    imp = sum(_dot(part, gsum) for part in _split3(p_t))
    imp = sum(_dot(pair, part) for part in _split3(imp))
    sblk = lax.broadcasted_iota(jnp.int32, (nb, LANE), 0)
    cur = t // SEL_BLOCK
    forced = (sblk == 0) | (sblk == cur) | (sblk == cur - 1)
    score = jnp.where(forced, jnp.inf, jnp.where(sblk * SEL_BLOCK <= t, imp, -jnp.inf))
    sblk_f = sblk.astype(F32)
    pickable = score > -jnp.inf
    for _ in range(N_SEL):
        mx = jnp.max(score, axis=0, keepdims=True)
        first = jnp.min(jnp.where(score == mx, sblk_f, float(nb)), axis=0, keepdims=True)
        score = jnp.where(sblk_f == first, -jnp.inf, score)
    notsel = jnp.where(pickable, jnp.where(score > -jnp.inf, -MASK_BIG, 0.0), -MASK_BIG)
    bias = _dot_nt(gsum, notsel.astype(BF16)).astype(BF16)
    expand = (lax.broadcasted_iota(jnp.int32, (nb, length), 1) // SEL_BLOCK
              == lax.broadcasted_iota(jnp.int32, (nb, length), 0)).astype(BF16)
    nkv = nkv_ref[...]
    nwin = nwin_ref[...]

    def attend(s, k_new, v_new, weighted_values):
        s_new = jnp.sum(qr.astype(F32) * k_new.astype(BF16).astype(F32), axis=1, keepdims=True)
        m = jnp.maximum(jnp.max(s, axis=1, keepdims=True), s_new)
        e = jnp.exp(s - m)
        e_new = jnp.exp(s_new - m)
        den = jnp.sum(e, axis=1, keepdims=True) + e_new
        num = weighted_values(e.astype(BF16)) + e_new.astype(BF16).astype(F32) * v_new.astype(BF16).astype(F32)
        return num / den

    s_sel = jnp.concatenate([_dot(qr, pages[p][2 * w:3 * w, :].astype(BF16)) for p in range(n_pages)], axis=1)
    o_s = attend(s_sel + _dot(bias, expand), nkv[:, 2 * w:3 * w], nkv[:, 3 * w:],
                 lambda e: sum(_dot_nt(e[:, p * page:(p + 1) * page], pages[p][3 * w:, :].astype(BF16))
                               for p in range(n_pages)))
    wb = win_ref.shape[1]
    pos_w = t - wb + lax.broadcasted_iota(jnp.int32, (nh, wb), 1)
    ok_w = (pos_w >= 0) & (t - pos_w <= WINDOW)
    s_w = jnp.where(ok_w, _dot(qr, win_ref[:w, :].astype(BF16)), -MASK_BIG)
    o_w = attend(s_w, nwin[:, :w], nwin[:, w:], lambda e: _dot_nt(e, win_ref[w:, :].astype(BF16)))
    o_ref[...] = gates[:, 0:1] * o_c + gates[:, 1:2] * o_s + gates[:, 2:3] * o_w


def _nsa_step(page_table, cache, win_cache, qr, qn, gates, new_kv, new_win, pe, w1, w2, kc_g, past_len):
    bsz, n_pages = page_table.shape
    page = cache.shape[2]
    assert page == LANE
    w = N_KV_B * HD_B
    nb = n_pages * page // CMP_BLOCK
    wb = win_cache.shape[2]
    per = lambda shape: pl.BlockSpec((None,) + shape, lambda b, pt: (b,) + (0,) * len(shape))
    const = lambda a: pl.BlockSpec(a.shape, lambda b, pt: (0,) * a.ndim)
    page_specs = [pl.BlockSpec((None, 4 * w, page), lambda b, pt, p=p: (pt[b, p], 0, 0)) for p in range(n_pages)]
    return pl.pallas_call(
        functools.partial(_nsa_step_body, n_pages=n_pages, page=page, past_len=past_len),
        out_shape=jax.ShapeDtypeStruct((bsz, N_HEADS_B, w), F32),
        grid_spec=pltpu.PrefetchScalarGridSpec(
            num_scalar_prefetch=1,
            grid=(bsz,),
            in_specs=page_specs + [per((2 * w, wb)), per((N_HEADS_B, w)), per((N_HEADS_B, w)), per((N_HEADS_B, LANE)),
                                   per((1, 4 * w)), per((1, 2 * w)), const(pe), const(w1), const(w2), const(kc_g)],
            out_specs=per((N_HEADS_B, w)),
            scratch_shapes=[pltpu.VMEM((2 * w // LANE, n_pages * page, LANE), F32),
                            pltpu.VMEM((2 * N_KV_B, nb, CMP_BLOCK * HD_B), F32)]),
        compiler_params=_params("parallel"),
        name="nsa_step",
    )(page_table, *([cache] * n_pages), win_cache, qr, qn, gates, new_kv, new_win, pe, w1, w2, kc_g)


def _nsa_sample_step(x, g_mix, past_len, kv_cache, win_cache, page_table, w_in, q_g, k_g, pe, w_c1, w_c2, w_out):
    bsz, d = x.shape
    w = N_KV_B * HD_B
    assert past_len % CMP_BLOCK == 0 and past_len // SEL_BLOCK + 1 <= past_len // CMP_BLOCK
    wts = _nsa_weights(w_in, q_g, k_g, w_out)
    pos = jnp.full((bsz,), past_len, jnp.int32)
    qcat, gates, kv_rows, win_rows, _, _, _, _ = _nsa_proj(
        x, g_mix, wts, _rope_tables(pos, LANE), _rope_tables(pos, HD_B), 1, bsz)
    q5 = qcat.reshape(bsz, N_KV_B, REP_B, 2, HD_B)
    eye = jnp.eye(N_KV_B, dtype=BF16)
    qrows = (q5[:, :, :, :, None, :] * eye[None, :, None, None, :, None])
    qr = qrows[:, :, :, 0].reshape(bsz, N_HEADS_B, w)
    qn = qrows[:, :, :, 1].reshape(bsz, N_HEADS_B, w)
    gts = gates.reshape(bsz, N_KV_B, LANE)[:, :, :REP_B * 3].reshape(bsz, N_HEADS_B, 3)
    gts = jnp.pad(gts, ((0, 0), (0, 0), (0, LANE - 3)))
    pool, page = kv_cache.shape[:2]
    cache_t = kv_cache.reshape(pool, page, 4 * w).transpose(0, 2, 1)
    win_t = win_cache.reshape(bsz, -1, 2 * w).transpose(0, 2, 1)
    o = _nsa_step(page_table, cache_t, win_t, qr, qn, gts,
                  kv_rows.reshape(bsz, 1, 4 * w), win_rows.reshape(bsz, 1, 2 * w),
                  *_cmp_weights(pe, w_c1, w_c2, k_g[0]), past_len)
    own = (jnp.arange(N_HEADS_B)[:, None] // REP_B == jnp.arange(N_KV_B)[None, :]).astype(F32)
    w_exp = own[:, :, None, None] * w_out.reshape(N_HEADS_B, 1, HD_B, d)
    y = _proj_add(o.reshape(bsz, N_HEADS_B * w), x, w_exp.reshape(N_HEADS_B * w, d).astype(BF16))
    return y, kv_rows.reshape(bsz, 1, N_KV_SLOTS, N_KV_B, HD_B), win_rows.reshape(bsz, 1, 2, N_KV_B, HD_B)


def _mlstm_step_body(q_ref, k_ref, qt_ref, kt_ref, v_ref, gi_ref, ct_ref, n_ref, m_ref,
                     h_ref, cto_ref, no_ref, mo_ref, *, sb):
    gi = gi_ref[...]
    logf = jax.nn.log_sigmoid(gi)
    m_all = m_ref[...]
    lane_m = lax.broadcasted_iota(jnp.int32, (1, N_HEADS_C), 1)
    for s in range(sb):
        m_new_row = jnp.zeros((1, N_HEADS_C), F32)
        for h in range(N_HEADS_C):
            q = q_ref[s:s + 1, h * LANE:h * LANE + DK_C].astype(F32)
            k = k_ref[s:s + 1, h * LANE:h * LANE + DK_C].astype(F32)
            qc = qt_ref[s, :, h:h + 1]
            kc = kt_ref[s, :, h:h + 1]
            v = v_ref[s:s + 1, h * DV_C:(h + 1) * DV_C].astype(F32)
            ct = ct_ref[s, h]
            n = n_ref[s, h:h + 1, :]
            it = gi[s:s + 1, h:h + 1]
            b = logf[s:s + 1, N_HEADS_C + h:N_HEADS_C + h + 1]
            m0 = m_all[s:s + 1, h:h + 1]
            inter = b + m0
            m_t = jnp.maximum(inter, it)
            wgt = jnp.exp(it - m_t)
            a = jnp.exp(inter - m_t)
            sc = jnp.sum(q * k, axis=1, keepdims=True) * wgt
            num = a * jnp.sum(ct * qc, axis=0, keepdims=True) + sc * v
            den = a * jnp.sum(n * q, axis=1, keepdims=True) + sc
            h_ref[s:s + 1, h * DV_C:(h + 1) * DV_C] = num / jnp.maximum(jnp.abs(den), jnp.exp(-m_t))
            cto_ref[s, h] = a * ct + (wgt * kc) * v
            no_ref[s, h:h + 1, :] = a * n + wgt * k
            m_new_row = jnp.where(lane_m == h, m_t, m_new_row)
        mo_ref[s:s + 1, :] = m_new_row


def _mlstm_step(q, k, qt, kt, v, gi, ct0, n0, m0):
    bsz = q.shape[0]
    sb = 8
    row = lambda n: pl.BlockSpec((sb, n), lambda i: (i, 0))
    c_spec = pl.BlockSpec((sb, N_HEADS_C, DK_C, DV_C), lambda i: (i, 0, 0, 0))
    n_spec = pl.BlockSpec((sb, N_HEADS_C, DK_C), lambda i: (i, 0, 0))
    col_spec = pl.BlockSpec((sb, DK_C, N_HEADS_C), lambda i: (i, 0, 0))
    hv = N_HEADS_C * DV_C
    return pl.pallas_call(
        functools.partial(_mlstm_step_body, sb=sb),
        out_shape=(jax.ShapeDtypeStruct((bsz, hv), F32), jax.ShapeDtypeStruct(ct0.shape, F32),
                   jax.ShapeDtypeStruct(n0.shape, F32), jax.ShapeDtypeStruct(m0.shape, F32)),
        grid=(bsz // sb,),
        in_specs=[row(N_HEADS_C * LANE), row(N_HEADS_C * LANE), col_spec, col_spec, row(hv), row(LANE),
                  c_spec, n_spec, row(N_HEADS_C)],
        out_specs=(row(hv), c_spec, n_spec, row(N_HEADS_C)),
        compiler_params=_params("parallel"),
        name="mlstm_step",
    )(q, k, qt, kt, v, gi, ct0, n0, m0)


def _mlstm_sample_step(x, g_mix, c0, n0, m0, w_in, b_if, h_g, w_out):
    bsz = x.shape[0]
    wts = _mlstm_weights(w_in, b_if)
    q, k, v, gi, og = _mlstm_proj(x, g_mix, wts)
    cols = lambda a: a.astype(F32).reshape(bsz, N_HEADS_C, LANE)[:, :, :DK_C].transpose(0, 2, 1)
    hs, ct, n, m = _mlstm_step(q, k, cols(q), cols(k), v, gi, c0.transpose(0, 1, 3, 2), n0, m0)
    y = _mlstm_out(hs, og, h_g[None, :], x, w_out.astype(BF16))
    return y, ct.transpose(0, 1, 3, 2), n, m


def kernel(x_prompt, x_sample, cache_nsa_kv, cache_nsa_win, state_mlstm_C, state_mlstm_n, state_mlstm_m, page_table,
           norm_mix_g, norm_ffn_g, ffn_w1, ffn_w2, a_w_in, a_ln_g, a_ln_b, a_w_s, a_b_s, a_w_out,
           b_w_in, b_q_g, b_k_g, b_pe, b_w_c1, b_w_c2, b_w_out, c_w_in, c_b_if, c_h_g, c_w_out):
    bp, t, d = x_prompt.shape
    bs, ts, _ = x_sample.shape
    assert ts == 1
    past_len = page_table.shape[1] * cache_nsa_kv.shape[2]
    xp = x_prompt.reshape(bp * t, d)
    xs = x_sample.reshape(bs, d)
    out = {k: [] for k in ("v_s", "kv_p", "win_p", "kv_s", "win_s", "C_p", "n_p", "m_p", "C_s", "n_s", "m_s")}
    ffn_w1_b, ffn_w2_b = ffn_w1.astype(BF16), ffn_w2.astype(BF16)
    a_w_in_b, a_w_out_b = a_w_in.astype(BF16), a_w_out.astype(BF16)
    for layer in range(norm_mix_g.shape[0]):
        kind, j = layer % 3, layer // 3
        gm = norm_mix_g[layer]
        if kind == 0:
            args = (a_w_in_b, a_ln_g[j], a_ln_b[j], a_w_s[j], a_b_s[j], a_w_out_b, j)
            xp = _gmlp_layer(xp, gm, *args, single=False)[0]
            xs, v = _gmlp_layer(xs, gm, *args, single=True)
            out["v_s"].append(v.reshape(bs, ts, -1))
        elif kind == 1:
            args = (b_w_in[j], b_q_g[j], b_k_g[j], b_pe[j], b_w_c1[j], b_w_c2[j], b_w_out[j])
            xp, kv, win = _nsa_prompt(xp, gm[None, :], bp, t, *args)
            out["kv_p"].append(kv)
            out["win_p"].append(win)
            xs, kv, win = _nsa_sample_step(xs, gm[None, :], past_len, cache_nsa_kv[j], cache_nsa_win[j], page_table,
                                           *args)
            out["kv_s"].append(kv)
            out["win_s"].append(win)
        else:
            args = (c_w_in[j], c_b_if[j], c_h_g[j], c_w_out[j])
            xp, c, n, m = _mlstm_prompt(xp, gm[None, :], bp, t, *args)
            out["C_p"].append(c)
            out["n_p"].append(n)
            out["m_p"].append(m)
            xs, c, n, m = _mlstm_sample_step(xs, gm[None, :], state_mlstm_C[j], state_mlstm_n[j], state_mlstm_m[j],
                                             *args)
            out["C_s"].append(c)
            out["n_s"].append(n)
            out["m_s"].append(m)
        gf = norm_ffn_g[layer][None, :]
        xp = _ffn(xp, gf, ffn_w1_b, ffn_w2_b, layer)
        xs = _ffn(xs, gf, ffn_w1_b, ffn_w2_b, layer)
    st = {k: jnp.stack(v) for k, v in out.items()}
    return (xp.reshape(bp, t, d), xs.reshape(bs, ts, d), st["v_s"], st["kv_p"], st["win_p"], st["kv_s"], st["win_s"],
            st["C_p"], st["n_p"], st["m_p"], st["C_s"], st["n_s"], st["m_s"])
```

```python
import functools
import math

import jax
import jax.numpy as jnp
from jax import lax
from jax.experimental import pallas as pl
from jax.experimental.pallas import tpu as pltpu

F32 = jnp.float32
BF16 = jnp.bfloat16

EPS = 1e-6
CHUNK_A = 128
N_GROUPS_A = 8
N_HEADS_B = 16
N_KV_B = 4
REP_B = N_HEADS_B // N_KV_B
HD_B = 64
ROT_DIM = 16
ROPE_THETA = 500000.0
CMP_BLOCK = 32
SEL_BLOCK = 64
N_SEL = 16
WINDOW = 512
N_KV_SLOTS = 4
N_HEADS_C = 8
DK_C = 64
DV_C = 128
CHUNK_C = 128
SCALE_B = HD_B ** -0.5

LANE = 128
VMEM_LIMIT_BYTES = 56 * 1024 * 1024
MASK_BIG = 1e30
M_INIT = -1e20
N_SEL_PAD = 128
N_CMP_PAD = 2 * N_SEL_PAD
SHIFT_MAX = 40.0
FLASH_SEL_TILES = (1024, 512)
WINDOW_TILE = 256
CMP_SELECT_TILE = 512


def _params(*sem):
    return pltpu.CompilerParams(dimension_semantics=sem, vmem_limit_bytes=VMEM_LIMIT_BYTES)


def _dot(a, b):
    return jnp.dot(a, b, preferred_element_type=F32)


def _dot_nt(a, b):
    return lax.dot_general(a, b, (((1,), (1,)), ((), ())), preferred_element_type=F32)


def _dot_tn(a, b):
    return lax.dot_general(a, b, (((0,), (0,)), ((), ())), preferred_element_type=F32)


def _rms(x, g):
    return x * lax.rsqrt(jnp.mean(x * x, axis=-1, keepdims=True) + EPS) * g


def _split3(x):
    a = x.astype(BF16)
    r = x - a.astype(F32)
    b = r.astype(BF16)
    c = (r - b.astype(F32)).astype(BF16)
    return a, b, c


def _const_spec(shape):
    n = len(shape)
    return pl.BlockSpec(shape, lambda *_: (0,) * n)


ROW_TILE = 512


def _row_tile(m, pref=ROW_TILE):
    t = min(pref, m)
    while m % t:
        t //= 2
    return t


def _ffn_body(x_ref, g_ref, w1_ref, w2_ref, o_ref, *, ck):
    x = x_ref[...]
    xb = _rms(x, g_ref[...]).astype(BF16)
    acc = x
    for j in range(w1_ref.shape[1] // ck):
        h = jnp.maximum(_dot(xb, w1_ref[:, j * ck:(j + 1) * ck]), 0.0)
        acc = acc + _dot((h * h).astype(BF16), w2_ref[j * ck:(j + 1) * ck, :])
    o_ref[...] = acc


def _layer_spec(stacked, layer):
    n = stacked.ndim - 1
    return pl.BlockSpec((None,) + stacked.shape[1:], lambda *_: (layer,) + (0,) * n)


def _ffn(x, g, w1s, w2s, layer):
    m, d = x.shape
    tm = _row_tile(m)
    return pl.pallas_call(
        functools.partial(_ffn_body, ck=1024),
        out_shape=jax.ShapeDtypeStruct((m, d), F32),
        grid=(m // tm,),
        in_specs=[pl.BlockSpec((tm, d), lambda i: (i, 0)), _const_spec(g.shape),
                  _layer_spec(w1s, layer), _layer_spec(w2s, layer)],
        out_specs=pl.BlockSpec((tm, d), lambda i: (i, 0)),
        compiler_params=_params("parallel"),
        name="ffn",
    )(x, g, w1s, w2s)


def _gmlp_body(x_ref, g_ref, win_ref, lng_ref, lnb_ref, ws_ref, bs_ref, wout_ref, o_ref, *maybe_v_ref, single):
    x = x_ref[...]
    dg = lng_ref.shape[1]
    xb = _rms(x, g_ref[...]).astype(BF16)
    u = jax.nn.gelu(_dot(xb, win_ref[:, :dg]))
    v = jax.nn.gelu(_dot(xb, win_ref[:, dg:]))
    mu = jnp.mean(v, axis=-1, keepdims=True)
    vc = v - mu
    var = jnp.mean(vc * vc, axis=-1, keepdims=True)
    v = vc * lax.rsqrt(var + EPS) * lng_ref[...] + lnb_ref[...]
    if single:
        maybe_v_ref[0][...] = v
        gate = v * ws_ref[...] + bs_ref[...]
    else:
        gw = dg // N_GROUPS_A
        row = lax.broadcasted_iota(jnp.int32, (CHUNK_A, CHUNK_A), 0)
        col = lax.broadcasted_iota(jnp.int32, (CHUNK_A, CHUNK_A), 1)
        causal = col <= row
        vb = v.astype(BF16)
        chunks = []
        for c in range(x.shape[0] // CHUNK_A):
            parts = []
            for gi in range(N_GROUPS_A):
                w = jnp.where(causal, ws_ref[gi], 0.0).astype(BF16)
                parts.append(_dot(w, vb[c * CHUNK_A:(c + 1) * CHUNK_A, gi * gw:(gi + 1) * gw]))
            chunks.append(jnp.concatenate(parts, axis=1) + bs_ref[...])
        gate = jnp.concatenate(chunks, axis=0)
    o_ref[...] = x + _dot((u * gate).astype(BF16), wout_ref[...])


def _gmlp(x, g, w_ins, ln_g, ln_b, ws, bs, w_outs, layer, *, single):
    m, d = x.shape
    dg = w_outs.shape[1]
    tm = _row_tile(m)
    n_out = 2 if single else 1
    outs = pl.pallas_call(
        functools.partial(_gmlp_body, single=single),
        out_shape=(jax.ShapeDtypeStruct((m, d), F32), jax.ShapeDtypeStruct((m, dg), F32))[:n_out],
        grid=(m // tm,),
        in_specs=[pl.BlockSpec((tm, d), lambda i: (i, 0)), _const_spec(g.shape), _layer_spec(w_ins, layer),
                  _const_spec(ln_g.shape), _const_spec(ln_b.shape), _const_spec(ws.shape), _const_spec(bs.shape),
                  _layer_spec(w_outs, layer)],
        out_specs=(pl.BlockSpec((tm, d), lambda i: (i, 0)), pl.BlockSpec((tm, dg), lambda i: (i, 0)))[:n_out],
        compiler_params=_params("parallel"),
        name="gmlp_single" if single else "gmlp",
    )(x, g, w_ins, ln_g, ln_b, ws, bs, w_outs)
    return outs if single else (outs[0], None)


def _rope_tables(pos, seg):
    half = ROT_DIM // 2
    freq = jnp.power(ROPE_THETA, -jnp.arange(half, dtype=F32) * 2.0 / ROT_DIM)
    ang = pos.astype(F32)[:, None] * freq[None, :]
    cos, sin = jnp.cos(ang), jnp.sin(ang)
    t = pos.shape[0]
    one = jnp.ones((t, seg - ROT_DIM), F32)
    zero = jnp.zeros((t, seg - ROT_DIM), F32)
    z8 = jnp.zeros((t, half), F32)
    tabs = [jnp.concatenate([cos, cos, one], 1), jnp.concatenate([-sin, z8, zero], 1),
            jnp.concatenate([z8, sin, zero], 1)]
    return jnp.stack([jnp.tile(a, (1, LANE // seg)) for a in tabs])


def _rope128(x, tab):
    return x * tab[0] + pltpu.roll(x, LANE - ROT_DIM // 2, 1) * tab[1] + pltpu.roll(x, ROT_DIM // 2, 1) * tab[2]


def _nsa_proj_body(x_ref, g_ref, wq_ref, wg_ref, wkv_ref, seg_ref, spread_ref, qg_ref, kg_ref, tq_ref, tk_ref,
                   qcat_ref, gates_ref, *out_refs, position_minor):
    if position_minor:
        kvt_ref, cmp_ref, win_ref, ks_ref, vs_ref, kw_ref, vw_ref = out_refs
    else:
        kv_ref, win_ref, ks_ref, vs_ref, kw_ref, vw_ref = out_refs
    x = x_ref[...]
    xb = _rms(x, g_ref[...]).astype(BF16)
    tq = tq_ref[...]
    tk = tk_ref[...]
    qg = qg_ref[...]
    for h in range(N_HEADS_B):
        q = _dot(xb, wq_ref[:, h * LANE:(h + 1) * LANE])
        ms = jnp.sum(q * q, axis=-1, keepdims=True) * (1.0 / LANE)
        qn = q * lax.rsqrt(ms + EPS) * qg
        qcat_ref[:, h * LANE:(h + 1) * LANE] = (_rope128(qn, tq) * SCALE_B).astype(BF16)
    gates_ref[...] = jax.nn.sigmoid(_dot(xb, wg_ref[...]))
    kv = _dot(xb, wkv_ref[...])
    w = N_KV_B * HD_B
    seg = seg_ref[...]
    spread = spread_ref[...]

    def head_norm(k, gain):
        k2 = k * k
        hi = k2.astype(BF16)
        lo = (k2 - hi.astype(F32)).astype(BF16)
        ss = _dot(hi, seg) + _dot(lo, seg)
        return k * lax.rsqrt(ss * (1.0 / HD_B) + EPS) * gain

    def rope(k):
        return jnp.concatenate([_rope128(k[:, j * LANE:(j + 1) * LANE], tk) for j in range(w // LANE)], axis=1)

    ks = rope(head_norm(kv[:, 2 * w:3 * w], kg_ref[0:1, :]))
    kw = rope(head_norm(kv[:, 4 * w:5 * w], kg_ref[1:2, :]))
    vs = kv[:, 3 * w:4 * w]
    vw = kv[:, 5 * w:6 * w]
    kv_rows = jnp.concatenate([kv[:, :2 * w], ks, vs], axis=1)
    if position_minor:
        kvt_ref[...] = kv_rows.T
        cmp_ref[...] = kv[:, :2 * w]
    else:
        kv_ref[...] = kv_rows
    win_ref[:, :w] = kw
    win_ref[:, w:] = vw
    lane = lax.broadcasted_iota(jnp.int32, (1, N_KV_B * LANE), 1)
    ones_hi = ((lane & HD_B) != 0).astype(F32)
    ks_ref[...] = _dot(ks.astype(BF16), spread).astype(BF16)
    kw_ref[...] = _dot(kw.astype(BF16), spread).astype(BF16)
    vs_ref[...] = (_dot(vs.astype(BF16), spread) + ones_hi).astype(BF16)
    vw_ref[...] = (_dot(vw.astype(BF16), spread) + ones_hi).astype(BF16)


def _nsa_proj(x, g, wts, tab_q, tab_k, n_tab_tiles, tm, *, position_minor):
    m, d = x.shape
    w = N_KV_B * HD_B
    ws = N_KV_B * LANE
    tile = lambda n: pl.BlockSpec((tm, n), lambda i: (i, 0))
    tab = pl.BlockSpec((3, tm, LANE), lambda i: (0, i % n_tab_tiles, 0))
    consts = [g, wts["wq"], wts["wg"], wts["wkv"], wts["seg"], wts["spread"], wts["qg"], wts["kg"]]
    sds = jax.ShapeDtypeStruct
    if position_minor:
        seq = n_tab_tiles * tm
        kv_shapes = (sds((m // seq, 4 * w, seq), F32), sds((m, 2 * w), F32))
        kv_specs = (pl.BlockSpec((None, 4 * w, tm), lambda i: (i // n_tab_tiles, 0, i % n_tab_tiles)), tile(2 * w))
    else:
        kv_shapes = (sds((m, 4 * w), F32),)
        kv_specs = (tile(4 * w),)
    slab = sds((m, ws), BF16)
    return pl.pallas_call(
        functools.partial(_nsa_proj_body, position_minor=position_minor),
        out_shape=(sds((m, N_HEADS_B * LANE), BF16), sds((m, ws), F32)) + kv_shapes
                  + (sds((m, 2 * w), F32), slab, slab, slab, slab),
        grid=(m // tm,),
        in_specs=[tile(d)] + [_const_spec(c.shape) for c in consts] + [tab, tab],
        out_specs=(tile(N_HEADS_B * LANE), tile(ws)) + kv_specs + (tile(2 * w), tile(ws), tile(ws), tile(ws), tile(ws)),
        compiler_params=_params("parallel"),
        name="nsa_proj",
    )(x, *consts, tab_q, tab_k)


CMP_PLANES = 2 * N_KV_B * HD_B // LANE


def _flatten_cmp_blocks(load_rows, x_s, n_blocks):
    for ng in range(n_blocks // 8):
        for l in range(CMP_BLOCK):
            for c in range(CMP_PLANES):
                rows = load_rows(c, ng * 8 * CMP_BLOCK + l)
                for half in range(LANE // HD_B):
                    x_s[c * (LANE // HD_B) + half, ng * 8:(ng + 1) * 8, l * HD_B:(l + 1) * HD_B] = (
                        rows[:, half * HD_B:(half + 1) * HD_B])


def _compress_slot(x_s, slot, pe_ref, w1_ref, w2_ref, kcg_ref):
    n_blocks = x_s.shape[1]
    xs = x_s[slot * N_KV_B:(slot + 1) * N_KV_B].reshape(N_KV_B * n_blocks, CMP_BLOCK * HD_B)
    hid = jax.nn.gelu(_dot((xs + pe_ref[slot]).astype(BF16), w1_ref[slot]))
    y = _dot(hid.astype(BF16), w2_ref[slot])
    return _rms(y, kcg_ref[...]) if slot == 0 else y


def _cmp_prompt_body(*refs, n_blocks):
    planes = refs[:CMP_PLANES]
    pe_ref, w1_ref, w2_ref, kcg_ref, kc_ref, vc_ref, x_s = refs[CMP_PLANES:]
    _flatten_cmp_blocks(lambda c, start: planes[c][pl.ds(start, 8, stride=CMP_BLOCK), :], x_s, n_blocks)
    kc_ref[...] = _compress_slot(x_s, 0, pe_ref, w1_ref, w2_ref, kcg_ref).reshape(N_KV_B, n_blocks, HD_B)
    vc_ref[...] = _compress_slot(x_s, 1, pe_ref, w1_ref, w2_ref, kcg_ref).reshape(N_KV_B, n_blocks, HD_B)


def _cmp_prompt(kv_rows, b, t, pe, w1, w2, kc_g):
    tt = min(t, 2048)
    nbk = tt // CMP_BLOCK
    assert t % tt == 0 and nbk % 8 == 0
    steps = t // tt
    out = jax.ShapeDtypeStruct((b, N_KV_B, t // CMP_BLOCK, HD_B), F32)
    out_spec = pl.BlockSpec((None, N_KV_B, nbk, HD_B), lambda bi, i: (bi, 0, i, 0))
    plane_specs = [pl.BlockSpec((tt, LANE), lambda bi, i, c=c: (bi * steps + i, c)) for c in range(CMP_PLANES)]
    return pl.pallas_call(
        functools.partial(_cmp_prompt_body, n_blocks=nbk),
        out_shape=(out, out),
        grid=(b, steps),
        in_specs=plane_specs + [_const_spec(pe.shape), _const_spec(w1.shape), _const_spec(w2.shape),
                                _const_spec(kc_g.shape)],
        out_specs=(out_spec, out_spec),
        scratch_shapes=[pltpu.VMEM((2 * N_KV_B, nbk, CMP_BLOCK * HD_B), F32)],
        compiler_params=_params("parallel", "parallel"),
        name="nsa_compress",
    )(*([kv_rows] * CMP_PLANES), pe, w1, w2, kc_g)


def _masked_softmax(s, mask, axis):
    sm = jnp.where(mask, s, -jnp.inf)
    mx = jnp.max(sm, axis=axis, keepdims=True)
    mx = jnp.where(mx > -jnp.inf, mx, 0.0)
    e = jnp.where(mask, jnp.exp(s - mx), 0.0)
    return e / jnp.maximum(jnp.sum(e, axis=axis, keepdims=True), 1e-30)


CMP_SEGMENTS = (64, 64, 128)


def _cmp_block_order():
    order, base = [], 0
    for size in CMP_SEGMENTS:
        order += list(range(base, base + size, 2)) + list(range(base + 1, base + size, 2))
        base += size
    assert base == N_CMP_PAD
    return order


def _cmp_select_body(q_ref, kct_ref, vct_ref, gates_ref, o_ref, mnot_ref, *, tq):
    i = pl.program_id(2)
    q0 = i * tq
    gates = gates_ref[...]

    def run(nv):
        hv = nv // 2
        row = lax.broadcasted_iota(jnp.int32, (nv, tq), 0)
        tok = q0 + lax.broadcasted_iota(jnp.int32, (nv, tq), 1)
        blk_c = jnp.zeros((nv, tq), jnp.int32)
        base = 0
        for size in CMP_SEGMENTS:
            if base < nv:
                local = row - base
                seg_blk = base + jnp.where(local < size // 2, 2 * local, 2 * (local - size // 2) + 1)
                blk_c = jnp.where((row >= base) & (row < base + size), seg_blk, blk_c)
            base += size
        mask = (blk_c + 1) * CMP_BLOCK - 1 <= tok
        kct = kct_ref[:nv, :]
        vct = vct_ref[:, :nv]
        imp_parts = None
        for r in range(REP_B):
            qh = q_ref[:, r * LANE:(r + 1) * LANE]
            pt = _masked_softmax(_dot_nt(kct, qh), mask, 0)
            o_ref[:, r * LANE:(r + 1) * LANE] = _dot(vct, pt.astype(BF16)).T * gates[:, 3 * r:3 * r + 1]
            parts, base = [], 0
            for size in CMP_SEGMENTS:
                if base < nv:
                    parts.append(pt[base:base + size // 2] + pt[base + size // 2:base + size])
                base += size
            imp_parts = parts if imp_parts is None else [a + b for a, b in zip(imp_parts, parts)]
        imp = jnp.concatenate(imp_parts, axis=0)
        blk = lax.broadcasted_iota(jnp.int32, (hv, tq), 0)
        t_s = q0 + lax.broadcasted_iota(jnp.int32, (hv, tq), 1)
        cur = t_s // SEL_BLOCK
        forced = (blk == 0) | (blk == cur) | (blk == cur - 1)
        score = jnp.where(forced, jnp.inf, jnp.where(blk * SEL_BLOCK <= t_s, imp, -jnp.inf))
        blk_f = blk.astype(F32)
        pickable = score > -jnp.inf
        for _ in range(N_SEL):
            mx = jnp.max(score, axis=0, keepdims=True)
            first = jnp.min(jnp.where(score == mx, blk_f, float(N_SEL_PAD)), axis=0, keepdims=True)
            score = jnp.where(blk_f == first, -jnp.inf, score)
        mnot = jnp.where(pickable, jnp.where(score > -jnp.inf, 1.0, 0.0), 1.0)
        if hv < N_SEL_PAD:
            mnot = jnp.concatenate([mnot, jnp.ones((N_SEL_PAD - hv, tq), F32)], axis=0)
        mnot_ref[...] = mnot.T.astype(BF16)

    need = (q0 + tq) // CMP_BLOCK
    bounds, base = [], 0
    for size in CMP_SEGMENTS:
        base += size
        bounds.append(base)
    lo = 0
    for nv in bounds:
        pl.when((need > lo) & (need <= nv))(functools.partial(run, nv))
        lo = nv


def _cmp_select(qcat, kct, vct, gates, b, t, tq):
    m = b * t
    nq = t // tq
    return pl.pallas_call(
        functools.partial(_cmp_select_body, tq=tq),
        out_shape=(jax.ShapeDtypeStruct((m, N_HEADS_B * LANE), F32), jax.ShapeDtypeStruct((m, N_KV_B * LANE), BF16)),
        grid=(b, N_KV_B, nq),
        in_specs=[pl.BlockSpec((tq, REP_B * LANE), lambda bi, g, i: (bi * nq + i, g)),
                  pl.BlockSpec((None, None, N_CMP_PAD, LANE), lambda bi, g, i: (bi, g, 0, 0)),
                  pl.BlockSpec((None, None, LANE, N_CMP_PAD), lambda bi, g, i: (bi, g, 0, 0)),
                  pl.BlockSpec((tq, LANE), lambda bi, g, i: (bi * nq + i, g))],
        out_specs=(pl.BlockSpec((tq, REP_B * LANE), lambda bi, g, i: (bi * nq + i, g)),
                   pl.BlockSpec((tq, LANE), lambda bi, g, i: (bi * nq + i, g))),
        compiler_params=_params("parallel", "parallel", "parallel"),
        name="nsa_cmp_select",
    )(qcat, kct, vct, gates)


def _flash_body(tab_ref, bound_ref, q_ref, mnot_ref, k_ref, v_ref, gates_ref, mask_ref, o_ref,
                qs_ref, ks_ref, m_ref, acc_ref, *, tq, tk, fixed):
    step_id = pl.program_id(2)
    j = tab_ref[1, step_id]
    rows = REP_B * tq

    @pl.when(tab_ref[3, step_id] == 1)
    def _():
        if not fixed:
            m_ref[...] = jnp.full(m_ref.shape, M_INIT, F32)
        acc_ref[...] = jnp.zeros(acc_ref.shape, F32)
        lane = lax.broadcasted_iota(jnp.int32, (tq, LANE), 1)
        for r in range(REP_B):
            qh = q_ref[:, r * LANE:(r + 1) * LANE]
            if fixed:
                qh = jnp.where(lane < HD_B, qh, jnp.ones_like(qh))
            qs_ref[r * tq:(r + 1) * tq, :LANE] = mnot_ref[...]
            qs_ref[r * tq:(r + 1) * tq, LANE:] = qh

    def step(masked):
        kt = k_ref[...]
        lane = lax.broadcasted_iota(jnp.int32, (tk, LANE), 1)
        if fixed:
            kt = jnp.where(lane == HD_B, -bound_ref[0], kt.astype(F32)).astype(BF16)
        kpos = j * tk + lax.broadcasted_iota(jnp.int32, (tk, LANE), 0)
        ks_ref[:, :LANE] = jnp.where(kpos // SEL_BLOCK == lane, -MASK_BIG, 0.0).astype(BF16)
        ks_ref[:, LANE:] = kt
        s = _dot_nt(qs_ref[...], ks_ref[...])
        if masked:
            s = (s.reshape(REP_B, tq, tk) + mask_ref[...][None]).reshape(rows, tk)
        if fixed:
            acc_ref[...] += _dot(jnp.exp(s).astype(BF16), v_ref[...])
        else:
            m_old = m_ref[...]
            m_new = jnp.maximum(m_old, jnp.max(s, axis=-1, keepdims=True))
            p = jnp.exp(s - m_new[:, :1])
            acc_ref[...] = jnp.exp(m_old - m_new) * acc_ref[...] + _dot(p.astype(BF16), v_ref[...])
            m_ref[...] = m_new

    pl.when(tab_ref[2, step_id] == 0)(functools.partial(step, False))
    pl.when(tab_ref[2, step_id] == 1)(functools.partial(step, True))

    @pl.when(tab_ref[4, step_id] == 1)
    def _():
        gates = gates_ref[...]
        lane = lax.broadcasted_iota(jnp.int32, (tq, LANE), 1)
        for r in range(REP_B):
            a = acc_ref[r * tq:(r + 1) * tq, :]
            o = a / a[:, HD_B:HD_B + 1]
            g = gates[:, 3 * r + 1:3 * r + 2]
            o_ref[:, r * LANE:(r + 1) * LANE] = jnp.where(lane < HD_B, o * g, 0.0)


def _flash_steps(t, tq, tk):
    steps, offsets = [], []
    for i in range(t // tq):
        q_lo, q_hi = i * tq, (i + 1) * tq - 1
        js = list(range(q_hi // tk + 1))
        for j in js:
            masked = (j + 1) * tk - 1 > q_lo
            if masked and q_lo - j * tk not in offsets:
                offsets.append(q_lo - j * tk)
            pattern = offsets.index(q_lo - j * tk) if masked else (steps[-1][5] if steps else 0)
            steps.append((i, j, int(masked), j == js[0], j == js[-1], pattern))
    return steps, offsets


def _flash(qcat, mnot, k, v, gates, bound, b, t, tq, tk):
    m = b * t
    nq, nk = t // tq, t // tk
    steps, offsets = _flash_steps(t, tq, tk)
    tab = jnp.asarray(steps, jnp.int32).T
    dist = (jnp.asarray(offsets, jnp.int32)[:, None, None] + jnp.arange(tq, dtype=jnp.int32)[None, :, None]
            - jnp.arange(tk, dtype=jnp.int32)[None, None, :])
    masks = jnp.where(dist >= 0, 0.0, -MASK_BIG).astype(F32)
    kdim = 2 * LANE
    qidx = lambda bi, g, p, *pf: (bi * nq + pf[0][0, p], g)
    kidx = lambda bi, g, p, *pf: (bi * nk + pf[0][1, p], g)
    midx = lambda bi, g, p, *pf: (pf[0][5, p], 0, 0)

    def call(fixed):
        name = "nsa_flash_sel" + ("" if fixed else "_online")
        return pl.pallas_call(
            functools.partial(_flash_body, tq=tq, tk=tk, fixed=fixed),
            out_shape=jax.ShapeDtypeStruct((m, N_HEADS_B * LANE), F32),
            grid_spec=pltpu.PrefetchScalarGridSpec(
                num_scalar_prefetch=2,
                grid=(b, N_KV_B, len(steps)),
                in_specs=[pl.BlockSpec((tq, REP_B * LANE), qidx), pl.BlockSpec((tq, LANE), qidx),
                          pl.BlockSpec((tk, LANE), kidx), pl.BlockSpec((tk, LANE), kidx),
                          pl.BlockSpec((tq, LANE), qidx), pl.BlockSpec((None, tq, tk), midx)],
                out_specs=pl.BlockSpec((tq, REP_B * LANE), qidx),
                scratch_shapes=[pltpu.VMEM((REP_B * tq, kdim), BF16), pltpu.VMEM((tk, 2 * LANE), BF16),
                                pltpu.VMEM((REP_B * tq, LANE), F32), pltpu.VMEM((REP_B * tq, LANE), F32)]),
            compiler_params=_params("parallel", "parallel", "arbitrary"),
            name=name,
        )(tab, bound.reshape(1), qcat, mnot, k, v, gates, masks)

    return lax.cond(bound <= SHIFT_MAX, lambda: call(True), lambda: call(False))


def _window_body(bound_ref, q_ref, *refs, tile, n_key_tiles, fixed):
    k_refs = refs[:n_key_tiles]
    v_refs = refs[n_key_tiles:2 * n_key_tiles]
    gates_ref, mask_ref, o_ref = refs[2 * n_key_tiles:]
    keys = jnp.concatenate([r[...] for r in k_refs], axis=0)
    vals = jnp.concatenate([r[...] for r in v_refs], axis=0)
    q = jnp.concatenate([q_ref[:, r * LANE:(r + 1) * LANE] for r in range(REP_B)], axis=0)
    if fixed:
        lane_k = lax.broadcasted_iota(jnp.int32, keys.shape, 1)
        keys = jnp.where(lane_k == HD_B, -bound_ref[0], keys.astype(F32)).astype(BF16)
        lane_q = lax.broadcasted_iota(jnp.int32, q.shape, 1)
        q = jnp.where(lane_q < HD_B, q, jnp.ones_like(q))
    else:
        lane_q = lax.broadcasted_iota(jnp.int32, q.shape, 1)
        q = jnp.where(lane_q < HD_B, q, jnp.zeros_like(q))
    span = n_key_tiles * tile
    s = (_dot_nt(q, keys).reshape(REP_B, tile, span) + mask_ref[...][None]).reshape(REP_B * tile, span)
    if not fixed:
        s = s - jnp.max(s, axis=-1, keepdims=True)
    acc = _dot(jnp.exp(s).astype(BF16), vals)
    gates = gates_ref[...]
    lane = lax.broadcasted_iota(jnp.int32, (tile, LANE), 1)
    for r in range(REP_B):
        a = acc[r * tile:(r + 1) * tile]
        g = gates[:, 3 * r + 2:3 * r + 3]
        o_ref[:, r * LANE:(r + 1) * LANE] = jnp.where(lane < HD_B, a / a[:, HD_B:HD_B + 1] * g, 0.0)


def _window_attn(qcat, k, v, gates, bound, b, t, tile):
    assert WINDOW % tile == 0 and t % tile == 0
    m = b * t
    nq = t // tile
    back = WINDOW // tile
    nkt = back + 1
    first_tile = jnp.arange(back + 1, dtype=jnp.int32)[:, None, None] - back
    col = jnp.arange(nkt * tile, dtype=jnp.int32)[None, None, :]
    key_tile = first_tile + col // tile
    dist = (jnp.arange(tile, dtype=jnp.int32)[None, :, None] + back * tile) - col
    masks = jnp.where((dist >= 0) & (dist <= WINDOW) & (key_tile >= 0), 0.0, -MASK_BIG).astype(F32)
    qidx = lambda bi, g, i, *_: (bi * nq + i, g)
    kspecs = [pl.BlockSpec((tile, LANE), lambda bi, g, i, *_, c=c: (bi * nq + jnp.maximum(i - back + c, 0), g))
              for c in range(nkt)]

    def call(fixed):
        return pl.pallas_call(
            functools.partial(_window_body, tile=tile, n_key_tiles=nkt, fixed=fixed),
            out_shape=jax.ShapeDtypeStruct((m, N_HEADS_B * LANE), F32),
            grid_spec=pltpu.PrefetchScalarGridSpec(
                num_scalar_prefetch=1,
                grid=(b, N_KV_B, nq),
                in_specs=[pl.BlockSpec((tile, REP_B * LANE), qidx)] + kspecs + kspecs
                         + [pl.BlockSpec((tile, LANE), qidx),
                            pl.BlockSpec((None, tile, nkt * tile), lambda bi, g, i, *_: (jnp.minimum(i, back), 0, 0))],
                out_specs=pl.BlockSpec((tile, REP_B * LANE), qidx)),
            compiler_params=_params("parallel", "parallel", "parallel"),
            name="nsa_window" if fixed else "nsa_window_rowmax",
        )(bound.reshape(1), qcat, *([k] * nkt), *([v] * nkt), gates, masks)

    return lax.cond(bound <= SHIFT_MAX, lambda: call(True), lambda: call(False))


def _sum_proj_body(a_ref, b_ref, c_ref, x_ref, w_ref, o_ref):
    o = (a_ref[...] + b_ref[...] + c_ref[...]).astype(BF16)
    o_ref[...] = x_ref[...] + _dot(o, w_ref[...])


def _sum_proj(a, b, c, x, w):
    m, d = x.shape
    kdim = a.shape[1]
    tm = _row_tile(m)
    big = pl.BlockSpec((tm, kdim), lambda i: (i, 0))
    row = pl.BlockSpec((tm, d), lambda i: (i, 0))
    return pl.pallas_call(
        _sum_proj_body,
        out_shape=jax.ShapeDtypeStruct((m, d), F32),
        grid=(m // tm,),
        in_specs=[big, big, big, row, _const_spec(w.shape)],
        out_specs=row,
        compiler_params=_params("parallel"),
        name="nsa_out_proj",
    )(a, b, c, x, w)


def _nsa_weights(w_in, q_g, k_g, w_out):
    d = w_in.shape[0]
    nq = N_HEADS_B * HD_B
    w = N_KV_B * HD_B
    wq = w_in[:, :nq].reshape(d, N_HEADS_B, 1, HD_B)
    wq = jnp.broadcast_to(wq, (d, N_HEADS_B, 2, HD_B)).reshape(d, N_HEADS_B * LANE)
    wg = w_in[:, nq:nq + 3 * N_HEADS_B].reshape(d, N_KV_B, REP_B * 3)
    wg = jnp.pad(wg, ((0, 0), (0, 0), (0, LANE - REP_B * 3))).reshape(d, N_KV_B * LANE)
    wkv = w_in[:, nq + 3 * N_HEADS_B:]
    lane = jnp.arange(w)
    seg = (lane[:, None] // HD_B == lane[None, :] // HD_B).astype(BF16)
    spread = (lane[:, None] // HD_B * LANE + lane[:, None] % HD_B == jnp.arange(N_KV_B * LANE)[None, :]).astype(BF16)
    wo = jnp.pad(w_out.reshape(N_HEADS_B, HD_B, -1), ((0, 0), (0, LANE - HD_B), (0, 0)))
    return {
        "wq": wq.astype(BF16), "wg": wg.astype(BF16), "wkv": wkv.astype(BF16), "seg": seg, "spread": spread,
        "qg": jnp.tile(q_g, 2)[None, :], "kg": jnp.stack([jnp.tile(k_g[1], N_KV_B), jnp.tile(k_g[2], N_KV_B)]),
        "wo": wo.reshape(N_HEADS_B * LANE, -1).astype(BF16),
    }


def _cmp_weights(pe, w_c1, w_c2, kc_g):
    return (pe.reshape(2, 1, CMP_BLOCK * HD_B), w_c1.reshape(2, CMP_BLOCK * HD_B, HD_B).astype(BF16),
            w_c2.astype(BF16), kc_g[None, :])


def _nsa_prompt(x, g_mix, b, t, w_in, q_g, k_g, pe, w_c1, w_c2, w_out):
    wts = _nsa_weights(w_in, q_g, k_g, w_out)
    tm = _row_tile(t)
    pos = jnp.arange(t)
    qcat, gates, kv_t, cmp_rows, win_rows, ks_s, vs_s, kw_s, vw_s = _nsa_proj(
        x, g_mix, wts, _rope_tables(pos, LANE), _rope_tables(pos, HD_B), t // tm, tm, position_minor=True)
    kc_blk, vc_blk = _cmp_prompt(cmp_rows, b, t, *_cmp_weights(pe, w_c1, w_c2, k_g[0]))
    nb = t // CMP_BLOCK
    assert nb <= N_CMP_PAD and t % SEL_BLOCK == 0

    order = jnp.asarray(_cmp_block_order(), jnp.int32)

    def blocks(a, lo):
        a = jnp.pad(a, ((0, 0), (0, 0), (0, N_CMP_PAD - nb), (lo, LANE - HD_B - lo)))
        return a[:, :, order].astype(BF16)

    kct = blocks(kc_blk, HD_B)
    vct = blocks(vc_blk, 0).transpose(0, 1, 3, 2)
    o_cmp, mnot = _cmp_select(qcat, kct, vct, gates, b, t, min(CMP_SELECT_TILE, t))
    qmax = jnp.max(jnp.abs(q_g))
    o_sel = _flash(qcat, mnot, ks_s, vs_s, gates, qmax * jnp.max(jnp.abs(k_g[1])) * math.sqrt(HD_B), b, t,
                   *(min(n, t) for n in FLASH_SEL_TILES))
    o_win = _window_attn(qcat, kw_s, vw_s, gates, qmax * jnp.max(jnp.abs(k_g[2])) * math.sqrt(HD_B), b, t,
                         min(WINDOW_TILE, t))
    y = _sum_proj(o_cmp, o_sel, o_win, x, wts["wo"])
    wb = min(WINDOW, t)
    kv_out = kv_t.reshape(b, N_KV_SLOTS, N_KV_B, HD_B, t).transpose(0, 4, 1, 2, 3)
    win_out = win_rows.reshape(b, t, 2, N_KV_B, HD_B)[:, t - wb:]
    return y, kv_out, win_out


def _gmlp_layer(x, g, w_ins, ln_g, ln_b, w_s, b_s, w_outs, layer, *, single):
    gw = w_outs.shape[1] // N_GROUPS_A
    if single:
        ws = jnp.repeat(w_s[:, 0, 0], gw)[None, :]
        bs = jnp.repeat(b_s[:, 0], gw)[None, :]
    else:
        ws = w_s
        bs = jnp.repeat(b_s.T, gw, axis=1)
    return _gmlp(x, g[None, :], w_ins, ln_g[None, :], ln_b[None, :], ws, bs, w_outs, layer, single=single)


def _mlstm_proj_body(x_ref, g_ref, wq_ref, wk_ref, wv_ref, wgi_ref, wo_ref, bif_ref,
                     q_ref, k_ref, v_ref, gi_ref, og_ref):
    xb = _rms(x_ref[...], g_ref[...]).astype(BF16)
    q_ref[...] = _dot(xb, wq_ref[...]).astype(BF16)
    k_ref[...] = _dot(xb, wk_ref[...]).astype(BF16)
    v_ref[...] = _dot(xb, wv_ref[...]).astype(BF16)
    gi_ref[...] = _dot(xb, wgi_ref[...]) + bif_ref[...]
    og_ref[...] = jax.nn.sigmoid(_dot(xb, wo_ref[...]))


def _mlstm_proj(x, g, wts):
    m, d = x.shape
    hv = N_HEADS_C * DV_C
    tm = _row_tile(m)
    consts = [g, wts["wq"], wts["wk"], wts["wv"], wts["wgi"], wts["wo"], wts["bif"]]
    tile = lambda n: pl.BlockSpec((tm, n), lambda i: (i, 0))
    return pl.pallas_call(
        _mlstm_proj_body,
        out_shape=(jax.ShapeDtypeStruct((m, N_HEADS_C * LANE), BF16), jax.ShapeDtypeStruct((m, N_HEADS_C * LANE), BF16),
                   jax.ShapeDtypeStruct((m, hv), BF16), jax.ShapeDtypeStruct((m, LANE), F32),
                   jax.ShapeDtypeStruct((m, hv), F32)),
        grid=(m // tm,),
        in_specs=[tile(d)] + [_const_spec(c.shape) for c in consts],
        out_specs=(tile(N_HEADS_C * LANE), tile(N_HEADS_C * LANE), tile(hv), tile(LANE), tile(hv)),
        compiler_params=_params("parallel"),
        name="mlstm_proj",
    )(x, *consts)


def _mlstm_scan_body(q_ref, k_ref, v_ref, gi_ref, git_ref, hs_ref, c_out, n_out, m_out, c_s, n_s, m_s):
    c = pl.program_id(0)
    nseq, L = q_ref.shape[:2]

    @pl.when(c == 0)
    def _():
        c_s[...] = jnp.zeros(c_s.shape, F32)
        n_s[...] = jnp.zeros(n_s.shape, F32)
        m_s[...] = jnp.zeros(m_s.shape, F32)

    row = lax.broadcasted_iota(jnp.int32, (L, L), 0)
    col = lax.broadcasted_iota(jnp.int32, (L, L), 1)
    causal = col <= row
    tril = causal.astype(BF16)
    gi, git, bcol_all, brow_all = [], [], [], []
    for b in range(nseq):
        gi.append(gi_ref[b])
        git.append(git_ref[b])
        fcol = jax.nn.log_sigmoid(gi[b])
        frow = jax.nn.log_sigmoid(git[b][N_HEADS_C:, :])
        bcol_all.append(sum(_dot(tril, part) for part in _split3(fcol)))
        brow_all.append(sum(_dot_nt(part, tril) for part in _split3(frow)))
    units = [(b, h) for b in range(nseq) for h in range(N_HEADS_C)]
    idx = range(len(units))
    sl = [slice(h * LANE, (h + 1) * LANE) for _, h in units]
    q = [q_ref[b, :, sl[u]] for u, (b, _) in enumerate(units)]
    k = [k_ref[b, :, sl[u]] for u, (b, _) in enumerate(units)]
    v = [v_ref[b, :, sl[u]] for u, (b, _) in enumerate(units)]
    qk = [_dot_nt(q[u], k[u]) for u in idx]
    cq = [_dot_nt(q[u], c_s[u].astype(BF16)) for u in idx]
    bcol = [bcol_all[b][:, N_HEADS_C + h:N_HEADS_C + h + 1] for b, h in units]
    m_prev = [m_s[u:u + 1, 0:1] for u in idx]
    s, a, m_t = [], [], []
    for u, (b, h) in enumerate(units):
        dlog = jnp.where(causal, bcol[u] - brow_all[b][h:h + 1, :] + git[b][h:h + 1, :], -jnp.inf)
        inter = bcol[u] + m_prev[u]
        m_t.append(jnp.maximum(inter, jnp.max(dlog, axis=1, keepdims=True)))
        s.append(qk[u] * jnp.exp(dlog - m_t[u]))
        a.append(jnp.exp(inter - m_t[u]))
    sv = [_dot(s[u].astype(BF16), v[u]) for u in idx]
    wk, decay, m_new = [], [], []
    for u, (b, h) in enumerate(units):
        nq = jnp.sum(q[u].astype(F32) * n_s[u:u + 1, :], axis=1, keepdims=True)
        den = a[u] * nq + jnp.sum(s[u], axis=1, keepdims=True)
        hs_ref[b, :, sl[u]] = (a[u] * cq[u] + sv[u]) / jnp.maximum(jnp.abs(den), jnp.exp(-m_t[u]))
        b_end = bcol[u][L - 1:L, :]
        wlog = b_end - bcol[u] + gi[b][:, h:h + 1]
        m_new.append(jnp.maximum(b_end + m_prev[u], jnp.max(wlog, axis=0, keepdims=True)))
        wk.append(jnp.exp(wlog - m_new[u]))
        decay.append(jnp.exp(b_end + m_prev[u] - m_new[u]))
    upd = [_dot_tn((v[u].astype(F32) * wk[u]).astype(BF16), k[u]) for u in idx]
    for u in idx:
        c_s[u] = decay[u] * c_s[u] + upd[u]
        n_s[u:u + 1, :] = decay[u] * n_s[u:u + 1, :] + jnp.sum(k[u].astype(F32) * wk[u], axis=0, keepdims=True)
        m_s[u:u + 1, :] = jnp.broadcast_to(m_new[u], (1, LANE))

    @pl.when(c == pl.num_programs(0) - 1)
    def _():
        c_out[...] = c_s[...]
        n_out[...] = n_s[...]
        m_out[...] = m_s[...]


def _mlstm_scan(q, k, v, gi, git, b, t):
    L = math.gcd(t, CHUNK_C)
    nc = t // L
    hv = N_HEADS_C * DV_C
    units = b * N_HEADS_C
    seq = lambda a: a.reshape(b, t, a.shape[-1])
    tile = lambda n: pl.BlockSpec((b, L, n), lambda c: (0, c, 0))
    state = lambda *shape: pl.BlockSpec(shape, lambda c: (0,) * len(shape))
    hs, c_fin, n_fin, m_fin = pl.pallas_call(
        _mlstm_scan_body,
        out_shape=(jax.ShapeDtypeStruct((b, t, hv), F32), jax.ShapeDtypeStruct((units, DV_C, LANE), F32),
                   jax.ShapeDtypeStruct((units, LANE), F32), jax.ShapeDtypeStruct((units, LANE), F32)),
        grid=(nc,),
        in_specs=[tile(N_HEADS_C * LANE), tile(N_HEADS_C * LANE), tile(hv), tile(LANE),
                  pl.BlockSpec((b, 2 * N_HEADS_C, L), lambda c: (0, 0, c))],
        out_specs=(tile(hv), state(units, DV_C, LANE), state(units, LANE), state(units, LANE)),
        scratch_shapes=[pltpu.VMEM((units, DV_C, LANE), F32), pltpu.VMEM((units, LANE), F32),
                        pltpu.VMEM((units, LANE), F32)],
        compiler_params=_params("arbitrary"),
        name="mlstm_scan",
    )(seq(q), seq(k), seq(v), seq(gi), git)
    return (hs.reshape(b * t, hv), c_fin.reshape(b, N_HEADS_C, DV_C, LANE), n_fin.reshape(b, N_HEADS_C, LANE),
            m_fin.reshape(b, N_HEADS_C, LANE))


def _mlstm_out_body(hs_ref, og_ref, hg_ref, x_ref, w_ref, o_ref):
    parts = []
    for h in range(N_HEADS_C):
        sl = slice(h * DV_C, (h + 1) * DV_C)
        parts.append((og_ref[:, sl] * _rms(hs_ref[:, sl], hg_ref[:, sl])).astype(BF16))
    o_ref[...] = x_ref[...] + _dot(jnp.concatenate(parts, axis=1), w_ref[...])


def _mlstm_out(hs, og, hg, x, w):
    m, d = x.shape
    hv = hs.shape[1]
    tm = _row_tile(m)
    wide = pl.BlockSpec((tm, hv), lambda i: (i, 0))
    row = pl.BlockSpec((tm, d), lambda i: (i, 0))
    return pl.pallas_call(
        _mlstm_out_body,
        out_shape=jax.ShapeDtypeStruct((m, d), F32),
        grid=(m // tm,),
        in_specs=[wide, wide, _const_spec(hg.shape), row, _const_spec(w.shape)],
        out_specs=row,
        compiler_params=_params("parallel"),
        name="mlstm_out",
    )(hs, og, hg, x, w)


def _mlstm_weights(w_in, b_if):
    d = w_in.shape[0]
    hk, hv = N_HEADS_C * DK_C, N_HEADS_C * DV_C

    def spread(w):
        w = w.reshape(d, N_HEADS_C, DK_C)
        return jnp.pad(w, ((0, 0), (0, 0), (0, LANE - DK_C))).reshape(d, N_HEADS_C * LANE)

    wgi = jnp.pad(w_in[:, 2 * hk + hv:2 * hk + hv + 2 * N_HEADS_C], ((0, 0), (0, LANE - 2 * N_HEADS_C)))
    return {
        "wq": spread(w_in[:, :hk]).astype(BF16),
        "wk": (spread(w_in[:, hk:2 * hk]) * (DK_C ** -0.5)).astype(BF16),
        "wv": w_in[:, 2 * hk:2 * hk + hv].astype(BF16),
        "wgi": wgi.astype(BF16),
        "wo": w_in[:, 2 * hk + hv + 2 * N_HEADS_C:].astype(BF16),
        "bif": jnp.pad(b_if, (0, LANE - 2 * N_HEADS_C))[None, :],
    }


def _mlstm_prompt(x, g_mix, b, t, w_in, b_if, h_g, w_out):
    wts = _mlstm_weights(w_in, b_if)
    q, k, v, gi, og = _mlstm_proj(x, g_mix, wts)
    git = gi[:, :2 * N_HEADS_C].reshape(b, t, 2 * N_HEADS_C).transpose(0, 2, 1)
    hs, c, n, m = _mlstm_scan(q, k, v, gi, git, b, t)
    y = _mlstm_out(hs, og, h_g[None, :], x, w_out.astype(BF16))
    return y, c[..., :DK_C], n[..., :DK_C], m[..., 0]


def _proj_add_body(o_ref, x_ref, w_ref, out_ref):
    out_ref[...] = x_ref[...] + _dot(o_ref[...].astype(BF16), w_ref[...])


def _proj_add(o, x, w):
    m, d = x.shape
    tm = _row_tile(m)
    return pl.pallas_call(
        _proj_add_body,
        out_shape=jax.ShapeDtypeStruct((m, d), F32),
        grid=(m // tm,),
        in_specs=[pl.BlockSpec((tm, o.shape[1]), lambda i: (i, 0)), pl.BlockSpec((tm, d), lambda i: (i, 0)),
                  _const_spec(w.shape)],
        out_specs=pl.BlockSpec((tm, d), lambda i: (i, 0)),
        compiler_params=_params("parallel"),
        name="proj_add",
    )(o, x, w)


def _nsa_step_body(pt_ref, *refs, n_pages, page, past_len):
    pages = refs[:n_pages]
    (win_ref, qr_ref, qn_ref, gates_ref, nkv_ref, nwin_ref, pe_ref, w1_ref, w2_ref, kcg_ref,
     o_ref, c_s, x_s) = refs[n_pages:]
    w = N_KV_B * HD_B
    length = n_pages * page
    nb = length // CMP_BLOCK
    t = past_len
    for p in range(n_pages):
        for c in range(CMP_PLANES):
            c_s[c, p * page:(p + 1) * page, :] = pages[p][c * LANE:(c + 1) * LANE, :].T
    _flatten_cmp_blocks(lambda c, start: c_s[c, pl.ds(start, 8, stride=CMP_BLOCK), :], x_s, nb)
    lane_w = lax.broadcasted_iota(jnp.int32, (HD_B, w), 1)
    row_w = lax.broadcasted_iota(jnp.int32, (HD_B, w), 0)
    cmp_nat = []
    for slot in range(2):
        y = _compress_slot(x_s, slot, pe_ref, w1_ref, w2_ref, kcg_ref)
        nat = jnp.zeros((nb, w), F32)
        for g in range(N_KV_B):
            place = (lane_w == row_w + g * HD_B).astype(BF16)
            nat = nat + _dot(y[g * nb:(g + 1) * nb].astype(BF16), place)
        cmp_nat.append(nat.astype(BF16))
    kc, vc = cmp_nat
    qr = qr_ref[...]
    qn = qn_ref[...]
    gates = gates_ref[...]
    nh = N_HEADS_B
    blk = lax.broadcasted_iota(jnp.int32, (nh, nb), 1)
    p_c = _masked_softmax(_dot_nt(qn, kc), (blk + 1) * CMP_BLOCK - 1 <= t, 1)
    o_c = _dot(p_c.astype(BF16), vc)
    blk_t = lax.broadcasted_iota(jnp.int32, (nb, nh), 0)
    p_t = _masked_softmax(_dot_nt(kc, qn), (blk_t + 1) * CMP_BLOCK - 1 <= t, 0)
    gsum = (lax.broadcasted_iota(jnp.int32, (nh, LANE), 0) // REP_B
            == lax.broadcasted_iota(jnp.int32, (nh, LANE), 1)).astype(BF16)
    pair = (lax.broadcasted_iota(jnp.int32, (nb, nb), 1) // (SEL_BLOCK // CMP_BLOCK)
            == lax.broadcasted_iota(jnp.int32, (nb, nb), 0)).astype(BF16)
    imp = sum(_dot(part, gsum) for part in _split3(p_t))
    imp = sum(_dot(pair, part) for part in _split3(imp))
    sblk = lax.broadcasted_iota(jnp.int32, (nb, LANE), 0)
    cur = t // SEL_BLOCK
    forced = (sblk == 0) | (sblk == cur) | (sblk == cur - 1)
    score = jnp.where(forced, jnp.inf, jnp.where(sblk * SEL_BLOCK <= t, imp, -jnp.inf))
    sblk_f = sblk.astype(F32)
    pickable = score > -jnp.inf
    for _ in range(N_SEL):
        mx = jnp.max(score, axis=0, keepdims=True)
        first = jnp.min(jnp.where(score == mx, sblk_f, float(nb)), axis=0, keepdims=True)
        score = jnp.where(sblk_f == first, -jnp.inf, score)
    notsel = jnp.where(pickable, jnp.where(score > -jnp.inf, -MASK_BIG, 0.0), -MASK_BIG)
    bias = _dot_nt(gsum, notsel.astype(BF16)).astype(BF16)
    expand = (lax.broadcasted_iota(jnp.int32, (nb, length), 1) // SEL_BLOCK
              == lax.broadcasted_iota(jnp.int32, (nb, length), 0)).astype(BF16)
    nkv = nkv_ref[...]
    nwin = nwin_ref[...]

    def attend(s, k_new, v_new, weighted_values):
        s_new = jnp.sum(qr.astype(F32) * k_new.astype(BF16).astype(F32), axis=1, keepdims=True)
        m = jnp.maximum(jnp.max(s, axis=1, keepdims=True), s_new)
        e = jnp.exp(s - m)
        e_new = jnp.exp(s_new - m)
        den = jnp.sum(e, axis=1, keepdims=True) + e_new
        num = weighted_values(e.astype(BF16)) + e_new.astype(BF16).astype(F32) * v_new.astype(BF16).astype(F32)
        return num / den

    s_sel = jnp.concatenate([_dot(qr, pages[p][2 * w:3 * w, :].astype(BF16)) for p in range(n_pages)], axis=1)
    o_s = attend(s_sel + _dot(bias, expand), nkv[:, 2 * w:3 * w], nkv[:, 3 * w:],
                 lambda e: sum(_dot_nt(e[:, p * page:(p + 1) * page], pages[p][3 * w:, :].astype(BF16))
                               for p in range(n_pages)))
    wb = win_ref.shape[1]
    pos_w = t - wb + lax.broadcasted_iota(jnp.int32, (nh, wb), 1)
    ok_w = (pos_w >= 0) & (t - pos_w <= WINDOW)
    s_w = jnp.where(ok_w, _dot(qr, win_ref[:w, :].astype(BF16)), -MASK_BIG)
    o_w = attend(s_w, nwin[:, :w], nwin[:, w:], lambda e: _dot_nt(e, win_ref[w:, :].astype(BF16)))
    o_ref[...] = gates[:, 0:1] * o_c + gates[:, 1:2] * o_s + gates[:, 2:3] * o_w


def _nsa_step(page_table, cache, win_cache, qr, qn, gates, new_kv, new_win, pe, w1, w2, kc_g, past_len):
    bsz, n_pages = page_table.shape
    page = cache.shape[2]
    assert page == LANE
    w = N_KV_B * HD_B
    nb = n_pages * page // CMP_BLOCK
    wb = win_cache.shape[2]
    per = lambda shape: pl.BlockSpec((None,) + shape, lambda b, pt: (b,) + (0,) * len(shape))
    const = lambda a: pl.BlockSpec(a.shape, lambda b, pt: (0,) * a.ndim)
    page_specs = [pl.BlockSpec((None, 4 * w, page), lambda b, pt, p=p: (pt[b, p], 0, 0)) for p in range(n_pages)]
    return pl.pallas_call(
        functools.partial(_nsa_step_body, n_pages=n_pages, page=page, past_len=past_len),
        out_shape=jax.ShapeDtypeStruct((bsz, N_HEADS_B, w), F32),
        grid_spec=pltpu.PrefetchScalarGridSpec(
            num_scalar_prefetch=1,
            grid=(bsz,),
            in_specs=page_specs + [per((2 * w, wb)), per((N_HEADS_B, w)), per((N_HEADS_B, w)), per((N_HEADS_B, LANE)),
                                   per((1, 4 * w)), per((1, 2 * w)), const(pe), const(w1), const(w2), const(kc_g)],
            out_specs=per((N_HEADS_B, w)),
            scratch_shapes=[pltpu.VMEM((2 * w // LANE, n_pages * page, LANE), F32),
                            pltpu.VMEM((2 * N_KV_B, nb, CMP_BLOCK * HD_B), F32)]),
        compiler_params=_params("parallel"),
        name="nsa_step",
    )(page_table, *([cache] * n_pages), win_cache, qr, qn, gates, new_kv, new_win, pe, w1, w2, kc_g)


def _nsa_sample_step(x, g_mix, past_len, kv_cache, win_cache, page_table, w_in, q_g, k_g, pe, w_c1, w_c2, w_out):
    bsz, d = x.shape
    w = N_KV_B * HD_B
    assert past_len % CMP_BLOCK == 0 and past_len // SEL_BLOCK + 1 <= past_len // CMP_BLOCK
    wts = _nsa_weights(w_in, q_g, k_g, w_out)
    pos = jnp.full((bsz,), past_len, jnp.int32)
    qcat, gates, kv_rows, win_rows, _, _, _, _ = _nsa_proj(
        x, g_mix, wts, _rope_tables(pos, LANE), _rope_tables(pos, HD_B), 1, bsz, position_minor=False)
    q5 = qcat.reshape(bsz, N_KV_B, REP_B, 2, HD_B)
    eye = jnp.eye(N_KV_B, dtype=BF16)
    qrows = (q5[:, :, :, :, None, :] * eye[None, :, None, None, :, None])
    qr = qrows[:, :, :, 0].reshape(bsz, N_HEADS_B, w)
    qn = qrows[:, :, :, 1].reshape(bsz, N_HEADS_B, w)
    gts = gates.reshape(bsz, N_KV_B, LANE)[:, :, :REP_B * 3].reshape(bsz, N_HEADS_B, 3)
    gts = jnp.pad(gts, ((0, 0), (0, 0), (0, LANE - 3)))
    pool, page = kv_cache.shape[:2]
    cache_t = kv_cache.reshape(pool, page, 4 * w).transpose(0, 2, 1)
    win_t = win_cache.reshape(bsz, -1, 2 * w).transpose(0, 2, 1)
    o = _nsa_step(page_table, cache_t, win_t, qr, qn, gts,
                  kv_rows.reshape(bsz, 1, 4 * w), win_rows.reshape(bsz, 1, 2 * w),
                  *_cmp_weights(pe, w_c1, w_c2, k_g[0]), past_len)
    own = (jnp.arange(N_HEADS_B)[:, None] // REP_B == jnp.arange(N_KV_B)[None, :]).astype(F32)
    w_exp = own[:, :, None, None] * w_out.reshape(N_HEADS_B, 1, HD_B, d)
    y = _proj_add(o.reshape(bsz, N_HEADS_B * w), x, w_exp.reshape(N_HEADS_B * w, d).astype(BF16))
    return y, kv_rows.reshape(bsz, 1, N_KV_SLOTS, N_KV_B, HD_B), win_rows.reshape(bsz, 1, 2, N_KV_B, HD_B)


def _mlstm_step_body(q_ref, k_ref, qt_ref, kt_ref, v_ref, gi_ref, ct_ref, n_ref, m_ref,
                     h_ref, cto_ref, no_ref, mo_ref, *, sb):
    gi = gi_ref[...]
    logf = jax.nn.log_sigmoid(gi)
    m_all = m_ref[...]
    lane_m = lax.broadcasted_iota(jnp.int32, (1, N_HEADS_C), 1)
    for s in range(sb):
        m_new_row = jnp.zeros((1, N_HEADS_C), F32)
        for h in range(N_HEADS_C):
            q = q_ref[s:s + 1, h * LANE:h * LANE + DK_C].astype(F32)
            k = k_ref[s:s + 1, h * LANE:h * LANE + DK_C].astype(F32)
            qc = qt_ref[s, :, h:h + 1]
            kc = kt_ref[s, :, h:h + 1]
            v = v_ref[s:s + 1, h * DV_C:(h + 1) * DV_C].astype(F32)
            ct = ct_ref[s, h]
            n = n_ref[s, h:h + 1, :]
            it = gi[s:s + 1, h:h + 1]
            b = logf[s:s + 1, N_HEADS_C + h:N_HEADS_C + h + 1]
            m0 = m_all[s:s + 1, h:h + 1]
            inter = b + m0
            m_t = jnp.maximum(inter, it)
            wgt = jnp.exp(it - m_t)
            a = jnp.exp(inter - m_t)
            sc = jnp.sum(q * k, axis=1, keepdims=True) * wgt
            num = a * jnp.sum(ct * qc, axis=0, keepdims=True) + sc * v
            den = a * jnp.sum(n * q, axis=1, keepdims=True) + sc
            h_ref[s:s + 1, h * DV_C:(h + 1) * DV_C] = num / jnp.maximum(jnp.abs(den), jnp.exp(-m_t))
            cto_ref[s, h] = a * ct + (wgt * kc) * v
            no_ref[s, h:h + 1, :] = a * n + wgt * k
            m_new_row = jnp.where(lane_m == h, m_t, m_new_row)
        mo_ref[s:s + 1, :] = m_new_row


def _mlstm_step(q, k, qt, kt, v, gi, ct0, n0, m0):
    bsz = q.shape[0]
    sb = 8
    row = lambda n: pl.BlockSpec((sb, n), lambda i: (i, 0))
    c_spec = pl.BlockSpec((sb, N_HEADS_C, DK_C, DV_C), lambda i: (i, 0, 0, 0))
    n_spec = pl.BlockSpec((sb, N_HEADS_C, DK_C), lambda i: (i, 0, 0))
    col_spec = pl.BlockSpec((sb, DK_C, N_HEADS_C), lambda i: (i, 0, 0))
    hv = N_HEADS_C * DV_C
    return pl.pallas_call(
        functools.partial(_mlstm_step_body, sb=sb),
        out_shape=(jax.ShapeDtypeStruct((bsz, hv), F32), jax.ShapeDtypeStruct(ct0.shape, F32),
                   jax.ShapeDtypeStruct(n0.shape, F32), jax.ShapeDtypeStruct(m0.shape, F32)),
        grid=(bsz // sb,),
        in_specs=[row(N_HEADS_C * LANE), row(N_HEADS_C * LANE), col_spec, col_spec, row(hv), row(LANE),
                  c_spec, n_spec, row(N_HEADS_C)],
        out_specs=(row(hv), c_spec, n_spec, row(N_HEADS_C)),
        compiler_params=_params("parallel"),
        name="mlstm_step",
    )(q, k, qt, kt, v, gi, ct0, n0, m0)


def _mlstm_sample_step(x, g_mix, c0, n0, m0, w_in, b_if, h_g, w_out):
    bsz = x.shape[0]
    wts = _mlstm_weights(w_in, b_if)
    q, k, v, gi, og = _mlstm_proj(x, g_mix, wts)
    cols = lambda a: a.astype(F32).reshape(bsz, N_HEADS_C, LANE)[:, :, :DK_C].transpose(0, 2, 1)
    hs, ct, n, m = _mlstm_step(q, k, cols(q), cols(k), v, gi, c0.transpose(0, 1, 3, 2), n0, m0)
    y = _mlstm_out(hs, og, h_g[None, :], x, w_out.astype(BF16))
    return y, ct.transpose(0, 1, 3, 2), n, m


def kernel(x_prompt, x_sample, cache_nsa_kv, cache_nsa_win, state_mlstm_C, state_mlstm_n, state_mlstm_m, page_table,
           norm_mix_g, norm_ffn_g, ffn_w1, ffn_w2, a_w_in, a_ln_g, a_ln_b, a_w_s, a_b_s, a_w_out,
           b_w_in, b_q_g, b_k_g, b_pe, b_w_c1, b_w_c2, b_w_out, c_w_in, c_b_if, c_h_g, c_w_out):
    bp, t, d = x_prompt.shape
    bs, ts, _ = x_sample.shape
    assert ts == 1
    past_len = page_table.shape[1] * cache_nsa_kv.shape[2]
    xp = x_prompt.reshape(bp * t, d)
    xs = x_sample.reshape(bs, d)
    out = {k: [] for k in ("v_s", "kv_p", "win_p", "kv_s", "win_s", "C_p", "n_p", "m_p", "C_s", "n_s", "m_s")}
    ffn_w1_b, ffn_w2_b = ffn_w1.astype(BF16), ffn_w2.astype(BF16)
    a_w_in_b, a_w_out_b = a_w_in.astype(BF16), a_w_out.astype(BF16)
    for layer in range(norm_mix_g.shape[0]):
        kind, j = layer % 3, layer // 3
        gm = norm_mix_g[layer]
        if kind == 0:
            args = (a_w_in_b, a_ln_g[j], a_ln_b[j], a_w_s[j], a_b_s[j], a_w_out_b, j)
            xp = _gmlp_layer(xp, gm, *args, single=False)[0]
            xs, v = _gmlp_layer(xs, gm, *args, single=True)
            out["v_s"].append(v.reshape(bs, ts, -1))
        elif kind == 1:
            args = (b_w_in[j], b_q_g[j], b_k_g[j], b_pe[j], b_w_c1[j], b_w_c2[j], b_w_out[j])
            xp, kv, win = _nsa_prompt(xp, gm[None, :], bp, t, *args)
            out["kv_p"].append(kv)
            out["win_p"].append(win)
            xs, kv, win = _nsa_sample_step(xs, gm[None, :], past_len, cache_nsa_kv[j], cache_nsa_win[j], page_table,
                                           *args)
            out["kv_s"].append(kv)
            out["win_s"].append(win)
        else:
            args = (c_w_in[j], c_b_if[j], c_h_g[j], c_w_out[j])
            xp, c, n, m = _mlstm_prompt(xp, gm[None, :], bp, t, *args)
            out["C_p"].append(c)
            out["n_p"].append(n)
            out["m_p"].append(m)
            xs, c, n, m = _mlstm_sample_step(xs, gm[None, :], state_mlstm_C[j], state_mlstm_n[j], state_mlstm_m[j],
                                             *args)
            out["C_s"].append(c)
            out["n_s"].append(n)
            out["m_s"].append(m)
        gf = norm_ffn_g[layer][None, :]
        xp = _ffn(xp, gf, ffn_w1_b, ffn_w2_b, layer)
        xs = _ffn(xs, gf, ffn_w1_b, ffn_w2_b, layer)
    st = {k: jnp.stack(v) for k, v in out.items()}
    return (xp.reshape(bp, t, d), xs.reshape(bs, ts, d), st["v_s"], st["kv_p"], st["win_p"], st["kv_s"], st["win_s"],
            st["C_p"], st["n_p"], st["m_p"], st["C_s"], st["n_s"], st["m_s"])
```

```python
import functools
import math

import jax
import jax.numpy as jnp
from jax import lax
from jax.experimental import pallas as pl
from jax.experimental.pallas import tpu as pltpu

F32 = jnp.float32
BF16 = jnp.bfloat16

EPS = 1e-6
CHUNK_A = 128
N_GROUPS_A = 8
N_HEADS_B = 16
N_KV_B = 4
REP_B = N_HEADS_B // N_KV_B
HD_B = 64
ROT_DIM = 16
ROPE_THETA = 500000.0
CMP_BLOCK = 32
SEL_BLOCK = 64
N_SEL = 16
WINDOW = 512
N_KV_SLOTS = 4
N_HEADS_C = 8
DK_C = 64
DV_C = 128
CHUNK_C = 128
SCALE_B = HD_B ** -0.5

LANE = 128
VMEM_LIMIT_BYTES = 56 * 1024 * 1024
MASK_BIG = 1e30
M_INIT = -1e20
N_SEL_PAD = 128
N_CMP_PAD = 2 * N_SEL_PAD
SHIFT_MAX = 40.0
FLASH_SEL_TILES = (1024, 1024)
WINDOW_TILE = 256
CMP_SELECT_TILE = 512


def _params(*sem):
    return pltpu.CompilerParams(dimension_semantics=sem, vmem_limit_bytes=VMEM_LIMIT_BYTES)


def _dot(a, b):
    return jnp.dot(a, b, preferred_element_type=F32)


def _dot_nt(a, b):
    return lax.dot_general(a, b, (((1,), (1,)), ((), ())), preferred_element_type=F32)


def _dot_tn(a, b):
    return lax.dot_general(a, b, (((0,), (0,)), ((), ())), preferred_element_type=F32)


def _rms(x, g):
    return x * lax.rsqrt(jnp.mean(x * x, axis=-1, keepdims=True) + EPS) * g


def _split3(x):
    a = x.astype(BF16)
    r = x - a.astype(F32)
    b = r.astype(BF16)
    c = (r - b.astype(F32)).astype(BF16)
    return a, b, c


def _const_spec(shape):
    n = len(shape)
    return pl.BlockSpec(shape, lambda *_: (0,) * n)


ROW_TILE = 512


def _row_tile(m, pref=ROW_TILE):
    t = min(pref, m)
    while m % t:
        t //= 2
    return t


def _ffn_body(x_ref, g_ref, w1_ref, w2_ref, o_ref, *, ck):
    x = x_ref[...]
    xb = _rms(x, g_ref[...]).astype(BF16)
    acc = x
    for j in range(w1_ref.shape[1] // ck):
        h = jnp.maximum(_dot(xb, w1_ref[:, j * ck:(j + 1) * ck]), 0.0)
        acc = acc + _dot((h * h).astype(BF16), w2_ref[j * ck:(j + 1) * ck, :])
    o_ref[...] = acc


def _layer_spec(stacked, layer):
    n = stacked.ndim - 1
    return pl.BlockSpec((None,) + stacked.shape[1:], lambda *_: (layer,) + (0,) * n)


def _ffn(x, g, w1s, w2s, layer):
    m, d = x.shape
    tm = _row_tile(m)
    return pl.pallas_call(
        functools.partial(_ffn_body, ck=1024),
        out_shape=jax.ShapeDtypeStruct((m, d), F32),
        grid=(m // tm,),
        in_specs=[pl.BlockSpec((tm, d), lambda i: (i, 0)), _const_spec(g.shape),
                  _layer_spec(w1s, layer), _layer_spec(w2s, layer)],
        out_specs=pl.BlockSpec((tm, d), lambda i: (i, 0)),
        compiler_params=_params("parallel"),
        name="ffn",
    )(x, g, w1s, w2s)


def _gmlp_body(x_ref, g_ref, win_ref, lng_ref, lnb_ref, ws_ref, bs_ref, wout_ref, o_ref, *maybe_v_ref, single):
    x = x_ref[...]
    dg = lng_ref.shape[1]
    xb = _rms(x, g_ref[...]).astype(BF16)
    u = jax.nn.gelu(_dot(xb, win_ref[:, :dg]))
    v = jax.nn.gelu(_dot(xb, win_ref[:, dg:]))
    mu = jnp.mean(v, axis=-1, keepdims=True)
    vc = v - mu
    var = jnp.mean(vc * vc, axis=-1, keepdims=True)
    v = vc * lax.rsqrt(var + EPS) * lng_ref[...] + lnb_ref[...]
    if single:
        maybe_v_ref[0][...] = v
        gate = v * ws_ref[...] + bs_ref[...]
    else:
        gw = dg // N_GROUPS_A
        row = lax.broadcasted_iota(jnp.int32, (CHUNK_A, CHUNK_A), 0)
        col = lax.broadcasted_iota(jnp.int32, (CHUNK_A, CHUNK_A), 1)
        causal = col <= row
        vb = v.astype(BF16)
        chunks = []
        for c in range(x.shape[0] // CHUNK_A):
            parts = []
            for gi in range(N_GROUPS_A):
                w = jnp.where(causal, ws_ref[gi], 0.0).astype(BF16)
                parts.append(_dot(w, vb[c * CHUNK_A:(c + 1) * CHUNK_A, gi * gw:(gi + 1) * gw]))
            chunks.append(jnp.concatenate(parts, axis=1) + bs_ref[...])
        gate = jnp.concatenate(chunks, axis=0)
    o_ref[...] = x + _dot((u * gate).astype(BF16), wout_ref[...])


def _gmlp(x, g, w_ins, ln_g, ln_b, ws, bs, w_outs, layer, *, single):
    m, d = x.shape
    dg = w_outs.shape[1]
    tm = _row_tile(m)
    n_out = 2 if single else 1
    outs = pl.pallas_call(
        functools.partial(_gmlp_body, single=single),
        out_shape=(jax.ShapeDtypeStruct((m, d), F32), jax.ShapeDtypeStruct((m, dg), F32))[:n_out],
        grid=(m // tm,),
        in_specs=[pl.BlockSpec((tm, d), lambda i: (i, 0)), _const_spec(g.shape), _layer_spec(w_ins, layer),
                  _const_spec(ln_g.shape), _const_spec(ln_b.shape), _const_spec(ws.shape), _const_spec(bs.shape),
                  _layer_spec(w_outs, layer)],
        out_specs=(pl.BlockSpec((tm, d), lambda i: (i, 0)), pl.BlockSpec((tm, dg), lambda i: (i, 0)))[:n_out],
        compiler_params=_params("parallel"),
        name="gmlp_single" if single else "gmlp",
    )(x, g, w_ins, ln_g, ln_b, ws, bs, w_outs)
    return outs if single else (outs[0], None)


def _rope_tables(pos, seg):
    half = ROT_DIM // 2
    freq = jnp.power(ROPE_THETA, -jnp.arange(half, dtype=F32) * 2.0 / ROT_DIM)
    ang = pos.astype(F32)[:, None] * freq[None, :]
    cos, sin = jnp.cos(ang), jnp.sin(ang)
    t = pos.shape[0]
    one = jnp.ones((t, seg - ROT_DIM), F32)
    zero = jnp.zeros((t, seg - ROT_DIM), F32)
    z8 = jnp.zeros((t, half), F32)
    tabs = [jnp.concatenate([cos, cos, one], 1), jnp.concatenate([-sin, z8, zero], 1),
            jnp.concatenate([z8, sin, zero], 1)]
    return jnp.stack([jnp.tile(a, (1, LANE // seg)) for a in tabs])


def _rope128(x, tab):
    return x * tab[0] + pltpu.roll(x, LANE - ROT_DIM // 2, 1) * tab[1] + pltpu.roll(x, ROT_DIM // 2, 1) * tab[2]


def _nsa_proj_body(x_ref, g_ref, wq_ref, wg_ref, wkv_ref, seg_ref, spread_ref, qg_ref, kg_ref, tq_ref, tk_ref,
                   qcat_ref, gates_ref, *out_refs, position_minor):
    if position_minor:
        kvt_ref, cmp_ref, win_ref, ks_ref, vs_ref, kw_ref, vw_ref = out_refs
    else:
        kv_ref, win_ref, ks_ref, vs_ref, kw_ref, vw_ref = out_refs
    x = x_ref[...]
    xb = _rms(x, g_ref[...]).astype(BF16)
    tq = tq_ref[...]
    tk = tk_ref[...]
    qg = qg_ref[...]
    for h in range(N_HEADS_B):
        q = _dot(xb, wq_ref[:, h * LANE:(h + 1) * LANE])
        ms = jnp.sum(q * q, axis=-1, keepdims=True) * (1.0 / LANE)
        qn = q * lax.rsqrt(ms + EPS) * qg
        qcat_ref[:, h * LANE:(h + 1) * LANE] = (_rope128(qn, tq) * SCALE_B).astype(BF16)
    gates_ref[...] = jax.nn.sigmoid(_dot(xb, wg_ref[...]))
    kv = _dot(xb, wkv_ref[...])
    w = N_KV_B * HD_B
    seg = seg_ref[...]
    spread = spread_ref[...]

    def head_norm(k, gain):
        k2 = k * k
        hi = k2.astype(BF16)
        lo = (k2 - hi.astype(F32)).astype(BF16)
        ss = _dot(hi, seg) + _dot(lo, seg)
        return k * lax.rsqrt(ss * (1.0 / HD_B) + EPS) * gain

    def rope(k):
        return jnp.concatenate([_rope128(k[:, j * LANE:(j + 1) * LANE], tk) for j in range(w // LANE)], axis=1)

    ks = rope(head_norm(kv[:, 2 * w:3 * w], kg_ref[0:1, :]))
    kw = rope(head_norm(kv[:, 4 * w:5 * w], kg_ref[1:2, :]))
    vs = kv[:, 3 * w:4 * w]
    vw = kv[:, 5 * w:6 * w]
    kv_rows = jnp.concatenate([kv[:, :2 * w], ks, vs], axis=1)
    if position_minor:
        kvt_ref[...] = kv_rows.T
        cmp_ref[...] = kv[:, :2 * w]
    else:
        kv_ref[...] = kv_rows
    win_ref[:, :w] = kw
    win_ref[:, w:] = vw
    lane = lax.broadcasted_iota(jnp.int32, (1, N_KV_B * LANE), 1)
    ones_hi = ((lane & HD_B) != 0).astype(F32)
    ks_ref[...] = _dot(ks.astype(BF16), spread).astype(BF16)
    kw_ref[...] = _dot(kw.astype(BF16), spread).astype(BF16)
    vs_ref[...] = (_dot(vs.astype(BF16), spread) + ones_hi).astype(BF16)
    vw_ref[...] = (_dot(vw.astype(BF16), spread) + ones_hi).astype(BF16)


def _nsa_proj(x, g, wts, tab_q, tab_k, n_tab_tiles, tm, *, position_minor):
    m, d = x.shape
    w = N_KV_B * HD_B
    ws = N_KV_B * LANE
    tile = lambda n: pl.BlockSpec((tm, n), lambda i: (i, 0))
    tab = pl.BlockSpec((3, tm, LANE), lambda i: (0, i % n_tab_tiles, 0))
    consts = [g, wts["wq"], wts["wg"], wts["wkv"], wts["seg"], wts["spread"], wts["qg"], wts["kg"]]
    sds = jax.ShapeDtypeStruct
    if position_minor:
        seq = n_tab_tiles * tm
        kv_shapes = (sds((m // seq, 4 * w, seq), F32), sds((m, 2 * w), F32))
        kv_specs = (pl.BlockSpec((None, 4 * w, tm), lambda i: (i // n_tab_tiles, 0, i % n_tab_tiles)), tile(2 * w))
    else:
        kv_shapes = (sds((m, 4 * w), F32),)
        kv_specs = (tile(4 * w),)
    slab = sds((m, ws), BF16)
    return pl.pallas_call(
        functools.partial(_nsa_proj_body, position_minor=position_minor),
        out_shape=(sds((m, N_HEADS_B * LANE), BF16), sds((m, ws), F32)) + kv_shapes
                  + (sds((m, 2 * w), F32), slab, slab, slab, slab),
        grid=(m // tm,),
        in_specs=[tile(d)] + [_const_spec(c.shape) for c in consts] + [tab, tab],
        out_specs=(tile(N_HEADS_B * LANE), tile(ws)) + kv_specs + (tile(2 * w), tile(ws), tile(ws), tile(ws), tile(ws)),
        compiler_params=_params("parallel"),
        name="nsa_proj",
    )(x, *consts, tab_q, tab_k)


CMP_PLANES = 2 * N_KV_B * HD_B // LANE


def _flatten_cmp_blocks(load_rows, x_s, n_blocks):
    for ng in range(n_blocks // 8):
        for l in range(CMP_BLOCK):
            for c in range(CMP_PLANES):
                rows = load_rows(c, ng * 8 * CMP_BLOCK + l)
                for half in range(LANE // HD_B):
                    x_s[c * (LANE // HD_B) + half, ng * 8:(ng + 1) * 8, l * HD_B:(l + 1) * HD_B] = (
                        rows[:, half * HD_B:(half + 1) * HD_B])


def _compress_slot(x_s, slot, pe_ref, w1_ref, w2_ref, kcg_ref):
    n_blocks = x_s.shape[1]
    xs = x_s[slot * N_KV_B:(slot + 1) * N_KV_B].reshape(N_KV_B * n_blocks, CMP_BLOCK * HD_B)
    hid = jax.nn.gelu(_dot((xs + pe_ref[slot]).astype(BF16), w1_ref[slot]))
    y = _dot(hid.astype(BF16), w2_ref[slot])
    return _rms(y, kcg_ref[...]) if slot == 0 else y


def _cmp_prompt_body(*refs, n_blocks):
    planes = refs[:CMP_PLANES]
    pe_ref, w1_ref, w2_ref, kcg_ref, kc_ref, vc_ref, x_s = refs[CMP_PLANES:]
    _flatten_cmp_blocks(lambda c, start: planes[c][pl.ds(start, 8, stride=CMP_BLOCK), :], x_s, n_blocks)
    kc_ref[...] = _compress_slot(x_s, 0, pe_ref, w1_ref, w2_ref, kcg_ref).reshape(N_KV_B, n_blocks, HD_B)
    vc_ref[...] = _compress_slot(x_s, 1, pe_ref, w1_ref, w2_ref, kcg_ref).reshape(N_KV_B, n_blocks, HD_B)


def _cmp_prompt(kv_rows, b, t, pe, w1, w2, kc_g):
    tt = min(t, 2048)
    nbk = tt // CMP_BLOCK
    assert t % tt == 0 and nbk % 8 == 0
    steps = t // tt
    out = jax.ShapeDtypeStruct((b, N_KV_B, t // CMP_BLOCK, HD_B), F32)
    out_spec = pl.BlockSpec((None, N_KV_B, nbk, HD_B), lambda bi, i: (bi, 0, i, 0))
    plane_specs = [pl.BlockSpec((tt, LANE), lambda bi, i, c=c: (bi * steps + i, c)) for c in range(CMP_PLANES)]
    return pl.pallas_call(
        functools.partial(_cmp_prompt_body, n_blocks=nbk),
        out_shape=(out, out),
        grid=(b, steps),
        in_specs=plane_specs + [_const_spec(pe.shape), _const_spec(w1.shape), _const_spec(w2.shape),
                                _const_spec(kc_g.shape)],
        out_specs=(out_spec, out_spec),
        scratch_shapes=[pltpu.VMEM((2 * N_KV_B, nbk, CMP_BLOCK * HD_B), F32)],
        compiler_params=_params("parallel", "parallel"),
        name="nsa_compress",
    )(*([kv_rows] * CMP_PLANES), pe, w1, w2, kc_g)


def _masked_softmax(s, mask, axis):
    sm = jnp.where(mask, s, -jnp.inf)
    mx = jnp.max(sm, axis=axis, keepdims=True)
    mx = jnp.where(mx > -jnp.inf, mx, 0.0)
    e = jnp.where(mask, jnp.exp(s - mx), 0.0)
    return e / jnp.maximum(jnp.sum(e, axis=axis, keepdims=True), 1e-30)


CMP_SEGMENTS = (64, 64, 128)


def _cmp_block_order():
    order, base = [], 0
    for size in CMP_SEGMENTS:
        order += list(range(base, base + size, 2)) + list(range(base + 1, base + size, 2))
        base += size
    assert base == N_CMP_PAD
    return order


def _cmp_select_body(q_ref, kct_ref, vct_ref, gates_ref, o_ref, mnot_ref, *, tq):
    i = pl.program_id(2)
    q0 = i * tq
    gates = gates_ref[...]

    def run(nv):
        hv = nv // 2
        row = lax.broadcasted_iota(jnp.int32, (nv, tq), 0)
        tok = q0 + lax.broadcasted_iota(jnp.int32, (nv, tq), 1)
        blk_c = jnp.zeros((nv, tq), jnp.int32)
        base = 0
        for size in CMP_SEGMENTS:
            if base < nv:
                local = row - base
                seg_blk = base + jnp.where(local < size // 2, 2 * local, 2 * (local - size // 2) + 1)
                blk_c = jnp.where((row >= base) & (row < base + size), seg_blk, blk_c)
            base += size
        mask = (blk_c + 1) * CMP_BLOCK - 1 <= tok
        kct = kct_ref[:nv, :]
        vct = vct_ref[:, :nv]
        imp_parts = None
        for r in range(REP_B):
            qh = q_ref[:, r * LANE:(r + 1) * LANE]
            pt = _masked_softmax(_dot_nt(kct, qh), mask, 0)
            o_ref[:, r * LANE:(r + 1) * LANE] = _dot(vct, pt.astype(BF16)).T * gates[:, 3 * r:3 * r + 1]
            parts, base = [], 0
            for size in CMP_SEGMENTS:
                if base < nv:
                    parts.append(pt[base:base + size // 2] + pt[base + size // 2:base + size])
                base += size
            imp_parts = parts if imp_parts is None else [a + b for a, b in zip(imp_parts, parts)]
        imp = jnp.concatenate(imp_parts, axis=0)
        blk = lax.broadcasted_iota(jnp.int32, (hv, tq), 0)
        t_s = q0 + lax.broadcasted_iota(jnp.int32, (hv, tq), 1)
        cur = t_s // SEL_BLOCK
        forced = (blk == 0) | (blk == cur) | (blk == cur - 1)
        score = jnp.where(forced, jnp.inf, jnp.where(blk * SEL_BLOCK <= t_s, imp, -jnp.inf))
        blk_f = blk.astype(F32)
        pickable = score > -jnp.inf
        for _ in range(N_SEL):
            mx = jnp.max(score, axis=0, keepdims=True)
            first = jnp.min(jnp.where(score == mx, blk_f, float(N_SEL_PAD)), axis=0, keepdims=True)
            score = jnp.where(blk_f == first, -jnp.inf, score)
        mnot = jnp.where(pickable, jnp.where(score > -jnp.inf, 1.0, 0.0), 1.0)
        if hv < N_SEL_PAD:
            mnot = jnp.concatenate([mnot, jnp.ones((N_SEL_PAD - hv, tq), F32)], axis=0)
        mnot_ref[...] = mnot.T.astype(BF16)

    need = (q0 + tq) // CMP_BLOCK
    bounds, base = [], 0
    for size in CMP_SEGMENTS:
        base += size
        bounds.append(base)
    lo = 0
    for nv in bounds:
        pl.when((need > lo) & (need <= nv))(functools.partial(run, nv))
        lo = nv


def _cmp_select(qcat, kct, vct, gates, b, t, tq):
    m = b * t
    nq = t // tq
    return pl.pallas_call(
        functools.partial(_cmp_select_body, tq=tq),
        out_shape=(jax.ShapeDtypeStruct((m, N_HEADS_B * LANE), F32), jax.ShapeDtypeStruct((m, N_KV_B * LANE), BF16)),
        grid=(b, N_KV_B, nq),
        in_specs=[pl.BlockSpec((tq, REP_B * LANE), lambda bi, g, i: (bi * nq + i, g)),
                  pl.BlockSpec((None, None, N_CMP_PAD, LANE), lambda bi, g, i: (bi, g, 0, 0)),
                  pl.BlockSpec((None, None, LANE, N_CMP_PAD), lambda bi, g, i: (bi, g, 0, 0)),
                  pl.BlockSpec((tq, LANE), lambda bi, g, i: (bi * nq + i, g))],
        out_specs=(pl.BlockSpec((tq, REP_B * LANE), lambda bi, g, i: (bi * nq + i, g)),
                   pl.BlockSpec((tq, LANE), lambda bi, g, i: (bi * nq + i, g))),
        compiler_params=_params("parallel", "parallel", "parallel"),
        name="nsa_cmp_select",
    )(qcat, kct, vct, gates)


def _flash_body(tab_ref, bound_ref, q_ref, mnot_ref, k_ref, v_ref, gates_ref, mask_ref, o_ref,
                qs_ref, ks_ref, m_ref, acc_ref, *, tq, tk, fixed):
    step_id = pl.program_id(2)
    j = tab_ref[1, step_id]
    rows = REP_B * tq

    @pl.when(tab_ref[3, step_id] == 1)
    def _():
        if not fixed:
            m_ref[...] = jnp.full(m_ref.shape, M_INIT, F32)
        acc_ref[...] = jnp.zeros(acc_ref.shape, F32)
        lane = lax.broadcasted_iota(jnp.int32, (tq, LANE), 1)
        for r in range(REP_B):
            qh = q_ref[:, r * LANE:(r + 1) * LANE]
            if fixed:
                qh = jnp.where(lane < HD_B, qh, jnp.ones_like(qh))
            qs_ref[r * tq:(r + 1) * tq, :LANE] = mnot_ref[...]
            qs_ref[r * tq:(r + 1) * tq, LANE:] = qh

    def step(masked):
        kt = k_ref[...]
        lane = lax.broadcasted_iota(jnp.int32, (tk, LANE), 1)
        if fixed:
            kt = jnp.where(lane == HD_B, -bound_ref[0], kt.astype(F32)).astype(BF16)
        kpos = j * tk + lax.broadcasted_iota(jnp.int32, (tk, LANE), 0)
        ks_ref[:, :LANE] = jnp.where(kpos // SEL_BLOCK == lane, -MASK_BIG, 0.0).astype(BF16)
        ks_ref[:, LANE:] = kt
        s = _dot_nt(qs_ref[...], ks_ref[...])
        if masked:
            s = (s.reshape(REP_B, tq, tk) + mask_ref[...][None]).reshape(rows, tk)
        if fixed:
            acc_ref[...] += _dot(jnp.exp(s).astype(BF16), v_ref[...])
        else:
            m_old = m_ref[...]
            m_new = jnp.maximum(m_old, jnp.max(s, axis=-1, keepdims=True))
            p = jnp.exp(s - m_new[:, :1])
            acc_ref[...] = jnp.exp(m_old - m_new) * acc_ref[...] + _dot(p.astype(BF16), v_ref[...])
            m_ref[...] = m_new

    pl.when(tab_ref[2, step_id] == 0)(functools.partial(step, False))
    pl.when(tab_ref[2, step_id] == 1)(functools.partial(step, True))

    @pl.when(tab_ref[4, step_id] == 1)
    def _():
        gates = gates_ref[...]
        lane = lax.broadcasted_iota(jnp.int32, (tq, LANE), 1)
        for r in range(REP_B):
            a = acc_ref[r * tq:(r + 1) * tq, :]
            o = a / a[:, HD_B:HD_B + 1]
            g = gates[:, 3 * r + 1:3 * r + 2]
            o_ref[:, r * LANE:(r + 1) * LANE] = jnp.where(lane < HD_B, o * g, 0.0)


def _flash_steps(t, tq, tk):
    steps, offsets = [], []
    for i in range(t // tq):
        q_lo, q_hi = i * tq, (i + 1) * tq - 1
        js = list(range(q_hi // tk + 1))
        for j in js:
            masked = (j + 1) * tk - 1 > q_lo
            if masked and q_lo - j * tk not in offsets:
                offsets.append(q_lo - j * tk)
            pattern = offsets.index(q_lo - j * tk) if masked else (steps[-1][5] if steps else 0)
            steps.append((i, j, int(masked), j == js[0], j == js[-1], pattern))
    return steps, offsets


def _flash(qcat, mnot, k, v, gates, bound, b, t, tq, tk):
    m = b * t
    nq, nk = t // tq, t // tk
    steps, offsets = _flash_steps(t, tq, tk)
    tab = jnp.asarray(steps, jnp.int32).T
    dist = (jnp.asarray(offsets, jnp.int32)[:, None, None] + jnp.arange(tq, dtype=jnp.int32)[None, :, None]
            - jnp.arange(tk, dtype=jnp.int32)[None, None, :])
    masks = jnp.where(dist >= 0, 0.0, -MASK_BIG).astype(F32)
    kdim = 2 * LANE
    qidx = lambda bi, g, p, *pf: (bi * nq + pf[0][0, p], g)
    kidx = lambda bi, g, p, *pf: (bi * nk + pf[0][1, p], g)
    midx = lambda bi, g, p, *pf: (pf[0][5, p], 0, 0)

    def call(fixed):
        name = "nsa_flash_sel" + ("" if fixed else "_online")
        return pl.pallas_call(
            functools.partial(_flash_body, tq=tq, tk=tk, fixed=fixed),
            out_shape=jax.ShapeDtypeStruct((m, N_HEADS_B * LANE), F32),
            grid_spec=pltpu.PrefetchScalarGridSpec(
                num_scalar_prefetch=2,
                grid=(b, N_KV_B, len(steps)),
                in_specs=[pl.BlockSpec((tq, REP_B * LANE), qidx), pl.BlockSpec((tq, LANE), qidx),
                          pl.BlockSpec((tk, LANE), kidx), pl.BlockSpec((tk, LANE), kidx),
                          pl.BlockSpec((tq, LANE), qidx), pl.BlockSpec((None, tq, tk), midx)],
                out_specs=pl.BlockSpec((tq, REP_B * LANE), qidx),
                scratch_shapes=[pltpu.VMEM((REP_B * tq, kdim), BF16), pltpu.VMEM((tk, 2 * LANE), BF16),
                                pltpu.VMEM((REP_B * tq, LANE), F32), pltpu.VMEM((REP_B * tq, LANE), F32)]),
            compiler_params=_params("parallel", "parallel", "arbitrary"),
            name=name,
        )(tab, bound.reshape(1), qcat, mnot, k, v, gates, masks)

    return lax.cond(bound <= SHIFT_MAX, lambda: call(True), lambda: call(False))


def _window_body(bound_ref, q_ref, *refs, tile, n_key_tiles, fixed):
    k_refs = refs[:n_key_tiles]
    v_refs = refs[n_key_tiles:2 * n_key_tiles]
    gates_ref, mask_ref, o_ref = refs[2 * n_key_tiles:]
    keys = jnp.concatenate([r[...] for r in k_refs], axis=0)
    vals = jnp.concatenate([r[...] for r in v_refs], axis=0)
    q = jnp.concatenate([q_ref[:, r * LANE:(r + 1) * LANE] for r in range(REP_B)], axis=0)
    if fixed:
        lane_k = lax.broadcasted_iota(jnp.int32, keys.shape, 1)
        keys = jnp.where(lane_k == HD_B, -bound_ref[0], keys.astype(F32)).astype(BF16)
        lane_q = lax.broadcasted_iota(jnp.int32, q.shape, 1)
        q = jnp.where(lane_q < HD_B, q, jnp.ones_like(q))
    else:
        lane_q = lax.broadcasted_iota(jnp.int32, q.shape, 1)
        q = jnp.where(lane_q < HD_B, q, jnp.zeros_like(q))
    span = n_key_tiles * tile
    s = (_dot_nt(q, keys).reshape(REP_B, tile, span) + mask_ref[...][None]).reshape(REP_B * tile, span)
    if not fixed:
        s = s - jnp.max(s, axis=-1, keepdims=True)
    acc = _dot(jnp.exp(s).astype(BF16), vals)
    gates = gates_ref[...]
    lane = lax.broadcasted_iota(jnp.int32, (tile, LANE), 1)
    for r in range(REP_B):
        a = acc[r * tile:(r + 1) * tile]
        g = gates[:, 3 * r + 2:3 * r + 3]
        o_ref[:, r * LANE:(r + 1) * LANE] = jnp.where(lane < HD_B, a / a[:, HD_B:HD_B + 1] * g, 0.0)


def _window_attn(qcat, k, v, gates, bound, b, t, tile):
    assert WINDOW % tile == 0 and t % tile == 0
    m = b * t
    nq = t // tile
    back = WINDOW // tile
    nkt = back + 1
    first_tile = jnp.arange(back + 1, dtype=jnp.int32)[:, None, None] - back
    col = jnp.arange(nkt * tile, dtype=jnp.int32)[None, None, :]
    key_tile = first_tile + col // tile
    dist = (jnp.arange(tile, dtype=jnp.int32)[None, :, None] + back * tile) - col
    masks = jnp.where((dist >= 0) & (dist <= WINDOW) & (key_tile >= 0), 0.0, -MASK_BIG).astype(F32)
    qidx = lambda bi, g, i, *_: (bi * nq + i, g)
    kspecs = [pl.BlockSpec((tile, LANE), lambda bi, g, i, *_, c=c: (bi * nq + jnp.maximum(i - back + c, 0), g))
              for c in range(nkt)]

    def call(fixed):
        return pl.pallas_call(
            functools.partial(_window_body, tile=tile, n_key_tiles=nkt, fixed=fixed),
            out_shape=jax.ShapeDtypeStruct((m, N_HEADS_B * LANE), F32),
            grid_spec=pltpu.PrefetchScalarGridSpec(
                num_scalar_prefetch=1,
                grid=(b, N_KV_B, nq),
                in_specs=[pl.BlockSpec((tile, REP_B * LANE), qidx)] + kspecs + kspecs
                         + [pl.BlockSpec((tile, LANE), qidx),
                            pl.BlockSpec((None, tile, nkt * tile), lambda bi, g, i, *_: (jnp.minimum(i, back), 0, 0))],
                out_specs=pl.BlockSpec((tile, REP_B * LANE), qidx)),
            compiler_params=_params("parallel", "parallel", "parallel"),
            name="nsa_window" if fixed else "nsa_window_rowmax",
        )(bound.reshape(1), qcat, *([k] * nkt), *([v] * nkt), gates, masks)

    return lax.cond(bound <= SHIFT_MAX, lambda: call(True), lambda: call(False))


def _sum_proj_body(a_ref, b_ref, c_ref, x_ref, w_ref, o_ref):
    o = (a_ref[...] + b_ref[...] + c_ref[...]).astype(BF16)
    o_ref[...] = x_ref[...] + _dot(o, w_ref[...])


def _sum_proj(a, b, c, x, w):
    m, d = x.shape
    kdim = a.shape[1]
    tm = _row_tile(m)
    big = pl.BlockSpec((tm, kdim), lambda i: (i, 0))
    row = pl.BlockSpec((tm, d), lambda i: (i, 0))
    return pl.pallas_call(
        _sum_proj_body,
        out_shape=jax.ShapeDtypeStruct((m, d), F32),
        grid=(m // tm,),
        in_specs=[big, big, big, row, _const_spec(w.shape)],
        out_specs=row,
        compiler_params=_params("parallel"),
        name="nsa_out_proj",
    )(a, b, c, x, w)


def _nsa_weights(w_in, q_g, k_g, w_out):
    d = w_in.shape[0]
    nq = N_HEADS_B * HD_B
    w = N_KV_B * HD_B
    wq = w_in[:, :nq].reshape(d, N_HEADS_B, 1, HD_B)
    wq = jnp.broadcast_to(wq, (d, N_HEADS_B, 2, HD_B)).reshape(d, N_HEADS_B * LANE)
    wg = w_in[:, nq:nq + 3 * N_HEADS_B].reshape(d, N_KV_B, REP_B * 3)
    wg = jnp.pad(wg, ((0, 0), (0, 0), (0, LANE - REP_B * 3))).reshape(d, N_KV_B * LANE)
    wkv = w_in[:, nq + 3 * N_HEADS_B:]
    lane = jnp.arange(w)
    seg = (lane[:, None] // HD_B == lane[None, :] // HD_B).astype(BF16)
    spread = (lane[:, None] // HD_B * LANE + lane[:, None] % HD_B == jnp.arange(N_KV_B * LANE)[None, :]).astype(BF16)
    wo = jnp.pad(w_out.reshape(N_HEADS_B, HD_B, -1), ((0, 0), (0, LANE - HD_B), (0, 0)))
    return {
        "wq": wq.astype(BF16), "wg": wg.astype(BF16), "wkv": wkv.astype(BF16), "seg": seg, "spread": spread,
        "qg": jnp.tile(q_g, 2)[None, :], "kg": jnp.stack([jnp.tile(k_g[1], N_KV_B), jnp.tile(k_g[2], N_KV_B)]),
        "wo": wo.reshape(N_HEADS_B * LANE, -1).astype(BF16),
    }


def _cmp_weights(pe, w_c1, w_c2, kc_g):
    return (pe.reshape(2, 1, CMP_BLOCK * HD_B), w_c1.reshape(2, CMP_BLOCK * HD_B, HD_B).astype(BF16),
            w_c2.astype(BF16), kc_g[None, :])


def _nsa_prompt(x, g_mix, b, t, w_in, q_g, k_g, pe, w_c1, w_c2, w_out):
    wts = _nsa_weights(w_in, q_g, k_g, w_out)
    tm = _row_tile(t)
    pos = jnp.arange(t)
    qcat, gates, kv_t, cmp_rows, win_rows, ks_s, vs_s, kw_s, vw_s = _nsa_proj(
        x, g_mix, wts, _rope_tables(pos, LANE), _rope_tables(pos, HD_B), t // tm, tm, position_minor=True)
    kc_blk, vc_blk = _cmp_prompt(cmp_rows, b, t, *_cmp_weights(pe, w_c1, w_c2, k_g[0]))
    nb = t // CMP_BLOCK
    assert nb <= N_CMP_PAD and t % SEL_BLOCK == 0

    order = jnp.asarray(_cmp_block_order(), jnp.int32)

    def blocks(a, lo):
        a = jnp.pad(a, ((0, 0), (0, 0), (0, N_CMP_PAD - nb), (lo, LANE - HD_B - lo)))
        return a[:, :, order].astype(BF16)

    kct = blocks(kc_blk, HD_B)
    vct = blocks(vc_blk, 0).transpose(0, 1, 3, 2)
    o_cmp, mnot = _cmp_select(qcat, kct, vct, gates, b, t, min(CMP_SELECT_TILE, t))
    qmax = jnp.max(jnp.abs(q_g))
    o_sel = _flash(qcat, mnot, ks_s, vs_s, gates, qmax * jnp.max(jnp.abs(k_g[1])) * math.sqrt(HD_B), b, t,
                   *(min(n, t) for n in FLASH_SEL_TILES))
    o_win = _window_attn(qcat, kw_s, vw_s, gates, qmax * jnp.max(jnp.abs(k_g[2])) * math.sqrt(HD_B), b, t,
                         min(WINDOW_TILE, t))
    y = _sum_proj(o_cmp, o_sel, o_win, x, wts["wo"])
    wb = min(WINDOW, t)
    kv_out = kv_t.reshape(b, N_KV_SLOTS, N_KV_B, HD_B, t).transpose(0, 4, 1, 2, 3)
    win_out = win_rows.reshape(b, t, 2, N_KV_B, HD_B)[:, t - wb:]
    return y, kv_out, win_out


def _gmlp_layer(x, g, w_ins, ln_g, ln_b, w_s, b_s, w_outs, layer, *, single):
    gw = w_outs.shape[1] // N_GROUPS_A
    if single:
        ws = jnp.repeat(w_s[:, 0, 0], gw)[None, :]
        bs = jnp.repeat(b_s[:, 0], gw)[None, :]
    else:
        ws = w_s
        bs = jnp.repeat(b_s.T, gw, axis=1)
    return _gmlp(x, g[None, :], w_ins, ln_g[None, :], ln_b[None, :], ws, bs, w_outs, layer, single=single)


def _mlstm_proj_body(x_ref, g_ref, wq_ref, wk_ref, wv_ref, wgi_ref, wo_ref, bif_ref,
                     q_ref, k_ref, v_ref, gi_ref, og_ref):
    xb = _rms(x_ref[...], g_ref[...]).astype(BF16)
    q_ref[...] = _dot(xb, wq_ref[...]).astype(BF16)
    k_ref[...] = _dot(xb, wk_ref[...]).astype(BF16)
    v_ref[...] = _dot(xb, wv_ref[...]).astype(BF16)
    gi_ref[...] = _dot(xb, wgi_ref[...]) + bif_ref[...]
    og_ref[...] = jax.nn.sigmoid(_dot(xb, wo_ref[...]))


def _mlstm_proj(x, g, wts):
    m, d = x.shape
    hv = N_HEADS_C * DV_C
    tm = _row_tile(m)
    consts = [g, wts["wq"], wts["wk"], wts["wv"], wts["wgi"], wts["wo"], wts["bif"]]
    tile = lambda n: pl.BlockSpec((tm, n), lambda i: (i, 0))
    return pl.pallas_call(
        _mlstm_proj_body,
        out_shape=(jax.ShapeDtypeStruct((m, N_HEADS_C * LANE), BF16), jax.ShapeDtypeStruct((m, N_HEADS_C * LANE), BF16),
                   jax.ShapeDtypeStruct((m, hv), BF16), jax.ShapeDtypeStruct((m, LANE), F32),
                   jax.ShapeDtypeStruct((m, hv), F32)),
        grid=(m // tm,),
        in_specs=[tile(d)] + [_const_spec(c.shape) for c in consts],
        out_specs=(tile(N_HEADS_C * LANE), tile(N_HEADS_C * LANE), tile(hv), tile(LANE), tile(hv)),
        compiler_params=_params("parallel"),
        name="mlstm_proj",
    )(x, *consts)


def _mlstm_scan_body(q_ref, k_ref, v_ref, gi_ref, git_ref, hs_ref, c_out, n_out, m_out, c_s, n_s, m_s):
    c = pl.program_id(0)
    nseq, L = q_ref.shape[:2]

    @pl.when(c == 0)
    def _():
        c_s[...] = jnp.zeros(c_s.shape, F32)
        n_s[...] = jnp.zeros(n_s.shape, F32)
        m_s[...] = jnp.zeros(m_s.shape, F32)

    row = lax.broadcasted_iota(jnp.int32, (L, L), 0)
    col = lax.broadcasted_iota(jnp.int32, (L, L), 1)
    causal = col <= row
    tril = causal.astype(BF16)
    gi, git, bcol_all, brow_all = [], [], [], []
    for b in range(nseq):
        gi.append(gi_ref[b])
        git.append(git_ref[b])
        fcol = jax.nn.log_sigmoid(gi[b])
        frow = jax.nn.log_sigmoid(git[b][N_HEADS_C:, :])
        bcol_all.append(sum(_dot(tril, part) for part in _split3(fcol)))
        brow_all.append(sum(_dot_nt(part, tril) for part in _split3(frow)))
    units = [(b, h) for b in range(nseq) for h in range(N_HEADS_C)]
    idx = range(len(units))
    sl = [slice(h * LANE, (h + 1) * LANE) for _, h in units]
    q = [q_ref[b, :, sl[u]] for u, (b, _) in enumerate(units)]
    k = [k_ref[b, :, sl[u]] for u, (b, _) in enumerate(units)]
    v = [v_ref[b, :, sl[u]] for u, (b, _) in enumerate(units)]
    qk = [_dot_nt(q[u], k[u]) for u in idx]
    cq = [_dot_nt(q[u], c_s[u].astype(BF16)) for u in idx]
    bcol = [bcol_all[b][:, N_HEADS_C + h:N_HEADS_C + h + 1] for b, h in units]
    m_prev = [m_s[u:u + 1, 0:1] for u in idx]
    s, a, m_t = [], [], []
    for u, (b, h) in enumerate(units):
        dlog = jnp.where(causal, bcol[u] - brow_all[b][h:h + 1, :] + git[b][h:h + 1, :], -jnp.inf)
        inter = bcol[u] + m_prev[u]
        m_t.append(jnp.maximum(inter, jnp.max(dlog, axis=1, keepdims=True)))
        s.append(qk[u] * jnp.exp(dlog - m_t[u]))
        a.append(jnp.exp(inter - m_t[u]))
    sv = [_dot(s[u].astype(BF16), v[u]) for u in idx]
    wk, decay, m_new = [], [], []
    for u, (b, h) in enumerate(units):
        nq = jnp.sum(q[u].astype(F32) * n_s[u:u + 1, :], axis=1, keepdims=True)
        den = a[u] * nq + jnp.sum(s[u], axis=1, keepdims=True)
        hs_ref[b, :, sl[u]] = (a[u] * cq[u] + sv[u]) / jnp.maximum(jnp.abs(den), jnp.exp(-m_t[u]))
        b_end = bcol[u][L - 1:L, :]
        wlog = b_end - bcol[u] + gi[b][:, h:h + 1]
        m_new.append(jnp.maximum(b_end + m_prev[u], jnp.max(wlog, axis=0, keepdims=True)))
        wk.append(jnp.exp(wlog - m_new[u]))
        decay.append(jnp.exp(b_end + m_prev[u] - m_new[u]))
    upd = [_dot_tn((v[u].astype(F32) * wk[u]).astype(BF16), k[u]) for u in idx]
    for u in idx:
        c_s[u] = decay[u] * c_s[u] + upd[u]
        n_s[u:u + 1, :] = decay[u] * n_s[u:u + 1, :] + jnp.sum(k[u].astype(F32) * wk[u], axis=0, keepdims=True)
        m_s[u:u + 1, :] = jnp.broadcast_to(m_new[u], (1, LANE))

    @pl.when(c == pl.num_programs(0) - 1)
    def _():
        c_out[...] = c_s[...]
        n_out[...] = n_s[...]
        m_out[...] = m_s[...]


def _mlstm_scan(q, k, v, gi, git, b, t):
    L = math.gcd(t, CHUNK_C)
    nc = t // L
    hv = N_HEADS_C * DV_C
    units = b * N_HEADS_C
    seq = lambda a: a.reshape(b, t, a.shape[-1])
    tile = lambda n: pl.BlockSpec((b, L, n), lambda c: (0, c, 0))
    state = lambda *shape: pl.BlockSpec(shape, lambda c: (0,) * len(shape))
    hs, c_fin, n_fin, m_fin = pl.pallas_call(
        _mlstm_scan_body,
        out_shape=(jax.ShapeDtypeStruct((b, t, hv), F32), jax.ShapeDtypeStruct((units, DV_C, LANE), F32),
                   jax.ShapeDtypeStruct((units, LANE), F32), jax.ShapeDtypeStruct((units, LANE), F32)),
        grid=(nc,),
        in_specs=[tile(N_HEADS_C * LANE), tile(N_HEADS_C * LANE), tile(hv), tile(LANE),
                  pl.BlockSpec((b, 2 * N_HEADS_C, L), lambda c: (0, 0, c))],
        out_specs=(tile(hv), state(units, DV_C, LANE), state(units, LANE), state(units, LANE)),
        scratch_shapes=[pltpu.VMEM((units, DV_C, LANE), F32), pltpu.VMEM((units, LANE), F32),
                        pltpu.VMEM((units, LANE), F32)],
        compiler_params=_params("arbitrary"),
        name="mlstm_scan",
    )(seq(q), seq(k), seq(v), seq(gi), git)
    return (hs.reshape(b * t, hv), c_fin.reshape(b, N_HEADS_C, DV_C, LANE), n_fin.reshape(b, N_HEADS_C, LANE),
            m_fin.reshape(b, N_HEADS_C, LANE))


def _mlstm_out_body(hs_ref, og_ref, hg_ref, x_ref, w_ref, o_ref):
    parts = []
    for h in range(N_HEADS_C):
        sl = slice(h * DV_C, (h + 1) * DV_C)
        parts.append((og_ref[:, sl] * _rms(hs_ref[:, sl], hg_ref[:, sl])).astype(BF16))
    o_ref[...] = x_ref[...] + _dot(jnp.concatenate(parts, axis=1), w_ref[...])


def _mlstm_out(hs, og, hg, x, w):
    m, d = x.shape
    hv = hs.shape[1]
    tm = _row_tile(m)
    wide = pl.BlockSpec((tm, hv), lambda i: (i, 0))
    row = pl.BlockSpec((tm, d), lambda i: (i, 0))
    return pl.pallas_call(
        _mlstm_out_body,
        out_shape=jax.ShapeDtypeStruct((m, d), F32),
        grid=(m // tm,),
        in_specs=[wide, wide, _const_spec(hg.shape), row, _const_spec(w.shape)],
        out_specs=row,
        compiler_params=_params("parallel"),
        name="mlstm_out",
    )(hs, og, hg, x, w)


def _mlstm_weights(w_in, b_if):
    d = w_in.shape[0]
    hk, hv = N_HEADS_C * DK_C, N_HEADS_C * DV_C

    def spread(w):
        w = w.reshape(d, N_HEADS_C, DK_C)
        return jnp.pad(w, ((0, 0), (0, 0), (0, LANE - DK_C))).reshape(d, N_HEADS_C * LANE)

    wgi = jnp.pad(w_in[:, 2 * hk + hv:2 * hk + hv + 2 * N_HEADS_C], ((0, 0), (0, LANE - 2 * N_HEADS_C)))
    return {
        "wq": spread(w_in[:, :hk]).astype(BF16),
        "wk": (spread(w_in[:, hk:2 * hk]) * (DK_C ** -0.5)).astype(BF16),
        "wv": w_in[:, 2 * hk:2 * hk + hv].astype(BF16),
        "wgi": wgi.astype(BF16),
        "wo": w_in[:, 2 * hk + hv + 2 * N_HEADS_C:].astype(BF16),
        "bif": jnp.pad(b_if, (0, LANE - 2 * N_HEADS_C))[None, :],
    }


def _mlstm_prompt(x, g_mix, b, t, w_in, b_if, h_g, w_out):
    wts = _mlstm_weights(w_in, b_if)
    q, k, v, gi, og = _mlstm_proj(x, g_mix, wts)
    git = gi[:, :2 * N_HEADS_C].reshape(b, t, 2 * N_HEADS_C).transpose(0, 2, 1)
    hs, c, n, m = _mlstm_scan(q, k, v, gi, git, b, t)
    y = _mlstm_out(hs, og, h_g[None, :], x, w_out.astype(BF16))
    return y, c[..., :DK_C], n[..., :DK_C], m[..., 0]


def _proj_add_body(o_ref, x_ref, w_ref, out_ref):
    out_ref[...] = x_ref[...] + _dot(o_ref[...].astype(BF16), w_ref[...])


def _proj_add(o, x, w):
    m, d = x.shape
    tm = _row_tile(m)
    return pl.pallas_call(
        _proj_add_body,
        out_shape=jax.ShapeDtypeStruct((m, d), F32),
        grid=(m // tm,),
        in_specs=[pl.BlockSpec((tm, o.shape[1]), lambda i: (i, 0)), pl.BlockSpec((tm, d), lambda i: (i, 0)),
                  _const_spec(w.shape)],
        out_specs=pl.BlockSpec((tm, d), lambda i: (i, 0)),
        compiler_params=_params("parallel"),
        name="proj_add",
    )(o, x, w)


def _nsa_step_body(pt_ref, *refs, n_pages, page, past_len):
    pages = refs[:n_pages]
    (win_ref, qr_ref, qn_ref, gates_ref, nkv_ref, nwin_ref, pe_ref, w1_ref, w2_ref, kcg_ref,
     o_ref, c_s, x_s) = refs[n_pages:]
    w = N_KV_B * HD_B
    length = n_pages * page
    nb = length // CMP_BLOCK
    t = past_len
    for p in range(n_pages):
        for c in range(CMP_PLANES):
            c_s[c, p * page:(p + 1) * page, :] = pages[p][c * LANE:(c + 1) * LANE, :].T
    _flatten_cmp_blocks(lambda c, start: c_s[c, pl.ds(start, 8, stride=CMP_BLOCK), :], x_s, nb)
    lane_w = lax.broadcasted_iota(jnp.int32, (HD_B, w), 1)
    row_w = lax.broadcasted_iota(jnp.int32, (HD_B, w), 0)
    cmp_nat = []
    for slot in range(2):
        y = _compress_slot(x_s, slot, pe_ref, w1_ref, w2_ref, kcg_ref)
        nat = jnp.zeros((nb, w), F32)
        for g in range(N_KV_B):
            place = (lane_w == row_w + g * HD_B).astype(BF16)
            nat = nat + _dot(y[g * nb:(g + 1) * nb].astype(BF16), place)
        cmp_nat.append(nat.astype(BF16))
    kc, vc = cmp_nat
    qr = qr_ref[...]
    qn = qn_ref[...]
    gates = gates_ref[...]
    nh = N_HEADS_B
    blk = lax.broadcasted_iota(jnp.int32, (nh, nb), 1)
    p_c = _masked_softmax(_dot_nt(qn, kc), (blk + 1) * CMP_BLOCK - 1 <= t, 1)
    o_c = _dot(p_c.astype(BF16), vc)
    blk_t = lax.broadcasted_iota(jnp.int32, (nb, nh), 0)
    p_t = _masked_softmax(_dot_nt(kc, qn), (blk_t + 1) * CMP_BLOCK - 1 <= t, 0)
    gsum = (lax.broadcasted_iota(jnp.int32, (nh, LANE), 0) // REP_B
            == lax.broadcasted_iota(jnp.int32, (nh, LANE), 1)).astype(BF16)
    pair = (lax.broadcasted_iota(jnp.int32, (nb, nb), 1) // (SEL_BLOCK // CMP_BLOCK)
            == lax.broadcasted_iota(jnp.int32, (nb, nb), 0)).astype(BF16)
    imp = sum(_dot(part, gsum) for part in _split3(p_t))
    imp = sum(_dot(pair, part) for part in _split3(imp))
    sblk = lax.broadcasted_iota(jnp.int32, (nb, LANE), 0)
    cur = t // SEL_BLOCK
    forced = (sblk == 0) | (sblk == cur) | (sblk == cur - 1)
    score = jnp.where(forced, jnp.inf, jnp.where(sblk * SEL_BLOCK <= t, imp, -jnp.inf))
    sblk_f = sblk.astype(F32)
    pickable = score > -jnp.inf
    for _ in range(N_SEL):
        mx = jnp.max(score, axis=0, keepdims=True)
        first = jnp.min(jnp.where(score == mx, sblk_f, float(nb)), axis=0, keepdims=True)
        score = jnp.where(sblk_f == first, -jnp.inf, score)
    notsel = jnp.where(pickable, jnp.where(score > -jnp.inf, -MASK_BIG, 0.0), -MASK_BIG)
    bias = _dot_nt(gsum, notsel.astype(BF16)).astype(BF16)
    expand = (lax.broadcasted_iota(jnp.int32, (nb, length), 1) // SEL_BLOCK
              == lax.broadcasted_iota(jnp.int32, (nb, length), 0)).astype(BF16)
    nkv = nkv_ref[...]
    nwin = nwin_ref[...]

    def attend(s, k_new, v_new, weighted_values):
        s_new = jnp.sum(qr.astype(F32) * k_new.astype(BF16).astype(F32), axis=1, keepdims=True)
        m = jnp.maximum(jnp.max(s, axis=1, keepdims=True), s_new)
        e = jnp.exp(s - m)
        e_new = jnp.exp(s_new - m)
        den = jnp.sum(e, axis=1, keepdims=True) + e_new
        num = weighted_values(e.astype(BF16)) + e_new.astype(BF16).astype(F32) * v_new.astype(BF16).astype(F32)
        return num / den

    s_sel = jnp.concatenate([_dot(qr, pages[p][2 * w:3 * w, :].astype(BF16)) for p in range(n_pages)], axis=1)
    o_s = attend(s_sel + _dot(bias, expand), nkv[:, 2 * w:3 * w], nkv[:, 3 * w:],
                 lambda e: sum(_dot_nt(e[:, p * page:(p + 1) * page], pages[p][3 * w:, :].astype(BF16))
                               for p in range(n_pages)))
    wb = win_ref.shape[1]
    pos_w = t - wb + lax.broadcasted_iota(jnp.int32, (nh, wb), 1)
    ok_w = (pos_w >= 0) & (t - pos_w <= WINDOW)
    s_w = jnp.where(ok_w, _dot(qr, win_ref[:w, :].astype(BF16)), -MASK_BIG)
    o_w = attend(s_w, nwin[:, :w], nwin[:, w:], lambda e: _dot_nt(e, win_ref[w:, :].astype(BF16)))
    o_ref[...] = gates[:, 0:1] * o_c + gates[:, 1:2] * o_s + gates[:, 2:3] * o_w


def _nsa_step(page_table, cache, win_cache, qr, qn, gates, new_kv, new_win, pe, w1, w2, kc_g, past_len):
    bsz, n_pages = page_table.shape
    page = cache.shape[2]
    assert page == LANE
    w = N_KV_B * HD_B
    nb = n_pages * page // CMP_BLOCK
    wb = win_cache.shape[2]
    per = lambda shape: pl.BlockSpec((None,) + shape, lambda b, pt: (b,) + (0,) * len(shape))
    const = lambda a: pl.BlockSpec(a.shape, lambda b, pt: (0,) * a.ndim)
    page_specs = [pl.BlockSpec((None, 4 * w, page), lambda b, pt, p=p: (pt[b, p], 0, 0)) for p in range(n_pages)]
    return pl.pallas_call(
        functools.partial(_nsa_step_body, n_pages=n_pages, page=page, past_len=past_len),
        out_shape=jax.ShapeDtypeStruct((bsz, N_HEADS_B, w), F32),
        grid_spec=pltpu.PrefetchScalarGridSpec(
            num_scalar_prefetch=1,
            grid=(bsz,),
            in_specs=page_specs + [per((2 * w, wb)), per((N_HEADS_B, w)), per((N_HEADS_B, w)), per((N_HEADS_B, LANE)),
                                   per((1, 4 * w)), per((1, 2 * w)), const(pe), const(w1), const(w2), const(kc_g)],
            out_specs=per((N_HEADS_B, w)),
            scratch_shapes=[pltpu.VMEM((2 * w // LANE, n_pages * page, LANE), F32),
                            pltpu.VMEM((2 * N_KV_B, nb, CMP_BLOCK * HD_B), F32)]),
        compiler_params=_params("parallel"),
        name="nsa_step",
    )(page_table, *([cache] * n_pages), win_cache, qr, qn, gates, new_kv, new_win, pe, w1, w2, kc_g)


def _nsa_sample_step(x, g_mix, past_len, kv_cache, win_cache, page_table, w_in, q_g, k_g, pe, w_c1, w_c2, w_out):
    bsz, d = x.shape
    w = N_KV_B * HD_B
    assert past_len % CMP_BLOCK == 0 and past_len // SEL_BLOCK + 1 <= past_len // CMP_BLOCK
    wts = _nsa_weights(w_in, q_g, k_g, w_out)
    pos = jnp.full((bsz,), past_len, jnp.int32)
    qcat, gates, kv_rows, win_rows, _, _, _, _ = _nsa_proj(
        x, g_mix, wts, _rope_tables(pos, LANE), _rope_tables(pos, HD_B), 1, bsz, position_minor=False)
    q5 = qcat.reshape(bsz, N_KV_B, REP_B, 2, HD_B)
    eye = jnp.eye(N_KV_B, dtype=BF16)
    qrows = (q5[:, :, :, :, None, :] * eye[None, :, None, None, :, None])
    qr = qrows[:, :, :, 0].reshape(bsz, N_HEADS_B, w)
    qn = qrows[:, :, :, 1].reshape(bsz, N_HEADS_B, w)
    gts = gates.reshape(bsz, N_KV_B, LANE)[:, :, :REP_B * 3].reshape(bsz, N_HEADS_B, 3)
    gts = jnp.pad(gts, ((0, 0), (0, 0), (0, LANE - 3)))
    pool, page = kv_cache.shape[:2]
    cache_t = kv_cache.reshape(pool, page, 4 * w).transpose(0, 2, 1)
    win_t = win_cache.reshape(bsz, -1, 2 * w).transpose(0, 2, 1)
    o = _nsa_step(page_table, cache_t, win_t, qr, qn, gts,
                  kv_rows.reshape(bsz, 1, 4 * w), win_rows.reshape(bsz, 1, 2 * w),
                  *_cmp_weights(pe, w_c1, w_c2, k_g[0]), past_len)
    own = (jnp.arange(N_HEADS_B)[:, None] // REP_B == jnp.arange(N_KV_B)[None, :]).astype(F32)
    w_exp = own[:, :, None, None] * w_out.reshape(N_HEADS_B, 1, HD_B, d)
    y = _proj_add(o.reshape(bsz, N_HEADS_B * w), x, w_exp.reshape(N_HEADS_B * w, d).astype(BF16))
    return y, kv_rows.reshape(bsz, 1, N_KV_SLOTS, N_KV_B, HD_B), win_rows.reshape(bsz, 1, 2, N_KV_B, HD_B)


def _mlstm_step_body(q_ref, k_ref, qt_ref, kt_ref, v_ref, gi_ref, ct_ref, n_ref, m_ref,
                     h_ref, cto_ref, no_ref, mo_ref, *, sb):
    gi = gi_ref[...]
    logf = jax.nn.log_sigmoid(gi)
    m_all = m_ref[...]
    lane_m = lax.broadcasted_iota(jnp.int32, (1, N_HEADS_C), 1)
    for s in range(sb):
        m_new_row = jnp.zeros((1, N_HEADS_C), F32)
        for h in range(N_HEADS_C):
            q = q_ref[s:s + 1, h * LANE:h * LANE + DK_C].astype(F32)
            k = k_ref[s:s + 1, h * LANE:h * LANE + DK_C].astype(F32)
            qc = qt_ref[s, :, h:h + 1]
            kc = kt_ref[s, :, h:h + 1]
            v = v_ref[s:s + 1, h * DV_C:(h + 1) * DV_C].astype(F32)
            ct = ct_ref[s, h]
            n = n_ref[s, h:h + 1, :]
            it = gi[s:s + 1, h:h + 1]
            b = logf[s:s + 1, N_HEADS_C + h:N_HEADS_C + h + 1]
            m0 = m_all[s:s + 1, h:h + 1]
            inter = b + m0
            m_t = jnp.maximum(inter, it)
            wgt = jnp.exp(it - m_t)
            a = jnp.exp(inter - m_t)
            sc = jnp.sum(q * k, axis=1, keepdims=True) * wgt
            num = a * jnp.sum(ct * qc, axis=0, keepdims=True) + sc * v
            den = a * jnp.sum(n * q, axis=1, keepdims=True) + sc
            h_ref[s:s + 1, h * DV_C:(h + 1) * DV_C] = num / jnp.maximum(jnp.abs(den), jnp.exp(-m_t))
            cto_ref[s, h] = a * ct + (wgt * kc) * v
            no_ref[s, h:h + 1, :] = a * n + wgt * k
            m_new_row = jnp.where(lane_m == h, m_t, m_new_row)
        mo_ref[s:s + 1, :] = m_new_row


def _mlstm_step(q, k, qt, kt, v, gi, ct0, n0, m0):
    bsz = q.shape[0]
    sb = 8
    row = lambda n: pl.BlockSpec((sb, n), lambda i: (i, 0))
    c_spec = pl.BlockSpec((sb, N_HEADS_C, DK_C, DV_C), lambda i: (i, 0, 0, 0))
    n_spec = pl.BlockSpec((sb, N_HEADS_C, DK_C), lambda i: (i, 0, 0))
    col_spec = pl.BlockSpec((sb, DK_C, N_HEADS_C), lambda i: (i, 0, 0))
    hv = N_HEADS_C * DV_C
    return pl.pallas_call(
        functools.partial(_mlstm_step_body, sb=sb),
        out_shape=(jax.ShapeDtypeStruct((bsz, hv), F32), jax.ShapeDtypeStruct(ct0.shape, F32),
                   jax.ShapeDtypeStruct(n0.shape, F32), jax.ShapeDtypeStruct(m0.shape, F32)),
        grid=(bsz // sb,),
        in_specs=[row(N_HEADS_C * LANE), row(N_HEADS_C * LANE), col_spec, col_spec, row(hv), row(LANE),
                  c_spec, n_spec, row(N_HEADS_C)],
        out_specs=(row(hv), c_spec, n_spec, row(N_HEADS_C)),
        compiler_params=_params("parallel"),
        name="mlstm_step",
    )(q, k, qt, kt, v, gi, ct0, n0, m0)


def _mlstm_sample_step(x, g_mix, c0, n0, m0, w_in, b_if, h_g, w_out):
    bsz = x.shape[0]
    wts = _mlstm_weights(w_in, b_if)
    q, k, v, gi, og = _mlstm_proj(x, g_mix, wts)
    cols = lambda a: a.astype(F32).reshape(bsz, N_HEADS_C, LANE)[:, :, :DK_C].transpose(0, 2, 1)
    hs, ct, n, m = _mlstm_step(q, k, cols(q), cols(k), v, gi, c0.transpose(0, 1, 3, 2), n0, m0)
    y = _mlstm_out(hs, og, h_g[None, :], x, w_out.astype(BF16))
    return y, ct.transpose(0, 1, 3, 2), n, m


def kernel(x_prompt, x_sample, cache_nsa_kv, cache_nsa_win, state_mlstm_C, state_mlstm_n, state_mlstm_m, page_table,
           norm_mix_g, norm_ffn_g, ffn_w1, ffn_w2, a_w_in, a_ln_g, a_ln_b, a_w_s, a_b_s, a_w_out,
           b_w_in, b_q_g, b_k_g, b_pe, b_w_c1, b_w_c2, b_w_out, c_w_in, c_b_if, c_h_g, c_w_out):
    bp, t, d = x_prompt.shape
    bs, ts, _ = x_sample.shape
    assert ts == 1
    past_len = page_table.shape[1] * cache_nsa_kv.shape[2]
    xp = x_prompt.reshape(bp * t, d)
    xs = x_sample.reshape(bs, d)
    out = {k: [] for k in ("v_s", "kv_p", "win_p", "kv_s", "win_s", "C_p", "n_p", "m_p", "C_s", "n_s", "m_s")}
    ffn_w1_b, ffn_w2_b = ffn_w1.astype(BF16), ffn_w2.astype(BF16)
    a_w_in_b, a_w_out_b = a_w_in.astype(BF16), a_w_out.astype(BF16)
    for layer in range(norm_mix_g.shape[0]):
        kind, j = layer % 3, layer // 3
        gm = norm_mix_g[layer]
        if kind == 0:
            args = (a_w_in_b, a_ln_g[j], a_ln_b[j], a_w_s[j], a_b_s[j], a_w_out_b, j)
            xp = _gmlp_layer(xp, gm, *args, single=False)[0]
            xs, v = _gmlp_layer(xs, gm, *args, single=True)
            out["v_s"].append(v.reshape(bs, ts, -1))
        elif kind == 1:
            args = (b_w_in[j], b_q_g[j], b_k_g[j], b_pe[j], b_w_c1[j], b_w_c2[j], b_w_out[j])
            xp, kv, win = _nsa_prompt(xp, gm[None, :], bp, t, *args)
            out["kv_p"].append(kv)
            out["win_p"].append(win)
            xs, kv, win = _nsa_sample_step(xs, gm[None, :], past_len, cache_nsa_kv[j], cache_nsa_win[j], page_table,
                                           *args)
            out["kv_s"].append(kv)
            out["win_s"].append(win)
        else:
            args = (c_w_in[j], c_b_if[j], c_h_g[j], c_w_out[j])
            xp, c, n, m = _mlstm_prompt(xp, gm[None, :], bp, t, *args)
            out["C_p"].append(c)
            out["n_p"].append(n)
            out["m_p"].append(m)
            xs, c, n, m = _mlstm_sample_step(xs, gm[None, :], state_mlstm_C[j], state_mlstm_n[j], state_mlstm_m[j],
                                             *args)
            out["C_s"].append(c)
            out["n_s"].append(n)
            out["m_s"].append(m)
        gf = norm_ffn_g[layer][None, :]
        xp = _ffn(xp, gf, ffn_w1_b, ffn_w2_b, layer)
        xs = _ffn(xs, gf, ffn_w1_b, ffn_w2_b, layer)
    st = {k: jnp.stack(v) for k, v in out.items()}
    return (xp.reshape(bp, t, d), xs.reshape(bs, ts, d), st["v_s"], st["kv_p"], st["win_p"], st["kv_s"], st["win_s"],
            st["C_p"], st["n_p"], st["m_p"], st["C_s"], st["n_s"], st["m_s"])
```

```python
import functools
import math

import jax
import jax.numpy as jnp
from jax import lax
from jax.experimental import pallas as pl
from jax.experimental.pallas import tpu as pltpu

F32 = jnp.float32
BF16 = jnp.bfloat16

EPS = 1e-6
CHUNK_A = 128
N_GROUPS_A = 8
N_HEADS_B = 16
N_KV_B = 4
REP_B = N_HEADS_B // N_KV_B
HD_B = 64
ROT_DIM = 16
ROPE_THETA = 500000.0
CMP_BLOCK = 32
SEL_BLOCK = 64
N_SEL = 16
WINDOW = 512
N_KV_SLOTS = 4
N_HEADS_C = 8
DK_C = 64
DV_C = 128
CHUNK_C = 128
SCALE_B = HD_B ** -0.5

LANE = 128
VMEM_LIMIT_BYTES = 56 * 1024 * 1024
MASK_BIG = 1e30
M_INIT = -1e20
N_SEL_PAD = 128
N_CMP_PAD = 2 * N_SEL_PAD
SHIFT_MAX = 40.0
FLASH_SEL_TILES = (1024, 1024)
WINDOW_TILE = 256
CMP_SELECT_TILE = 1024


def _params(*sem):
    return pltpu.CompilerParams(dimension_semantics=sem, vmem_limit_bytes=VMEM_LIMIT_BYTES)


def _dot(a, b):
    return jnp.dot(a, b, preferred_element_type=F32)


def _dot_nt(a, b):
    return lax.dot_general(a, b, (((1,), (1,)), ((), ())), preferred_element_type=F32)


def _dot_tn(a, b):
    return lax.dot_general(a, b, (((0,), (0,)), ((), ())), preferred_element_type=F32)


def _rms(x, g):
    return x * lax.rsqrt(jnp.mean(x * x, axis=-1, keepdims=True) + EPS) * g


def _split3(x):
    a = x.astype(BF16)
    r = x - a.astype(F32)
    b = r.astype(BF16)
    c = (r - b.astype(F32)).astype(BF16)
    return a, b, c


def _const_spec(shape):
    n = len(shape)
    return pl.BlockSpec(shape, lambda *_: (0,) * n)


ROW_TILE = 512


def _row_tile(m, pref=ROW_TILE):
    t = min(pref, m)
    while m % t:
        t //= 2
    return t


def _ffn_body(x_ref, g_ref, w1_ref, w2_ref, o_ref, *, ck):
    x = x_ref[...]
    xb = _rms(x, g_ref[...]).astype(BF16)
    acc = x
    for j in range(w1_ref.shape[1] // ck):
        h = jnp.maximum(_dot(xb, w1_ref[:, j * ck:(j + 1) * ck]), 0.0)
        acc = acc + _dot((h * h).astype(BF16), w2_ref[j * ck:(j + 1) * ck, :])
    o_ref[...] = acc


def _layer_spec(stacked, layer):
    n = stacked.ndim - 1
    return pl.BlockSpec((None,) + stacked.shape[1:], lambda *_: (layer,) + (0,) * n)


def _ffn(x, g, w1s, w2s, layer):
    m, d = x.shape
    tm = _row_tile(m)
    return pl.pallas_call(
        functools.partial(_ffn_body, ck=1024),
        out_shape=jax.ShapeDtypeStruct((m, d), F32),
        grid=(m // tm,),
        in_specs=[pl.BlockSpec((tm, d), lambda i: (i, 0)), _const_spec(g.shape),
                  _layer_spec(w1s, layer), _layer_spec(w2s, layer)],
        out_specs=pl.BlockSpec((tm, d), lambda i: (i, 0)),
        compiler_params=_params("parallel"),
        name="ffn",
    )(x, g, w1s, w2s)


def _gmlp_body(x_ref, g_ref, win_ref, lng_ref, lnb_ref, ws_ref, bs_ref, wout_ref, o_ref, *maybe_v_ref, single):
    x = x_ref[...]
    dg = lng_ref.shape[1]
    xb = _rms(x, g_ref[...]).astype(BF16)
    u = jax.nn.gelu(_dot(xb, win_ref[:, :dg]))
    v = jax.nn.gelu(_dot(xb, win_ref[:, dg:]))
    mu = jnp.mean(v, axis=-1, keepdims=True)
    vc = v - mu
    var = jnp.mean(vc * vc, axis=-1, keepdims=True)
    v = vc * lax.rsqrt(var + EPS) * lng_ref[...] + lnb_ref[...]
    if single:
        maybe_v_ref[0][...] = v
        gate = v * ws_ref[...] + bs_ref[...]
    else:
        gw = dg // N_GROUPS_A
        row = lax.broadcasted_iota(jnp.int32, (CHUNK_A, CHUNK_A), 0)
        col = lax.broadcasted_iota(jnp.int32, (CHUNK_A, CHUNK_A), 1)
        causal = col <= row
        vb = v.astype(BF16)
        chunks = []
        for c in range(x.shape[0] // CHUNK_A):
            parts = []
            for gi in range(N_GROUPS_A):
                w = jnp.where(causal, ws_ref[gi], 0.0).astype(BF16)
                parts.append(_dot(w, vb[c * CHUNK_A:(c + 1) * CHUNK_A, gi * gw:(gi + 1) * gw]))
            chunks.append(jnp.concatenate(parts, axis=1) + bs_ref[...])
        gate = jnp.concatenate(chunks, axis=0)
    o_ref[...] = x + _dot((u * gate).astype(BF16), wout_ref[...])


def _gmlp(x, g, w_ins, ln_g, ln_b, ws, bs, w_outs, layer, *, single):
    m, d = x.shape
    dg = w_outs.shape[1]
    tm = _row_tile(m)
    n_out = 2 if single else 1
    outs = pl.pallas_call(
        functools.partial(_gmlp_body, single=single),
        out_shape=(jax.ShapeDtypeStruct((m, d), F32), jax.ShapeDtypeStruct((m, dg), F32))[:n_out],
        grid=(m // tm,),
        in_specs=[pl.BlockSpec((tm, d), lambda i: (i, 0)), _const_spec(g.shape), _layer_spec(w_ins, layer),
                  _const_spec(ln_g.shape), _const_spec(ln_b.shape), _const_spec(ws.shape), _const_spec(bs.shape),
                  _layer_spec(w_outs, layer)],
        out_specs=(pl.BlockSpec((tm, d), lambda i: (i, 0)), pl.BlockSpec((tm, dg), lambda i: (i, 0)))[:n_out],
        compiler_params=_params("parallel"),
        name="gmlp_single" if single else "gmlp",
    )(x, g, w_ins, ln_g, ln_b, ws, bs, w_outs)
    return outs if single else (outs[0], None)


def _rope_tables(pos, seg):
    half = ROT_DIM // 2
    freq = jnp.power(ROPE_THETA, -jnp.arange(half, dtype=F32) * 2.0 / ROT_DIM)
    ang = pos.astype(F32)[:, None] * freq[None, :]
    cos, sin = jnp.cos(ang), jnp.sin(ang)
    t = pos.shape[0]
    one = jnp.ones((t, seg - ROT_DIM), F32)
    zero = jnp.zeros((t, seg - ROT_DIM), F32)
    z8 = jnp.zeros((t, half), F32)
    tabs = [jnp.concatenate([cos, cos, one], 1), jnp.concatenate([-sin, z8, zero], 1),
            jnp.concatenate([z8, sin, zero], 1)]
    return jnp.stack([jnp.tile(a, (1, LANE // seg)) for a in tabs])


def _rope128(x, tab):
    return x * tab[0] + pltpu.roll(x, LANE - ROT_DIM // 2, 1) * tab[1] + pltpu.roll(x, ROT_DIM // 2, 1) * tab[2]


def _nsa_proj_body(x_ref, g_ref, wq_ref, wg_ref, wkv_ref, seg_ref, spread_ref, qg_ref, kg_ref, tq_ref, tk_ref,
                   qcat_ref, gates_ref, *out_refs, position_minor):
    if position_minor:
        kvt_ref, cmp_ref, win_ref, ks_ref, vs_ref, kw_ref, vw_ref = out_refs
    else:
        kv_ref, win_ref, ks_ref, vs_ref, kw_ref, vw_ref = out_refs
    x = x_ref[...]
    xb = _rms(x, g_ref[...]).astype(BF16)
    tq = tq_ref[...]
    tk = tk_ref[...]
    qg = qg_ref[...]
    for h in range(N_HEADS_B):
        q = _dot(xb, wq_ref[:, h * LANE:(h + 1) * LANE])
        ms = jnp.sum(q * q, axis=-1, keepdims=True) * (1.0 / LANE)
        qn = q * lax.rsqrt(ms + EPS) * qg
        qcat_ref[:, h * LANE:(h + 1) * LANE] = (_rope128(qn, tq) * SCALE_B).astype(BF16)
    gates_ref[...] = jax.nn.sigmoid(_dot(xb, wg_ref[...]))
    kv = _dot(xb, wkv_ref[...])
    w = N_KV_B * HD_B
    seg = seg_ref[...]
    spread = spread_ref[...]

    def head_norm(k, gain):
        k2 = k * k
        hi = k2.astype(BF16)
        lo = (k2 - hi.astype(F32)).astype(BF16)
        ss = _dot(hi, seg) + _dot(lo, seg)
        return k * lax.rsqrt(ss * (1.0 / HD_B) + EPS) * gain

    def rope(k):
        return jnp.concatenate([_rope128(k[:, j * LANE:(j + 1) * LANE], tk) for j in range(w // LANE)], axis=1)

    ks = rope(head_norm(kv[:, 2 * w:3 * w], kg_ref[0:1, :]))
    kw = rope(head_norm(kv[:, 4 * w:5 * w], kg_ref[1:2, :]))
    vs = kv[:, 3 * w:4 * w]
    vw = kv[:, 5 * w:6 * w]
    kv_rows = jnp.concatenate([kv[:, :2 * w], ks, vs], axis=1)
    if position_minor:
        kvt_ref[...] = kv_rows.T
        cmp_ref[...] = kv[:, :2 * w]
    else:
        kv_ref[...] = kv_rows
    win_ref[:, :w] = kw
    win_ref[:, w:] = vw
    lane = lax.broadcasted_iota(jnp.int32, (1, N_KV_B * LANE), 1)
    ones_hi = ((lane & HD_B) != 0).astype(F32)
    ks_ref[...] = _dot(ks.astype(BF16), spread).astype(BF16)
    kw_ref[...] = _dot(kw.astype(BF16), spread).astype(BF16)
    vs_ref[...] = (_dot(vs.astype(BF16), spread) + ones_hi).astype(BF16)
    vw_ref[...] = (_dot(vw.astype(BF16), spread) + ones_hi).astype(BF16)


def _nsa_proj(x, g, wts, tab_q, tab_k, n_tab_tiles, tm, *, position_minor):
    m, d = x.shape
    w = N_KV_B * HD_B
    ws = N_KV_B * LANE
    tile = lambda n: pl.BlockSpec((tm, n), lambda i: (i, 0))
    tab = pl.BlockSpec((3, tm, LANE), lambda i: (0, i % n_tab_tiles, 0))
    consts = [g, wts["wq"], wts["wg"], wts["wkv"], wts["seg"], wts["spread"], wts["qg"], wts["kg"]]
    sds = jax.ShapeDtypeStruct
    if position_minor:
        seq = n_tab_tiles * tm
        kv_shapes = (sds((m // seq, 4 * w, seq), F32), sds((m, 2 * w), F32))
        kv_specs = (pl.BlockSpec((None, 4 * w, tm), lambda i: (i // n_tab_tiles, 0, i % n_tab_tiles)), tile(2 * w))
    else:
        kv_shapes = (sds((m, 4 * w), F32),)
        kv_specs = (tile(4 * w),)
    slab = sds((m, ws), BF16)
    return pl.pallas_call(
        functools.partial(_nsa_proj_body, position_minor=position_minor),
        out_shape=(sds((m, N_HEADS_B * LANE), BF16), sds((m, ws), F32)) + kv_shapes
                  + (sds((m, 2 * w), F32), slab, slab, slab, slab),
        grid=(m // tm,),
        in_specs=[tile(d)] + [_const_spec(c.shape) for c in consts] + [tab, tab],
        out_specs=(tile(N_HEADS_B * LANE), tile(ws)) + kv_specs + (tile(2 * w), tile(ws), tile(ws), tile(ws), tile(ws)),
        compiler_params=_params("parallel"),
        name="nsa_proj",
    )(x, *consts, tab_q, tab_k)


CMP_PLANES = 2 * N_KV_B * HD_B // LANE


def _flatten_cmp_blocks(load_rows, x_s, n_blocks):
    for ng in range(n_blocks // 8):
        for l in range(CMP_BLOCK):
            for c in range(CMP_PLANES):
                rows = load_rows(c, ng * 8 * CMP_BLOCK + l)
                for half in range(LANE // HD_B):
                    x_s[c * (LANE // HD_B) + half, ng * 8:(ng + 1) * 8, l * HD_B:(l + 1) * HD_B] = (
                        rows[:, half * HD_B:(half + 1) * HD_B])


def _compress_slot(x_s, slot, pe_ref, w1_ref, w2_ref, kcg_ref):
    n_blocks = x_s.shape[1]
    xs = x_s[slot * N_KV_B:(slot + 1) * N_KV_B].reshape(N_KV_B * n_blocks, CMP_BLOCK * HD_B)
    hid = jax.nn.gelu(_dot((xs + pe_ref[slot]).astype(BF16), w1_ref[slot]))
    y = _dot(hid.astype(BF16), w2_ref[slot])
    return _rms(y, kcg_ref[...]) if slot == 0 else y


def _cmp_prompt_body(*refs, n_blocks):
    planes = refs[:CMP_PLANES]
    pe_ref, w1_ref, w2_ref, kcg_ref, kc_ref, vc_ref, x_s = refs[CMP_PLANES:]
    _flatten_cmp_blocks(lambda c, start: planes[c][pl.ds(start, 8, stride=CMP_BLOCK), :], x_s, n_blocks)
    kc_ref[...] = _compress_slot(x_s, 0, pe_ref, w1_ref, w2_ref, kcg_ref).reshape(N_KV_B, n_blocks, HD_B)
    vc_ref[...] = _compress_slot(x_s, 1, pe_ref, w1_ref, w2_ref, kcg_ref).reshape(N_KV_B, n_blocks, HD_B)


def _cmp_prompt(kv_rows, b, t, pe, w1, w2, kc_g):
    tt = min(t, 2048)
    nbk = tt // CMP_BLOCK
    assert t % tt == 0 and nbk % 8 == 0
    steps = t // tt
    out = jax.ShapeDtypeStruct((b, N_KV_B, t // CMP_BLOCK, HD_B), F32)
    out_spec = pl.BlockSpec((None, N_KV_B, nbk, HD_B), lambda bi, i: (bi, 0, i, 0))
    plane_specs = [pl.BlockSpec((tt, LANE), lambda bi, i, c=c: (bi * steps + i, c)) for c in range(CMP_PLANES)]
    return pl.pallas_call(
        functools.partial(_cmp_prompt_body, n_blocks=nbk),
        out_shape=(out, out),
        grid=(b, steps),
        in_specs=plane_specs + [_const_spec(pe.shape), _const_spec(w1.shape), _const_spec(w2.shape),
                                _const_spec(kc_g.shape)],
        out_specs=(out_spec, out_spec),
        scratch_shapes=[pltpu.VMEM((2 * N_KV_B, nbk, CMP_BLOCK * HD_B), F32)],
        compiler_params=_params("parallel", "parallel"),
        name="nsa_compress",
    )(*([kv_rows] * CMP_PLANES), pe, w1, w2, kc_g)


def _masked_softmax(s, mask, axis):
    sm = jnp.where(mask, s, -jnp.inf)
    mx = jnp.max(sm, axis=axis, keepdims=True)
    mx = jnp.where(mx > -jnp.inf, mx, 0.0)
    e = jnp.where(mask, jnp.exp(s - mx), 0.0)
    return e / jnp.maximum(jnp.sum(e, axis=axis, keepdims=True), 1e-30)


CMP_SEGMENTS = (64, 64, 128)


def _cmp_block_order():
    order, base = [], 0
    for size in CMP_SEGMENTS:
        order += list(range(base, base + size, 2)) + list(range(base + 1, base + size, 2))
        base += size
    assert base == N_CMP_PAD
    return order


def _cmp_select_body(q_ref, kct_ref, vct_ref, gates_ref, o_ref, mnot_ref, *, tq):
    i = pl.program_id(2)
    q0 = i * tq
    gates = gates_ref[...]

    def run(nv):
        hv = nv // 2
        row = lax.broadcasted_iota(jnp.int32, (nv, tq), 0)
        tok = q0 + lax.broadcasted_iota(jnp.int32, (nv, tq), 1)
        blk_c = jnp.zeros((nv, tq), jnp.int32)
        base = 0
        for size in CMP_SEGMENTS:
            if base < nv:
                local = row - base
                seg_blk = base + jnp.where(local < size // 2, 2 * local, 2 * (local - size // 2) + 1)
                blk_c = jnp.where((row >= base) & (row < base + size), seg_blk, blk_c)
            base += size
        mask = (blk_c + 1) * CMP_BLOCK - 1 <= tok
        kct = kct_ref[:nv, :]
        vct = vct_ref[:, :nv]
        imp_parts = None
        for r in range(REP_B):
            qh = q_ref[:, r * LANE:(r + 1) * LANE]
            pt = _masked_softmax(_dot_nt(kct, qh), mask, 0)
            o_ref[:, r * LANE:(r + 1) * LANE] = _dot(vct, pt.astype(BF16)).T * gates[:, 3 * r:3 * r + 1]
            parts, base = [], 0
            for size in CMP_SEGMENTS:
                if base < nv:
                    parts.append(pt[base:base + size // 2] + pt[base + size // 2:base + size])
                base += size
            imp_parts = parts if imp_parts is None else [a + b for a, b in zip(imp_parts, parts)]
        imp = jnp.concatenate(imp_parts, axis=0)
        blk = lax.broadcasted_iota(jnp.int32, (hv, tq), 0)
        t_s = q0 + lax.broadcasted_iota(jnp.int32, (hv, tq), 1)
        cur = t_s // SEL_BLOCK
        forced = (blk == 0) | (blk == cur) | (blk == cur - 1)
        score = jnp.where(forced, jnp.inf, jnp.where(blk * SEL_BLOCK <= t_s, imp, -jnp.inf))
        blk_f = blk.astype(F32)
        pickable = score > -jnp.inf
        for _ in range(N_SEL):
            mx = jnp.max(score, axis=0, keepdims=True)
            first = jnp.min(jnp.where(score == mx, blk_f, float(N_SEL_PAD)), axis=0, keepdims=True)
            score = jnp.where(blk_f == first, -jnp.inf, score)
        mnot = jnp.where(pickable, jnp.where(score > -jnp.inf, 1.0, 0.0), 1.0)
        if hv < N_SEL_PAD:
            mnot = jnp.concatenate([mnot, jnp.ones((N_SEL_PAD - hv, tq), F32)], axis=0)
        mnot_ref[...] = mnot.T.astype(BF16)

    need = (q0 + tq) // CMP_BLOCK
    bounds, base = [], 0
    for size in CMP_SEGMENTS:
        base += size
        bounds.append(base)
    lo = 0
    for nv in bounds:
        pl.when((need > lo) & (need <= nv))(functools.partial(run, nv))
        lo = nv


def _cmp_select(qcat, kct, vct, gates, b, t, tq):
    m = b * t
    nq = t // tq
    return pl.pallas_call(
        functools.partial(_cmp_select_body, tq=tq),
        out_shape=(jax.ShapeDtypeStruct((m, N_HEADS_B * LANE), F32), jax.ShapeDtypeStruct((m, N_KV_B * LANE), BF16)),
        grid=(b, N_KV_B, nq),
        in_specs=[pl.BlockSpec((tq, REP_B * LANE), lambda bi, g, i: (bi * nq + i, g)),
                  pl.BlockSpec((None, None, N_CMP_PAD, LANE), lambda bi, g, i: (bi, g, 0, 0)),
                  pl.BlockSpec((None, None, LANE, N_CMP_PAD), lambda bi, g, i: (bi, g, 0, 0)),
                  pl.BlockSpec((tq, LANE), lambda bi, g, i: (bi * nq + i, g))],
        out_specs=(pl.BlockSpec((tq, REP_B * LANE), lambda bi, g, i: (bi * nq + i, g)),
                   pl.BlockSpec((tq, LANE), lambda bi, g, i: (bi * nq + i, g))),
        compiler_params=_params("parallel", "parallel", "parallel"),
        name="nsa_cmp_select",
    )(qcat, kct, vct, gates)


def _flash_body(tab_ref, bound_ref, q_ref, mnot_ref, k_ref, v_ref, gates_ref, mask_ref, o_ref,
                qs_ref, ks_ref, m_ref, acc_ref, *, tq, tk, fixed):
    step_id = pl.program_id(2)
    j = tab_ref[1, step_id]
    rows = REP_B * tq

    @pl.when(tab_ref[3, step_id] == 1)
    def _():
        if not fixed:
            m_ref[...] = jnp.full(m_ref.shape, M_INIT, F32)
        acc_ref[...] = jnp.zeros(acc_ref.shape, F32)
        lane = lax.broadcasted_iota(jnp.int32, (tq, LANE), 1)
        for r in range(REP_B):
            qh = q_ref[:, r * LANE:(r + 1) * LANE]
            if fixed:
                qh = jnp.where(lane < HD_B, qh, jnp.ones_like(qh))
            qs_ref[r * tq:(r + 1) * tq, :LANE] = mnot_ref[...]
            qs_ref[r * tq:(r + 1) * tq, LANE:] = qh

    def step(masked):
        kt = k_ref[...]
        lane = lax.broadcasted_iota(jnp.int32, (tk, LANE), 1)
        if fixed:
            kt = jnp.where(lane == HD_B, -bound_ref[0], kt.astype(F32)).astype(BF16)
        kpos = j * tk + lax.broadcasted_iota(jnp.int32, (tk, LANE), 0)
        ks_ref[:, :LANE] = jnp.where(kpos // SEL_BLOCK == lane, -MASK_BIG, 0.0).astype(BF16)
        ks_ref[:, LANE:] = kt
        s = _dot_nt(qs_ref[...], ks_ref[...])
        if masked:
            s = (s.reshape(REP_B, tq, tk) + mask_ref[...][None]).reshape(rows, tk)
        if fixed:
            acc_ref[...] += _dot(jnp.exp(s).astype(BF16), v_ref[...])
        else:
            m_old = m_ref[...]
            m_new = jnp.maximum(m_old, jnp.max(s, axis=-1, keepdims=True))
            p = jnp.exp(s - m_new[:, :1])
            acc_ref[...] = jnp.exp(m_old - m_new) * acc_ref[...] + _dot(p.astype(BF16), v_ref[...])
            m_ref[...] = m_new

    pl.when(tab_ref[2, step_id] == 0)(functools.partial(step, False))
    pl.when(tab_ref[2, step_id] == 1)(functools.partial(step, True))

    @pl.when(tab_ref[4, step_id] == 1)
    def _():
        gates = gates_ref[...]
        lane = lax.broadcasted_iota(jnp.int32, (tq, LANE), 1)
        for r in range(REP_B):
            a = acc_ref[r * tq:(r + 1) * tq, :]
            o = a / a[:, HD_B:HD_B + 1]
            g = gates[:, 3 * r + 1:3 * r + 2]
            o_ref[:, r * LANE:(r + 1) * LANE] = jnp.where(lane < HD_B, o * g, 0.0)


def _flash_steps(t, tq, tk):
    steps, offsets = [], []
    for i in range(t // tq):
        q_lo, q_hi = i * tq, (i + 1) * tq - 1
        js = list(range(q_hi // tk + 1))
        for j in js:
            masked = (j + 1) * tk - 1 > q_lo
            if masked and q_lo - j * tk not in offsets:
                offsets.append(q_lo - j * tk)
            pattern = offsets.index(q_lo - j * tk) if masked else (steps[-1][5] if steps else 0)
            steps.append((i, j, int(masked), j == js[0], j == js[-1], pattern))
    return steps, offsets


def _flash(qcat, mnot, k, v, gates, bound, b, t, tq, tk):
    m = b * t
    nq, nk = t // tq, t // tk
    steps, offsets = _flash_steps(t, tq, tk)
    tab = jnp.asarray(steps, jnp.int32).T
    dist = (jnp.asarray(offsets, jnp.int32)[:, None, None] + jnp.arange(tq, dtype=jnp.int32)[None, :, None]
            - jnp.arange(tk, dtype=jnp.int32)[None, None, :])
    masks = jnp.where(dist >= 0, 0.0, -MASK_BIG).astype(F32)
    kdim = 2 * LANE
    qidx = lambda bi, g, p, *pf: (bi * nq + pf[0][0, p], g)
    kidx = lambda bi, g, p, *pf: (bi * nk + pf[0][1, p], g)
    midx = lambda bi, g, p, *pf: (pf[0][5, p], 0, 0)

    def call(fixed):
        name = "nsa_flash_sel" + ("" if fixed else "_online")
        return pl.pallas_call(
            functools.partial(_flash_body, tq=tq, tk=tk, fixed=fixed),
            out_shape=jax.ShapeDtypeStruct((m, N_HEADS_B * LANE), F32),
            grid_spec=pltpu.PrefetchScalarGridSpec(
                num_scalar_prefetch=2,
                grid=(b, N_KV_B, len(steps)),
                in_specs=[pl.BlockSpec((tq, REP_B * LANE), qidx), pl.BlockSpec((tq, LANE), qidx),
                          pl.BlockSpec((tk, LANE), kidx), pl.BlockSpec((tk, LANE), kidx),
                          pl.BlockSpec((tq, LANE), qidx), pl.BlockSpec((None, tq, tk), midx)],
                out_specs=pl.BlockSpec((tq, REP_B * LANE), qidx),
                scratch_shapes=[pltpu.VMEM((REP_B * tq, kdim), BF16), pltpu.VMEM((tk, 2 * LANE), BF16),
                                pltpu.VMEM((REP_B * tq, LANE), F32), pltpu.VMEM((REP_B * tq, LANE), F32)]),
            compiler_params=_params("parallel", "parallel", "arbitrary"),
            name=name,
        )(tab, bound.reshape(1), qcat, mnot, k, v, gates, masks)

    return lax.cond(bound <= SHIFT_MAX, lambda: call(True), lambda: call(False))


def _window_body(bound_ref, q_ref, *refs, tile, n_key_tiles, fixed):
    k_refs = refs[:n_key_tiles]
    v_refs = refs[n_key_tiles:2 * n_key_tiles]
    gates_ref, mask_ref, o_ref = refs[2 * n_key_tiles:]
    keys = jnp.concatenate([r[...] for r in k_refs], axis=0)
    vals = jnp.concatenate([r[...] for r in v_refs], axis=0)
    q = jnp.concatenate([q_ref[:, r * LANE:(r + 1) * LANE] for r in range(REP_B)], axis=0)
    if fixed:
        lane_k = lax.broadcasted_iota(jnp.int32, keys.shape, 1)
        keys = jnp.where(lane_k == HD_B, -bound_ref[0], keys.astype(F32)).astype(BF16)
        lane_q = lax.broadcasted_iota(jnp.int32, q.shape, 1)
        q = jnp.where(lane_q < HD_B, q, jnp.ones_like(q))
    else:
        lane_q = lax.broadcasted_iota(jnp.int32, q.shape, 1)
        q = jnp.where(lane_q < HD_B, q, jnp.zeros_like(q))
    span = n_key_tiles * tile
    s = (_dot_nt(q, keys).reshape(REP_B, tile, span) + mask_ref[...][None]).reshape(REP_B * tile, span)
    if not fixed:
        s = s - jnp.max(s, axis=-1, keepdims=True)
    acc = _dot(jnp.exp(s).astype(BF16), vals)
    gates = gates_ref[...]
    lane = lax.broadcasted_iota(jnp.int32, (tile, LANE), 1)
    for r in range(REP_B):
        a = acc[r * tile:(r + 1) * tile]
        g = gates[:, 3 * r + 2:3 * r + 3]
        o_ref[:, r * LANE:(r + 1) * LANE] = jnp.where(lane < HD_B, a / a[:, HD_B:HD_B + 1] * g, 0.0)


def _window_attn(qcat, k, v, gates, bound, b, t, tile):
    assert WINDOW % tile == 0 and t % tile == 0
    m = b * t
    nq = t // tile
    back = WINDOW // tile
    nkt = back + 1
    first_tile = jnp.arange(back + 1, dtype=jnp.int32)[:, None, None] - back
    col = jnp.arange(nkt * tile, dtype=jnp.int32)[None, None, :]
    key_tile = first_tile + col // tile
    dist = (jnp.arange(tile, dtype=jnp.int32)[None, :, None] + back * tile) - col
    masks = jnp.where((dist >= 0) & (dist <= WINDOW) & (key_tile >= 0), 0.0, -MASK_BIG).astype(F32)
    qidx = lambda bi, g, i, *_: (bi * nq + i, g)
    kspecs = [pl.BlockSpec((tile, LANE), lambda bi, g, i, *_, c=c: (bi * nq + jnp.maximum(i - back + c, 0), g))
              for c in range(nkt)]

    def call(fixed):
        return pl.pallas_call(
            functools.partial(_window_body, tile=tile, n_key_tiles=nkt, fixed=fixed),
            out_shape=jax.ShapeDtypeStruct((m, N_HEADS_B * LANE), F32),
            grid_spec=pltpu.PrefetchScalarGridSpec(
                num_scalar_prefetch=1,
                grid=(b, N_KV_B, nq),
                in_specs=[pl.BlockSpec((tile, REP_B * LANE), qidx)] + kspecs + kspecs
                         + [pl.BlockSpec((tile, LANE), qidx),
                            pl.BlockSpec((None, tile, nkt * tile), lambda bi, g, i, *_: (jnp.minimum(i, back), 0, 0))],
                out_specs=pl.BlockSpec((tile, REP_B * LANE), qidx)),
            compiler_params=_params("parallel", "parallel", "parallel"),
            name="nsa_window" if fixed else "nsa_window_rowmax",
        )(bound.reshape(1), qcat, *([k] * nkt), *([v] * nkt), gates, masks)

    return lax.cond(bound <= SHIFT_MAX, lambda: call(True), lambda: call(False))


def _sum_proj_body(a_ref, b_ref, c_ref, x_ref, w_ref, o_ref):
    o = (a_ref[...] + b_ref[...] + c_ref[...]).astype(BF16)
    o_ref[...] = x_ref[...] + _dot(o, w_ref[...])


def _sum_proj(a, b, c, x, w):
    m, d = x.shape
    kdim = a.shape[1]
    tm = _row_tile(m)
    big = pl.BlockSpec((tm, kdim), lambda i: (i, 0))
    row = pl.BlockSpec((tm, d), lambda i: (i, 0))
    return pl.pallas_call(
        _sum_proj_body,
        out_shape=jax.ShapeDtypeStruct((m, d), F32),
        grid=(m // tm,),
        in_specs=[big, big, big, row, _const_spec(w.shape)],
        out_specs=row,
        compiler_params=_params("parallel"),
        name="nsa_out_proj",
    )(a, b, c, x, w)


def _nsa_weights(w_in, q_g, k_g, w_out):
    d = w_in.shape[0]
    nq = N_HEADS_B * HD_B
    w = N_KV_B * HD_B
    wq = w_in[:, :nq].reshape(d, N_HEADS_B, 1, HD_B)
    wq = jnp.broadcast_to(wq, (d, N_HEADS_B, 2, HD_B)).reshape(d, N_HEADS_B * LANE)
    wg = w_in[:, nq:nq + 3 * N_HEADS_B].reshape(d, N_KV_B, REP_B * 3)
    wg = jnp.pad(wg, ((0, 0), (0, 0), (0, LANE - REP_B * 3))).reshape(d, N_KV_B * LANE)
    wkv = w_in[:, nq + 3 * N_HEADS_B:]
    lane = jnp.arange(w)
    seg = (lane[:, None] // HD_B == lane[None, :] // HD_B).astype(BF16)
    spread = (lane[:, None] // HD_B * LANE + lane[:, None] % HD_B == jnp.arange(N_KV_B * LANE)[None, :]).astype(BF16)
    wo = jnp.pad(w_out.reshape(N_HEADS_B, HD_B, -1), ((0, 0), (0, LANE - HD_B), (0, 0)))
    return {
        "wq": wq.astype(BF16), "wg": wg.astype(BF16), "wkv": wkv.astype(BF16), "seg": seg, "spread": spread,
        "qg": jnp.tile(q_g, 2)[None, :], "kg": jnp.stack([jnp.tile(k_g[1], N_KV_B), jnp.tile(k_g[2], N_KV_B)]),
        "wo": wo.reshape(N_HEADS_B * LANE, -1).astype(BF16),
    }


def _cmp_weights(pe, w_c1, w_c2, kc_g):
    return (pe.reshape(2, 1, CMP_BLOCK * HD_B), w_c1.reshape(2, CMP_BLOCK * HD_B, HD_B).astype(BF16),
            w_c2.astype(BF16), kc_g[None, :])


def _nsa_prompt(x, g_mix, b, t, w_in, q_g, k_g, pe, w_c1, w_c2, w_out):
    wts = _nsa_weights(w_in, q_g, k_g, w_out)
    tm = _row_tile(t)
    pos = jnp.arange(t)
    qcat, gates, kv_t, cmp_rows, win_rows, ks_s, vs_s, kw_s, vw_s = _nsa_proj(
        x, g_mix, wts, _rope_tables(pos, LANE), _rope_tables(pos, HD_B), t // tm, tm, position_minor=True)
    kc_blk, vc_blk = _cmp_prompt(cmp_rows, b, t, *_cmp_weights(pe, w_c1, w_c2, k_g[0]))
    nb = t // CMP_BLOCK
    assert nb <= N_CMP_PAD and t % SEL_BLOCK == 0

    order = jnp.asarray(_cmp_block_order(), jnp.int32)

    def blocks(a, lo):
        a = jnp.pad(a, ((0, 0), (0, 0), (0, N_CMP_PAD - nb), (lo, LANE - HD_B - lo)))
        return a[:, :, order].astype(BF16)

    kct = blocks(kc_blk, HD_B)
    vct = blocks(vc_blk, 0).transpose(0, 1, 3, 2)
    o_cmp, mnot = _cmp_select(qcat, kct, vct, gates, b, t, min(CMP_SELECT_TILE, t))
    qmax = jnp.max(jnp.abs(q_g))
    o_sel = _flash(qcat, mnot, ks_s, vs_s, gates, qmax * jnp.max(jnp.abs(k_g[1])) * math.sqrt(HD_B), b, t,
                   *(min(n, t) for n in FLASH_SEL_TILES))
    o_win = _window_attn(qcat, kw_s, vw_s, gates, qmax * jnp.max(jnp.abs(k_g[2])) * math.sqrt(HD_B), b, t,
                         min(WINDOW_TILE, t))
    y = _sum_proj(o_cmp, o_sel, o_win, x, wts["wo"])
    wb = min(WINDOW, t)
    kv_out = kv_t.reshape(b, N_KV_SLOTS, N_KV_B, HD_B, t).transpose(0, 4, 1, 2, 3)
    win_out = win_rows.reshape(b, t, 2, N_KV_B, HD_B)[:, t - wb:]
    return y, kv_out, win_out


def _gmlp_layer(x, g, w_ins, ln_g, ln_b, w_s, b_s, w_outs, layer, *, single):
    gw = w_outs.shape[1] // N_GROUPS_A
    if single:
        ws = jnp.repeat(w_s[:, 0, 0], gw)[None, :]
        bs = jnp.repeat(b_s[:, 0], gw)[None, :]
    else:
        ws = w_s
        bs = jnp.repeat(b_s.T, gw, axis=1)
    return _gmlp(x, g[None, :], w_ins, ln_g[None, :], ln_b[None, :], ws, bs, w_outs, layer, single=single)


def _mlstm_proj_body(x_ref, g_ref, wq_ref, wk_ref, wv_ref, wgi_ref, wo_ref, bif_ref,
                     q_ref, k_ref, v_ref, gi_ref, og_ref):
    xb = _rms(x_ref[...], g_ref[...]).astype(BF16)
    q_ref[...] = _dot(xb, wq_ref[...]).astype(BF16)
    k_ref[...] = _dot(xb, wk_ref[...]).astype(BF16)
    v_ref[...] = _dot(xb, wv_ref[...]).astype(BF16)
    gi_ref[...] = _dot(xb, wgi_ref[...]) + bif_ref[...]
    og_ref[...] = jax.nn.sigmoid(_dot(xb, wo_ref[...]))


def _mlstm_proj(x, g, wts):
    m, d = x.shape
    hv = N_HEADS_C * DV_C
    tm = _row_tile(m)
    consts = [g, wts["wq"], wts["wk"], wts["wv"], wts["wgi"], wts["wo"], wts["bif"]]
    tile = lambda n: pl.BlockSpec((tm, n), lambda i: (i, 0))
    return pl.pallas_call(
        _mlstm_proj_body,
        out_shape=(jax.ShapeDtypeStruct((m, N_HEADS_C * LANE), BF16), jax.ShapeDtypeStruct((m, N_HEADS_C * LANE), BF16),
                   jax.ShapeDtypeStruct((m, hv), BF16), jax.ShapeDtypeStruct((m, LANE), F32),
                   jax.ShapeDtypeStruct((m, hv), F32)),
        grid=(m // tm,),
        in_specs=[tile(d)] + [_const_spec(c.shape) for c in consts],
        out_specs=(tile(N_HEADS_C * LANE), tile(N_HEADS_C * LANE), tile(hv), tile(LANE), tile(hv)),
        compiler_params=_params("parallel"),
        name="mlstm_proj",
    )(x, *consts)


def _mlstm_scan_body(q_ref, k_ref, v_ref, gi_ref, git_ref, hs_ref, c_out, n_out, m_out, c_s, n_s, m_s):
    c = pl.program_id(0)
    nseq, L = q_ref.shape[:2]

    @pl.when(c == 0)
    def _():
        c_s[...] = jnp.zeros(c_s.shape, F32)
        n_s[...] = jnp.zeros(n_s.shape, F32)
        m_s[...] = jnp.zeros(m_s.shape, F32)

    row = lax.broadcasted_iota(jnp.int32, (L, L), 0)
    col = lax.broadcasted_iota(jnp.int32, (L, L), 1)
    causal = col <= row
    tril = causal.astype(BF16)
    gi, git, bcol_all, brow_all = [], [], [], []
    for b in range(nseq):
        gi.append(gi_ref[b])
        git.append(git_ref[b])
        fcol = jax.nn.log_sigmoid(gi[b])
        frow = jax.nn.log_sigmoid(git[b][N_HEADS_C:, :])
        bcol_all.append(sum(_dot(tril, part) for part in _split3(fcol)))
        brow_all.append(sum(_dot_nt(part, tril) for part in _split3(frow)))
    units = [(b, h) for b in range(nseq) for h in range(N_HEADS_C)]
    idx = range(len(units))
    sl = [slice(h * LANE, (h + 1) * LANE) for _, h in units]
    q = [q_ref[b, :, sl[u]] for u, (b, _) in enumerate(units)]
    k = [k_ref[b, :, sl[u]] for u, (b, _) in enumerate(units)]
    v = [v_ref[b, :, sl[u]] for u, (b, _) in enumerate(units)]
    qk = [_dot_nt(q[u], k[u]) for u in idx]
    cq = [_dot_nt(q[u], c_s[u].astype(BF16)) for u in idx]
    bcol = [bcol_all[b][:, N_HEADS_C + h:N_HEADS_C + h + 1] for b, h in units]
    m_prev = [m_s[u:u + 1, 0:1] for u in idx]
    s, a, m_t = [], [], []
    for u, (b, h) in enumerate(units):
        dlog = jnp.where(causal, bcol[u] - brow_all[b][h:h + 1, :] + git[b][h:h + 1, :], -jnp.inf)
        inter = bcol[u] + m_prev[u]
        m_t.append(jnp.maximum(inter, jnp.max(dlog, axis=1, keepdims=True)))
        s.append(qk[u] * jnp.exp(dlog - m_t[u]))
        a.append(jnp.exp(inter - m_t[u]))
    sv = [_dot(s[u].astype(BF16), v[u]) for u in idx]
    wk, decay, m_new = [], [], []
    for u, (b, h) in enumerate(units):
        nq = jnp.sum(q[u].astype(F32) * n_s[u:u + 1, :], axis=1, keepdims=True)
        den = a[u] * nq + jnp.sum(s[u], axis=1, keepdims=True)
        hs_ref[b, :, sl[u]] = (a[u] * cq[u] + sv[u]) / jnp.maximum(jnp.abs(den), jnp.exp(-m_t[u]))
        b_end = bcol[u][L - 1:L, :]
        wlog = b_end - bcol[u] + gi[b][:, h:h + 1]
        m_new.append(jnp.maximum(b_end + m_prev[u], jnp.max(wlog, axis=0, keepdims=True)))
        wk.append(jnp.exp(wlog - m_new[u]))
        decay.append(jnp.exp(b_end + m_prev[u] - m_new[u]))
    upd = [_dot_tn((v[u].astype(F32) * wk[u]).astype(BF16), k[u]) for u in idx]
    for u in idx:
        c_s[u] = decay[u] * c_s[u] + upd[u]
        n_s[u:u + 1, :] = decay[u] * n_s[u:u + 1, :] + jnp.sum(k[u].astype(F32) * wk[u], axis=0, keepdims=True)
        m_s[u:u + 1, :] = jnp.broadcast_to(m_new[u], (1, LANE))

    @pl.when(c == pl.num_programs(0) - 1)
    def _():
        c_out[...] = c_s[...]
        n_out[...] = n_s[...]
        m_out[...] = m_s[...]


def _mlstm_scan(q, k, v, gi, git, b, t):
    L = math.gcd(t, CHUNK_C)
    nc = t // L
    hv = N_HEADS_C * DV_C
    units = b * N_HEADS_C
    seq = lambda a: a.reshape(b, t, a.shape[-1])
    tile = lambda n: pl.BlockSpec((b, L, n), lambda c: (0, c, 0))
    state = lambda *shape: pl.BlockSpec(shape, lambda c: (0,) * len(shape))
    hs, c_fin, n_fin, m_fin = pl.pallas_call(
        _mlstm_scan_body,
        out_shape=(jax.ShapeDtypeStruct((b, t, hv), F32), jax.ShapeDtypeStruct((units, DV_C, LANE), F32),
                   jax.ShapeDtypeStruct((units, LANE), F32), jax.ShapeDtypeStruct((units, LANE), F32)),
        grid=(nc,),
        in_specs=[tile(N_HEADS_C * LANE), tile(N_HEADS_C * LANE), tile(hv), tile(LANE),
                  pl.BlockSpec((b, 2 * N_HEADS_C, L), lambda c: (0, 0, c))],
        out_specs=(tile(hv), state(units, DV_C, LANE), state(units, LANE), state(units, LANE)),
        scratch_shapes=[pltpu.VMEM((units, DV_C, LANE), F32), pltpu.VMEM((units, LANE), F32),
                        pltpu.VMEM((units, LANE), F32)],
        compiler_params=_params("arbitrary"),
        name="mlstm_scan",
    )(seq(q), seq(k), seq(v), seq(gi), git)
    return (hs.reshape(b * t, hv), c_fin.reshape(b, N_HEADS_C, DV_C, LANE), n_fin.reshape(b, N_HEADS_C, LANE),
            m_fin.reshape(b, N_HEADS_C, LANE))


def _mlstm_out_body(hs_ref, og_ref, hg_ref, x_ref, w_ref, o_ref):
    parts = []
    for h in range(N_HEADS_C):
        sl = slice(h * DV_C, (h + 1) * DV_C)
        parts.append((og_ref[:, sl] * _rms(hs_ref[:, sl], hg_ref[:, sl])).astype(BF16))
    o_ref[...] = x_ref[...] + _dot(jnp.concatenate(parts, axis=1), w_ref[...])


def _mlstm_out(hs, og, hg, x, w):
    m, d = x.shape
    hv = hs.shape[1]
    tm = _row_tile(m)
    wide = pl.BlockSpec((tm, hv), lambda i: (i, 0))
    row = pl.BlockSpec((tm, d), lambda i: (i, 0))
    return pl.pallas_call(
        _mlstm_out_body,
        out_shape=jax.ShapeDtypeStruct((m, d), F32),
        grid=(m // tm,),
        in_specs=[wide, wide, _const_spec(hg.shape), row, _const_spec(w.shape)],
        out_specs=row,
        compiler_params=_params("parallel"),
        name="mlstm_out",
    )(hs, og, hg, x, w)


def _mlstm_weights(w_in, b_if):
    d = w_in.shape[0]
    hk, hv = N_HEADS_C * DK_C, N_HEADS_C * DV_C

    def spread(w):
        w = w.reshape(d, N_HEADS_C, DK_C)
        return jnp.pad(w, ((0, 0), (0, 0), (0, LANE - DK_C))).reshape(d, N_HEADS_C * LANE)

    wgi = jnp.pad(w_in[:, 2 * hk + hv:2 * hk + hv + 2 * N_HEADS_C], ((0, 0), (0, LANE - 2 * N_HEADS_C)))
    return {
        "wq": spread(w_in[:, :hk]).astype(BF16),
        "wk": (spread(w_in[:, hk:2 * hk]) * (DK_C ** -0.5)).astype(BF16),
        "wv": w_in[:, 2 * hk:2 * hk + hv].astype(BF16),
        "wgi": wgi.astype(BF16),
        "wo": w_in[:, 2 * hk + hv + 2 * N_HEADS_C:].astype(BF16),
        "bif": jnp.pad(b_if, (0, LANE - 2 * N_HEADS_C))[None, :],
    }


def _mlstm_prompt(x, g_mix, b, t, w_in, b_if, h_g, w_out):
    wts = _mlstm_weights(w_in, b_if)
    q, k, v, gi, og = _mlstm_proj(x, g_mix, wts)
    git = gi[:, :2 * N_HEADS_C].reshape(b, t, 2 * N_HEADS_C).transpose(0, 2, 1)
    hs, c, n, m = _mlstm_scan(q, k, v, gi, git, b, t)
    y = _mlstm_out(hs, og, h_g[None, :], x, w_out.astype(BF16))
    return y, c[..., :DK_C], n[..., :DK_C], m[..., 0]


def _proj_add_body(o_ref, x_ref, w_ref, out_ref):
    out_ref[...] = x_ref[...] + _dot(o_ref[...].astype(BF16), w_ref[...])


def _proj_add(o, x, w):
    m, d = x.shape
    tm = _row_tile(m)
    return pl.pallas_call(
        _proj_add_body,
        out_shape=jax.ShapeDtypeStruct((m, d), F32),
        grid=(m // tm,),
        in_specs=[pl.BlockSpec((tm, o.shape[1]), lambda i: (i, 0)), pl.BlockSpec((tm, d), lambda i: (i, 0)),
                  _const_spec(w.shape)],
        out_specs=pl.BlockSpec((tm, d), lambda i: (i, 0)),
        compiler_params=_params("parallel"),
        name="proj_add",
    )(o, x, w)


def _nsa_step_body(pt_ref, *refs, n_pages, page, past_len):
    pages = refs[:n_pages]
    (win_ref, qr_ref, qn_ref, gates_ref, nkv_ref, nwin_ref, pe_ref, w1_ref, w2_ref, kcg_ref,
     o_ref, c_s, x_s) = refs[n_pages:]
    w = N_KV_B * HD_B
    length = n_pages * page
    nb = length // CMP_BLOCK
    t = past_len
    for p in range(n_pages):
        for c in range(CMP_PLANES):
            c_s[c, p * page:(p + 1) * page, :] = pages[p][c * LANE:(c + 1) * LANE, :].T
    _flatten_cmp_blocks(lambda c, start: c_s[c, pl.ds(start, 8, stride=CMP_BLOCK), :], x_s, nb)
    lane_w = lax.broadcasted_iota(jnp.int32, (HD_B, w), 1)
    row_w = lax.broadcasted_iota(jnp.int32, (HD_B, w), 0)
    cmp_nat = []
    for slot in range(2):
        y = _compress_slot(x_s, slot, pe_ref, w1_ref, w2_ref, kcg_ref)
        nat = jnp.zeros((nb, w), F32)
        for g in range(N_KV_B):
            place = (lane_w == row_w + g * HD_B).astype(BF16)
            nat = nat + _dot(y[g * nb:(g + 1) * nb].astype(BF16), place)
        cmp_nat.append(nat.astype(BF16))
    kc, vc = cmp_nat
    qr = qr_ref[...]
    qn = qn_ref[...]
    gates = gates_ref[...]
    nh = N_HEADS_B
    blk = lax.broadcasted_iota(jnp.int32, (nh, nb), 1)
    p_c = _masked_softmax(_dot_nt(qn, kc), (blk + 1) * CMP_BLOCK - 1 <= t, 1)
    o_c = _dot(p_c.astype(BF16), vc)
    blk_t = lax.broadcasted_iota(jnp.int32, (nb, nh), 0)
    p_t = _masked_softmax(_dot_nt(kc, qn), (blk_t + 1) * CMP_BLOCK - 1 <= t, 0)
    gsum = (lax.broadcasted_iota(jnp.int32, (nh, LANE), 0) // REP_B
            == lax.broadcasted_iota(jnp.int32, (nh, LANE), 1)).astype(BF16)
    pair = (lax.broadcasted_iota(jnp.int32, (nb, nb), 1) // (SEL_BLOCK // CMP_BLOCK)
            == lax.broadcasted_iota(jnp.int32, (nb, nb), 0)).astype(BF16)
    imp = sum(_dot(part, gsum) for part in _split3(p_t))
    imp = sum(_dot(pair, part) for part in _split3(imp))
    sblk = lax.broadcasted_iota(jnp.int32, (nb, LANE), 0)
    cur = t // SEL_BLOCK
    forced = (sblk == 0) | (sblk == cur) | (sblk == cur - 1)
    score = jnp.where(forced, jnp.inf, jnp.where(sblk * SEL_BLOCK <= t, imp, -jnp.inf))
    sblk_f = sblk.astype(F32)
    pickable = score > -jnp.inf
    for _ in range(N_SEL):
        mx = jnp.max(score, axis=0, keepdims=True)
        first = jnp.min(jnp.where(score == mx, sblk_f, float(nb)), axis=0, keepdims=True)
        score = jnp.where(sblk_f == first, -jnp.inf, score)
    notsel = jnp.where(pickable, jnp.where(score > -jnp.inf, -MASK_BIG, 0.0), -MASK_BIG)
    bias = _dot_nt(gsum, notsel.astype(BF16)).astype(BF16)
    expand = (lax.broadcasted_iota(jnp.int32, (nb, length), 1) // SEL_BLOCK
              == lax.broadcasted_iota(jnp.int32, (nb, length), 0)).astype(BF16)
    nkv = nkv_ref[...]
    nwin = nwin_ref[...]

    def attend(s, k_new, v_new, weighted_values):
        s_new = jnp.sum(qr.astype(F32) * k_new.astype(BF16).astype(F32), axis=1, keepdims=True)
        m = jnp.maximum(jnp.max(s, axis=1, keepdims=True), s_new)
        e = jnp.exp(s - m)
        e_new = jnp.exp(s_new - m)
        den = jnp.sum(e, axis=1, keepdims=True) + e_new
        num = weighted_values(e.astype(BF16)) + e_new.astype(BF16).astype(F32) * v_new.astype(BF16).astype(F32)
        return num / den

    s_sel = jnp.concatenate([_dot(qr, pages[p][2 * w:3 * w, :].astype(BF16)) for p in range(n_pages)], axis=1)
    o_s = attend(s_sel + _dot(bias, expand), nkv[:, 2 * w:3 * w], nkv[:, 3 * w:],
                 lambda e: sum(_dot_nt(e[:, p * page:(p + 1) * page], pages[p][3 * w:, :].astype(BF16))
                               for p in range(n_pages)))
    wb = win_ref.shape[1]
    pos_w = t - wb + lax.broadcasted_iota(jnp.int32, (nh, wb), 1)
    ok_w = (pos_w >= 0) & (t - pos_w <= WINDOW)
    s_w = jnp.where(ok_w, _dot(qr, win_ref[:w, :].astype(BF16)), -MASK_BIG)
    o_w = attend(s_w, nwin[:, :w], nwin[:, w:], lambda e: _dot_nt(e, win_ref[w:, :].astype(BF16)))
    o_ref[...] = gates[:, 0:1] * o_c + gates[:, 1:2] * o_s + gates[:, 2:3] * o_w


def _nsa_step(page_table, cache, win_cache, qr, qn, gates, new_kv, new_win, pe, w1, w2, kc_g, past_len):
    bsz, n_pages = page_table.shape
    page = cache.shape[2]
    assert page == LANE
    w = N_KV_B * HD_B
    nb = n_pages * page // CMP_BLOCK
    wb = win_cache.shape[2]
    per = lambda shape: pl.BlockSpec((None,) + shape, lambda b, pt: (b,) + (0,) * len(shape))
    const = lambda a: pl.BlockSpec(a.shape, lambda b, pt: (0,) * a.ndim)
    page_specs = [pl.BlockSpec((None, 4 * w, page), lambda b, pt, p=p: (pt[b, p], 0, 0)) for p in range(n_pages)]
    return pl.pallas_call(
        functools.partial(_nsa_step_body, n_pages=n_pages, page=page, past_len=past_len),
        out_shape=jax.ShapeDtypeStruct((bsz, N_HEADS_B, w), F32),
        grid_spec=pltpu.PrefetchScalarGridSpec(
            num_scalar_prefetch=1,
            grid=(bsz,),
            in_specs=page_specs + [per((2 * w, wb)), per((N_HEADS_B, w)), per((N_HEADS_B, w)), per((N_HEADS_B, LANE)),
                                   per((1, 4 * w)), per((1, 2 * w)), const(pe), const(w1), const(w2), const(kc_g)],
            out_specs=per((N_HEADS_B, w)),
            scratch_shapes=[pltpu.VMEM((2 * w // LANE, n_pages * page, LANE), F32),
                            pltpu.VMEM((2 * N_KV_B, nb, CMP_BLOCK * HD_B), F32)]),
        compiler_params=_params("parallel"),
        name="nsa_step",
    )(page_table, *([cache] * n_pages), win_cache, qr, qn, gates, new_kv, new_win, pe, w1, w2, kc_g)


def _nsa_sample_step(x, g_mix, past_len, kv_cache, win_cache, page_table, w_in, q_g, k_g, pe, w_c1, w_c2, w_out):
    bsz, d = x.shape
    w = N_KV_B * HD_B
    assert past_len % CMP_BLOCK == 0 and past_len // SEL_BLOCK + 1 <= past_len // CMP_BLOCK
    wts = _nsa_weights(w_in, q_g, k_g, w_out)
    pos = jnp.full((bsz,), past_len, jnp.int32)
    qcat, gates, kv_rows, win_rows, _, _, _, _ = _nsa_proj(
        x, g_mix, wts, _rope_tables(pos, LANE), _rope_tables(pos, HD_B), 1, bsz, position_minor=False)
    q5 = qcat.reshape(bsz, N_KV_B, REP_B, 2, HD_B)
    eye = jnp.eye(N_KV_B, dtype=BF16)
    qrows = (q5[:, :, :, :, None, :] * eye[None, :, None, None, :, None])
    qr = qrows[:, :, :, 0].reshape(bsz, N_HEADS_B, w)
    qn = qrows[:, :, :, 1].reshape(bsz, N_HEADS_B, w)
    gts = gates.reshape(bsz, N_KV_B, LANE)[:, :, :REP_B * 3].reshape(bsz, N_HEADS_B, 3)
    gts = jnp.pad(gts, ((0, 0), (0, 0), (0, LANE - 3)))
    pool, page = kv_cache.shape[:2]
    cache_t = kv_cache.reshape(pool, page, 4 * w).transpose(0, 2, 1)
    win_t = win_cache.reshape(bsz, -1, 2 * w).transpose(0, 2, 1)
    o = _nsa_step(page_table, cache_t, win_t, qr, qn, gts,
                  kv_rows.reshape(bsz, 1, 4 * w), win_rows.reshape(bsz, 1, 2 * w),
                  *_cmp_weights(pe, w_c1, w_c2, k_g[0]), past_len)
    own = (jnp.arange(N_HEADS_B)[:, None] // REP_B == jnp.arange(N_KV_B)[None, :]).astype(F32)
    w_exp = own[:, :, None, None] * w_out.reshape(N_HEADS_B, 1, HD_B, d)
    y = _proj_add(o.reshape(bsz, N_HEADS_B * w), x, w_exp.reshape(N_HEADS_B * w, d).astype(BF16))
    return y, kv_rows.reshape(bsz, 1, N_KV_SLOTS, N_KV_B, HD_B), win_rows.reshape(bsz, 1, 2, N_KV_B, HD_B)


def _mlstm_step_body(q_ref, k_ref, qt_ref, kt_ref, v_ref, gi_ref, ct_ref, n_ref, m_ref,
                     h_ref, cto_ref, no_ref, mo_ref, *, sb):
    gi = gi_ref[...]
    logf = jax.nn.log_sigmoid(gi)
    m_all = m_ref[...]
    lane_m = lax.broadcasted_iota(jnp.int32, (1, N_HEADS_C), 1)
    for s in range(sb):
        m_new_row = jnp.zeros((1, N_HEADS_C), F32)
        for h in range(N_HEADS_C):
            q = q_ref[s:s + 1, h * LANE:h * LANE + DK_C].astype(F32)
            k = k_ref[s:s + 1, h * LANE:h * LANE + DK_C].astype(F32)
            qc = qt_ref[s, :, h:h + 1]
            kc = kt_ref[s, :, h:h + 1]
            v = v_ref[s:s + 1, h * DV_C:(h + 1) * DV_C].astype(F32)
            ct = ct_ref[s, h]
            n = n_ref[s, h:h + 1, :]
            it = gi[s:s + 1, h:h + 1]
            b = logf[s:s + 1, N_HEADS_C + h:N_HEADS_C + h + 1]
            m0 = m_all[s:s + 1, h:h + 1]
            inter = b + m0
            m_t = jnp.maximum(inter, it)
            wgt = jnp.exp(it - m_t)
            a = jnp.exp(inter - m_t)
            sc = jnp.sum(q * k, axis=1, keepdims=True) * wgt
            num = a * jnp.sum(ct * qc, axis=0, keepdims=True) + sc * v
            den = a * jnp.sum(n * q, axis=1, keepdims=True) + sc
            h_ref[s:s + 1, h * DV_C:(h + 1) * DV_C] = num / jnp.maximum(jnp.abs(den), jnp.exp(-m_t))
            cto_ref[s, h] = a * ct + (wgt * kc) * v
            no_ref[s, h:h + 1, :] = a * n + wgt * k
            m_new_row = jnp.where(lane_m == h, m_t, m_new_row)
        mo_ref[s:s + 1, :] = m_new_row


def _mlstm_step(q, k, qt, kt, v, gi, ct0, n0, m0):
    bsz = q.shape[0]
    sb = 8
    row = lambda n: pl.BlockSpec((sb, n), lambda i: (i, 0))
    c_spec = pl.BlockSpec((sb, N_HEADS_C, DK_C, DV_C), lambda i: (i, 0, 0, 0))
    n_spec = pl.BlockSpec((sb, N_HEADS_C, DK_C), lambda i: (i, 0, 0))
    col_spec = pl.BlockSpec((sb, DK_C, N_HEADS_C), lambda i: (i, 0, 0))
    hv = N_HEADS_C * DV_C
    return pl.pallas_call(
        functools.partial(_mlstm_step_body, sb=sb),
        out_shape=(jax.ShapeDtypeStruct((bsz, hv), F32), jax.ShapeDtypeStruct(ct0.shape, F32),
                   jax.ShapeDtypeStruct(n0.shape, F32), jax.ShapeDtypeStruct(m0.shape, F32)),
        grid=(bsz // sb,),
        in_specs=[row(N_HEADS_C * LANE), row(N_HEADS_C * LANE), col_spec, col_spec, row(hv), row(LANE),
                  c_spec, n_spec, row(N_HEADS_C)],
        out_specs=(row(hv), c_spec, n_spec, row(N_HEADS_C)),
        compiler_params=_params("parallel"),
        name="mlstm_step",
    )(q, k, qt, kt, v, gi, ct0, n0, m0)


def _mlstm_sample_step(x, g_mix, c0, n0, m0, w_in, b_if, h_g, w_out):
    bsz = x.shape[0]
    wts = _mlstm_weights(w_in, b_if)
    q, k, v, gi, og = _mlstm_proj(x, g_mix, wts)
    cols = lambda a: a.astype(F32).reshape(bsz, N_HEADS_C, LANE)[:, :, :DK_C].transpose(0, 2, 1)
    hs, ct, n, m = _mlstm_step(q, k, cols(q), cols(k), v, gi, c0.transpose(0, 1, 3, 2), n0, m0)
    y = _mlstm_out(hs, og, h_g[None, :], x, w_out.astype(BF16))
    return y, ct.transpose(0, 1, 3, 2), n, m


def kernel(x_prompt, x_sample, cache_nsa_kv, cache_nsa_win, state_mlstm_C, state_mlstm_n, state_mlstm_m, page_table,
           norm_mix_g, norm_ffn_g, ffn_w1, ffn_w2, a_w_in, a_ln_g, a_ln_b, a_w_s, a_b_s, a_w_out,
           b_w_in, b_q_g, b_k_g, b_pe, b_w_c1, b_w_c2, b_w_out, c_w_in, c_b_if, c_h_g, c_w_out):
    bp, t, d = x_prompt.shape
    bs, ts, _ = x_sample.shape
    assert ts == 1
    past_len = page_table.shape[1] * cache_nsa_kv.shape[2]
    xp = x_prompt.reshape(bp * t, d)
    xs = x_sample.reshape(bs, d)
    out = {k: [] for k in ("v_s", "kv_p", "win_p", "kv_s", "win_s", "C_p", "n_p", "m_p", "C_s", "n_s", "m_s")}
    ffn_w1_b, ffn_w2_b = ffn_w1.astype(BF16), ffn_w2.astype(BF16)
    a_w_in_b, a_w_out_b = a_w_in.astype(BF16), a_w_out.astype(BF16)
    for layer in range(norm_mix_g.shape[0]):
        kind, j = layer % 3, layer // 3
        gm = norm_mix_g[layer]
        if kind == 0:
            args = (a_w_in_b, a_ln_g[j], a_ln_b[j], a_w_s[j], a_b_s[j], a_w_out_b, j)
            xp = _gmlp_layer(xp, gm, *args, single=False)[0]
            xs, v = _gmlp_layer(xs, gm, *args, single=True)
            out["v_s"].append(v.reshape(bs, ts, -1))
        elif kind == 1:
            args = (b_w_in[j], b_q_g[j], b_k_g[j], b_pe[j], b_w_c1[j], b_w_c2[j], b_w_out[j])
            xp, kv, win = _nsa_prompt(xp, gm[None, :], bp, t, *args)
            out["kv_p"].append(kv)
            out["win_p"].append(win)
            xs, kv, win = _nsa_sample_step(xs, gm[None, :], past_len, cache_nsa_kv[j], cache_nsa_win[j], page_table,
                                           *args)
            out["kv_s"].append(kv)
            out["win_s"].append(win)
        else:
            args = (c_w_in[j], c_b_if[j], c_h_g[j], c_w_out[j])
            xp, c, n, m = _mlstm_prompt(xp, gm[None, :], bp, t, *args)
            out["C_p"].append(c)
            out["n_p"].append(n)
            out["m_p"].append(m)
            xs, c, n, m = _mlstm_sample_step(xs, gm[None, :], state_mlstm_C[j], state_mlstm_n[j], state_mlstm_m[j],
                                             *args)
            out["C_s"].append(c)
            out["n_s"].append(n)
            out["m_s"].append(m)
        gf = norm_ffn_g[layer][None, :]
        xp = _ffn(xp, gf, ffn_w1_b, ffn_w2_b, layer)
        xs = _ffn(xs, gf, ffn_w1_b, ffn_w2_b, layer)
    st = {k: jnp.stack(v) for k, v in out.items()}
    return (xp.reshape(bp, t, d), xs.reshape(bs, ts, d), st["v_s"], st["kv_p"], st["win_p"], st["kv_s"], st["win_s"],
            st["C_p"], st["n_p"], st["m_p"], st["C_s"], st["n_s"], st["m_s"])
```
